```python
import jax
import jax.numpy as jnp
from jax import lax
import numpy as np

D_MODEL = 1024
BATCH = 4
SEQ = 4096
DEPTH = 4
DEC_BATCH = 128
DEC_SEQ = 1
PAST_LEN = 8192
PAGE_SIZE = 128

HEAD_DIM = 64
MIX_W = D_MODEL
MEM_HEADS = 4
MEM_W = MEM_HEADS * HEAD_DIM
N_MEM = 256
D_RNN = MIX_W - MEM_W
RNN_BLOCKS = D_RNN // HEAD_DIM
RNN_BLOCK = D_RNN // RNN_BLOCKS
CONV_A = 4
LRU_C = 8.0
N_Q = (MIX_W - MEM_W) // HEAD_DIM
N_KV = 4
GROUP = N_Q // N_KV
Q_W = N_Q * HEAD_DIM
KV_W = N_KV * HEAD_DIM
WINDOW = 128
ROPE_THETA = 10000.0
D_FF = 3 * D_MODEL
CONV_F = 3
N_A = DEPTH // 2
N_B = DEPTH - N_A
EPS = 1e-6
NEG = -1e30
ATT_SCALE = HEAD_DIM ** -0.5

kernel_name = 'yoco_hawk_swa_sink_convffn_decoder'


def _rmsnorm(x, g):
    x32 = x.astype(jnp.float32)
    r = lax.rsqrt(jnp.mean(x32 * x32, axis=-1, keepdims=True) + EPS)
    return (x32 * r * g.astype(jnp.float32)).astype(x.dtype)


def _rope(x, pos):
    half = HEAD_DIM // 2
    inv = ROPE_THETA ** (-jnp.arange(half, dtype=jnp.float32) / half)
    ang = pos.astype(jnp.float32)[:, None] * inv[None, :]
    cos = jnp.cos(ang)[:, None, :]
    sin = jnp.sin(ang)[:, None, :]
    x32 = x.astype(jnp.float32)
    x1, x2 = x32[..., :half], x32[..., half:]
    return jnp.concatenate([x1 * cos - x2 * sin, x2 * cos + x1 * sin], axis=-1).astype(x.dtype)


def _causal_dwconv(x, buf, w, b):
    k = w.shape[0]
    s = x.shape[1]
    xp = jnp.concatenate([buf.astype(x.dtype), x], axis=1)
    out = xp[:, 0:s] * w[0]
    for j in range(1, k):
        out = out + xp[:, j:j + s] * w[j]
    return out + b, xp[:, s:]


def _rglru(x, h0, w_gx, b_gx, w_ga, b_ga, lru_param):
    bsz, s, _ = x.shape
    x32 = x.astype(jnp.float32)
    xb = x32.reshape(bsz, s, RNN_BLOCKS, RNN_BLOCK)
    gx = jax.nn.sigmoid(jnp.einsum('bsnk,nkj->bsnj', xb, w_gx.astype(jnp.float32)).reshape(bsz, s, D_RNN) + b_gx.astype(jnp.float32))
    ga = jax.nn.sigmoid(jnp.einsum('bsnk,nkj->bsnj', xb, w_ga.astype(jnp.float32)).reshape(bsz, s, D_RNN) + b_ga.astype(jnp.float32))
    log_a = LRU_C * ga * jax.nn.log_sigmoid(lru_param.astype(jnp.float32))
    a = jnp.exp(log_a)
    bx = jnp.sqrt(-jnp.expm1(2.0 * log_a)) * gx * x32
    bx = bx.at[:, 0].add(a[:, 0] * h0.astype(jnp.float32))

    def comb(c1, c2):
        a1, b1 = c1
        a2, b2 = c2
        return a1 * a2, a2 * b1 + b2

    _, h = lax.associative_scan(comb, (a, bx), axis=1)
    return h.astype(x.dtype), h[:, -1].astype(x.dtype)


def _attend_sink(q, k, v, mask, sink):
    s = jnp.einsum('...tkgd,...lkd->...kgtl', q.astype(jnp.float32), k.astype(jnp.float32)) * ATT_SCALE
    s = jnp.where(mask, s, NEG)
    sk = sink.astype(jnp.float32)[..., None, None]
    m = jnp.maximum(jnp.max(s, axis=-1, keepdims=True), sk)
    p = jnp.exp(s - m)
    den = jnp.sum(p, axis=-1, keepdims=True) + jnp.exp(sk - m)
    o = jnp.einsum('...kgtl,...lkd->...tkgd', p / den, v.astype(jnp.float32))
    return o.astype(q.dtype)


def _swa_prompt(q, k, v, sink):
    bsz, s = q.shape[0], q.shape[1]
    nb = s // WINDOW
    qb = q.reshape(bsz, nb, WINDOW, N_KV, GROUP, HEAD_DIM)
    kb = k.reshape(bsz, nb, WINDOW, N_KV, HEAD_DIM)
    vb = v.reshape(bsz, nb, WINDOW, N_KV, HEAD_DIM)
    kk = jnp.concatenate([jnp.concatenate([jnp.zeros_like(kb[:, :1]), kb[:, :-1]], axis=1), kb], axis=2)
    vv = jnp.concatenate([jnp.concatenate([jnp.zeros_like(vb[:, :1]), vb[:, :-1]], axis=1), vb], axis=2)
    qi = jnp.arange(WINDOW)[:, None]
    kj = jnp.arange(2 * WINDOW)[None, :] - WINDOW
    band = (kj <= qi) & (kj >= qi - WINDOW)
    valid = (jnp.arange(nb)[:, None, None] > 0) | (kj >= 0)[None]
    mask = (band[None] & valid)[:, None, None]
    o = _attend_sink(qb, kk, vv, mask, sink.reshape(N_KV, GROUP))
    return o.reshape(bsz, s, Q_W)


def _swa_sample(q, kbuf, vbuf, knew, vnew, sink):
    dbsz, t = q.shape[0], q.shape[1]
    wb = kbuf.shape[1]
    k = jnp.concatenate([kbuf.astype(knew.dtype), knew], axis=1)
    v = jnp.concatenate([vbuf.astype(vnew.dtype), vnew], axis=1)
    qpos = PAST_LEN + jnp.arange(t, dtype=jnp.int32)
    kpos = jnp.concatenate([PAST_LEN - wb + jnp.arange(wb, dtype=jnp.int32), qpos])
    mask = (kpos[None, :] <= qpos[:, None]) & (kpos[None, :] >= qpos[:, None] - WINDOW)
    o = _attend_sink(q.reshape(dbsz, t, N_KV, GROUP, HEAD_DIM), k, v, mask, sink.reshape(N_KV, GROUP))
    return o.reshape(dbsz, t, Q_W)


def _mem_attn(q, k, v):
    s = jnp.einsum('bthd,bmhd->bhtm', q.astype(jnp.float32), k.astype(jnp.float32)) * ATT_SCALE
    p = jax.nn.softmax(s, axis=-1)
    o = jnp.einsum('bhtm,bmhd->bthd', p, v.astype(jnp.float32))
    return o.reshape(o.shape[0], o.shape[1], MEM_W).astype(q.dtype)


def _mem_kv(mem, g, w, kg):
    bsz = mem.shape[0]
    h = _rmsnorm(mem, g) @ w
    k = _rmsnorm(h[..., :MEM_W].reshape(bsz, N_MEM, MEM_HEADS, HEAD_DIM), kg)
    v = h[..., MEM_W:].reshape(bsz, N_MEM, MEM_HEADS, HEAD_DIM)
    return k, v


def _shared_kv(x, pos, g, w, kg):
    bsz, s = x.shape[0], x.shape[1]
    h = _rmsnorm(x, g) @ w
    k = _rope(_rmsnorm(h[..., :KV_W].reshape(bsz, s, N_KV, HEAD_DIM), kg), pos)
    v = h[..., KV_W:].reshape(bsz, s, N_KV, HEAD_DIM)
    return k, v


def _conv_ffn(x, buf, w_up, cw, cb, w_down):
    u = x @ w_up
    u, nbuf = _causal_dwconv(u, buf, cw, cb)
    return (jax.nn.gelu(u[..., :D_FF]) * u[..., D_FF:]) @ w_down, nbuf


def _trunk(x, pos, rnn_h0, rnn_conv0, ffn_conv0, mem_k, mem_v, kv_past, prm):
    bsz, s = x.shape[0], x.shape[1]
    hs, rcs, fcs = [], [], []
    k_sh = None
    v_sh = None
    for l in range(DEPTH):
        hn = _rmsnorm(x, prm['norm_mix_g'][l])
        if l < N_A:
            u = hn @ prm['w_in_a'][l]
            gate, xr, qm = u[..., :D_RNN], u[..., D_RNN:2 * D_RNN], u[..., 2 * D_RNN:]
            xr, rc = _causal_dwconv(xr, rnn_conv0[l], prm['rnn_conv_w'][l], prm['rnn_conv_b'][l])
            y, hl = _rglru(xr, rnn_h0[l], prm['w_gate_x'][l], prm['b_gate_x'][l], prm['w_gate_a'][l], prm['b_gate_a'][l], prm['lru_param'][l])
            main = y * jax.nn.gelu(gate)
            hs.append(hl)
            rcs.append(rc)
        else:
            j = l - N_A
            u = hn @ prm['w_in_b'][j]
            q, qm = u[..., :Q_W], u[..., Q_W:]
            q = _rope(_rmsnorm(q.reshape(bsz, s, N_Q, HEAD_DIM), prm['q_norm_g'][j]), pos)
            if kv_past is None:
                main = _swa_prompt(q, k_sh, v_sh, prm['sinks'][j])
            else:
                main = _swa_sample(q, kv_past[0], kv_past[1], k_sh, v_sh, prm['sinks'][j])
        qm = _rmsnorm(qm.reshape(bsz, s, MEM_HEADS, HEAD_DIM), prm['mem_q_norm_g'][l])
        mo = _mem_attn(qm, mem_k[l], mem_v[l])
        x = x + jnp.concatenate([main, mo], axis=-1) @ prm['w_out'][l]
        hn = _rmsnorm(x, prm['norm_ffn_g'][l])
        f, fc = _conv_ffn(hn, ffn_conv0[l], prm['w_ffn_up'][l], prm['ffn_conv_w'][l], prm['ffn_conv_b'][l], prm['w_ffn_down'][l])
        x = x + f
        fcs.append(fc)
        if l == N_A - 1:
            k_sh, v_sh = _shared_kv(x, pos, prm['kv_norm_g'], prm['w_kv'], prm['k_norm_g'])
    if kv_past is None:
        keep = min(WINDOW, s)
        k_state, v_state = k_sh[:, s - keep:], v_sh[:, s - keep:]
    else:
        k_state, v_state = k_sh, v_sh
    return x, jnp.stack(hs), jnp.stack(rcs), jnp.stack(fcs), k_state, v_state


def setup_inputs(seed: int = 0) -> dict:
    key = jax.random.key(seed)
    ks = jax.random.split(key, 36)
    f32 = jnp.float32

    def nrm(i, shape, scale):
        return jax.random.normal(ks[i], shape, f32) * scale

    def gain(i, shape):
        return 1.0 + nrm(i, shape, 0.1)

    w_buf = min(WINDOW, PAST_LEN)
    lru_a = jax.random.uniform(ks[19], (N_A, D_RNN), f32, 0.9, 0.999)
    out_scale = MIX_W ** -0.5 * (2 * DEPTH) ** -0.5
    return {
        'x_prompt': nrm(0, (BATCH, SEQ, D_MODEL), 1.0),
        'x_sample': nrm(1, (DEC_BATCH, DEC_SEQ, D_MODEL), 1.0),
        'state_rglru_h': nrm(2, (N_A, DEC_BATCH, D_RNN), 0.5),
        'state_rglru_conv': nrm(3, (N_A, DEC_BATCH, CONV_A - 1, D_RNN), 0.5),
        'state_ffn_conv': nrm(4, (DEPTH, DEC_BATCH, CONV_F - 1, 2 * D_FF), 0.5),
        'cache_swa_k': nrm(5, (DEC_BATCH, w_buf, N_KV, HEAD_DIM), 1.0),
        'cache_swa_v': nrm(6, (DEC_BATCH, w_buf, N_KV, HEAD_DIM), 1.0),
        'cache_mem_k': nrm(7, (DEPTH, DEC_BATCH, N_MEM, MEM_HEADS, HEAD_DIM), 1.0),
        'cache_mem_v': nrm(8, (DEPTH, DEC_BATCH, N_MEM, MEM_HEADS, HEAD_DIM), 1.0),
        'mem_prompt': nrm(9, (BATCH, N_MEM, D_MODEL), 1.0),
        'norm_mix_g': gain(10, (DEPTH, D_MODEL)),
        'norm_ffn_g': gain(11, (DEPTH, D_MODEL)),
        'w_in_a': nrm(12, (N_A, D_MODEL, 2 * D_RNN + MEM_W), D_MODEL ** -0.5),
        'rnn_conv_w': nrm(13, (N_A, CONV_A, D_RNN), CONV_A ** -0.5),
        'rnn_conv_b': nrm(14, (N_A, D_RNN), 0.02),
        'w_gate_x': nrm(15, (N_A, RNN_BLOCKS, RNN_BLOCK, RNN_BLOCK), RNN_BLOCK ** -0.5),
        'b_gate_x': nrm(16, (N_A, D_RNN), 0.02),
        'w_gate_a': nrm(17, (N_A, RNN_BLOCKS, RNN_BLOCK, RNN_BLOCK), RNN_BLOCK ** -0.5),
        'b_gate_a': nrm(18, (N_A, D_RNN), 0.02),
        'lru_param': jnp.log(lru_a) - jnp.log1p(-lru_a),
        'w_in_b': nrm(20, (N_B, D_MODEL, Q_W + MEM_W), D_MODEL ** -0.5),
        'q_norm_g': gain(21, (N_B, HEAD_DIM)),
        'sinks': nrm(22, (N_B, N_Q), 0.5),
        'kv_norm_g': gain(23, (D_MODEL,)),
        'w_kv': nrm(24, (D_MODEL, 2 * KV_W), D_MODEL ** -0.5),
        'k_norm_g': gain(25, (HEAD_DIM,)),
        'mem_norm_g': gain(26, (DEPTH, D_MODEL)),
        'w_mem_kv': nrm(27, (DEPTH, D_MODEL, 2 * MEM_W), D_MODEL ** -0.5),
        'mem_q_norm_g': gain(28, (DEPTH, HEAD_DIM)),
        'mem_k_norm_g': gain(29, (DEPTH, HEAD_DIM)),
        'w_out': nrm(30, (DEPTH, MIX_W, D_MODEL), out_scale),
        'w_ffn_up': nrm(31, (DEPTH, D_MODEL, 2 * D_FF), D_MODEL ** -0.5),
        'ffn_conv_w': nrm(32, (DEPTH, CONV_F, 2 * D_FF), CONV_F ** -0.5),
        'ffn_conv_b': nrm(33, (DEPTH, 2 * D_FF), 0.02),
        'w_ffn_down': nrm(34, (DEPTH, D_FF, D_MODEL), D_FF ** -0.5 * (2 * DEPTH) ** -0.5),
    }


def reference(x_prompt, x_sample, state_rglru_h, state_rglru_conv, state_ffn_conv, cache_swa_k, cache_swa_v, cache_mem_k, cache_mem_v, mem_prompt, norm_mix_g, norm_ffn_g, w_in_a, rnn_conv_w, rnn_conv_b, w_gate_x, b_gate_x, w_gate_a, b_gate_a, lru_param, w_in_b, q_norm_g, sinks, kv_norm_g, w_kv, k_norm_g, mem_norm_g, w_mem_kv, mem_q_norm_g, mem_k_norm_g, w_out, w_ffn_up, ffn_conv_w, ffn_conv_b, w_ffn_down):
    prm = {
        'norm_mix_g': norm_mix_g, 'norm_ffn_g': norm_ffn_g, 'w_in_a': w_in_a,
        'rnn_conv_w': rnn_conv_w, 'rnn_conv_b': rnn_conv_b, 'w_gate_x': w_gate_x,
        'b_gate_x': b_gate_x, 'w_gate_a': w_gate_a, 'b_gate_a': b_gate_a,
        'lru_param': lru_param, 'w_in_b': w_in_b, 'q_norm_g': q_norm_g, 'sinks': sinks,
        'kv_norm_g': kv_norm_g, 'w_kv': w_kv, 'k_norm_g': k_norm_g,
        'mem_q_norm_g': mem_q_norm_g, 'w_out': w_out, 'w_ffn_up': w_ffn_up,
        'ffn_conv_w': ffn_conv_w, 'ffn_conv_b': ffn_conv_b, 'w_ffn_down': w_ffn_down,
    }
    mks, mvs = [], []
    for l in range(DEPTH):
        mk, mv = _mem_kv(mem_prompt, mem_norm_g[l], w_mem_kv[l], mem_k_norm_g[l])
        mks.append(mk)
        mvs.append(mv)
    p_mem_k = jnp.stack(mks)
    p_mem_v = jnp.stack(mvs)
    bsz, s = x_prompt.shape[0], x_prompt.shape[1]
    dt = x_prompt.dtype
    h0 = jnp.zeros((N_A, bsz, D_RNN), dt)
    rc0 = jnp.zeros((N_A, bsz, CONV_A - 1, D_RNN), dt)
    fc0 = jnp.zeros((DEPTH, bsz, CONV_F - 1, 2 * D_FF), dt)
    pos_p = jnp.arange(s, dtype=jnp.int32)
    y_prompt, p_h, p_rc, p_fc, p_k, p_v = _trunk(x_prompt, pos_p, h0, rc0, fc0, p_mem_k, p_mem_v, None, prm)
    pos_s = PAST_LEN + jnp.arange(x_sample.shape[1], dtype=jnp.int32)
    y_sample, s_h, s_rc, s_fc, s_k, s_v = _trunk(x_sample, pos_s, state_rglru_h, state_rglru_conv, state_ffn_conv, cache_mem_k, cache_mem_v, (cache_swa_k, cache_swa_v), prm)
    return (y_prompt, y_sample, p_h, p_rc, p_fc, p_k, p_v, p_mem_k, p_mem_v, s_h, s_rc, s_fc, s_k, s_v)
```

```python
import functools
import math

import jax
import jax.numpy as jnp
from jax import lax
from jax.experimental import pallas as pl
from jax.experimental.pallas import tpu as pltpu

F32 = jnp.float32
BF16 = jnp.bfloat16

D_MODEL = 1024
HEAD_DIM = 64
MEM_HEADS = 4
MEM_W = MEM_HEADS * HEAD_DIM
N_MEM = 256
D_RNN = D_MODEL - MEM_W
RNN_BLOCKS = D_RNN // HEAD_DIM
CONV_A = 4
LRU_C = 8.0
N_Q = D_RNN // HEAD_DIM
N_KV = 4
GROUP = N_Q // N_KV
Q_W = N_Q * HEAD_DIM
KV_W = N_KV * HEAD_DIM
WINDOW = 128
ROPE_THETA = 10000.0
D_FF = 3 * D_MODEL
CONV_F = 3
EPS = 1e-6
NEG = -1e30
ATT_SCALE = HEAD_DIM ** -0.5
PAST_LEN = 8192

SUBLANES = 8
LANES = 128
MXU_DIM = 256
VMEM_LIMIT_BYTES = 56 * 1024 * 1024

TS_MIX = 256
TS_FFN = 256
TS_KV = 512
FF_CHUNK = 512
N_FF_CHUNKS = D_FF // FF_CHUNK
DEC_SEQ_BLOCK = 8


def _mm(a, b):
    return jnp.dot(a.astype(BF16), b, preferred_element_type=F32)


def _mm_nt(a, b):
    return lax.dot_general(a.astype(BF16), b, (((1,), (1,)), ((), ())),
                           preferred_element_type=F32)


def _mm_split(a, b):
    hi = a.astype(BF16)
    lo = (a - hi.astype(F32)).astype(BF16)
    return (jnp.dot(hi, b, preferred_element_type=F32)
            + jnp.dot(lo, b, preferred_element_type=F32))


def _rmsnorm(x, g):
    ms = jnp.mean(x * x, axis=-1, keepdims=True)
    return x * lax.rsqrt(ms + EPS) * g


def _head_rmsnorm(x, bd, g):
    parts = []
    for c in range(x.shape[1] // MXU_DIM):
        xc = x[:, c * MXU_DIM:(c + 1) * MXU_DIM]
        ms = _mm_split(xc * xc, bd)
        parts.append(xc * lax.rsqrt(ms + EPS))
    y = parts[0] if len(parts) == 1 else jnp.concatenate(parts, axis=1)
    return y * g


def _tile_lanes(t, width):
    reps = width // t.shape[1]
    return t if reps == 1 else jnp.concatenate([t] * reps, axis=1)


def _rope(x, cos_t, sin_t):
    w = x.shape[1]
    lane = lax.broadcasted_iota(jnp.int32, x.shape, 1)
    first = (lane % HEAD_DIM) < (HEAD_DIM // 2)
    swapped = jnp.where(first, pltpu.roll(x, w - HEAD_DIM // 2, 1),
                        pltpu.roll(x, HEAD_DIM // 2, 1))
    return x * _tile_lanes(cos_t, w) + swapped * _tile_lanes(sin_t, w)


def _gelu(x):
    c = math.sqrt(2.0 / math.pi)
    return x * (0.5 * (1.0 + jnp.tanh(c * (x + 0.044715 * (x * x * x)))))


def _log_sigmoid(x):
    return jnp.minimum(x, 0.0) - jnp.log1p(jnp.exp(-jnp.abs(x)))


def _shift_rows(x, pad8, s):
    return pltpu.roll(jnp.concatenate([pad8, x], axis=0), s, 0)[SUBLANES:]


def _causal_conv(x, tail8, w_ref, b):
    k = w_ref.shape[0]
    acc = _shift_rows(x, tail8, k - 1) * w_ref[0:1, :]
    for j in range(1, k - 1):
        acc = acc + _shift_rows(x, tail8, k - 1 - j) * w_ref[j:j + 1, :]
    acc = acc + x * w_ref[k - 1:k, :]
    return acc + b


def _lru_coeffs(xr, wg_ref, bgx, bga, logsig):
    xb = xr.astype(BF16)
    gxs, gas = [], []
    for c in range(D_RNN // MXU_DIM):
        gg = jnp.dot(xb[:, c * MXU_DIM:(c + 1) * MXU_DIM], wg_ref[c],
                     preferred_element_type=F32)
        gxs.append(gg[:, :MXU_DIM])
        gas.append(gg[:, MXU_DIM:])
    gx = jax.nn.sigmoid(jnp.concatenate(gxs, axis=1) + bgx)
    ga = jax.nn.sigmoid(jnp.concatenate(gas, axis=1) + bga)
    log_a = LRU_C * ga * logsig
    a = jnp.exp(log_a)
    mult = jnp.sqrt(-jnp.tanh(log_a) * (a * a + 1.0))
    return a, mult * gx * xr


def _lru_scan(a, b, hc):
    ts = a.shape[0]
    ones8 = jnp.ones((SUBLANES, a.shape[1]), F32)
    zeros8 = jnp.zeros((SUBLANES, a.shape[1]), F32)
    s = 1
    while s < SUBLANES:
        a_sh = _shift_rows(a, ones8, s)
        b_sh = _shift_rows(b, zeros8, s)
        b = a * b_sh + b
        a = a * a_sh
        s *= 2
    h = hc
    outs = []
    for q in range(ts // SUBLANES):
        h = a[q * SUBLANES:(q + 1) * SUBLANES] * h + b[q * SUBLANES:(q + 1) * SUBLANES]
        outs.append(h)
    return jnp.concatenate(outs, axis=0)


def _head_mask(shape, h):
    lane = lax.broadcasted_iota(jnp.int32, shape, 1)
    return (lane >= h * HEAD_DIM) & (lane < (h + 1) * HEAD_DIM)


def _head_masked_stack(blocks, n_heads):
    parts = []
    for h in range(n_heads):
        for blk in blocks:
            parts.append(jnp.where(_head_mask(blk.shape, h), blk, 0.0).astype(BF16))
    return jnp.concatenate(parts, axis=0)


def _mem_attention(qn, kcat, vcat):
    s = _mm_nt(qn, kcat)
    parts = []
    for h in range(MEM_HEADS):
        sh = s[:, h * N_MEM:(h + 1) * N_MEM]
        m = jnp.max(sh, axis=-1, keepdims=True)
        p = jnp.exp(sh - m)
        den = jnp.sum(p, axis=-1, keepdims=True)
        parts.append((p * (1.0 / den)).astype(BF16))
    return jnp.dot(jnp.concatenate(parts, axis=1), vcat, preferred_element_type=F32)


def _mem_kv_kernel(mem_ref, g_ref, w_ref, kg_ref, bd_ref, k_ref, v_ref):
    h = _mm(_rmsnorm(mem_ref[...], g_ref[0]), w_ref[0])
    k_ref[0] = _head_rmsnorm(h[:, :MEM_W], bd_ref[...], kg_ref[0])
    v_ref[0] = h[:, MEM_W:]


def _shared_kv_kernel(x_ref, g_ref, w_ref, kg_ref, bd_ref, cos_ref, sin_ref,
                      k_ref, v_ref):
    h = _mm(_rmsnorm(x_ref[...], g_ref[...]), w_ref[...])
    k = _head_rmsnorm(h[:, :KV_W], bd_ref[...], kg_ref[...])
    k_ref[...] = _rope(k, cos_ref[...], sin_ref[...])
    v_ref[...] = h[:, KV_W:]


def _mixer_a_kernel(x_ref, g_ref, win_ref, cw_ref, cb_ref, wg_ref, bgx_ref, bga_ref,
                    lru_ref, mqg_ref, bd_ref, mk_ref, mv_ref, wout_ref, rc0_ref, h0_ref,
                    xo_ref, hlast_ref, rctail_ref,
                    tail_s, hc_s, kcat_s, vcat_s):
    ts = x_ref.shape[0]

    @pl.when(pl.program_id(1) == 0)
    def _():
        tail_s[...] = rc0_ref[0]
        hc_s[...] = jnp.broadcast_to(h0_ref[0], hc_s.shape)
        kcat_s[...] = _head_masked_stack([mk_ref[0]], MEM_HEADS)
        vcat_s[...] = _head_masked_stack([mv_ref[0]], MEM_HEADS)

    x = x_ref[...]
    u = _mm(_rmsnorm(x, g_ref[...]), win_ref[...])
    gate = u[:, :D_RNN]
    xr_pre = u[:, D_RNN:2 * D_RNN]
    qm = u[:, 2 * D_RNN:]

    xr = _causal_conv(xr_pre, tail_s[...], cw_ref, cb_ref[...])
    tail_s[...] = xr_pre[ts - SUBLANES:]
    rctail_ref[0] = xr_pre[ts - SUBLANES:]

    a, b = _lru_coeffs(xr, wg_ref, bgx_ref[...], bga_ref[...], _log_sigmoid(lru_ref[...]))
    h = _lru_scan(a, b, hc_s[...])
    hc_s[...] = jnp.broadcast_to(h[ts - 1:ts], hc_s.shape)
    hlast_ref[0] = h[ts - 1:ts]
    main = h * _gelu(gate)

    qn = _head_rmsnorm(qm, bd_ref[...], mqg_ref[...]) * ATT_SCALE
    mo = _mem_attention(qn, kcat_s[...], vcat_s[...])

    y = _mm(jnp.concatenate([main, mo], axis=1), wout_ref[...])
    xo_ref[...] = x + y


def _mixer_b_kernel(sink_ref, x_ref, g_ref, win_ref, qg_ref, cos_ref, sin_ref,
                    kcur_ref, kprev_ref, vcur_ref, vprev_ref,
                    mqg_ref, bd_ref, mk_ref, mv_ref, wout_ref,
                    xo_ref, kcat_s, vcat_s):
    ts = x_ref.shape[0]
    i = pl.program_id(1)

    @pl.when(i == 0)
    def _():
        kcat_s[...] = _head_masked_stack([mk_ref[0]], MEM_HEADS)
        vcat_s[...] = _head_masked_stack([mv_ref[0]], MEM_HEADS)

    x = x_ref[...]
    u = _mm(_rmsnorm(x, g_ref[...]), win_ref[...])
    q = _head_rmsnorm(u[:, :Q_W], bd_ref[...], qg_ref[...])
    q = (_rope(q, cos_ref[...], sin_ref[...]) * ATT_SCALE).astype(BF16)
    qm = u[:, Q_W:]

    row = lax.broadcasted_iota(jnp.int32, (WINDOW, 2 * WINDOW), 0)
    kj = lax.broadcasted_iota(jnp.int32, (WINDOW, 2 * WINDOW), 1) - WINDOW
    mask_inner = (kj <= row) & (kj >= row - WINDOW)
    mask_first = (kj <= row) & (kj >= jnp.maximum(row - WINDOW, jnp.where(i > 0, -WINDOW, 0)))

    kt = kcur_ref[...]
    vt = vcur_ref[...]
    mains = []
    for jb in range(ts // WINDOW):
        lo, hi = jb * WINDOW, (jb + 1) * WINDOW
        if jb == 0:
            kp, vp = kprev_ref[...], vprev_ref[...]
            mask = mask_first
        else:
            kp, vp = kt[lo - WINDOW:lo], vt[lo - WINDOW:lo]
            mask = mask_inner
        kcat = _head_masked_stack([kp, kt[lo:hi]], N_KV)
        vcat = _head_masked_stack([vp, vt[lo:hi]], N_KV)
        qs = jnp.concatenate([q[lo:hi, g * KV_W:(g + 1) * KV_W] for g in range(GROUP)], axis=0)
        s = lax.dot_general(qs, kcat, (((1,), (1,)), ((), ())), preferred_element_type=F32)
        prow = []
        for g in range(GROUP):
            pseg = []
            for kv in range(N_KV):
                seg = s[g * WINDOW:(g + 1) * WINDOW, kv * 2 * WINDOW:(kv + 1) * 2 * WINDOW]
                seg = jnp.where(mask, seg, NEG)
                sink = sink_ref[g * N_KV + kv]
                m = jnp.maximum(jnp.max(seg, axis=-1, keepdims=True), sink)
                p = jnp.exp(seg - m)
                den = jnp.sum(p, axis=-1, keepdims=True) + jnp.exp(sink - m)
                pseg.append((p * (1.0 / den)).astype(BF16))
            prow.append(jnp.concatenate(pseg, axis=1))
        o = jnp.dot(jnp.concatenate(prow, axis=0), vcat, preferred_element_type=F32)
        mains.append(jnp.concatenate([o[g * WINDOW:(g + 1) * WINDOW] for g in range(GROUP)], axis=1))
    main = mains[0] if len(mains) == 1 else jnp.concatenate(mains, axis=0)

    qn = _head_rmsnorm(qm, bd_ref[...], mqg_ref[...]) * ATT_SCALE
    mo = _mem_attention(qn, kcat_s[...], vcat_s[...])

    y = _mm(jnp.concatenate([main, mo], axis=1), wout_ref[...])
    xo_ref[...] = x + y


def _ffn_kernel(x_ref, g_ref, wup_ref, cw_ref, cb_ref, wdn_ref,
                xo_ref, utail_ref, tail_s):
    ts = x_ref.shape[0]

    @pl.when(pl.program_id(1) == 0)
    def _():
        tail_s[...] = jnp.zeros_like(tail_s)

    x = x_ref[...]
    hn = _rmsnorm(x, g_ref[...]).astype(BF16)
    acc = jnp.zeros((ts, D_MODEL), F32)
    for j in range(N_FF_CHUNKS):
        halves = []
        for c in (j, N_FF_CHUNKS + j):
            u = jnp.dot(hn, wup_ref[c], preferred_element_type=F32)
            halves.append(_causal_conv(u, tail_s[c], cw_ref.at[c], cb_ref[c]))
            tail_s[c] = u[ts - SUBLANES:]
            utail_ref[0, :, c * FF_CHUNK:(c + 1) * FF_CHUNK] = u[ts - SUBLANES:]
        act = (_gelu(halves[0]) * halves[1]).astype(BF16)
        acc = acc + jnp.dot(act, wdn_ref[j], preferred_element_type=F32)
    xo_ref[...] = x + acc


def _dec_in_a_kernel(x_ref, g_ref, win_ref, cw_ref, cb_ref, wg_ref, bgx_ref, bga_ref,
                     lru_ref, mqg_ref, bd_ref, b0_ref, b1_ref, b2_ref, h0_ref,
                     main_ref, qn_ref, hnew_ref, xrpre_ref):
    u = _mm(_rmsnorm(x_ref[...], g_ref[...]), win_ref[...])
    gate = u[:, :D_RNN]
    xr_pre = u[:, D_RNN:2 * D_RNN]
    qm = u[:, 2 * D_RNN:]
    xr = b0_ref[...] * cw_ref[0:1, :]
    xr = xr + b1_ref[...] * cw_ref[1:2, :]
    xr = xr + b2_ref[...] * cw_ref[2:3, :]
    xr = xr + xr_pre * cw_ref[3:4, :]
    xr = xr + cb_ref[...]
    a, b = _lru_coeffs(xr, wg_ref, bgx_ref[...], bga_ref[...], _log_sigmoid(lru_ref[...]))
    h = a * h0_ref[...] + b
    main_ref[...] = h * _gelu(gate)
    qn_ref[...] = _head_rmsnorm(qm, bd_ref[...], mqg_ref[...]) * ATT_SCALE
    hnew_ref[...] = h
    xrpre_ref[...] = xr_pre


def _dec_in_b_kernel(x_ref, g_ref, win_ref, qg_ref, cos_ref, sin_ref, mqg_ref, bd_ref,
                     q_ref, qn_ref):
    u = _mm(_rmsnorm(x_ref[...], g_ref[...]), win_ref[...])
    q = _head_rmsnorm(u[:, :Q_W], bd_ref[...], qg_ref[...])
    q_ref[...] = _rope(q, cos_ref[...], sin_ref[...]) * ATT_SCALE
    qn_ref[...] = _head_rmsnorm(u[:, Q_W:], bd_ref[...], mqg_ref[...]) * ATT_SCALE


def _seq_head_scores(q_row, k, e):
    return _mm_split(k * q_row, e)


def _dec_mem_attn_kernel(q_ref, k_ref, v_ref, e_ref, et_ref, o_ref):
    e, et = e_ref[...], et_ref[...]

    def body(s, carry):
        sc = _seq_head_scores(q_ref[pl.ds(s, 1), :], k_ref[s], e)
        m = jnp.max(sc, axis=0, keepdims=True)
        p = jnp.exp(sc - m)
        den = jnp.sum(p, axis=0, keepdims=True)
        pe = _mm_split(p * (1.0 / den), et)
        o_ref[pl.ds(s, 1), :] = jnp.sum(pe * v_ref[s], axis=0, keepdims=True)
        return carry

    lax.fori_loop(0, q_ref.shape[0], body, 0)


def _dec_swa_kernel(q_ref, kb_ref, vb_ref, kn_ref, vn_ref, sink_ref, e_ref, et_ref, o_ref):
    e, et = e_ref[...], et_ref[...]

    def body(s, carry):
        kb, vb = kb_ref[s], vb_ref[s]
        kn, vn = kn_ref[pl.ds(s, 1), :], vn_ref[pl.ds(s, 1), :]
        for g in range(GROUP):
            qg = q_ref[pl.ds(s, 1), g * KV_W:(g + 1) * KV_W]
            sink = sink_ref[g:g + 1, :]
            s_new = _seq_head_scores(qg, kn, e)
            s_buf = _seq_head_scores(qg, kb, e)
            m = jnp.maximum(jnp.maximum(s_new, sink), jnp.max(s_buf, axis=0, keepdims=True))
            p_new = jnp.exp(s_new - m)
            p_buf = jnp.exp(s_buf - m)
            den = jnp.sum(p_buf, axis=0, keepdims=True) + p_new + jnp.exp(sink - m)
            r = 1.0 / den
            o = jnp.sum(_mm_split(p_buf * r, et) * vb, axis=0, keepdims=True)
            o = o + _mm_split(p_new * r, et) * vn
            o_ref[pl.ds(s, 1), g * KV_W:(g + 1) * KV_W] = o
        return carry

    lax.fori_loop(0, q_ref.shape[0], body, 0)


def _dec_out_ffn_kernel(x_ref, main_ref, mo_ref, wout_ref, g_ref,
                        wug_ref, wuv_ref, cwg_ref, cwv_ref, cbg_ref, cbv_ref, wdn_ref,
                        sg0_ref, sg1_ref, sv0_ref, sv1_ref,
                        xo_ref, ug_ref, uv_ref,
                        xmid_s, hn_s, acc_s):
    j = pl.program_id(0)

    @pl.when(j == 0)
    def _():
        y = _mm(jnp.concatenate([main_ref[...], mo_ref[...]], axis=1), wout_ref[...])
        xmid = x_ref[...] + y
        xmid_s[...] = xmid
        hn_s[...] = _rmsnorm(xmid, g_ref[...]).astype(BF16)
        acc_s[...] = jnp.zeros_like(acc_s)

    hn = hn_s[...]
    ug = jnp.dot(hn, wug_ref[0], preferred_element_type=F32)
    uv = jnp.dot(hn, wuv_ref[0], preferred_element_type=F32)
    ug_ref[...] = ug
    uv_ref[...] = uv
    cg = (sg0_ref[...] * cwg_ref[0, 0:1, :] + sg1_ref[...] * cwg_ref[0, 1:2, :]
          + ug * cwg_ref[0, 2:3, :] + cbg_ref[0])
    cv = (sv0_ref[...] * cwv_ref[0, 0:1, :] + sv1_ref[...] * cwv_ref[0, 1:2, :]
          + uv * cwv_ref[0, 2:3, :] + cbv_ref[0])
    act = (_gelu(cg) * cv).astype(BF16)
    acc_s[...] += jnp.dot(act, wdn_ref[0], preferred_element_type=F32)

    @pl.when(j == pl.num_programs(0) - 1)
    def _():
        xo_ref[...] = xmid_s[...] + acc_s[...]


def _const_spec(shape):
    nd = len(shape)
    return pl.BlockSpec(shape, lambda *_: (0,) * nd)


def _params(*sem):
    return pltpu.CompilerParams(dimension_semantics=sem, vmem_limit_bytes=VMEM_LIMIT_BYTES)


def _mem_kv(mem2d, g, w, kg, bd):
    depth = w.shape[0]
    rows = mem2d.shape[0]
    out = jax.ShapeDtypeStruct((depth, rows, MEM_W), F32)
    return pl.pallas_call(
        _mem_kv_kernel,
        grid=(depth,),
        in_specs=[
            _const_spec(mem2d.shape),
            pl.BlockSpec((1, 1, D_MODEL), lambda l: (l, 0, 0)),
            pl.BlockSpec((1, D_MODEL, 2 * MEM_W), lambda l: (l, 0, 0)),
            pl.BlockSpec((1, 1, MEM_W), lambda l: (l, 0, 0)),
            _const_spec(bd.shape),
        ],
        out_specs=[pl.BlockSpec((1, rows, MEM_W), lambda l: (l, 0, 0))] * 2,
        out_shape=[out, out],
        compiler_params=_params("arbitrary"),
        name="mem_kv",
    )(mem2d, g, w, kg, bd)


def _shared_kv(x2d, g, w, kg, bd, cos_t, sin_t, ts):
    rows = x2d.shape[0]
    tab_blocks = cos_t.shape[0] // ts
    out = jax.ShapeDtypeStruct((rows, KV_W), F32)
    return pl.pallas_call(
        _shared_kv_kernel,
        grid=(rows // ts,),
        in_specs=[
            pl.BlockSpec((ts, D_MODEL), lambda i: (i, 0)),
            _const_spec(g.shape), _const_spec(w.shape), _const_spec(kg.shape),
            _const_spec(bd.shape),
            pl.BlockSpec((ts, LANES), lambda i: (i % tab_blocks, 0)),
            pl.BlockSpec((ts, LANES), lambda i: (i % tab_blocks, 0)),
        ],
        out_specs=[pl.BlockSpec((ts, KV_W), lambda i: (i, 0))] * 2,
        out_shape=[out, out],
        compiler_params=_params("arbitrary"),
        name="shared_kv",
    )(x2d, g, w, kg, bd, cos_t, sin_t)


def _mixer_a(x2d, bsz, g, win, cw, cb, wg, bgx, bga, lru, mqg, bd, mk, mv, wout, rc0, h0):
    rows = x2d.shape[0]
    ts = TS_MIX
    nt = rows // bsz // ts
    consts = (g, win, cw, cb, wg, bgx, bga, lru, mqg, bd)
    per_b3 = lambda b, i: (b, 0, 0)
    return pl.pallas_call(
        _mixer_a_kernel,
        grid=(bsz, nt),
        in_specs=[pl.BlockSpec((ts, D_MODEL), lambda b, i: (b * nt + i, 0))]
        + [_const_spec(c.shape) for c in consts]
        + [pl.BlockSpec((1, N_MEM, MEM_W), per_b3)] * 2
        + [_const_spec(wout.shape),
           pl.BlockSpec((1, SUBLANES, D_RNN), per_b3),
           pl.BlockSpec((1, 1, D_RNN), per_b3)],
        out_specs=[
            pl.BlockSpec((ts, D_MODEL), lambda b, i: (b * nt + i, 0)),
            pl.BlockSpec((1, 1, D_RNN), per_b3),
            pl.BlockSpec((1, SUBLANES, D_RNN), per_b3),
        ],
        out_shape=[
            jax.ShapeDtypeStruct((rows, D_MODEL), F32),
            jax.ShapeDtypeStruct((bsz, 1, D_RNN), F32),
            jax.ShapeDtypeStruct((bsz, SUBLANES, D_RNN), F32),
        ],
        scratch_shapes=[
            pltpu.VMEM((SUBLANES, D_RNN), F32),
            pltpu.VMEM((SUBLANES, D_RNN), F32),
            pltpu.VMEM((MEM_HEADS * N_MEM, MEM_W), BF16),
            pltpu.VMEM((MEM_HEADS * N_MEM, MEM_W), BF16),
        ],
        compiler_params=_params("arbitrary", "arbitrary"),
        name="mixer_a",
    )(x2d, *consts, mk, mv, wout, rc0, h0)


def _mixer_b(x2d, bsz, sink_tab, g, win, qg, cos_t, sin_t, ksh, vsh, mqg, bd, mk, mv, wout):
    rows = x2d.shape[0]
    ts = TS_MIX
    nt = rows // bsz // ts
    wpt = ts // WINDOW
    cur = lambda b, i: (b * nt + i, 0)
    prev = lambda b, i: (jnp.maximum((b * nt + i) * wpt - 1, 0), 0)
    per_b3 = lambda b, i: (b, 0, 0)
    return pl.pallas_call(
        _mixer_b_kernel,
        grid=(bsz, nt),
        in_specs=[
            pl.BlockSpec(memory_space=pltpu.SMEM),
            pl.BlockSpec((ts, D_MODEL), cur),
            _const_spec(g.shape), _const_spec(win.shape), _const_spec(qg.shape),
            pl.BlockSpec((ts, LANES), lambda b, i: (i, 0)),
            pl.BlockSpec((ts, LANES), lambda b, i: (i, 0)),
            pl.BlockSpec((ts, KV_W), cur), pl.BlockSpec((WINDOW, KV_W), prev),
            pl.BlockSpec((ts, KV_W), cur), pl.BlockSpec((WINDOW, KV_W), prev),
            _const_spec(mqg.shape), _const_spec(bd.shape),
            pl.BlockSpec((1, N_MEM, MEM_W), per_b3), pl.BlockSpec((1, N_MEM, MEM_W), per_b3),
            _const_spec(wout.shape),
        ],
        out_specs=pl.BlockSpec((ts, D_MODEL), cur),
        out_shape=jax.ShapeDtypeStruct((rows, D_MODEL), F32),
        scratch_shapes=[
            pltpu.VMEM((MEM_HEADS * N_MEM, MEM_W), BF16),
            pltpu.VMEM((MEM_HEADS * N_MEM, MEM_W), BF16),
        ],
        compiler_params=_params("arbitrary", "arbitrary"),
        name="mixer_b",
    )(sink_tab, x2d, g, win, qg, cos_t, sin_t, ksh, ksh, vsh, vsh, mqg, bd, mk, mv, wout)


def _ffn(x2d, bsz, g, wup, cw, cb, wdn):
    rows = x2d.shape[0]
    ts = TS_FFN
    nt = rows // bsz // ts
    return pl.pallas_call(
        _ffn_kernel,
        grid=(bsz, nt),
        in_specs=[pl.BlockSpec((ts, D_MODEL), lambda b, i: (b * nt + i, 0)),
                  _const_spec(g.shape), _const_spec(wup.shape), _const_spec(cw.shape),
                  _const_spec(cb.shape), _const_spec(wdn.shape)],
        out_specs=[
            pl.BlockSpec((ts, D_MODEL), lambda b, i: (b * nt + i, 0)),
            pl.BlockSpec((1, SUBLANES, 2 * D_FF), lambda b, i: (b, 0, 0)),
        ],
        out_shape=[
            jax.ShapeDtypeStruct((rows, D_MODEL), F32),
            jax.ShapeDtypeStruct((bsz, SUBLANES, 2 * D_FF), F32),
        ],
        scratch_shapes=[pltpu.VMEM((2 * N_FF_CHUNKS, SUBLANES, FF_CHUNK), F32)],
        compiler_params=_params("arbitrary", "arbitrary"),
        name="ffn",
    )(x2d, g, wup, cw, cb, wdn)


def _dec_in_a(x, g, win, cw, cb, wg, bgx, bga, lru, mqg, bd, rc, h0):
    n = x.shape[0]
    consts = (x, g, win, cw, cb, wg, bgx, bga, lru, mqg, bd)
    buf = lambda j: pl.BlockSpec((None, n, D_RNN), lambda i: (j, 0, 0))
    return pl.pallas_call(
        _dec_in_a_kernel,
        grid=(1,),
        in_specs=[_const_spec(c.shape) for c in consts]
        + [buf(0), buf(1), buf(2), _const_spec(h0.shape)],
        out_specs=[_const_spec((n, D_RNN)), _const_spec((n, MEM_W)),
                   _const_spec((n, D_RNN)), _const_spec((n, D_RNN))],
        out_shape=[jax.ShapeDtypeStruct((n, D_RNN), F32), jax.ShapeDtypeStruct((n, MEM_W), F32),
                   jax.ShapeDtypeStruct((n, D_RNN), F32), jax.ShapeDtypeStruct((n, D_RNN), F32)],
        compiler_params=_params("arbitrary"),
        name="dec_in_a",
    )(*consts, rc, rc, rc, h0)


def _dec_in_b(x, g, win, qg, cos_t, sin_t, mqg, bd):
    n = x.shape[0]
    args = (x, g, win, qg, cos_t, sin_t, mqg, bd)
    return pl.pallas_call(
        _dec_in_b_kernel,
        grid=(1,),
        in_specs=[_const_spec(a.shape) for a in args],
        out_specs=[_const_spec((n, Q_W)), _const_spec((n, MEM_W))],
        out_shape=[jax.ShapeDtypeStruct((n, Q_W), F32), jax.ShapeDtypeStruct((n, MEM_W), F32)],
        compiler_params=_params("arbitrary"),
        name="dec_in_b",
    )(*args)


def _dec_mem_attn(qn, ck, cv, e, et):
    n = qn.shape[0]
    sb = DEC_SEQ_BLOCK
    return pl.pallas_call(
        _dec_mem_attn_kernel,
        grid=(n // sb,),
        in_specs=[pl.BlockSpec((sb, MEM_W), lambda i: (i, 0)),
                  pl.BlockSpec((sb, N_MEM, MEM_W), lambda i: (i, 0, 0)),
                  pl.BlockSpec((sb, N_MEM, MEM_W), lambda i: (i, 0, 0)),
                  _const_spec(e.shape), _const_spec(et.shape)],
        out_specs=pl.BlockSpec((sb, MEM_W), lambda i: (i, 0)),
        out_shape=jax.ShapeDtypeStruct((n, MEM_W), F32),
        compiler_params=_params("arbitrary"),
        name="dec_mem_attn",
    )(qn, ck, cv, e, et)


def _dec_swa(q, kb, vb, kn, vn, sink_rows, e, et):
    n = q.shape[0]
    wb = kb.shape[1]
    sb = DEC_SEQ_BLOCK
    assert wb <= WINDOW
    return pl.pallas_call(
        _dec_swa_kernel,
        grid=(n // sb,),
        in_specs=[pl.BlockSpec((sb, Q_W), lambda i: (i, 0)),
                  pl.BlockSpec((sb, wb, KV_W), lambda i: (i, 0, 0)),
                  pl.BlockSpec((sb, wb, KV_W), lambda i: (i, 0, 0)),
                  pl.BlockSpec((sb, KV_W), lambda i: (i, 0)),
                  pl.BlockSpec((sb, KV_W), lambda i: (i, 0)),
                  _const_spec(sink_rows.shape), _const_spec(e.shape), _const_spec(et.shape)],
        out_specs=pl.BlockSpec((sb, Q_W), lambda i: (i, 0)),
        out_shape=jax.ShapeDtypeStruct((n, Q_W), F32),
        compiler_params=_params("arbitrary"),
        name="dec_swa",
    )(q, kb, vb, kn, vn, sink_rows, e, et)


def _dec_out_ffn(x, main, mo, wout, g, wup, cw, cb, wdn, st):
    n = x.shape[0]
    nch = N_FF_CHUNKS
    lo3 = lambda j: (j, 0, 0)
    hi3 = lambda j: (nch + j, 0, 0)
    state = lambda r, off: pl.BlockSpec((None, n, FF_CHUNK), lambda j: (r, 0, off + j))
    return pl.pallas_call(
        _dec_out_ffn_kernel,
        grid=(nch,),
        in_specs=[_const_spec(x.shape), _const_spec(main.shape), _const_spec(mo.shape),
                  _const_spec(wout.shape), _const_spec(g.shape),
                  pl.BlockSpec((1, D_MODEL, FF_CHUNK), lo3), pl.BlockSpec((1, D_MODEL, FF_CHUNK), hi3),
                  pl.BlockSpec((1, CONV_F, FF_CHUNK), lo3), pl.BlockSpec((1, CONV_F, FF_CHUNK), hi3),
                  pl.BlockSpec((1, 1, FF_CHUNK), lo3), pl.BlockSpec((1, 1, FF_CHUNK), hi3),
                  pl.BlockSpec((1, FF_CHUNK, D_MODEL), lo3),
                  state(0, 0), state(1, 0), state(0, nch), state(1, nch)],
        out_specs=[_const_spec((n, D_MODEL)),
                   pl.BlockSpec((n, FF_CHUNK), lambda j: (0, j)),
                   pl.BlockSpec((n, FF_CHUNK), lambda j: (0, j))],
        out_shape=[jax.ShapeDtypeStruct((n, D_MODEL), F32),
                   jax.ShapeDtypeStruct((n, D_FF), F32),
                   jax.ShapeDtypeStruct((n, D_FF), F32)],
        scratch_shapes=[pltpu.VMEM((n, D_MODEL), F32), pltpu.VMEM((n, D_MODEL), BF16),
                        pltpu.VMEM((n, D_MODEL), F32)],
        compiler_params=_params("arbitrary"),
        name="dec_out_ffn",
    )(x, main, mo, wout, g, wup, wup, cw, cw, cb, cb, wdn, st, st, st, st)


def _rope_tables(pos):
    half = HEAD_DIM // 2
    inv = ROPE_THETA ** (-jnp.arange(half, dtype=F32) / half)
    ang = pos.astype(F32)[:, None] * inv[None, :]
    cos = jnp.cos(ang)
    sin = jnp.sin(ang)
    reps = LANES // HEAD_DIM
    cos_t = jnp.tile(jnp.concatenate([cos, cos], axis=1), (1, reps))
    sin_t = jnp.tile(jnp.concatenate([-sin, sin], axis=1), (1, reps))
    return cos_t, sin_t


def _block_diag_gates(wx, wa):
    per = MXU_DIM // HEAD_DIM
    eye = jnp.eye(per, dtype=F32)

    def bd(w):
        w4 = w.reshape(RNN_BLOCKS // per, per, HEAD_DIM, HEAD_DIM)
        return jnp.einsum('ckij,kK->ckiKj', w4, eye).reshape(RNN_BLOCKS // per, MXU_DIM, MXU_DIM)

    return jnp.concatenate([bd(wx), bd(wa)], axis=2).astype(BF16)


def _chunk_cols(w):
    r = w.shape[0]
    return w.reshape(r, 2 * N_FF_CHUNKS, FF_CHUNK).transpose(1, 0, 2)


def kernel(x_prompt, x_sample, state_rglru_h, state_rglru_conv, state_ffn_conv, cache_swa_k, cache_swa_v, cache_mem_k, cache_mem_v, mem_prompt, norm_mix_g, norm_ffn_g, w_in_a, rnn_conv_w, rnn_conv_b, w_gate_x, b_gate_x, w_gate_a, b_gate_a, lru_param, w_in_b, q_norm_g, sinks, kv_norm_g, w_kv, k_norm_g, mem_norm_g, w_mem_kv, mem_q_norm_g, mem_k_norm_g, w_out, w_ffn_up, ffn_conv_w, ffn_conv_b, w_ffn_down):
    bsz, seq, _ = x_prompt.shape
    dbsz = x_sample.shape[0]
    depth = norm_mix_g.shape[0]
    n_a = w_in_a.shape[0]
    assert x_sample.shape[1] == 1
    assert seq % TS_MIX == 0 and seq % TS_FFN == 0 and seq % TS_KV == 0 and TS_MIX % WINDOW == 0

    head_of = jnp.arange(Q_W) // HEAD_DIM
    perm = ((head_of % N_KV) * GROUP + head_of // N_KV) * HEAD_DIM + jnp.arange(Q_W) % HEAD_DIM
    bd = (jnp.kron(jnp.eye(MXU_DIM // HEAD_DIM, dtype=F32),
                   jnp.ones((HEAD_DIM, HEAD_DIM), F32)) / HEAD_DIM).astype(BF16)
    head_cols = (jnp.arange(MEM_W)[:, None] // HEAD_DIM == jnp.arange(LANES)[None, :])
    e_mat = head_cols.astype(BF16)
    et_mat = head_cols.T.astype(BF16)

    row = lambda v: v.reshape(1, -1)
    tile_row = lambda v, n: jnp.tile(v, n).reshape(1, -1)
    w_in_a_b = w_in_a.astype(BF16)
    w_in_b_b = jnp.concatenate([w_in_b[:, :, :Q_W][:, :, perm], w_in_b[:, :, Q_W:]], axis=2).astype(BF16)
    w_out_b = w_out.astype(BF16)
    w_out_perm_b = jnp.concatenate([w_out[n_a:, :Q_W][:, perm], w_out[n_a:, Q_W:]], axis=1).astype(BF16)
    w_kv_b = w_kv.astype(BF16)
    w_mem_b = w_mem_kv.astype(BF16)
    wup_b = [_chunk_cols(w_ffn_up[l]).astype(BF16) for l in range(depth)]
    wdn_b = [w_ffn_down[l].reshape(N_FF_CHUNKS, FF_CHUNK, D_MODEL).astype(BF16) for l in range(depth)]
    fcw = [_chunk_cols(ffn_conv_w[l]) for l in range(depth)]
    fcb = [_chunk_cols(ffn_conv_b[l].reshape(1, -1)) for l in range(depth)]
    wg_b = [_block_diag_gates(w_gate_x[l], w_gate_a[l]) for l in range(n_a)]
    sink_gk = sinks.reshape(-1, N_KV, GROUP).transpose(0, 2, 1)

    cos_p, sin_p = _rope_tables(jnp.arange(seq, dtype=jnp.int32))
    pos_s = PAST_LEN + jnp.zeros((dbsz,), jnp.int32)
    cos_s, sin_s = _rope_tables(pos_s)

    mem2d = mem_prompt.reshape(bsz * N_MEM, D_MODEL)
    pmk, pmv = _mem_kv(mem2d, mem_norm_g.reshape(depth, 1, D_MODEL), w_mem_b,
                       jnp.tile(mem_k_norm_g, (1, MEM_HEADS)).reshape(depth, 1, MEM_W), bd)
    pmk4 = pmk.reshape(depth, bsz, N_MEM, MEM_W)
    pmv4 = pmv.reshape(depth, bsz, N_MEM, MEM_W)

    x = x_prompt.reshape(bsz * seq, D_MODEL)
    zeros_rc = jnp.zeros((bsz, SUBLANES, D_RNN), F32)
    zeros_h = jnp.zeros((bsz, 1, D_RNN), F32)
    p_h, p_rc, p_fc = [], [], []
    ksh = vsh = None
    for l in range(depth):
        mqg = tile_row(mem_q_norm_g[l], MEM_HEADS)
        if l < n_a:
            x, hl, rct = _mixer_a(
                x, bsz, row(norm_mix_g[l]), w_in_a_b[l], rnn_conv_w[l], row(rnn_conv_b[l]),
                wg_b[l], row(b_gate_x[l]), row(b_gate_a[l]), row(lru_param[l]), mqg, bd,
                pmk4[l], pmv4[l], w_out_b[l], zeros_rc, zeros_h)
            p_h.append(hl.reshape(bsz, D_RNN))
            p_rc.append(rct[:, SUBLANES - (CONV_A - 1):])
        else:
            j = l - n_a
            x = _mixer_b(
                x, bsz, sink_gk[j].reshape(-1), row(norm_mix_g[l]), w_in_b_b[j],
                tile_row(q_norm_g[j], N_Q), cos_p, sin_p, ksh, vsh, mqg, bd,
                pmk4[l], pmv4[l], w_out_perm_b[j])
        x, ut = _ffn(x, bsz, row(norm_ffn_g[l]), wup_b[l], fcw[l], fcb[l], wdn_b[l])
        p_fc.append(ut[:, SUBLANES - (CONV_F - 1):])
        if l == n_a - 1:
            ksh, vsh = _shared_kv(x, row(kv_norm_g), w_kv_b, tile_row(k_norm_g, N_KV), bd,
                                  cos_p, sin_p, TS_KV)
    y_prompt = x.reshape(bsz, seq, D_MODEL)
    keep = min(WINDOW, seq)
    p_k = ksh.reshape(bsz, seq, N_KV, HEAD_DIM)[:, seq - keep:]
    p_v = vsh.reshape(bsz, seq, N_KV, HEAD_DIM)[:, seq - keep:]
    p_mem_k = pmk.reshape(depth, bsz, N_MEM, MEM_HEADS, HEAD_DIM)
    p_mem_v = pmv.reshape(depth, bsz, N_MEM, MEM_HEADS, HEAD_DIM)

    xs = x_sample.reshape(dbsz, D_MODEL)
    cmk = cache_mem_k.reshape(depth, dbsz, N_MEM, MEM_W)
    cmv = cache_mem_v.reshape(depth, dbsz, N_MEM, MEM_W)
    wb = cache_swa_k.shape[1]
    ckb = cache_swa_k.reshape(dbsz, wb, KV_W)
    cvb = cache_swa_v.reshape(dbsz, wb, KV_W)
    s_h, s_rc, s_fc = [], [], []
    kn = vn = None
    for l in range(depth):
        mqg = tile_row(mem_q_norm_g[l], MEM_HEADS)
        if l < n_a:
            main, qn, hnew, xrpre = _dec_in_a(
                xs, row(norm_mix_g[l]), w_in_a_b[l], rnn_conv_w[l], row(rnn_conv_b[l]),
                wg_b[l], row(b_gate_x[l]), row(b_gate_a[l]), row(lru_param[l]), mqg, bd,
                state_rglru_conv[l].transpose(1, 0, 2), state_rglru_h[l])
            s_h.append(hnew)
            s_rc.append(jnp.concatenate([state_rglru_conv[l][:, 1:], xrpre[:, None, :]], axis=1))
            wo = w_out_b[l]
        else:
            j = l - n_a
            q, qn = _dec_in_b(xs, row(norm_mix_g[l]), w_in_b_b[j], tile_row(q_norm_g[j], N_Q),
                              cos_s, sin_s, mqg, bd)
            sink_rows = jnp.zeros((SUBLANES, LANES), F32).at[:GROUP, :N_KV].set(sink_gk[j])
            main = _dec_swa(q, ckb, cvb, kn, vn, sink_rows, e_mat, et_mat)
            wo = w_out_perm_b[j]
        mo = _dec_mem_attn(qn, cmk[l], cmv[l], e_mat, et_mat)
        xs, ug, uv = _dec_out_ffn(xs, main, mo, wo, row(norm_ffn_g[l]), wup_b[l], fcw[l], fcb[l],
                                  wdn_b[l], state_ffn_conv[l].transpose(1, 0, 2))
        unew = jnp.concatenate([ug, uv], axis=1)
        s_fc.append(jnp.concatenate([state_ffn_conv[l][:, 1:], unew[:, None, :]], axis=1))
        if l == n_a - 1:
            kn, vn = _shared_kv(xs, row(kv_norm_g), w_kv_b, tile_row(k_norm_g, N_KV), bd,
                                cos_s, sin_s, dbsz)
    y_sample = xs.reshape(dbsz, 1, D_MODEL)
    s_k = kn.reshape(dbsz, 1, N_KV, HEAD_DIM)
    s_v = vn.reshape(dbsz, 1, N_KV, HEAD_DIM)

    return (y_prompt, y_sample, jnp.stack(p_h), jnp.stack(p_rc), jnp.stack(p_fc), p_k, p_v,
            p_mem_k, p_mem_v, jnp.stack(s_h), jnp.stack(s_rc), jnp.stack(s_fc), s_k, s_v)
```

```python
import functools
import math

import jax
import jax.numpy as jnp
from jax import lax
from jax.experimental import pallas as pl
from jax.experimental.pallas import tpu as pltpu

F32 = jnp.float32
BF16 = jnp.bfloat16

D_MODEL = 1024
HEAD_DIM = 64
MEM_HEADS = 4
MEM_W = MEM_HEADS * HEAD_DIM
N_MEM = 256
D_RNN = D_MODEL - MEM_W
RNN_BLOCKS = D_RNN // HEAD_DIM
CONV_A = 4
LRU_C = 8.0
N_Q = D_RNN // HEAD_DIM
N_KV = 4
GROUP = N_Q // N_KV
Q_W = N_Q * HEAD_DIM
KV_W = N_KV * HEAD_DIM
WINDOW = 128
ROPE_THETA = 10000.0
D_FF = 3 * D_MODEL
CONV_F = 3
EPS = 1e-6
NEG = -1e30
ATT_SCALE = HEAD_DIM ** -0.5
PAST_LEN = 8192

SUBLANES = 8
LANES = 128
MXU_DIM = 256
VMEM_LIMIT_BYTES = 56 * 1024 * 1024

TS_MIX = 256
TS_FFN = 512
TS_KV = 512
FF_CHUNK = 512
N_FF_CHUNKS = D_FF // FF_CHUNK
DEC_SEQ_BLOCK = 8


def _mm(a, b):
    return jnp.dot(a.astype(BF16), b, preferred_element_type=F32)


def _mm_nt(a, b):
    return lax.dot_general(a.astype(BF16), b, (((1,), (1,)), ((), ())),
                           preferred_element_type=F32)


def _mm_split(a, b):
    hi = a.astype(BF16)
    lo = (a - hi.astype(F32)).astype(BF16)
    return (jnp.dot(hi, b, preferred_element_type=F32)
            + jnp.dot(lo, b, preferred_element_type=F32))


def _rmsnorm(x, g):
    ms = jnp.mean(x * x, axis=-1, keepdims=True)
    return x * lax.rsqrt(ms + EPS) * g


def _head_rmsnorm(x, bd, g):
    parts = []
    for c in range(x.shape[1] // MXU_DIM):
        xc = x[:, c * MXU_DIM:(c + 1) * MXU_DIM]
        ms = _mm_split(xc * xc, bd)
        parts.append(xc * lax.rsqrt(ms + EPS))
    y = parts[0] if len(parts) == 1 else jnp.concatenate(parts, axis=1)
    return y * g


def _tile_lanes(t, width):
    reps = width // t.shape[1]
    return t if reps == 1 else jnp.concatenate([t] * reps, axis=1)


def _rope(x, cos_t, sin_t):
    w = x.shape[1]
    lane = lax.broadcasted_iota(jnp.int32, x.shape, 1)
    first = (lane % HEAD_DIM) < (HEAD_DIM // 2)
    swapped = jnp.where(first, pltpu.roll(x, w - HEAD_DIM // 2, 1),
                        pltpu.roll(x, HEAD_DIM // 2, 1))
    return x * _tile_lanes(cos_t, w) + swapped * _tile_lanes(sin_t, w)


def _gelu(x):
    c = math.sqrt(2.0 / math.pi)
    return x * (0.5 * (1.0 + jnp.tanh(c * (x + 0.044715 * (x * x * x)))))


def _log_sigmoid(x):
    return jnp.minimum(x, 0.0) - jnp.log1p(jnp.exp(-jnp.abs(x)))


def _slab_conv(x, work_ref, work0, tail_ref, tail0, w_ref, b_ref, col0):
    ts = x.shape[0]
    k = w_ref.shape[0]
    outs = []
    for s in range(x.shape[1] // LANES):
        xs = x[:, s * LANES:(s + 1) * LANES]
        lanes = slice(col0 + s * LANES, col0 + (s + 1) * LANES)
        buf = work_ref.at[work0 + s]
        buf[0:SUBLANES, :] = tail_ref[tail0 + s]
        buf[SUBLANES:SUBLANES + ts, :] = xs
        tail_ref[tail0 + s] = xs[ts - SUBLANES:]
        acc = buf[SUBLANES - (k - 1):SUBLANES - (k - 1) + ts, :] * w_ref[0:1, lanes]
        for j in range(1, k - 1):
            off = SUBLANES - (k - 1 - j)
            acc = acc + buf[off:off + ts, :] * w_ref[j:j + 1, lanes]
        acc = acc + xs * w_ref[k - 1:k, lanes]
        outs.append(acc + b_ref[:, lanes])
    return outs


def _sqrt_pos(x):
    return jnp.where(x > 0.0, x * lax.rsqrt(x), 0.0)


def _lru_coeffs(xr, wg_ref, bgx, bga, logsig):
    xb = xr.astype(BF16)
    gxs, gas = [], []
    for c in range(D_RNN // MXU_DIM):
        gg = jnp.dot(xb[:, c * MXU_DIM:(c + 1) * MXU_DIM], wg_ref[c],
                     preferred_element_type=F32)
        gxs.append(gg[:, :MXU_DIM])
        gas.append(gg[:, MXU_DIM:])
    gx = jax.nn.sigmoid(jnp.concatenate(gxs, axis=1) + bgx)
    ga = jax.nn.sigmoid(jnp.concatenate(gas, axis=1) + bga)
    log_a = LRU_C * ga * logsig
    a = jnp.exp(log_a)
    mult = _sqrt_pos(-jnp.tanh(log_a) * (a * a + 1.0))
    return a, mult * gx * xr


def _lru_scan(a, b, a_ref, b_ref, hc_ref):
    ts = a.shape[0]
    outs = []
    for s in range(a.shape[1] // LANES):
        lanes = slice(s * LANES, (s + 1) * LANES)
        a_s, b_s = a[:, lanes], b[:, lanes]
        abuf, bbuf = a_ref.at[s], b_ref.at[s]
        d = 1
        while d < SUBLANES:
            abuf[SUBLANES:SUBLANES + ts, :] = a_s
            bbuf[SUBLANES:SUBLANES + ts, :] = b_s
            b_s = a_s * bbuf[SUBLANES - d:SUBLANES - d + ts, :] + b_s
            a_s = a_s * abuf[SUBLANES - d:SUBLANES - d + ts, :]
            d *= 2
        h = hc_ref[:, lanes]
        hs = []
        for q in range(ts // SUBLANES):
            rows = slice(q * SUBLANES, (q + 1) * SUBLANES)
            h = a_s[rows] * h + b_s[rows]
            hs.append(h)
        hc_ref[:, lanes] = jnp.broadcast_to(h[SUBLANES - 1:], (SUBLANES, LANES))
        outs.append(jnp.concatenate(hs, axis=0))
    return jnp.concatenate(outs, axis=1)


def _head_mask(shape, h):
    lane = lax.broadcasted_iota(jnp.int32, shape, 1)
    return (lane >= h * HEAD_DIM) & (lane < (h + 1) * HEAD_DIM)


def _head_masked_stack(blocks, n_heads):
    parts = []
    for h in range(n_heads):
        for blk in blocks:
            parts.append(jnp.where(_head_mask(blk.shape, h), blk, 0.0).astype(BF16))
    return jnp.concatenate(parts, axis=0)


def _mem_attention(qn, kcat, vcat):
    s = _mm_nt(qn, kcat)
    parts = []
    for h in range(MEM_HEADS):
        sh = s[:, h * N_MEM:(h + 1) * N_MEM]
        m = jnp.max(sh, axis=-1, keepdims=True)
        p = jnp.exp(sh - m)
        den = jnp.sum(p, axis=-1, keepdims=True)
        parts.append((p * (1.0 / den)).astype(BF16))
    return jnp.dot(jnp.concatenate(parts, axis=1), vcat, preferred_element_type=F32)


def _mem_kv_kernel(mem_ref, g_ref, w_ref, kg_ref, bd_ref, k_ref, v_ref):
    h = _mm(_rmsnorm(mem_ref[...], g_ref[0]), w_ref[0])
    k_ref[0] = _head_rmsnorm(h[:, :MEM_W], bd_ref[...], kg_ref[0])
    v_ref[0] = h[:, MEM_W:]


def _shared_kv_kernel(x_ref, g_ref, w_ref, kg_ref, bd_ref, cos_ref, sin_ref,
                      k_ref, v_ref):
    h = _mm(_rmsnorm(x_ref[...], g_ref[...]), w_ref[...])
    k = _head_rmsnorm(h[:, :KV_W], bd_ref[...], kg_ref[...])
    k_ref[...] = _rope(k, cos_ref[...], sin_ref[...])
    v_ref[...] = h[:, KV_W:]


def _mixer_a_kernel(x_ref, g_ref, win_ref, cw_ref, cb_ref, wg_ref, bgx_ref, bga_ref,
                    lru_ref, mqg_ref, bd_ref, mk_ref, mv_ref, wout_ref, rc0_ref, h0_ref,
                    xo_ref, hlast_ref, rctail_ref,
                    conv_s, tail_s, a_s, b_s, hc_s, kcat_s, vcat_s):
    ts = x_ref.shape[0]
    n_slabs = D_RNN // LANES

    @pl.when(pl.program_id(1) == 0)
    def _():
        for s in range(n_slabs):
            tail_s[s] = rc0_ref[0, :, s * LANES:(s + 1) * LANES]
        a_s[:, 0:SUBLANES, :] = jnp.ones((n_slabs, SUBLANES, LANES), F32)
        b_s[:, 0:SUBLANES, :] = jnp.zeros((n_slabs, SUBLANES, LANES), F32)
        hc_s[...] = jnp.broadcast_to(h0_ref[0], hc_s.shape)
        kcat_s[...] = _head_masked_stack([mk_ref[0]], MEM_HEADS)
        vcat_s[...] = _head_masked_stack([mv_ref[0]], MEM_HEADS)

    x = x_ref[...]
    u = _mm(_rmsnorm(x, g_ref[...]), win_ref[...])
    gate = u[:, :D_RNN]
    xr_pre = u[:, D_RNN:2 * D_RNN]
    qm = u[:, 2 * D_RNN:]

    xr = jnp.concatenate(_slab_conv(xr_pre, conv_s, 0, tail_s, 0, cw_ref, cb_ref, 0), axis=1)
    rctail_ref[0] = xr_pre[ts - SUBLANES:]

    a, b = _lru_coeffs(xr, wg_ref, bgx_ref[...], bga_ref[...], _log_sigmoid(lru_ref[...]))
    h = _lru_scan(a, b, a_s, b_s, hc_s)
    hlast_ref[0] = h[ts - 1:ts]
    main = h * _gelu(gate)

    qn = _head_rmsnorm(qm, bd_ref[...], mqg_ref[...]) * ATT_SCALE
    mo = _mem_attention(qn, kcat_s[...], vcat_s[...])

    y = _mm(jnp.concatenate([main, mo], axis=1), wout_ref[...])
    xo_ref[...] = x + y


def _mixer_b_kernel(sink_ref, x_ref, g_ref, win_ref, qg_ref, cos_ref, sin_ref,
                    kcur_ref, kprev_ref, vcur_ref, vprev_ref,
                    mqg_ref, bd_ref, mk_ref, mv_ref, wout_ref,
                    xo_ref, kcat_s, vcat_s):
    ts = x_ref.shape[0]
    i = pl.program_id(1)

    @pl.when(i == 0)
    def _():
        kcat_s[...] = _head_masked_stack([mk_ref[0]], MEM_HEADS)
        vcat_s[...] = _head_masked_stack([mv_ref[0]], MEM_HEADS)

    x = x_ref[...]
    u = _mm(_rmsnorm(x, g_ref[...]), win_ref[...])
    q = _head_rmsnorm(u[:, :Q_W], bd_ref[...], qg_ref[...])
    q = (_rope(q, cos_ref[...], sin_ref[...]) * ATT_SCALE).astype(BF16)
    qm = u[:, Q_W:]

    row = lax.broadcasted_iota(jnp.int32, (WINDOW, 2 * WINDOW), 0)
    kj = lax.broadcasted_iota(jnp.int32, (WINDOW, 2 * WINDOW), 1) - WINDOW
    mask_inner = (kj <= row) & (kj >= row - WINDOW)
    mask_first = (kj <= row) & (kj >= jnp.maximum(row - WINDOW, jnp.where(i > 0, -WINDOW, 0)))

    kt = kcur_ref[...]
    vt = vcur_ref[...]
    mains = []
    for jb in range(ts // WINDOW):
        lo, hi = jb * WINDOW, (jb + 1) * WINDOW
        if jb == 0:
            kp, vp = kprev_ref[...], vprev_ref[...]
            mask = mask_first
        else:
            kp, vp = kt[lo - WINDOW:lo], vt[lo - WINDOW:lo]
            mask = mask_inner
        kcat = _head_masked_stack([kp, kt[lo:hi]], N_KV)
        vcat = _head_masked_stack([vp, vt[lo:hi]], N_KV)
        qs = jnp.concatenate([q[lo:hi, g * KV_W:(g + 1) * KV_W] for g in range(GROUP)], axis=0)
        s = lax.dot_general(qs, kcat, (((1,), (1,)), ((), ())), preferred_element_type=F32)
        prow = []
        for g in range(GROUP):
            pseg = []
            for kv in range(N_KV):
                seg = s[g * WINDOW:(g + 1) * WINDOW, kv * 2 * WINDOW:(kv + 1) * 2 * WINDOW]
                seg = jnp.where(mask, seg, NEG)
                sink = sink_ref[g * N_KV + kv]
                m = jnp.maximum(jnp.max(seg, axis=-1, keepdims=True), sink)
                p = jnp.exp(seg - m)
                den = jnp.sum(p, axis=-1, keepdims=True) + jnp.exp(sink - m)
                pseg.append((p * (1.0 / den)).astype(BF16))
            prow.append(jnp.concatenate(pseg, axis=1))
        o = jnp.dot(jnp.concatenate(prow, axis=0), vcat, preferred_element_type=F32)
        mains.append(jnp.concatenate([o[g * WINDOW:(g + 1) * WINDOW] for g in range(GROUP)], axis=1))
    main = mains[0] if len(mains) == 1 else jnp.concatenate(mains, axis=0)

    qn = _head_rmsnorm(qm, bd_ref[...], mqg_ref[...]) * ATT_SCALE
    mo = _mem_attention(qn, kcat_s[...], vcat_s[...])

    y = _mm(jnp.concatenate([main, mo], axis=1), wout_ref[...])
    xo_ref[...] = x + y


def _ffn_kernel(x_ref, g_ref, wup_ref, cw_ref, cb_ref, wdn_ref,
                xo_ref, utail_ref, conv_s, tail_s):
    ts = x_ref.shape[0]
    slabs = FF_CHUNK // LANES

    @pl.when(pl.program_id(1) == 0)
    def _():
        tail_s[...] = jnp.zeros_like(tail_s)

    x = x_ref[...]
    hn = _rmsnorm(x, g_ref[...]).astype(BF16)
    acc = jnp.zeros((ts, D_MODEL), F32)
    for j in range(N_FF_CHUNKS):
        halves = []
        for half, c in enumerate((j, N_FF_CHUNKS + j)):
            cols = slice(c * FF_CHUNK, (c + 1) * FF_CHUNK)
            u = jnp.dot(hn, wup_ref[:, cols], preferred_element_type=F32)
            work0 = ((j % 2) * 2 + half) * slabs
            halves.append(_slab_conv(u, conv_s, work0, tail_s, c * slabs, cw_ref, cb_ref,
                                     c * FF_CHUNK))
            utail_ref[0, :, cols] = u[ts - SUBLANES:]
        act = jnp.concatenate([_gelu(cg) * cv for cg, cv in zip(*halves)], axis=1).astype(BF16)
        acc = acc + jnp.dot(act, wdn_ref[j * FF_CHUNK:(j + 1) * FF_CHUNK, :],
                            preferred_element_type=F32)
    xo_ref[...] = x + acc


def _dec_in_a_kernel(x_ref, g_ref, win_ref, cw_ref, cb_ref, wg_ref, bgx_ref, bga_ref,
                     lru_ref, mqg_ref, bd_ref, b0_ref, b1_ref, b2_ref, h0_ref,
                     main_ref, qn_ref, hnew_ref, xrpre_ref):
    u = _mm(_rmsnorm(x_ref[...], g_ref[...]), win_ref[...])
    gate = u[:, :D_RNN]
    xr_pre = u[:, D_RNN:2 * D_RNN]
    qm = u[:, 2 * D_RNN:]
    xr = b0_ref[...] * cw_ref[0:1, :]
    xr = xr + b1_ref[...] * cw_ref[1:2, :]
    xr = xr + b2_ref[...] * cw_ref[2:3, :]
    xr = xr + xr_pre * cw_ref[3:4, :]
    xr = xr + cb_ref[...]
    a, b = _lru_coeffs(xr, wg_ref, bgx_ref[...], bga_ref[...], _log_sigmoid(lru_ref[...]))
    h = a * h0_ref[...] + b
    main_ref[...] = h * _gelu(gate)
    qn_ref[...] = _head_rmsnorm(qm, bd_ref[...], mqg_ref[...]) * ATT_SCALE
    hnew_ref[...] = h
    xrpre_ref[...] = xr_pre


def _dec_in_b_kernel(x_ref, g_ref, win_ref, qg_ref, cos_ref, sin_ref, mqg_ref, bd_ref,
                     q_ref, qn_ref):
    u = _mm(_rmsnorm(x_ref[...], g_ref[...]), win_ref[...])
    q = _head_rmsnorm(u[:, :Q_W], bd_ref[...], qg_ref[...])
    q_ref[...] = _rope(q, cos_ref[...], sin_ref[...]) * ATT_SCALE
    qn_ref[...] = _head_rmsnorm(u[:, Q_W:], bd_ref[...], mqg_ref[...]) * ATT_SCALE


def _seq_head_scores(q_row, k, e):
    return _mm_split(k * q_row, e)


def _dec_mem_attn_kernel(q_ref, k_ref, v_ref, e_ref, et_ref, o_ref):
    e, et = e_ref[...], et_ref[...]

    def body(s, carry):
        sc = _seq_head_scores(q_ref[pl.ds(s, 1), :], k_ref[s], e)
        m = jnp.max(sc, axis=0, keepdims=True)
        p = jnp.exp(sc - m)
        den = jnp.sum(p, axis=0, keepdims=True)
        pe = _mm_split(p * (1.0 / den), et)
        o_ref[pl.ds(s, 1), :] = jnp.sum(pe * v_ref[s], axis=0, keepdims=True)
        return carry

    lax.fori_loop(0, q_ref.shape[0], body, 0)


def _dec_swa_kernel(q_ref, kb_ref, vb_ref, kn_ref, vn_ref, sink_ref, e_ref, et_ref, o_ref):
    e, et = e_ref[...], et_ref[...]

    def body(s, carry):
        kb, vb = kb_ref[s], vb_ref[s]
        kn, vn = kn_ref[pl.ds(s, 1), :], vn_ref[pl.ds(s, 1), :]
        for g in range(GROUP):
            qg = q_ref[pl.ds(s, 1), g * KV_W:(g + 1) * KV_W]
            sink = sink_ref[g:g + 1, :]
            s_new = _seq_head_scores(qg, kn, e)
            s_buf = _seq_head_scores(qg, kb, e)
            m = jnp.maximum(jnp.maximum(s_new, sink), jnp.max(s_buf, axis=0, keepdims=True))
            p_new = jnp.exp(s_new - m)
            p_buf = jnp.exp(s_buf - m)
            den = jnp.sum(p_buf, axis=0, keepdims=True) + p_new + jnp.exp(sink - m)
            r = 1.0 / den
            o = jnp.sum(_mm_split(p_buf * r, et) * vb, axis=0, keepdims=True)
            o = o + _mm_split(p_new * r, et) * vn
            o_ref[pl.ds(s, 1), g * KV_W:(g + 1) * KV_W] = o
        return carry

    lax.fori_loop(0, q_ref.shape[0], body, 0)


def _dec_out_ffn_kernel(x_ref, main_ref, mo_ref, wout_ref, g_ref,
                        wug_ref, wuv_ref, cwg_ref, cwv_ref, cbg_ref, cbv_ref, wdn_ref,
                        sg0_ref, sg1_ref, sv0_ref, sv1_ref,
                        xo_ref, ug_ref, uv_ref,
                        xmid_s, hn_s, acc_s):
    j = pl.program_id(0)

    @pl.when(j == 0)
    def _():
        y = _mm(jnp.concatenate([main_ref[...], mo_ref[...]], axis=1), wout_ref[...])
        xmid = x_ref[...] + y
        xmid_s[...] = xmid
        hn_s[...] = _rmsnorm(xmid, g_ref[...]).astype(BF16)
        acc_s[...] = jnp.zeros_like(acc_s)

    hn = hn_s[...]
    ug = jnp.dot(hn, wug_ref[...], preferred_element_type=F32)
    uv = jnp.dot(hn, wuv_ref[...], preferred_element_type=F32)
    ug_ref[...] = ug
    uv_ref[...] = uv
    cg = (sg0_ref[...] * cwg_ref[0:1, :] + sg1_ref[...] * cwg_ref[1:2, :]
          + ug * cwg_ref[2:3, :] + cbg_ref[...])
    cv = (sv0_ref[...] * cwv_ref[0:1, :] + sv1_ref[...] * cwv_ref[1:2, :]
          + uv * cwv_ref[2:3, :] + cbv_ref[...])
    act = (_gelu(cg) * cv).astype(BF16)
    acc_s[...] += jnp.dot(act, wdn_ref[...], preferred_element_type=F32)

    @pl.when(j == pl.num_programs(0) - 1)
    def _():
        xo_ref[...] = xmid_s[...] + acc_s[...]


def _const_spec(shape):
    nd = len(shape)
    return pl.BlockSpec(shape, lambda *_: (0,) * nd)


def _params(*sem):
    return pltpu.CompilerParams(dimension_semantics=sem, vmem_limit_bytes=VMEM_LIMIT_BYTES)


def _mem_kv(mem2d, g, w, kg, bd):
    depth = w.shape[0]
    rows = mem2d.shape[0]
    out = jax.ShapeDtypeStruct((depth, rows, MEM_W), F32)
    return pl.pallas_call(
        _mem_kv_kernel,
        grid=(depth,),
        in_specs=[
            _const_spec(mem2d.shape),
            pl.BlockSpec((1, 1, D_MODEL), lambda l: (l, 0, 0)),
            pl.BlockSpec((1, D_MODEL, 2 * MEM_W), lambda l: (l, 0, 0)),
            pl.BlockSpec((1, 1, MEM_W), lambda l: (l, 0, 0)),
            _const_spec(bd.shape),
        ],
        out_specs=[pl.BlockSpec((1, rows, MEM_W), lambda l: (l, 0, 0))] * 2,
        out_shape=[out, out],
        compiler_params=_params("arbitrary"),
        name="mem_kv",
    )(mem2d, g, w, kg, bd)


def _shared_kv(x2d, g, w, kg, bd, cos_t, sin_t, ts):
    rows = x2d.shape[0]
    tab_blocks = cos_t.shape[0] // ts
    out = jax.ShapeDtypeStruct((rows, KV_W), F32)
    return pl.pallas_call(
        _shared_kv_kernel,
        grid=(rows // ts,),
        in_specs=[
            pl.BlockSpec((ts, D_MODEL), lambda i: (i, 0)),
            _const_spec(g.shape), _const_spec(w.shape), _const_spec(kg.shape),
            _const_spec(bd.shape),
            pl.BlockSpec((ts, LANES), lambda i: (i % tab_blocks, 0)),
            pl.BlockSpec((ts, LANES), lambda i: (i % tab_blocks, 0)),
        ],
        out_specs=[pl.BlockSpec((ts, KV_W), lambda i: (i, 0))] * 2,
        out_shape=[out, out],
        compiler_params=_params("arbitrary"),
        name="shared_kv",
    )(x2d, g, w, kg, bd, cos_t, sin_t)


def _mixer_a(x2d, bsz, g, win, cw, cb, wg, bgx, bga, lru, mqg, bd, mk, mv, wout, rc0, h0):
    rows = x2d.shape[0]
    ts = TS_MIX
    nt = rows // bsz // ts
    consts = (g, win, cw, cb, wg, bgx, bga, lru, mqg, bd)
    per_b3 = lambda b, i: (b, 0, 0)
    return pl.pallas_call(
        _mixer_a_kernel,
        grid=(bsz, nt),
        in_specs=[pl.BlockSpec((ts, D_MODEL), lambda b, i: (b * nt + i, 0))]
        + [_const_spec(c.shape) for c in consts]
        + [pl.BlockSpec((1, N_MEM, MEM_W), per_b3)] * 2
        + [_const_spec(wout.shape),
           pl.BlockSpec((1, SUBLANES, D_RNN), per_b3),
           pl.BlockSpec((1, 1, D_RNN), per_b3)],
        out_specs=[
            pl.BlockSpec((ts, D_MODEL), lambda b, i: (b * nt + i, 0)),
            pl.BlockSpec((1, 1, D_RNN), per_b3),
            pl.BlockSpec((1, SUBLANES, D_RNN), per_b3),
        ],
        out_shape=[
            jax.ShapeDtypeStruct((rows, D_MODEL), F32),
            jax.ShapeDtypeStruct((bsz, 1, D_RNN), F32),
            jax.ShapeDtypeStruct((bsz, SUBLANES, D_RNN), F32),
        ],
        scratch_shapes=[
            pltpu.VMEM((D_RNN // LANES, SUBLANES + ts, LANES), F32),
            pltpu.VMEM((D_RNN // LANES, SUBLANES, LANES), F32),
            pltpu.VMEM((D_RNN // LANES, SUBLANES + ts, LANES), F32),
            pltpu.VMEM((D_RNN // LANES, SUBLANES + ts, LANES), F32),
            pltpu.VMEM((SUBLANES, D_RNN), F32),
            pltpu.VMEM((MEM_HEADS * N_MEM, MEM_W), BF16),
            pltpu.VMEM((MEM_HEADS * N_MEM, MEM_W), BF16),
        ],
        compiler_params=_params("arbitrary", "arbitrary"),
        name="mixer_a",
    )(x2d, *consts, mk, mv, wout, rc0, h0)


def _mixer_b(x2d, bsz, sink_tab, g, win, qg, cos_t, sin_t, ksh, vsh, mqg, bd, mk, mv, wout):
    rows = x2d.shape[0]
    ts = TS_MIX
    nt = rows // bsz // ts
    wpt = ts // WINDOW
    cur = lambda b, i: (b * nt + i, 0)
    prev = lambda b, i: (jnp.maximum((b * nt + i) * wpt - 1, 0), 0)
    per_b3 = lambda b, i: (b, 0, 0)
    return pl.pallas_call(
        _mixer_b_kernel,
        grid=(bsz, nt),
        in_specs=[
            pl.BlockSpec(memory_space=pltpu.SMEM),
            pl.BlockSpec((ts, D_MODEL), cur),
            _const_spec(g.shape), _const_spec(win.shape), _const_spec(qg.shape),
            pl.BlockSpec((ts, LANES), lambda b, i: (i, 0)),
            pl.BlockSpec((ts, LANES), lambda b, i: (i, 0)),
            pl.BlockSpec((ts, KV_W), cur), pl.BlockSpec((WINDOW, KV_W), prev),
            pl.BlockSpec((ts, KV_W), cur), pl.BlockSpec((WINDOW, KV_W), prev),
            _const_spec(mqg.shape), _const_spec(bd.shape),
            pl.BlockSpec((1, N_MEM, MEM_W), per_b3), pl.BlockSpec((1, N_MEM, MEM_W), per_b3),
            _const_spec(wout.shape),
        ],
        out_specs=pl.BlockSpec((ts, D_MODEL), cur),
        out_shape=jax.ShapeDtypeStruct((rows, D_MODEL), F32),
        scratch_shapes=[
            pltpu.VMEM((MEM_HEADS * N_MEM, MEM_W), BF16),
            pltpu.VMEM((MEM_HEADS * N_MEM, MEM_W), BF16),
        ],
        compiler_params=_params("arbitrary", "arbitrary"),
        name="mixer_b",
    )(sink_tab, x2d, g, win, qg, cos_t, sin_t, ksh, ksh, vsh, vsh, mqg, bd, mk, mv, wout)


def _ffn(x2d, bsz, g, wup, cw, cb, wdn):
    rows = x2d.shape[0]
    ts = TS_FFN
    nt = rows // bsz // ts
    return pl.pallas_call(
        _ffn_kernel,
        grid=(bsz, nt),
        in_specs=[pl.BlockSpec((ts, D_MODEL), lambda b, i: (b * nt + i, 0)),
                  _const_spec(g.shape), _const_spec(wup.shape), _const_spec(cw.shape),
                  _const_spec(cb.shape), _const_spec(wdn.shape)],
        out_specs=[
            pl.BlockSpec((ts, D_MODEL), lambda b, i: (b * nt + i, 0)),
            pl.BlockSpec((1, SUBLANES, 2 * D_FF), lambda b, i: (b, 0, 0)),
        ],
        out_shape=[
            jax.ShapeDtypeStruct((rows, D_MODEL), F32),
            jax.ShapeDtypeStruct((bsz, SUBLANES, 2 * D_FF), F32),
        ],
        scratch_shapes=[pltpu.VMEM((4 * FF_CHUNK // LANES, SUBLANES + ts, LANES), F32),
                        pltpu.VMEM((2 * D_FF // LANES, SUBLANES, LANES), F32)],
        compiler_params=_params("arbitrary", "arbitrary"),
        name="ffn",
    )(x2d, g, wup, cw, cb, wdn)


def _dec_in_a(x, g, win, cw, cb, wg, bgx, bga, lru, mqg, bd, rc, h0):
    n = x.shape[0]
    consts = (x, g, win, cw, cb, wg, bgx, bga, lru, mqg, bd)
    buf = lambda j: pl.BlockSpec((None, n, D_RNN), lambda i: (j, 0, 0))
    return pl.pallas_call(
        _dec_in_a_kernel,
        grid=(1,),
        in_specs=[_const_spec(c.shape) for c in consts]
        + [buf(0), buf(1), buf(2), _const_spec(h0.shape)],
        out_specs=[_const_spec((n, D_RNN)), _const_spec((n, MEM_W)),
                   _const_spec((n, D_RNN)), _const_spec((n, D_RNN))],
        out_shape=[jax.ShapeDtypeStruct((n, D_RNN), F32), jax.ShapeDtypeStruct((n, MEM_W), F32),
                   jax.ShapeDtypeStruct((n, D_RNN), F32), jax.ShapeDtypeStruct((n, D_RNN), F32)],
        compiler_params=_params("arbitrary"),
        name="dec_in_a",
    )(*consts, rc, rc, rc, h0)


def _dec_in_b(x, g, win, qg, cos_t, sin_t, mqg, bd):
    n = x.shape[0]
    args = (x, g, win, qg, cos_t, sin_t, mqg, bd)
    return pl.pallas_call(
        _dec_in_b_kernel,
        grid=(1,),
        in_specs=[_const_spec(a.shape) for a in args],
        out_specs=[_const_spec((n, Q_W)), _const_spec((n, MEM_W))],
        out_shape=[jax.ShapeDtypeStruct((n, Q_W), F32), jax.ShapeDtypeStruct((n, MEM_W), F32)],
        compiler_params=_params("arbitrary"),
        name="dec_in_b",
    )(*args)


def _dec_mem_attn(qn, ck, cv, e, et):
    n = qn.shape[0]
    sb = DEC_SEQ_BLOCK
    return pl.pallas_call(
        _dec_mem_attn_kernel,
        grid=(n // sb,),
        in_specs=[pl.BlockSpec((sb, MEM_W), lambda i: (i, 0)),
                  pl.BlockSpec((sb, N_MEM, MEM_W), lambda i: (i, 0, 0)),
                  pl.BlockSpec((sb, N_MEM, MEM_W), lambda i: (i, 0, 0)),
                  _const_spec(e.shape), _const_spec(et.shape)],
        out_specs=pl.BlockSpec((sb, MEM_W), lambda i: (i, 0)),
        out_shape=jax.ShapeDtypeStruct((n, MEM_W), F32),
        compiler_params=_params("arbitrary"),
        name="dec_mem_attn",
    )(qn, ck, cv, e, et)


def _dec_swa(q, kb, vb, kn, vn, sink_rows, e, et):
    n = q.shape[0]
    wb = kb.shape[1]
    sb = DEC_SEQ_BLOCK
    assert wb <= WINDOW
    return pl.pallas_call(
        _dec_swa_kernel,
        grid=(n // sb,),
        in_specs=[pl.BlockSpec((sb, Q_W), lambda i: (i, 0)),
                  pl.BlockSpec((sb, wb, KV_W), lambda i: (i, 0, 0)),
                  pl.BlockSpec((sb, wb, KV_W), lambda i: (i, 0, 0)),
                  pl.BlockSpec((sb, KV_W), lambda i: (i, 0)),
                  pl.BlockSpec((sb, KV_W), lambda i: (i, 0)),
                  _const_spec(sink_rows.shape), _const_spec(e.shape), _const_spec(et.shape)],
        out_specs=pl.BlockSpec((sb, Q_W), lambda i: (i, 0)),
        out_shape=jax.ShapeDtypeStruct((n, Q_W), F32),
        compiler_params=_params("arbitrary"),
        name="dec_swa",
    )(q, kb, vb, kn, vn, sink_rows, e, et)


def _dec_out_ffn(x, main, mo, wout, g, wup, cw, cb, wdn, st):
    n = x.shape[0]
    nch = N_FF_CHUNKS
    lo = lambda j: (0, j)
    hi = lambda j: (0, nch + j)
    state = lambda r, off: pl.BlockSpec((None, n, FF_CHUNK), lambda j: (r, 0, off + j))
    return pl.pallas_call(
        _dec_out_ffn_kernel,
        grid=(nch,),
        in_specs=[_const_spec(x.shape), _const_spec(main.shape), _const_spec(mo.shape),
                  _const_spec(wout.shape), _const_spec(g.shape),
                  pl.BlockSpec((D_MODEL, FF_CHUNK), lo), pl.BlockSpec((D_MODEL, FF_CHUNK), hi),
                  pl.BlockSpec((CONV_F, FF_CHUNK), lo), pl.BlockSpec((CONV_F, FF_CHUNK), hi),
                  pl.BlockSpec((1, FF_CHUNK), lo), pl.BlockSpec((1, FF_CHUNK), hi),
                  pl.BlockSpec((FF_CHUNK, D_MODEL), lambda j: (j, 0)),
                  state(0, 0), state(1, 0), state(0, nch), state(1, nch)],
        out_specs=[_const_spec((n, D_MODEL)),
                   pl.BlockSpec((n, FF_CHUNK), lambda j: (0, j)),
                   pl.BlockSpec((n, FF_CHUNK), lambda j: (0, j))],
        out_shape=[jax.ShapeDtypeStruct((n, D_MODEL), F32),
                   jax.ShapeDtypeStruct((n, D_FF), F32),
                   jax.ShapeDtypeStruct((n, D_FF), F32)],
        scratch_shapes=[pltpu.VMEM((n, D_MODEL), F32), pltpu.VMEM((n, D_MODEL), BF16),
                        pltpu.VMEM((n, D_MODEL), F32)],
        compiler_params=_params("arbitrary"),
        name="dec_out_ffn",
    )(x, main, mo, wout, g, wup, wup, cw, cw, cb, cb, wdn, st, st, st, st)


def _rope_tables(pos):
    half = HEAD_DIM // 2
    inv = ROPE_THETA ** (-jnp.arange(half, dtype=F32) / half)
    ang = pos.astype(F32)[:, None] * inv[None, :]
    cos = jnp.cos(ang)
    sin = jnp.sin(ang)
    reps = LANES // HEAD_DIM
    cos_t = jnp.tile(jnp.concatenate([cos, cos], axis=1), (1, reps))
    sin_t = jnp.tile(jnp.concatenate([-sin, sin], axis=1), (1, reps))
    return cos_t, sin_t


def _block_diag_gates(wx, wa):
    per = MXU_DIM // HEAD_DIM
    eye = jnp.eye(per, dtype=F32)

    def bd(w):
        w4 = w.reshape(RNN_BLOCKS // per, per, HEAD_DIM, HEAD_DIM)
        return jnp.einsum('ckij,kK->ckiKj', w4, eye).reshape(RNN_BLOCKS // per, MXU_DIM, MXU_DIM)

    return jnp.concatenate([bd(wx), bd(wa)], axis=2).astype(BF16)


def kernel(x_prompt, x_sample, state_rglru_h, state_rglru_conv, state_ffn_conv, cache_swa_k, cache_swa_v, cache_mem_k, cache_mem_v, mem_prompt, norm_mix_g, norm_ffn_g, w_in_a, rnn_conv_w, rnn_conv_b, w_gate_x, b_gate_x, w_gate_a, b_gate_a, lru_param, w_in_b, q_norm_g, sinks, kv_norm_g, w_kv, k_norm_g, mem_norm_g, w_mem_kv, mem_q_norm_g, mem_k_norm_g, w_out, w_ffn_up, ffn_conv_w, ffn_conv_b, w_ffn_down):
    bsz, seq, _ = x_prompt.shape
    dbsz = x_sample.shape[0]
    depth = norm_mix_g.shape[0]
    n_a = w_in_a.shape[0]
    assert x_sample.shape[1] == 1
    assert seq % TS_MIX == 0 and seq % TS_FFN == 0 and seq % TS_KV == 0 and TS_MIX % WINDOW == 0

    n_b = w_in_b.shape[0]
    wq = w_in_b[:, :, :Q_W].astype(BF16).reshape(n_b, D_MODEL, N_KV, GROUP, HEAD_DIM)
    wq = wq.transpose(0, 1, 3, 2, 4).reshape(n_b, D_MODEL, Q_W)
    wo_main = w_out[n_a:, :Q_W].astype(BF16).reshape(n_b, N_KV, GROUP, HEAD_DIM, D_MODEL)
    wo_main = wo_main.transpose(0, 2, 1, 3, 4).reshape(n_b, Q_W, D_MODEL)
    bd =(jnp.kron(jnp.eye(MXU_DIM // HEAD_DIM, dtype=F32),
                   jnp.ones((HEAD_DIM, HEAD_DIM), F32)) / HEAD_DIM).astype(BF16)
    head_cols = (jnp.arange(MEM_W)[:, None] // HEAD_DIM == jnp.arange(LANES)[None, :])
    e_mat = head_cols.astype(BF16)
    et_mat = head_cols.T.astype(BF16)

    row = lambda v: v.reshape(1, -1)
    tile_row = lambda v, n: jnp.tile(v, n).reshape(1, -1)
    w_in_a_b = w_in_a.astype(BF16)
    w_in_b_b = jnp.concatenate([wq, w_in_b[:, :, Q_W:].astype(BF16)], axis=2)
    w_out_b = w_out.astype(BF16)
    w_out_perm_b = jnp.concatenate([wo_main, w_out_b[n_a:, Q_W:]], axis=1)
    w_kv_b = w_kv.astype(BF16)
    w_mem_b = w_mem_kv.astype(BF16)
    wup_b = w_ffn_up.astype(BF16)
    wdn_b = w_ffn_down.astype(BF16)
    fcw = ffn_conv_w
    fcb = ffn_conv_b.reshape(depth, 1, 2 * D_FF)
    wg_b = [_block_diag_gates(w_gate_x[l], w_gate_a[l]) for l in range(n_a)]
    sink_gk = sinks.reshape(-1, N_KV, GROUP).transpose(0, 2, 1)

    cos_p, sin_p = _rope_tables(jnp.arange(seq, dtype=jnp.int32))
    pos_s = PAST_LEN + jnp.zeros((dbsz,), jnp.int32)
    cos_s, sin_s = _rope_tables(pos_s)

    mem2d = mem_prompt.reshape(bsz * N_MEM, D_MODEL)
    pmk, pmv = _mem_kv(mem2d, mem_norm_g.reshape(depth, 1, D_MODEL), w_mem_b,
                       jnp.tile(mem_k_norm_g, (1, MEM_HEADS)).reshape(depth, 1, MEM_W), bd)
    pmk4 = pmk.reshape(depth, bsz, N_MEM, MEM_W)
    pmv4 = pmv.reshape(depth, bsz, N_MEM, MEM_W)

    x = x_prompt.reshape(bsz * seq, D_MODEL)
    zeros_rc = jnp.zeros((bsz, SUBLANES, D_RNN), F32)
    zeros_h = jnp.zeros((bsz, 1, D_RNN), F32)
    p_h, p_rc, p_fc = [], [], []
    ksh = vsh = None
    for l in range(depth):
        mqg = tile_row(mem_q_norm_g[l], MEM_HEADS)
        if l < n_a:
            x, hl, rct = _mixer_a(
                x, bsz, row(norm_mix_g[l]), w_in_a_b[l], rnn_conv_w[l], row(rnn_conv_b[l]),
                wg_b[l], row(b_gate_x[l]), row(b_gate_a[l]), row(lru_param[l]), mqg, bd,
                pmk4[l], pmv4[l], w_out_b[l], zeros_rc, zeros_h)
            p_h.append(hl.reshape(bsz, D_RNN))
            p_rc.append(rct[:, SUBLANES - (CONV_A - 1):])
        else:
            j = l - n_a
            x = _mixer_b(
                x, bsz, sink_gk[j].reshape(-1), row(norm_mix_g[l]), w_in_b_b[j],
                tile_row(q_norm_g[j], N_Q), cos_p, sin_p, ksh, vsh, mqg, bd,
                pmk4[l], pmv4[l], w_out_perm_b[j])
        x, ut = _ffn(x, bsz, row(norm_ffn_g[l]), wup_b[l], fcw[l], fcb[l], wdn_b[l])
        p_fc.append(ut[:, SUBLANES - (CONV_F - 1):])
        if l == n_a - 1:
            ksh, vsh = _shared_kv(x, row(kv_norm_g), w_kv_b, tile_row(k_norm_g, N_KV), bd,
                                  cos_p, sin_p, TS_KV)
    y_prompt = x.reshape(bsz, seq, D_MODEL)
    keep = min(WINDOW, seq)
    p_k = ksh.reshape(bsz, seq, N_KV, HEAD_DIM)[:, seq - keep:]
    p_v = vsh.reshape(bsz, seq, N_KV, HEAD_DIM)[:, seq - keep:]
    p_mem_k = pmk.reshape(depth, bsz, N_MEM, MEM_HEADS, HEAD_DIM)
    p_mem_v = pmv.reshape(depth, bsz, N_MEM, MEM_HEADS, HEAD_DIM)

    xs = x_sample.reshape(dbsz, D_MODEL)
    cmk = cache_mem_k.reshape(depth, dbsz, N_MEM, MEM_W)
    cmv = cache_mem_v.reshape(depth, dbsz, N_MEM, MEM_W)
    wb = cache_swa_k.shape[1]
    ckb = cache_swa_k.reshape(dbsz, wb, KV_W)
    cvb = cache_swa_v.reshape(dbsz, wb, KV_W)
    s_h, s_rc, s_fc = [], [], []
    kn = vn = None
    for l in range(depth):
        mqg = tile_row(mem_q_norm_g[l], MEM_HEADS)
        if l < n_a:
            main, qn, hnew, xrpre = _dec_in_a(
                xs, row(norm_mix_g[l]), w_in_a_b[l], rnn_conv_w[l], row(rnn_conv_b[l]),
                wg_b[l], row(b_gate_x[l]), row(b_gate_a[l]), row(lru_param[l]), mqg, bd,
                state_rglru_conv[l].transpose(1, 0, 2), state_rglru_h[l])
            s_h.append(hnew)
            s_rc.append(jnp.concatenate([state_rglru_conv[l][:, 1:], xrpre[:, None, :]], axis=1))
            wo = w_out_b[l]
        else:
            j = l - n_a
            q, qn = _dec_in_b(xs, row(norm_mix_g[l]), w_in_b_b[j], tile_row(q_norm_g[j], N_Q),
                              cos_s, sin_s, mqg, bd)
            sink_rows = jnp.zeros((SUBLANES, LANES), F32).at[:GROUP, :N_KV].set(sink_gk[j])
            main = _dec_swa(q, ckb, cvb, kn, vn, sink_rows, e_mat, et_mat)
            wo = w_out_perm_b[j]
        mo = _dec_mem_attn(qn, cmk[l], cmv[l], e_mat, et_mat)
        xs, ug, uv = _dec_out_ffn(xs, main, mo, wo, row(norm_ffn_g[l]), wup_b[l], fcw[l], fcb[l],
                                  wdn_b[l], state_ffn_conv[l].transpose(1, 0, 2))
        unew = jnp.concatenate([ug, uv], axis=1)
        s_fc.append(jnp.concatenate([state_ffn_conv[l][:, 1:], unew[:, None, :]], axis=1))
        if l == n_a - 1:
            kn, vn = _shared_kv(xs, row(kv_norm_g), w_kv_b, tile_row(k_norm_g, N_KV), bd,
                                cos_s, sin_s, dbsz)
    y_sample = xs.reshape(dbsz, 1, D_MODEL)
    s_k = kn.reshape(dbsz, 1, N_KV, HEAD_DIM)
    s_v = vn.reshape(dbsz, 1, N_KV, HEAD_DIM)

    return (y_prompt, y_sample, jnp.stack(p_h), jnp.stack(p_rc), jnp.stack(p_fc), p_k, p_v,
            p_mem_k, p_mem_v, jnp.stack(s_h), jnp.stack(s_rc), jnp.stack(s_fc), s_k, s_v)
```

```python
import math

import jax
import jax.numpy as jnp
from jax import lax
from jax.experimental import pallas as pl
from jax.experimental.pallas import tpu as pltpu

F32 = jnp.float32
BF16 = jnp.bfloat16

D_MODEL = 1024
HEAD_DIM = 64
MEM_HEADS = 4
MEM_W = MEM_HEADS * HEAD_DIM
N_MEM = 256
D_RNN = D_MODEL - MEM_W
RNN_BLOCKS = D_RNN // HEAD_DIM
CONV_A = 4
LRU_C = 8.0
N_Q = D_RNN // HEAD_DIM
N_KV = 4
GROUP = N_Q // N_KV
Q_W = N_Q * HEAD_DIM
KV_W = N_KV * HEAD_DIM
WINDOW = 128
ROPE_THETA = 10000.0
D_FF = 3 * D_MODEL
CONV_F = 3
EPS = 1e-6
NEG = -1e30
ATT_SCALE = HEAD_DIM ** -0.5
PAST_LEN = 8192

SUBLANES = 8
LANES = 128
MXU_DIM = 256
VMEM_LIMIT_BYTES = 56 * 1024 * 1024

TS_MIX = 512
TS_FFN = 512
TS_KV = 512
FF_CHUNK = 512
N_FF_CHUNKS = D_FF // FF_CHUNK
DEC_SEQ_BLOCK = 8


def _mm(a, b):
    return jnp.dot(a.astype(BF16), b, preferred_element_type=F32)


def _mm_nt(a, b):
    return lax.dot_general(a.astype(BF16), b, (((1,), (1,)), ((), ())),
                           preferred_element_type=F32)


def _mm_split(a, b):
    hi = a.astype(BF16)
    lo = (a - hi.astype(F32)).astype(BF16)
    return (jnp.dot(hi, b, preferred_element_type=F32)
            + jnp.dot(lo, b, preferred_element_type=F32))


def _rmsnorm(x, g):
    ms = jnp.mean(x * x, axis=-1, keepdims=True)
    return x * lax.rsqrt(ms + EPS) * g


def _head_rmsnorm(x, bd, g):
    parts = []
    for c in range(x.shape[1] // MXU_DIM):
        xc = x[:, c * MXU_DIM:(c + 1) * MXU_DIM]
        ms = _mm_split(xc * xc, bd)
        parts.append(xc * lax.rsqrt(ms + EPS))
    y = parts[0] if len(parts) == 1 else jnp.concatenate(parts, axis=1)
    return y * g


def _tile_lanes(t, width):
    reps = width // t.shape[1]
    return t if reps == 1 else jnp.concatenate([t] * reps, axis=1)


def _rope(x, cos_t, sin_t):
    w = x.shape[1]
    lane = lax.broadcasted_iota(jnp.int32, x.shape, 1)
    first = (lane % HEAD_DIM) < (HEAD_DIM // 2)
    swapped = jnp.where(first, pltpu.roll(x, w - HEAD_DIM // 2, 1),
                        pltpu.roll(x, HEAD_DIM // 2, 1))
    return x * _tile_lanes(cos_t, w) + swapped * _tile_lanes(sin_t, w)


def _gelu(x):
    c = math.sqrt(2.0 / math.pi)
    return x * (0.5 * (1.0 + jnp.tanh(c * (x + 0.044715 * (x * x * x)))))


def _log_sigmoid(x):
    return jnp.minimum(x, 0.0) - jnp.log1p(jnp.exp(-jnp.abs(x)))


def _slab_conv(x, work_ref, work0, tail_ref, tail0, w_ref, b_ref, col0):
    ts = x.shape[0]
    k = w_ref.shape[0]
    outs = []
    for s in range(x.shape[1] // LANES):
        xs = x[:, s * LANES:(s + 1) * LANES]
        lanes = slice(col0 + s * LANES, col0 + (s + 1) * LANES)
        buf = work_ref.at[work0 + s]
        buf[0:SUBLANES, :] = tail_ref[tail0 + s]
        buf[SUBLANES:SUBLANES + ts, :] = xs
        tail_ref[tail0 + s] = xs[ts - SUBLANES:]
        acc = buf[SUBLANES - (k - 1):SUBLANES - (k - 1) + ts, :] * w_ref[0:1, lanes]
        for j in range(1, k - 1):
            off = SUBLANES - (k - 1 - j)
            acc = acc + buf[off:off + ts, :] * w_ref[j:j + 1, lanes]
        acc = acc + xs * w_ref[k - 1:k, lanes]
        outs.append(acc + b_ref[:, lanes])
    return outs


def _sqrt_pos(x):
    return jnp.where(x > 0.0, x * lax.rsqrt(x), 0.0)


def _lru_coeffs(xr, wg_ref, bgx, bga, logsig):
    xb = xr.astype(BF16)
    gxs, gas = [], []
    for c in range(D_RNN // MXU_DIM):
        gg = jnp.dot(xb[:, c * MXU_DIM:(c + 1) * MXU_DIM], wg_ref[c],
                     preferred_element_type=F32)
        gxs.append(gg[:, :MXU_DIM])
        gas.append(gg[:, MXU_DIM:])
    gx = jax.nn.sigmoid(jnp.concatenate(gxs, axis=1) + bgx)
    ga = jax.nn.sigmoid(jnp.concatenate(gas, axis=1) + bga)
    log_a = LRU_C * ga * logsig
    a = jnp.exp(log_a)
    mult = _sqrt_pos(-jnp.tanh(log_a) * (a * a + 1.0))
    return a, mult * gx * xr


def _lru_scan(a, b, a_ref, b_ref, hc_ref):
    ts = a.shape[0]
    outs = []
    for s in range(a.shape[1] // LANES):
        lanes = slice(s * LANES, (s + 1) * LANES)
        a_s, b_s = a[:, lanes], b[:, lanes]
        abuf, bbuf = a_ref.at[s], b_ref.at[s]
        d = 1
        while d < SUBLANES:
            abuf[SUBLANES:SUBLANES + ts, :] = a_s
            bbuf[SUBLANES:SUBLANES + ts, :] = b_s
            b_s = a_s * bbuf[SUBLANES - d:SUBLANES - d + ts, :] + b_s
            a_s = a_s * abuf[SUBLANES - d:SUBLANES - d + ts, :]
            d *= 2
        h = hc_ref[:, lanes]
        hs = []
        for q in range(ts // SUBLANES):
            rows = slice(q * SUBLANES, (q + 1) * SUBLANES)
            h = a_s[rows] * h + b_s[rows]
            hs.append(h)
        hc_ref[:, lanes] = jnp.broadcast_to(h[SUBLANES - 1:], (SUBLANES, LANES))
        outs.append(jnp.concatenate(hs, axis=0))
    return jnp.concatenate(outs, axis=1)


def _head_mask(shape, h):
    lane = lax.broadcasted_iota(jnp.int32, shape, 1)
    return (lane >= h * HEAD_DIM) & (lane < (h + 1) * HEAD_DIM)


def _head_masked_stack(blocks, n_heads):
    parts = []
    for h in range(n_heads):
        for blk in blocks:
            parts.append(jnp.where(_head_mask(blk.shape, h), blk, 0.0).astype(BF16))
    return jnp.concatenate(parts, axis=0)


def _mem_attention(qn, kcat, vcat):
    s = _mm_nt(qn, kcat)
    parts = []
    for h in range(MEM_HEADS):
        sh = s[:, h * N_MEM:(h + 1) * N_MEM]
        m = jnp.max(sh, axis=-1, keepdims=True)
        p = jnp.exp(sh - m)
        den = jnp.sum(p, axis=-1, keepdims=True)
        parts.append((p * (1.0 / den)).astype(BF16))
    return jnp.dot(jnp.concatenate(parts, axis=1), vcat, preferred_element_type=F32)


def _mem_kv_kernel(mem_ref, g_ref, w_ref, kg_ref, bd_ref, k_ref, v_ref):
    h = _mm(_rmsnorm(mem_ref[...], g_ref[0]), w_ref[0])
    k_ref[0] = _head_rmsnorm(h[:, :MEM_W], bd_ref[...], kg_ref[0])
    v_ref[0] = h[:, MEM_W:]


def _shared_kv_kernel(x_ref, g_ref, w_ref, kg_ref, bd_ref, cos_ref, sin_ref,
                      k_ref, v_ref):
    h = _mm(_rmsnorm(x_ref[...], g_ref[...]), w_ref[...])
    k = _head_rmsnorm(h[:, :KV_W], bd_ref[...], kg_ref[...])
    k_ref[...] = _rope(k, cos_ref[...], sin_ref[...])
    v_ref[...] = h[:, KV_W:]


def _mixer_a_kernel(x_ref, g_ref, win_ref, cw_ref, cb_ref, wg_ref, bgx_ref, bga_ref,
                    lru_ref, mqg_ref, bd_ref, mk_ref, mv_ref, wout_ref, rc0_ref, h0_ref,
                    xo_ref, hlast_ref, rctail_ref,
                    conv_s, tail_s, a_s, b_s, hc_s, kcat_s, vcat_s):
    ts = x_ref.shape[0]
    n_slabs = D_RNN // LANES

    @pl.when(pl.program_id(1) == 0)
    def _():
        for s in range(n_slabs):
            tail_s[s] = rc0_ref[0, :, s * LANES:(s + 1) * LANES]
        a_s[:, 0:SUBLANES, :] = jnp.ones((n_slabs, SUBLANES, LANES), F32)
        b_s[:, 0:SUBLANES, :] = jnp.zeros((n_slabs, SUBLANES, LANES), F32)
        hc_s[...] = jnp.broadcast_to(h0_ref[0], hc_s.shape)
        kcat_s[...] = _head_masked_stack([mk_ref[0]], MEM_HEADS)
        vcat_s[...] = _head_masked_stack([mv_ref[0]], MEM_HEADS)

    x = x_ref[...]
    u = _mm(_rmsnorm(x, g_ref[...]), win_ref[...])
    gate = u[:, :D_RNN]
    xr_pre = u[:, D_RNN:2 * D_RNN]
    qm = u[:, 2 * D_RNN:]

    xr = jnp.concatenate(_slab_conv(xr_pre, conv_s, 0, tail_s, 0, cw_ref, cb_ref, 0), axis=1)
    rctail_ref[0] = xr_pre[ts - SUBLANES:]

    a, b = _lru_coeffs(xr, wg_ref, bgx_ref[...], bga_ref[...], _log_sigmoid(lru_ref[...]))
    h = _lru_scan(a, b, a_s, b_s, hc_s)
    hlast_ref[0] = h[ts - 1:ts]
    main = h * _gelu(gate)

    qn = _head_rmsnorm(qm, bd_ref[...], mqg_ref[...]) * ATT_SCALE
    mo = _mem_attention(qn, kcat_s[...], vcat_s[...])

    y = _mm(jnp.concatenate([main, mo], axis=1), wout_ref[...])
    xo_ref[...] = x + y


def _mixer_b_kernel(sink_ref, x_ref, g_ref, win_ref, qg_ref, cos_ref, sin_ref,
                    kcur_ref, kprev_ref, vcur_ref, vprev_ref,
                    mqg_ref, bd_ref, mk_ref, mv_ref, wout_ref,
                    xo_ref, kcat_s, vcat_s):
    ts = x_ref.shape[0]
    i = pl.program_id(1)

    @pl.when(i == 0)
    def _():
        kcat_s[...] = _head_masked_stack([mk_ref[0]], MEM_HEADS)
        vcat_s[...] = _head_masked_stack([mv_ref[0]], MEM_HEADS)

    x = x_ref[...]
    u = _mm(_rmsnorm(x, g_ref[...]), win_ref[...])
    q = _head_rmsnorm(u[:, :Q_W], bd_ref[...], qg_ref[...])
    q = (_rope(q, cos_ref[...], sin_ref[...]) * ATT_SCALE).astype(BF16)
    qm = u[:, Q_W:]

    row = lax.broadcasted_iota(jnp.int32, (WINDOW, 2 * WINDOW), 0)
    kj = lax.broadcasted_iota(jnp.int32, (WINDOW, 2 * WINDOW), 1) - WINDOW
    mask_inner = (kj <= row) & (kj >= row - WINDOW)
    mask_first = (kj <= row) & (kj >= jnp.maximum(row - WINDOW, jnp.where(i > 0, -WINDOW, 0)))

    kt = kcur_ref[...]
    vt = vcur_ref[...]
    mains = []
    for jb in range(ts // WINDOW):
        lo, hi = jb * WINDOW, (jb + 1) * WINDOW
        if jb == 0:
            kp, vp = kprev_ref[...], vprev_ref[...]
            mask = mask_first
        else:
            kp, vp = kt[lo - WINDOW:lo], vt[lo - WINDOW:lo]
            mask = mask_inner
        kcat = _head_masked_stack([kp, kt[lo:hi]], N_KV)
        vcat = _head_masked_stack([vp, vt[lo:hi]], N_KV)
        qs = jnp.concatenate([q[lo:hi, g * KV_W:(g + 1) * KV_W] for g in range(GROUP)], axis=0)
        s = lax.dot_general(qs, kcat, (((1,), (1,)), ((), ())), preferred_element_type=F32)
        prow = []
        for g in range(GROUP):
            pseg = []
            for kv in range(N_KV):
                seg = s[g * WINDOW:(g + 1) * WINDOW, kv * 2 * WINDOW:(kv + 1) * 2 * WINDOW]
                seg = jnp.where(mask, seg, NEG)
                sink = sink_ref[g * N_KV + kv]
                m = jnp.maximum(jnp.max(seg, axis=-1, keepdims=True), sink)
                p = jnp.exp(seg - m)
                den = jnp.sum(p, axis=-1, keepdims=True) + jnp.exp(sink - m)
                pseg.append((p * (1.0 / den)).astype(BF16))
            prow.append(jnp.concatenate(pseg, axis=1))
        o = jnp.dot(jnp.concatenate(prow, axis=0), vcat, preferred_element_type=F32)
        mains.append(jnp.concatenate([o[g * WINDOW:(g + 1) * WINDOW] for g in range(GROUP)], axis=1))
    main = mains[0] if len(mains) == 1 else jnp.concatenate(mains, axis=0)

    qn = _head_rmsnorm(qm, bd_ref[...], mqg_ref[...]) * ATT_SCALE
    mo = _mem_attention(qn, kcat_s[...], vcat_s[...])

    y = _mm(jnp.concatenate([main, mo], axis=1), wout_ref[...])
    xo_ref[...] = x + y


def _ffn_kernel(x_ref, g_ref, wup_ref, cw_ref, cb_ref, wdn_ref,
                xo_ref, utail_ref, conv_s, tail_s):
    ts = x_ref.shape[0]
    slabs = FF_CHUNK // LANES

    @pl.when(pl.program_id(1) == 0)
    def _():
        tail_s[...] = jnp.zeros_like(tail_s)

    x = x_ref[...]
    hn = _rmsnorm(x, g_ref[...]).astype(BF16)
    acc = jnp.zeros((ts, D_MODEL), F32)
    for j in range(N_FF_CHUNKS):
        halves = []
        for half, c in enumerate((j, N_FF_CHUNKS + j)):
            cols = slice(c * FF_CHUNK, (c + 1) * FF_CHUNK)
            u = jnp.dot(hn, wup_ref[:, cols], preferred_element_type=F32)
            work0 = ((j % 2) * 2 + half) * slabs
            halves.append(_slab_conv(u, conv_s, work0, tail_s, c * slabs, cw_ref, cb_ref,
                                     c * FF_CHUNK))
            utail_ref[0, :, cols] = u[ts - SUBLANES:]
        act = jnp.concatenate([_gelu(cg) * cv for cg, cv in zip(*halves)], axis=1).astype(BF16)
        acc = acc + jnp.dot(act, wdn_ref[j * FF_CHUNK:(j + 1) * FF_CHUNK, :],
                            preferred_element_type=F32)
    xo_ref[...] = x + acc


def _dec_in_a_kernel(x_ref, g_ref, win_ref, cw_ref, cb_ref, wg_ref, bgx_ref, bga_ref,
                     lru_ref, mqg_ref, bd_ref, b0_ref, b1_ref, b2_ref, h0_ref,
                     main_ref, qn_ref, hnew_ref, xrpre_ref):
    u = _mm(_rmsnorm(x_ref[...], g_ref[...]), win_ref[...])
    gate = u[:, :D_RNN]
    xr_pre = u[:, D_RNN:2 * D_RNN]
    qm = u[:, 2 * D_RNN:]
    xr = b0_ref[...] * cw_ref[0:1, :]
    xr = xr + b1_ref[...] * cw_ref[1:2, :]
    xr = xr + b2_ref[...] * cw_ref[2:3, :]
    xr = xr + xr_pre * cw_ref[3:4, :]
    xr = xr + cb_ref[...]
    a, b = _lru_coeffs(xr, wg_ref, bgx_ref[...], bga_ref[...], _log_sigmoid(lru_ref[...]))
    h = a * h0_ref[...] + b
    main_ref[...] = h * _gelu(gate)
    qn_ref[...] = _head_rmsnorm(qm, bd_ref[...], mqg_ref[...]) * ATT_SCALE
    hnew_ref[...] = h
    xrpre_ref[...] = xr_pre


def _dec_in_b_kernel(x_ref, g_ref, win_ref, qg_ref, cos_ref, sin_ref, mqg_ref, bd_ref,
                     q_ref, qn_ref):
    u = _mm(_rmsnorm(x_ref[...], g_ref[...]), win_ref[...])
    q = _head_rmsnorm(u[:, :Q_W], bd_ref[...], qg_ref[...])
    q_ref[...] = _rope(q, cos_ref[...], sin_ref[...]) * ATT_SCALE
    qn_ref[...] = _head_rmsnorm(u[:, Q_W:], bd_ref[...], mqg_ref[...]) * ATT_SCALE


DEC_HEAD_ROWS = 16


def _own_head_lanes(n_heads, width):
    row = lax.broadcasted_iota(jnp.int32, (DEC_HEAD_ROWS, width), 0)
    lane = lax.broadcasted_iota(jnp.int32, (DEC_HEAD_ROWS, width), 1)
    start = (row % (width // HEAD_DIM)) * HEAD_DIM
    return (lane >= start) & (lane < start + HEAD_DIM) & (row < n_heads)


def _dec_mem_attn_kernel(q_ref, kt_ref, vt_ref, o_ref):
    own = _own_head_lanes(MEM_HEADS, MEM_W)
    for s in range(q_ref.shape[0]):
        q_rows = jnp.broadcast_to(q_ref[s:s + 1, :], (DEC_HEAD_ROWS, MEM_W))
        qbd = jnp.where(own, q_rows, 0.0).astype(BF16)
        sc = jnp.dot(qbd, kt_ref[s].astype(BF16), preferred_element_type=F32)
        m = jnp.max(sc, axis=-1, keepdims=True)
        p = jnp.exp(sc - m)
        den = jnp.sum(p, axis=-1, keepdims=True)
        pn = (p * (1.0 / den)).astype(BF16)
        o_all = lax.dot_general(pn, vt_ref[s].astype(BF16), (((1,), (1,)), ((), ())),
                                preferred_element_type=F32)
        o_ref[s:s + 1, :] = jnp.sum(jnp.where(own, o_all, 0.0), axis=0, keepdims=True)


def _dec_swa_kernel(q_ref, kt_ref, vt_ref, kn_ref, vn_ref, sink_ref, o_ref):
    n_heads = GROUP * N_KV
    own = _own_head_lanes(n_heads, KV_W)
    grp = lax.broadcasted_iota(jnp.int32, (DEC_HEAD_ROWS, KV_W), 0) // N_KV
    sink = sink_ref[:, 0:1]
    for s in range(q_ref.shape[0]):
        q_rows = jnp.zeros((DEC_HEAD_ROWS, KV_W), F32)
        for g in range(GROUP):
            qg = jnp.broadcast_to(q_ref[s:s + 1, g * KV_W:(g + 1) * KV_W], (DEC_HEAD_ROWS, KV_W))
            q_rows = jnp.where(grp == g, qg, q_rows)
        qbd = jnp.where(own, q_rows, 0.0)
        s_buf = jnp.dot(qbd.astype(BF16), kt_ref[s].astype(BF16), preferred_element_type=F32)
        s_new = jnp.sum(qbd * kn_ref[s:s + 1, :], axis=-1, keepdims=True)
        m = jnp.maximum(jnp.maximum(jnp.max(s_buf, axis=-1, keepdims=True), s_new), sink)
        p_buf = jnp.exp(s_buf - m)
        p_new = jnp.exp(s_new - m)
        den = jnp.sum(p_buf, axis=-1, keepdims=True) + p_new + jnp.exp(sink - m)
        r = 1.0 / den
        o_all = lax.dot_general((p_buf * r).astype(BF16), vt_ref[s].astype(BF16),
                                (((1,), (1,)), ((), ())), preferred_element_type=F32)
        o_all = jnp.where(own, o_all + (p_new * r) * vn_ref[s:s + 1, :], 0.0)
        o_sum = o_all + pltpu.roll(o_all, 1, 0)
        o_sum = o_sum + pltpu.roll(o_sum, 2, 0)
        for g in range(GROUP):
            last = (g + 1) * N_KV - 1
            o_ref[s:s + 1, g * KV_W:(g + 1) * KV_W] = o_sum[last:last + 1, :]


def _dec_out_ffn_kernel(x_ref, main_ref, mo_ref, wout_ref, g_ref,
                        wug_ref, wuv_ref, cwg_ref, cwv_ref, cbg_ref, cbv_ref, wdn_ref,
                        sg0_ref, sg1_ref, sv0_ref, sv1_ref,
                        xo_ref, ug_ref, uv_ref,
                        xmid_s, hn_s, acc_s):
    j = pl.program_id(0)

    @pl.when(j == 0)
    def _():
        y = _mm(jnp.concatenate([main_ref[...], mo_ref[...]], axis=1), wout_ref[...])
        xmid = x_ref[...] + y
        xmid_s[...] = xmid
        hn_s[...] = _rmsnorm(xmid, g_ref[...]).astype(BF16)
        acc_s[...] = jnp.zeros_like(acc_s)

    hn = hn_s[...]
    ug = jnp.dot(hn, wug_ref[...], preferred_element_type=F32)
    uv = jnp.dot(hn, wuv_ref[...], preferred_element_type=F32)
    ug_ref[...] = ug
    uv_ref[...] = uv
    cg = (sg0_ref[...] * cwg_ref[0:1, :] + sg1_ref[...] * cwg_ref[1:2, :]
          + ug * cwg_ref[2:3, :] + cbg_ref[...])
    cv = (sv0_ref[...] * cwv_ref[0:1, :] + sv1_ref[...] * cwv_ref[1:2, :]
          + uv * cwv_ref[2:3, :] + cbv_ref[...])
    act = (_gelu(cg) * cv).astype(BF16)
    acc_s[...] += jnp.dot(act, wdn_ref[...], preferred_element_type=F32)

    @pl.when(j == pl.num_programs(0) - 1)
    def _():
        xo_ref[...] = xmid_s[...] + acc_s[...]


def _const_spec(shape):
    nd = len(shape)
    return pl.BlockSpec(shape, lambda *_: (0,) * nd)


def _params(*sem):
    return pltpu.CompilerParams(dimension_semantics=sem, vmem_limit_bytes=VMEM_LIMIT_BYTES)


def _mem_kv(mem2d, g, w, kg, bd):
    depth = w.shape[0]
    rows = mem2d.shape[0]
    out = jax.ShapeDtypeStruct((depth, rows, MEM_W), F32)
    return pl.pallas_call(
        _mem_kv_kernel,
        grid=(depth,),
        in_specs=[
            _const_spec(mem2d.shape),
            pl.BlockSpec((1, 1, D_MODEL), lambda l: (l, 0, 0)),
            pl.BlockSpec((1, D_MODEL, 2 * MEM_W), lambda l: (l, 0, 0)),
            pl.BlockSpec((1, 1, MEM_W), lambda l: (l, 0, 0)),
            _const_spec(bd.shape),
        ],
        out_specs=[pl.BlockSpec((1, rows, MEM_W), lambda l: (l, 0, 0))] * 2,
        out_shape=[out, out],
        compiler_params=_params("arbitrary"),
        name="mem_kv",
    )(mem2d, g, w, kg, bd)


def _shared_kv(x2d, g, w, kg, bd, cos_t, sin_t, ts):
    rows = x2d.shape[0]
    tab_blocks = cos_t.shape[0] // ts
    out = jax.ShapeDtypeStruct((rows, KV_W), F32)
    return pl.pallas_call(
        _shared_kv_kernel,
        grid=(rows // ts,),
        in_specs=[
            pl.BlockSpec((ts, D_MODEL), lambda i: (i, 0)),
            _const_spec(g.shape), _const_spec(w.shape), _const_spec(kg.shape),
            _const_spec(bd.shape),
            pl.BlockSpec((ts, LANES), lambda i: (i % tab_blocks, 0)),
            pl.BlockSpec((ts, LANES), lambda i: (i % tab_blocks, 0)),
        ],
        out_specs=[pl.BlockSpec((ts, KV_W), lambda i: (i, 0))] * 2,
        out_shape=[out, out],
        compiler_params=_params("arbitrary"),
        name="shared_kv",
    )(x2d, g, w, kg, bd, cos_t, sin_t)


def _mixer_a(x2d, bsz, g, win, cw, cb, wg, bgx, bga, lru, mqg, bd, mk, mv, wout, rc0, h0):
    rows = x2d.shape[0]
    ts = TS_MIX
    nt = rows // bsz // ts
    consts = (g, win, cw, cb, wg, bgx, bga, lru, mqg, bd)
    per_b3 = lambda b, i: (b, 0, 0)
    return pl.pallas_call(
        _mixer_a_kernel,
        grid=(bsz, nt),
        in_specs=[pl.BlockSpec((ts, D_MODEL), lambda b, i: (b * nt + i, 0))]
        + [_const_spec(c.shape) for c in consts]
        + [pl.BlockSpec((1, N_MEM, MEM_W), per_b3)] * 2
        + [_const_spec(wout.shape),
           pl.BlockSpec((1, SUBLANES, D_RNN), per_b3),
           pl.BlockSpec((1, 1, D_RNN), per_b3)],
        out_specs=[
            pl.BlockSpec((ts, D_MODEL), lambda b, i: (b * nt + i, 0)),
            pl.BlockSpec((1, 1, D_RNN), per_b3),
            pl.BlockSpec((1, SUBLANES, D_RNN), per_b3),
        ],
        out_shape=[
            jax.ShapeDtypeStruct((rows, D_MODEL), F32),
            jax.ShapeDtypeStruct((bsz, 1, D_RNN), F32),
            jax.ShapeDtypeStruct((bsz, SUBLANES, D_RNN), F32),
        ],
        scratch_shapes=[
            pltpu.VMEM((D_RNN // LANES, SUBLANES + ts, LANES), F32),
            pltpu.VMEM((D_RNN // LANES, SUBLANES, LANES), F32),
            pltpu.VMEM((D_RNN // LANES, SUBLANES + ts, LANES), F32),
            pltpu.VMEM((D_RNN // LANES, SUBLANES + ts, LANES), F32),
            pltpu.VMEM((SUBLANES, D_RNN), F32),
            pltpu.VMEM((MEM_HEADS * N_MEM, MEM_W), BF16),
            pltpu.VMEM((MEM_HEADS * N_MEM, MEM_W), BF16),
        ],
        compiler_params=_params("arbitrary", "arbitrary"),
        name="mixer_a",
    )(x2d, *consts, mk, mv, wout, rc0, h0)


def _mixer_b(x2d, bsz, sink_tab, g, win, qg, cos_t, sin_t, ksh, vsh, mqg, bd, mk, mv, wout):
    rows = x2d.shape[0]
    ts = TS_MIX
    nt = rows // bsz // ts
    wpt = ts // WINDOW
    cur = lambda b, i: (b * nt + i, 0)
    prev = lambda b, i: (jnp.maximum((b * nt + i) * wpt - 1, 0), 0)
    per_b3 = lambda b, i: (b, 0, 0)
    return pl.pallas_call(
        _mixer_b_kernel,
        grid=(bsz, nt),
        in_specs=[
            pl.BlockSpec(memory_space=pltpu.SMEM),
            pl.BlockSpec((ts, D_MODEL), cur),
            _const_spec(g.shape), _const_spec(win.shape), _const_spec(qg.shape),
            pl.BlockSpec((ts, LANES), lambda b, i: (i, 0)),
            pl.BlockSpec((ts, LANES), lambda b, i: (i, 0)),
            pl.BlockSpec((ts, KV_W), cur), pl.BlockSpec((WINDOW, KV_W), prev),
            pl.BlockSpec((ts, KV_W), cur), pl.BlockSpec((WINDOW, KV_W), prev),
            _const_spec(mqg.shape), _const_spec(bd.shape),
            pl.BlockSpec((1, N_MEM, MEM_W), per_b3), pl.BlockSpec((1, N_MEM, MEM_W), per_b3),
            _const_spec(wout.shape),
        ],
        out_specs=pl.BlockSpec((ts, D_MODEL), cur),
        out_shape=jax.ShapeDtypeStruct((rows, D_MODEL), F32),
        scratch_shapes=[
            pltpu.VMEM((MEM_HEADS * N_MEM, MEM_W), BF16),
            pltpu.VMEM((MEM_HEADS * N_MEM, MEM_W), BF16),
        ],
        compiler_params=_params("arbitrary", "arbitrary"),
        name="mixer_b",
    )(sink_tab, x2d, g, win, qg, cos_t, sin_t, ksh, ksh, vsh, vsh, mqg, bd, mk, mv, wout)


def _ffn(x2d, bsz, g, wup, cw, cb, wdn):
    rows = x2d.shape[0]
    ts = TS_FFN
    nt = rows // bsz // ts
    return pl.pallas_call(
        _ffn_kernel,
        grid=(bsz, nt),
        in_specs=[pl.BlockSpec((ts, D_MODEL), lambda b, i: (b * nt + i, 0)),
                  _const_spec(g.shape), _const_spec(wup.shape), _const_spec(cw.shape),
                  _const_spec(cb.shape), _const_spec(wdn.shape)],
        out_specs=[
            pl.BlockSpec((ts, D_MODEL), lambda b, i: (b * nt + i, 0)),
            pl.BlockSpec((1, SUBLANES, 2 * D_FF), lambda b, i: (b, 0, 0)),
        ],
        out_shape=[
            jax.ShapeDtypeStruct((rows, D_MODEL), F32),
            jax.ShapeDtypeStruct((bsz, SUBLANES, 2 * D_FF), F32),
        ],
        scratch_shapes=[pltpu.VMEM((4 * FF_CHUNK // LANES, SUBLANES + ts, LANES), F32),
                        pltpu.VMEM((2 * D_FF // LANES, SUBLANES, LANES), F32)],
        compiler_params=_params("arbitrary", "arbitrary"),
        name="ffn",
    )(x2d, g, wup, cw, cb, wdn)


def _dec_in_a(x, g, win, cw, cb, wg, bgx, bga, lru, mqg, bd, rc, h0):
    n = x.shape[0]
    consts = (x, g, win, cw, cb, wg, bgx, bga, lru, mqg, bd)
    buf = lambda j: pl.BlockSpec((None, n, D_RNN), lambda i: (j, 0, 0))
    return pl.pallas_call(
        _dec_in_a_kernel,
        grid=(1,),
        in_specs=[_const_spec(c.shape) for c in consts]
        + [buf(0), buf(1), buf(2), _const_spec(h0.shape)],
        out_specs=[_const_spec((n, D_RNN)), _const_spec((n, MEM_W)),
                   _const_spec((n, D_RNN)), _const_spec((n, D_RNN))],
        out_shape=[jax.ShapeDtypeStruct((n, D_RNN), F32), jax.ShapeDtypeStruct((n, MEM_W), F32),
                   jax.ShapeDtypeStruct((n, D_RNN), F32), jax.ShapeDtypeStruct((n, D_RNN), F32)],
        compiler_params=_params("arbitrary"),
        name="dec_in_a",
    )(*consts, rc, rc, rc, h0)


def _dec_in_b(x, g, win, qg, cos_t, sin_t, mqg, bd):
    n = x.shape[0]
    args = (x, g, win, qg, cos_t, sin_t, mqg, bd)
    return pl.pallas_call(
        _dec_in_b_kernel,
        grid=(1,),
        in_specs=[_const_spec(a.shape) for a in args],
        out_specs=[_const_spec((n, Q_W)), _const_spec((n, MEM_W))],
        out_shape=[jax.ShapeDtypeStruct((n, Q_W), F32), jax.ShapeDtypeStruct((n, MEM_W), F32)],
        compiler_params=_params("arbitrary"),
        name="dec_in_b",
    )(*args)


def _dec_mem_attn(qn, ckt, cvt, layer):
    n = qn.shape[0]
    sb = DEC_SEQ_BLOCK
    return pl.pallas_call(
        _dec_mem_attn_kernel,
        grid=(n // sb,),
        in_specs=[pl.BlockSpec((sb, MEM_W), lambda i: (i, 0)),
                  pl.BlockSpec((None, sb, MEM_W, N_MEM), lambda i: (layer, i, 0, 0)),
                  pl.BlockSpec((None, sb, MEM_W, N_MEM), lambda i: (layer, i, 0, 0))],
        out_specs=pl.BlockSpec((sb, MEM_W), lambda i: (i, 0)),
        out_shape=jax.ShapeDtypeStruct((n, MEM_W), F32),
        compiler_params=_params("arbitrary"),
        name="dec_mem_attn",
    )(qn, ckt, cvt)


def _dec_swa(q, kbt, vbt, kn, vn, sink_rows):
    n = q.shape[0]
    wb = kbt.shape[2]
    sb = DEC_SEQ_BLOCK
    assert wb <= WINDOW
    return pl.pallas_call(
        _dec_swa_kernel,
        grid=(n // sb,),
        in_specs=[pl.BlockSpec((sb, Q_W), lambda i: (i, 0)),
                  pl.BlockSpec((sb, KV_W, wb), lambda i: (i, 0, 0)),
                  pl.BlockSpec((sb, KV_W, wb), lambda i: (i, 0, 0)),
                  pl.BlockSpec((sb, KV_W), lambda i: (i, 0)),
                  pl.BlockSpec((sb, KV_W), lambda i: (i, 0)),
                  _const_spec(sink_rows.shape)],
        out_specs=pl.BlockSpec((sb, Q_W), lambda i: (i, 0)),
        out_shape=jax.ShapeDtypeStruct((n, Q_W), F32),
        compiler_params=_params("arbitrary"),
        name="dec_swa",
    )(q, kbt, vbt, kn, vn, sink_rows)


def _dec_out_ffn(x, main, mo, wout, g, wup, cw, cb, wdn, st):
    n = x.shape[0]
    nch = N_FF_CHUNKS
    lo = lambda j: (0, j)
    hi = lambda j: (0, nch + j)
    state = lambda r, off: pl.BlockSpec((None, n, FF_CHUNK), lambda j: (r, 0, off + j))
    return pl.pallas_call(
        _dec_out_ffn_kernel,
        grid=(nch,),
        in_specs=[_const_spec(x.shape), _const_spec(main.shape), _const_spec(mo.shape),
                  _const_spec(wout.shape), _const_spec(g.shape),
                  pl.BlockSpec((D_MODEL, FF_CHUNK), lo), pl.BlockSpec((D_MODEL, FF_CHUNK), hi),
                  pl.BlockSpec((CONV_F, FF_CHUNK), lo), pl.BlockSpec((CONV_F, FF_CHUNK), hi),
                  pl.BlockSpec((1, FF_CHUNK), lo), pl.BlockSpec((1, FF_CHUNK), hi),
                  pl.BlockSpec((FF_CHUNK, D_MODEL), lambda j: (j, 0)),
                  state(0, 0), state(1, 0), state(0, nch), state(1, nch)],
        out_specs=[_const_spec((n, D_MODEL)),
                   pl.BlockSpec((n, FF_CHUNK), lambda j: (0, j)),
                   pl.BlockSpec((n, FF_CHUNK), lambda j: (0, j))],
        out_shape=[jax.ShapeDtypeStruct((n, D_MODEL), F32),
                   jax.ShapeDtypeStruct((n, D_FF), F32),
                   jax.ShapeDtypeStruct((n, D_FF), F32)],
        scratch_shapes=[pltpu.VMEM((n, D_MODEL), F32), pltpu.VMEM((n, D_MODEL), BF16),
                        pltpu.VMEM((n, D_MODEL), F32)],
        compiler_params=_params("arbitrary"),
        name="dec_out_ffn",
    )(x, main, mo, wout, g, wup, wup, cw, cw, cb, cb, wdn, st, st, st, st)


def _rope_tables(pos):
    half = HEAD_DIM // 2
    inv = ROPE_THETA ** (-jnp.arange(half, dtype=F32) / half)
    ang = pos.astype(F32)[:, None] * inv[None, :]
    cos = jnp.cos(ang)
    sin = jnp.sin(ang)
    reps = LANES // HEAD_DIM
    cos_t = jnp.tile(jnp.concatenate([cos, cos], axis=1), (1, reps))
    sin_t = jnp.tile(jnp.concatenate([-sin, sin], axis=1), (1, reps))
    return cos_t, sin_t


def _block_diag_gates(wx, wa):
    per = MXU_DIM // HEAD_DIM
    eye = jnp.eye(per, dtype=F32)

    def bd(w):
        w4 = w.reshape(RNN_BLOCKS // per, per, HEAD_DIM, HEAD_DIM)
        return jnp.einsum('ckij,kK->ckiKj', w4, eye).reshape(RNN_BLOCKS // per, MXU_DIM, MXU_DIM)

    return jnp.concatenate([bd(wx), bd(wa)], axis=2).astype(BF16)


def kernel(x_prompt, x_sample, state_rglru_h, state_rglru_conv, state_ffn_conv, cache_swa_k, cache_swa_v, cache_mem_k, cache_mem_v, mem_prompt, norm_mix_g, norm_ffn_g, w_in_a, rnn_conv_w, rnn_conv_b, w_gate_x, b_gate_x, w_gate_a, b_gate_a, lru_param, w_in_b, q_norm_g, sinks, kv_norm_g, w_kv, k_norm_g, mem_norm_g, w_mem_kv, mem_q_norm_g, mem_k_norm_g, w_out, w_ffn_up, ffn_conv_w, ffn_conv_b, w_ffn_down):
    bsz, seq, _ = x_prompt.shape
    dbsz = x_sample.shape[0]
    depth = norm_mix_g.shape[0]
    n_a = w_in_a.shape[0]
    assert x_sample.shape[1] == 1
    assert seq % TS_MIX == 0 and seq % TS_FFN == 0 and seq % TS_KV == 0 and TS_MIX % WINDOW == 0

    n_b = w_in_b.shape[0]
    wq = w_in_b[:, :, :Q_W].astype(BF16).reshape(n_b, D_MODEL, N_KV, GROUP, HEAD_DIM)
    wq = wq.transpose(0, 1, 3, 2, 4).reshape(n_b, D_MODEL, Q_W)
    wo_main = w_out[n_a:, :Q_W].astype(BF16).reshape(n_b, N_KV, GROUP, HEAD_DIM, D_MODEL)
    wo_main = wo_main.transpose(0, 2, 1, 3, 4).reshape(n_b, Q_W, D_MODEL)
    bd =(jnp.kron(jnp.eye(MXU_DIM // HEAD_DIM, dtype=F32),
                   jnp.ones((HEAD_DIM, HEAD_DIM), F32)) / HEAD_DIM).astype(BF16)

    row = lambda v: v.reshape(1, -1)
    tile_row = lambda v, n: jnp.tile(v, n).reshape(1, -1)
    w_in_a_b = w_in_a.astype(BF16)
    w_in_b_b = jnp.concatenate([wq, w_in_b[:, :, Q_W:].astype(BF16)], axis=2)
    w_out_b = w_out.astype(BF16)
    w_out_perm_b = jnp.concatenate([wo_main, w_out_b[n_a:, Q_W:]], axis=1)
    w_kv_b = w_kv.astype(BF16)
    w_mem_b = w_mem_kv.astype(BF16)
    wup_b = w_ffn_up.astype(BF16)
    wdn_b = w_ffn_down.astype(BF16)
    fcw = ffn_conv_w
    fcb = ffn_conv_b.reshape(depth, 1, 2 * D_FF)
    wg_b = [_block_diag_gates(w_gate_x[l], w_gate_a[l]) for l in range(n_a)]
    sink_gk = sinks.reshape(-1, N_KV, GROUP).transpose(0, 2, 1)

    cos_p, sin_p = _rope_tables(jnp.arange(seq, dtype=jnp.int32))
    pos_s = PAST_LEN + jnp.zeros((dbsz,), jnp.int32)
    cos_s, sin_s = _rope_tables(pos_s)

    mem2d = mem_prompt.reshape(bsz * N_MEM, D_MODEL)
    pmk, pmv = _mem_kv(mem2d, mem_norm_g.reshape(depth, 1, D_MODEL), w_mem_b,
                       jnp.tile(mem_k_norm_g, (1, MEM_HEADS)).reshape(depth, 1, MEM_W), bd)
    pmk4 = pmk.reshape(depth, bsz, N_MEM, MEM_W)
    pmv4 = pmv.reshape(depth, bsz, N_MEM, MEM_W)

    x = x_prompt.reshape(bsz * seq, D_MODEL)
    zeros_rc = jnp.zeros((bsz, SUBLANES, D_RNN), F32)
    zeros_h = jnp.zeros((bsz, 1, D_RNN), F32)
    p_h, p_rc, p_fc = [], [], []
    ksh = vsh = None
    for l in range(depth):
        mqg = tile_row(mem_q_norm_g[l], MEM_HEADS)
        if l < n_a:
            x, hl, rct = _mixer_a(
                x, bsz, row(norm_mix_g[l]), w_in_a_b[l], rnn_conv_w[l], row(rnn_conv_b[l]),
                wg_b[l], row(b_gate_x[l]), row(b_gate_a[l]), row(lru_param[l]), mqg, bd,
                pmk4[l], pmv4[l], w_out_b[l], zeros_rc, zeros_h)
            p_h.append(hl.reshape(bsz, D_RNN))
            p_rc.append(rct[:, SUBLANES - (CONV_A - 1):])
        else:
            j = l - n_a
            x = _mixer_b(
                x, bsz, sink_gk[j].reshape(-1), row(norm_mix_g[l]), w_in_b_b[j],
                tile_row(q_norm_g[j], N_Q), cos_p, sin_p, ksh, vsh, mqg, bd,
                pmk4[l], pmv4[l], w_out_perm_b[j])
        x, ut = _ffn(x, bsz, row(norm_ffn_g[l]), wup_b[l], fcw[l], fcb[l], wdn_b[l])
        p_fc.append(ut[:, SUBLANES - (CONV_F - 1):])
        if l == n_a - 1:
            ksh, vsh = _shared_kv(x, row(kv_norm_g), w_kv_b, tile_row(k_norm_g, N_KV), bd,
                                  cos_p, sin_p, TS_KV)
    y_prompt = x.reshape(bsz, seq, D_MODEL)
    keep = min(WINDOW, seq)
    p_k = ksh.reshape(bsz, seq, KV_W)[:, seq - keep:].reshape(bsz, keep, N_KV, HEAD_DIM)
    p_v = vsh.reshape(bsz, seq, KV_W)[:, seq - keep:].reshape(bsz, keep, N_KV, HEAD_DIM)
    p_mem_k = pmk.reshape(depth, bsz, N_MEM, MEM_HEADS, HEAD_DIM)
    p_mem_v = pmv.reshape(depth, bsz, N_MEM, MEM_HEADS, HEAD_DIM)

    xs = x_sample.reshape(dbsz, D_MODEL)
    cmk = cache_mem_k.transpose(0, 1, 3, 4, 2).reshape(depth, dbsz, MEM_W, N_MEM)
    cmv = cache_mem_v.transpose(0, 1, 3, 4, 2).reshape(depth, dbsz, MEM_W, N_MEM)
    wb = cache_swa_k.shape[1]
    ckb = cache_swa_k.transpose(0, 2, 3, 1).reshape(dbsz, KV_W, wb)
    cvb = cache_swa_v.transpose(0, 2, 3, 1).reshape(dbsz, KV_W, wb)
    s_h, s_rc, s_fc = [], [], []
    kn = vn = None
    for l in range(depth):
        mqg = tile_row(mem_q_norm_g[l], MEM_HEADS)
        if l < n_a:
            main, qn, hnew, xrpre = _dec_in_a(
                xs, row(norm_mix_g[l]), w_in_a_b[l], rnn_conv_w[l], row(rnn_conv_b[l]),
                wg_b[l], row(b_gate_x[l]), row(b_gate_a[l]), row(lru_param[l]), mqg, bd,
                state_rglru_conv[l].transpose(1, 0, 2), state_rglru_h[l])
            s_h.append(hnew)
            s_rc.append(jnp.concatenate([state_rglru_conv[l][:, 1:], xrpre[:, None, :]], axis=1))
            wo = w_out_b[l]
        else:
            j = l - n_a
            q, qn = _dec_in_b(xs, row(norm_mix_g[l]), w_in_b_b[j], tile_row(q_norm_g[j], N_Q),
                              cos_s, sin_s, mqg, bd)
            sink_rows = jnp.zeros((DEC_HEAD_ROWS, LANES), F32).at[:N_Q].set(
                jnp.broadcast_to(sink_gk[j].reshape(N_Q, 1), (N_Q, LANES)))
            main = _dec_swa(q, ckb, cvb, kn, vn, sink_rows)
            wo = w_out_perm_b[j]
        mo = _dec_mem_attn(qn, cmk, cmv, l)
        xs, ug, uv = _dec_out_ffn(xs, main, mo, wo, row(norm_ffn_g[l]), wup_b[l], fcw[l], fcb[l],
                                  wdn_b[l], state_ffn_conv[l].transpose(1, 0, 2))
        unew = jnp.concatenate([ug, uv], axis=1)
        s_fc.append(jnp.concatenate([state_ffn_conv[l][:, 1:], unew[:, None, :]], axis=1))
        if l == n_a - 1:
            kn, vn = _shared_kv(xs, row(kv_norm_g), w_kv_b, tile_row(k_norm_g, N_KV), bd,
                                cos_s, sin_s, dbsz)
    y_sample = xs.reshape(dbsz, 1, D_MODEL)
    s_k = kn.reshape(dbsz, 1, N_KV, HEAD_DIM)
    s_v = vn.reshape(dbsz, 1, N_KV, HEAD_DIM)

    return (y_prompt, y_sample, jnp.stack(p_h), jnp.stack(p_rc), jnp.stack(p_fc), p_k, p_v,
            p_mem_k, p_mem_v, jnp.stack(s_h), jnp.stack(s_rc), jnp.stack(s_fc), s_k, s_v)
```

```python
import functools
import math

import jax
import jax.numpy as jnp
from jax import lax
from jax.experimental import pallas as pl
from jax.experimental.pallas import tpu as pltpu

F32 = jnp.float32
BF16 = jnp.bfloat16

D_MODEL = 1024
HEAD_DIM = 64
MEM_HEADS = 4
MEM_W = MEM_HEADS * HEAD_DIM
N_MEM = 256
D_RNN = D_MODEL - MEM_W
RNN_BLOCKS = D_RNN // HEAD_DIM
CONV_A = 4
LRU_C = 8.0
N_Q = D_RNN // HEAD_DIM
N_KV = 4
GROUP = N_Q // N_KV
Q_W = N_Q * HEAD_DIM
KV_W = N_KV * HEAD_DIM
WINDOW = 128
ROPE_THETA = 10000.0
D_FF = 3 * D_MODEL
CONV_F = 3
EPS = 1e-6
NEG = -1e30
ATT_SCALE = HEAD_DIM ** -0.5
PAST_LEN = 8192

SUBLANES = 8
LANES = 128
MXU_DIM = 256
VMEM_LIMIT_BYTES = 56 * 1024 * 1024

TS_MIX = 512
TS_FFN = 512
TS_KV = 512
FF_CHUNK = 512
N_FF_CHUNKS = D_FF // FF_CHUNK
DEC_SEQ_BLOCK = 8


def _mm(a, b):
    return jnp.dot(a.astype(BF16), b, preferred_element_type=F32)


def _mm_nt(a, b):
    return lax.dot_general(a.astype(BF16), b, (((1,), (1,)), ((), ())),
                           preferred_element_type=F32)


def _mm_split(a, b):
    hi = a.astype(BF16)
    lo = (a - hi.astype(F32)).astype(BF16)
    return (jnp.dot(hi, b, preferred_element_type=F32)
            + jnp.dot(lo, b, preferred_element_type=F32))


def _rmsnorm(x, g):
    ms = jnp.mean(x * x, axis=-1, keepdims=True)
    return x * lax.rsqrt(ms + EPS) * g


def _head_rmsnorm(x, bd, g):
    parts = []
    for c in range(x.shape[1] // MXU_DIM):
        xc = x[:, c * MXU_DIM:(c + 1) * MXU_DIM]
        ms = _mm_split(xc * xc, bd)
        parts.append(xc * lax.rsqrt(ms + EPS))
    y = parts[0] if len(parts) == 1 else jnp.concatenate(parts, axis=1)
    return y * g


def _tile_lanes(t, width):
    reps = width // t.shape[1]
    return t if reps == 1 else jnp.concatenate([t] * reps, axis=1)


def _rope(x, cos_t, sin_t):
    w = x.shape[1]
    lane = lax.broadcasted_iota(jnp.int32, x.shape, 1)
    first = (lane % HEAD_DIM) < (HEAD_DIM // 2)
    swapped = jnp.where(first, pltpu.roll(x, w - HEAD_DIM // 2, 1),
                        pltpu.roll(x, HEAD_DIM // 2, 1))
    return x * _tile_lanes(cos_t, w) + swapped * _tile_lanes(sin_t, w)


def _gelu(x):
    c = math.sqrt(2.0 / math.pi)
    return x * (0.5 * (1.0 + jnp.tanh(c * (x + 0.044715 * (x * x * x)))))


def _log_sigmoid(x):
    return jnp.minimum(x, 0.0) - jnp.log1p(jnp.exp(-jnp.abs(x)))


def _slab_stage(x, work_ref, work0, tail_ref, tail0):
    ts = x.shape[0]
    for s in range(x.shape[1] // LANES):
        xs = x[:, s * LANES:(s + 1) * LANES]
        buf = work_ref.at[work0 + s]
        buf[0:SUBLANES, :] = tail_ref[tail0 + s]
        buf[SUBLANES:SUBLANES + ts, :] = xs
        tail_ref[tail0 + s] = xs[ts - SUBLANES:]


def _slab_taps(ts, n_slabs, work_ref, work0, w_ref, b_ref, col0):
    k = w_ref.shape[0]
    outs = []
    for s in range(n_slabs):
        lanes = slice(col0 + s * LANES, col0 + (s + 1) * LANES)
        buf = work_ref.at[work0 + s]
        acc = buf[SUBLANES - (k - 1):SUBLANES - (k - 1) + ts, :] * w_ref[0:1, lanes]
        for j in range(1, k):
            off = SUBLANES - (k - 1 - j)
            acc = acc + buf[off:off + ts, :] * w_ref[j:j + 1, lanes]
        outs.append(acc + b_ref[:, lanes])
    return outs


def _sqrt_pos(x):
    return jnp.where(x > 0.0, x * lax.rsqrt(x), 0.0)


def _lru_coeffs(xr, wg_ref, bgx, bga, logsig):
    xb = xr.astype(BF16)
    gxs, gas = [], []
    for c in range(D_RNN // MXU_DIM):
        gg = jnp.dot(xb[:, c * MXU_DIM:(c + 1) * MXU_DIM], wg_ref[c],
                     preferred_element_type=F32)
        gxs.append(gg[:, :MXU_DIM])
        gas.append(gg[:, MXU_DIM:])
    gx = jax.nn.sigmoid(jnp.concatenate(gxs, axis=1) + bgx)
    ga = jax.nn.sigmoid(jnp.concatenate(gas, axis=1) + bga)
    log_a = LRU_C * ga * logsig
    a = jnp.exp(log_a)
    mult = _sqrt_pos(-jnp.tanh(log_a) * (a * a + 1.0))
    return a, mult * gx * xr


def _lru_scan(a, b, a_ref, b_ref, hc_ref):
    ts = a.shape[0]
    outs = []
    for s in range(a.shape[1] // LANES):
        lanes = slice(s * LANES, (s + 1) * LANES)
        a_s, b_s = a[:, lanes], b[:, lanes]
        abuf, bbuf = a_ref.at[s], b_ref.at[s]
        d = 1
        while d < SUBLANES:
            abuf[SUBLANES:SUBLANES + ts, :] = a_s
            bbuf[SUBLANES:SUBLANES + ts, :] = b_s
            b_s = a_s * bbuf[SUBLANES - d:SUBLANES - d + ts, :] + b_s
            a_s = a_s * abuf[SUBLANES - d:SUBLANES - d + ts, :]
            d *= 2
        h = hc_ref[:, lanes]
        hs = []
        for q in range(ts // SUBLANES):
            rows = slice(q * SUBLANES, (q + 1) * SUBLANES)
            h = a_s[rows] * h + b_s[rows]
            hs.append(h)
        hc_ref[:, lanes] = jnp.broadcast_to(h[SUBLANES - 1:], (SUBLANES, LANES))
        outs.append(jnp.concatenate(hs, axis=0))
    return jnp.concatenate(outs, axis=1)


def _head_mask(shape, h):
    lane = lax.broadcasted_iota(jnp.int32, shape, 1)
    return (lane >= h * HEAD_DIM) & (lane < (h + 1) * HEAD_DIM)


def _head_masked_stack(blocks, n_heads):
    parts = []
    for h in range(n_heads):
        for blk in blocks:
            parts.append(jnp.where(_head_mask(blk.shape, h), blk, 0.0).astype(BF16))
    return jnp.concatenate(parts, axis=0)


def _mem_attention(qn, kcat, vcat):
    s = _mm_nt(qn, kcat)
    parts = []
    for h in range(MEM_HEADS):
        sh = s[:, h * N_MEM:(h + 1) * N_MEM]
        m = jnp.max(sh, axis=-1, keepdims=True)
        p = jnp.exp(sh - m)
        den = jnp.sum(p, axis=-1, keepdims=True)
        parts.append((p * (1.0 / den)).astype(BF16))
    return jnp.dot(jnp.concatenate(parts, axis=1), vcat, preferred_element_type=F32)


def _mem_kv_kernel(mem_ref, g_ref, w_ref, kg_ref, bd_ref, k_ref, v_ref):
    h = _mm(_rmsnorm(mem_ref[...], g_ref[0]), w_ref[0])
    k_ref[0] = _head_rmsnorm(h[:, :MEM_W], bd_ref[...], kg_ref[0])
    v_ref[0] = h[:, MEM_W:]


def _shared_kv_kernel(x_ref, g_ref, w_ref, kg_ref, bd_ref, cos_ref, sin_ref,
                      k_ref, v_ref):
    h = _mm(_rmsnorm(x_ref[...], g_ref[...]), w_ref[...])
    k = _head_rmsnorm(h[:, :KV_W], bd_ref[...], kg_ref[...])
    k_ref[...] = _rope(k, cos_ref[...], sin_ref[...])
    v_ref[...] = h[:, KV_W:]


def _mixer_a_kernel(x_ref, g_ref, win_ref, cw_ref, cb_ref, wg_ref, bgx_ref, bga_ref,
                    lru_ref, mqg_ref, bd_ref, mk_ref, mv_ref, wout_ref, rc0_ref, h0_ref,
                    xo_ref, hlast_ref, rctail_ref,
                    conv_s, tail_s, a_s, b_s, hc_s, kcat_s, vcat_s):
    ts = x_ref.shape[0]
    n_slabs = D_RNN // LANES

    @pl.when(pl.program_id(1) == 0)
    def _():
        for s in range(n_slabs):
            tail_s[s] = rc0_ref[0, :, s * LANES:(s + 1) * LANES]
        a_s[:, 0:SUBLANES, :] = jnp.ones((n_slabs, SUBLANES, LANES), F32)
        b_s[:, 0:SUBLANES, :] = jnp.zeros((n_slabs, SUBLANES, LANES), F32)
        hc_s[...] = jnp.broadcast_to(h0_ref[0], hc_s.shape)
        kcat_s[...] = _head_masked_stack([mk_ref[0]], MEM_HEADS)
        vcat_s[...] = _head_masked_stack([mv_ref[0]], MEM_HEADS)

    x = x_ref[...]
    u = _mm(_rmsnorm(x, g_ref[...]), win_ref[...])
    gate = u[:, :D_RNN]
    xr_pre = u[:, D_RNN:2 * D_RNN]
    qm = u[:, 2 * D_RNN:]

    _slab_stage(xr_pre, conv_s, 0, tail_s, 0)
    xr = jnp.concatenate(_slab_taps(ts, n_slabs, conv_s, 0, cw_ref, cb_ref, 0), axis=1)
    rctail_ref[0] = xr_pre[ts - SUBLANES:]

    a, b = _lru_coeffs(xr, wg_ref, bgx_ref[...], bga_ref[...], _log_sigmoid(lru_ref[...]))
    h = _lru_scan(a, b, a_s, b_s, hc_s)
    hlast_ref[0] = h[ts - 1:ts]
    main = h * _gelu(gate)

    qn = _head_rmsnorm(qm, bd_ref[...], mqg_ref[...]) * ATT_SCALE
    mo = _mem_attention(qn, kcat_s[...], vcat_s[...])

    y = _mm(jnp.concatenate([main, mo], axis=1), wout_ref[...])
    xo_ref[...] = x + y


def _mixer_b_kernel(sink_ref, x_ref, g_ref, win_ref, qg_ref, cos_ref, sin_ref,
                    kcur_ref, kprev_ref, vcur_ref, vprev_ref,
                    mqg_ref, bd_ref, mk_ref, mv_ref, wout_ref,
                    xo_ref, kcat_s, vcat_s):
    ts = x_ref.shape[0]
    i = pl.program_id(1)

    @pl.when(i == 0)
    def _():
        kcat_s[...] = _head_masked_stack([mk_ref[0]], MEM_HEADS)
        vcat_s[...] = _head_masked_stack([mv_ref[0]], MEM_HEADS)

    x = x_ref[...]
    u = _mm(_rmsnorm(x, g_ref[...]), win_ref[...])
    q = _head_rmsnorm(u[:, :Q_W], bd_ref[...], qg_ref[...])
    q = (_rope(q, cos_ref[...], sin_ref[...]) * ATT_SCALE).astype(BF16)
    qm = u[:, Q_W:]

    row = lax.broadcasted_iota(jnp.int32, (WINDOW, 2 * WINDOW), 0)
    kj = lax.broadcasted_iota(jnp.int32, (WINDOW, 2 * WINDOW), 1) - WINDOW
    mask_inner = (kj <= row) & (kj >= row - WINDOW)
    mask_first = (kj <= row) & (kj >= jnp.maximum(row - WINDOW, jnp.where(i > 0, -WINDOW, 0)))

    kt = kcur_ref[...]
    vt = vcur_ref[...]
    mains = []
    for jb in range(ts // WINDOW):
        lo, hi = jb * WINDOW, (jb + 1) * WINDOW
        if jb == 0:
            kp, vp = kprev_ref[...], vprev_ref[...]
            mask = mask_first
        else:
            kp, vp = kt[lo - WINDOW:lo], vt[lo - WINDOW:lo]
            mask = mask_inner
        kcat = _head_masked_stack([kp, kt[lo:hi]], N_KV)
        vcat = _head_masked_stack([vp, vt[lo:hi]], N_KV)
        qs = jnp.concatenate([q[lo:hi, g * KV_W:(g + 1) * KV_W] for g in range(GROUP)], axis=0)
        s = lax.dot_general(qs, kcat, (((1,), (1,)), ((), ())), preferred_element_type=F32)
        prow = []
        for g in range(GROUP):
            pseg = []
            for kv in range(N_KV):
                seg = s[g * WINDOW:(g + 1) * WINDOW, kv * 2 * WINDOW:(kv + 1) * 2 * WINDOW]
                seg = jnp.where(mask, seg, NEG)
                sink = sink_ref[g * N_KV + kv]
                m = jnp.maximum(jnp.max(seg, axis=-1, keepdims=True), sink)
                p = jnp.exp(seg - m)
                den = jnp.sum(p, axis=-1, keepdims=True) + jnp.exp(sink - m)
                pseg.append((p * (1.0 / den)).astype(BF16))
            prow.append(jnp.concatenate(pseg, axis=1))
        o = jnp.dot(jnp.concatenate(prow, axis=0), vcat, preferred_element_type=F32)
        mains.append(jnp.concatenate([o[g * WINDOW:(g + 1) * WINDOW] for g in range(GROUP)], axis=1))
    main = mains[0] if len(mains) == 1 else jnp.concatenate(mains, axis=0)

    qn = _head_rmsnorm(qm, bd_ref[...], mqg_ref[...]) * ATT_SCALE
    mo = _mem_attention(qn, kcat_s[...], vcat_s[...])

    y = _mm(jnp.concatenate([main, mo], axis=1), wout_ref[...])
    xo_ref[...] = x + y


def _ffn_kernel(x_ref, g_ref, wup_ref, cw_ref, cb_ref, wdn_ref,
                xo_ref, utail_ref, conv_s, tail_s, act_s, xres_s, *, tiles_per_seq):
    ts = x_ref.shape[0]
    slabs = FF_CHUNK // LANES
    blk_slabs = MXU_DIM // LANES
    blks = FF_CHUNK // MXU_DIM
    i = pl.program_id(0)
    slot = i % 2
    pslot = 1 - slot

    @pl.when(i == 0)
    def _():
        act_s[1] = jnp.zeros(act_s.shape[1:], act_s.dtype)
        xres_s[1] = jnp.zeros(xres_s.shape[1:], xres_s.dtype)

    @pl.when(i % tiles_per_seq == 0)
    def _():
        tail_s[...] = jnp.zeros_like(tail_s)

    x = x_ref[...]
    xres_s[slot] = x
    hn = _rmsnorm(x, g_ref[...]).astype(BF16)

    def chunk_cols(j):
        return [(c, ((j % 2) * 2 + half) * slabs) for half, c in enumerate((j, N_FF_CHUNKS + j))]

    def up_project(j):
        for c, work0 in chunk_cols(j):
            for b in range(blks):
                blk = c * blks + b
                u = jnp.dot(hn, wup_ref[blk], preferred_element_type=F32)
                _slab_stage(u, conv_s, work0 + b * blk_slabs, tail_s, c * slabs + b * blk_slabs)
                utail_ref[0, :, blk * MXU_DIM:(blk + 1) * MXU_DIM] = u[ts - SUBLANES:]

    def activation(j):
        cg, cv = [_slab_taps(ts, slabs, conv_s, work0, cw_ref, cb_ref, c * FF_CHUNK)
                  for c, work0 in chunk_cols(j)]
        return jnp.concatenate([_gelu(a) * b for a, b in zip(cg, cv)], axis=1).astype(BF16)

    def down_prev(n):
        return jnp.dot(act_s[pslot], wdn_ref[n], preferred_element_type=F32)

    n_dn = wdn_ref.shape[0]
    outs = [None] * n_dn
    outs[0] = down_prev(0)
    up_project(0)
    for j in range(N_FF_CHUNKS):
        if j + 1 < N_FF_CHUNKS:
            up_project(j + 1)
        else:
            outs[1] = down_prev(1)
        act_s[slot, :, j * FF_CHUNK:(j + 1) * FF_CHUNK] = activation(j)
    for n in range(2, n_dn):
        outs[n] = down_prev(n)
    xo_ref[...] = xres_s[pslot] + jnp.concatenate(outs, axis=1)


def _dec_in_a_kernel(x_ref, g_ref, win_ref, cw_ref, cb_ref, wg_ref, bgx_ref, bga_ref,
                     lru_ref, mqg_ref, bd_ref, b0_ref, b1_ref, b2_ref, h0_ref,
                     main_ref, qn_ref, hnew_ref, xrpre_ref):
    u = _mm(_rmsnorm(x_ref[...], g_ref[...]), win_ref[...])
    gate = u[:, :D_RNN]
    xr_pre = u[:, D_RNN:2 * D_RNN]
    qm = u[:, 2 * D_RNN:]
    xr = b0_ref[...] * cw_ref[0:1, :]
    xr = xr + b1_ref[...] * cw_ref[1:2, :]
    xr = xr + b2_ref[...] * cw_ref[2:3, :]
    xr = xr + xr_pre * cw_ref[3:4, :]
    xr = xr + cb_ref[...]
    a, b = _lru_coeffs(xr, wg_ref, bgx_ref[...], bga_ref[...], _log_sigmoid(lru_ref[...]))
    h = a * h0_ref[...] + b
    main_ref[...] = h * _gelu(gate)
    qn_ref[...] = _head_rmsnorm(qm, bd_ref[...], mqg_ref[...]) * ATT_SCALE
    hnew_ref[...] = h
    xrpre_ref[...] = xr_pre


def _dec_in_b_kernel(x_ref, g_ref, win_ref, qg_ref, cos_ref, sin_ref, mqg_ref, bd_ref,
                     q_ref, qn_ref):
    u = _mm(_rmsnorm(x_ref[...], g_ref[...]), win_ref[...])
    q = _head_rmsnorm(u[:, :Q_W], bd_ref[...], qg_ref[...])
    q_ref[...] = _rope(q, cos_ref[...], sin_ref[...]) * ATT_SCALE
    qn_ref[...] = _head_rmsnorm(u[:, Q_W:], bd_ref[...], mqg_ref[...]) * ATT_SCALE


DEC_HEAD_ROWS = 16


def _own_head_lanes(n_heads, width):
    row = lax.broadcasted_iota(jnp.int32, (DEC_HEAD_ROWS, width), 0)
    lane = lax.broadcasted_iota(jnp.int32, (DEC_HEAD_ROWS, width), 1)
    start = (row % (width // HEAD_DIM)) * HEAD_DIM
    return (lane >= start) & (lane < start + HEAD_DIM) & (row < n_heads)


def _dec_mem_attn_kernel(q_ref, kt_ref, vt_ref, o_ref):
    own = _own_head_lanes(MEM_HEADS, MEM_W)
    for s in range(q_ref.shape[0]):
        q_rows = jnp.broadcast_to(q_ref[s:s + 1, :], (DEC_HEAD_ROWS, MEM_W))
        qbd = jnp.where(own, q_rows, 0.0).astype(BF16)
        sc = jnp.dot(qbd, kt_ref[s].astype(BF16), preferred_element_type=F32)
        m = jnp.max(sc, axis=-1, keepdims=True)
        p = jnp.exp(sc - m)
        den = jnp.sum(p, axis=-1, keepdims=True)
        pn = (p * (1.0 / den)).astype(BF16)
        o_all = lax.dot_general(pn, vt_ref[s].astype(BF16), (((1,), (1,)), ((), ())),
                                preferred_element_type=F32)
        o_ref[s:s + 1, :] = jnp.sum(jnp.where(own, o_all, 0.0), axis=0, keepdims=True)


def _dec_swa_kernel(q_ref, kt_ref, vt_ref, kn_ref, vn_ref, sink_ref, o_ref):
    n_heads = GROUP * N_KV
    own = _own_head_lanes(n_heads, KV_W)
    grp = lax.broadcasted_iota(jnp.int32, (DEC_HEAD_ROWS, KV_W), 0) // N_KV
    sink = sink_ref[:, 0:1]
    for s in range(q_ref.shape[0]):
        q_rows = jnp.zeros((DEC_HEAD_ROWS, KV_W), F32)
        for g in range(GROUP):
            qg = jnp.broadcast_to(q_ref[s:s + 1, g * KV_W:(g + 1) * KV_W], (DEC_HEAD_ROWS, KV_W))
            q_rows = jnp.where(grp == g, qg, q_rows)
        qbd = jnp.where(own, q_rows, 0.0)
        s_buf = jnp.dot(qbd.astype(BF16), kt_ref[s].astype(BF16), preferred_element_type=F32)
        s_new = jnp.sum(qbd * kn_ref[s:s + 1, :], axis=-1, keepdims=True)
        m = jnp.maximum(jnp.maximum(jnp.max(s_buf, axis=-1, keepdims=True), s_new), sink)
        p_buf = jnp.exp(s_buf - m)
        p_new = jnp.exp(s_new - m)
        den = jnp.sum(p_buf, axis=-1, keepdims=True) + p_new + jnp.exp(sink - m)
        r = 1.0 / den
        o_all = lax.dot_general((p_buf * r).astype(BF16), vt_ref[s].astype(BF16),
                                (((1,), (1,)), ((), ())), preferred_element_type=F32)
        o_all = jnp.where(own, o_all + (p_new * r) * vn_ref[s:s + 1, :], 0.0)
        o_sum = o_all + pltpu.roll(o_all, 1, 0)
        o_sum = o_sum + pltpu.roll(o_sum, 2, 0)
        for g in range(GROUP):
            last = (g + 1) * N_KV - 1
            o_ref[s:s + 1, g * KV_W:(g + 1) * KV_W] = o_sum[last:last + 1, :]


def _dec_out_ffn_kernel(x_ref, main_ref, mo_ref, wout_ref, g_ref,
                        wug_ref, wuv_ref, cwg_ref, cwv_ref, cbg_ref, cbv_ref, wdn_ref,
                        sg0_ref, sg1_ref, sv0_ref, sv1_ref,
                        xo_ref, ug_ref, uv_ref,
                        xmid_s, hn_s, acc_s):
    j = pl.program_id(0)

    @pl.when(j == 0)
    def _():
        y = _mm(jnp.concatenate([main_ref[...], mo_ref[...]], axis=1), wout_ref[...])
        xmid = x_ref[...] + y
        xmid_s[...] = xmid
        hn_s[...] = _rmsnorm(xmid, g_ref[...]).astype(BF16)
        acc_s[...] = jnp.zeros_like(acc_s)

    hn = hn_s[...]
    up = lambda w_ref: jnp.concatenate(
        [jnp.dot(hn, w_ref[b], preferred_element_type=F32) for b in range(w_ref.shape[0])], axis=1)
    ug = up(wug_ref)
    uv = up(wuv_ref)
    ug_ref[...] = ug
    uv_ref[...] = uv
    cg = (sg0_ref[...] * cwg_ref[0:1, :] + sg1_ref[...] * cwg_ref[1:2, :]
          + ug * cwg_ref[2:3, :] + cbg_ref[...])
    cv = (sv0_ref[...] * cwv_ref[0:1, :] + sv1_ref[...] * cwv_ref[1:2, :]
          + uv * cwv_ref[2:3, :] + cbv_ref[...])
    act = (_gelu(cg) * cv).astype(BF16)
    for n in range(wdn_ref.shape[0]):
        acc_s[:, n * MXU_DIM:(n + 1) * MXU_DIM] += jnp.dot(act, wdn_ref[n],
                                                           preferred_element_type=F32)

    @pl.when(j == pl.num_programs(0) - 1)
    def _():
        xo_ref[...] = xmid_s[...] + acc_s[...]


def _const_spec(shape):
    nd = len(shape)
    return pl.BlockSpec(shape, lambda *_: (0,) * nd)


def _layer_spec(arr, layer):
    nd = arr.ndim - 1
    return pl.BlockSpec((None,) + arr.shape[1:], lambda *_: (layer,) + (0,) * nd)


def _resident_spec(op):
    return _layer_spec(*op) if isinstance(op, tuple) else _const_spec(op.shape)


def _operand(op):
    return op[0] if isinstance(op, tuple) else op


def _params(*sem):
    return pltpu.CompilerParams(dimension_semantics=sem, vmem_limit_bytes=VMEM_LIMIT_BYTES)


def _mem_kv(mem2d, g, w, kg, bd):
    depth = w.shape[0]
    rows = mem2d.shape[0]
    out = jax.ShapeDtypeStruct((depth, rows, MEM_W), F32)
    return pl.pallas_call(
        _mem_kv_kernel,
        grid=(depth,),
        in_specs=[
            _const_spec(mem2d.shape),
            pl.BlockSpec((1, 1, D_MODEL), lambda l: (l, 0, 0)),
            pl.BlockSpec((1, D_MODEL, 2 * MEM_W), lambda l: (l, 0, 0)),
            pl.BlockSpec((1, 1, MEM_W), lambda l: (l, 0, 0)),
            _const_spec(bd.shape),
        ],
        out_specs=[pl.BlockSpec((1, rows, MEM_W), lambda l: (l, 0, 0))] * 2,
        out_shape=[out, out],
        compiler_params=_params("arbitrary"),
        name="mem_kv",
    )(mem2d, g, w, kg, bd)


def _shared_kv(x2d, g, w, kg, bd, cos_t, sin_t, ts):
    rows = x2d.shape[0]
    tab_blocks = cos_t.shape[0] // ts
    out = jax.ShapeDtypeStruct((rows, KV_W), F32)
    return pl.pallas_call(
        _shared_kv_kernel,
        grid=(rows // ts,),
        in_specs=[
            pl.BlockSpec((ts, D_MODEL), lambda i: (i, 0)),
            _const_spec(g.shape), _const_spec(w.shape), _const_spec(kg.shape),
            _const_spec(bd.shape),
            pl.BlockSpec((ts, LANES), lambda i: (i % tab_blocks, 0)),
            pl.BlockSpec((ts, LANES), lambda i: (i % tab_blocks, 0)),
        ],
        out_specs=[pl.BlockSpec((ts, KV_W), lambda i: (i, 0))] * 2,
        out_shape=[out, out],
        compiler_params=_params("arbitrary"),
        name="shared_kv",
    )(x2d, g, w, kg, bd, cos_t, sin_t)


def _mixer_a(x2d, bsz, g, win, cw, cb, wg, bgx, bga, lru, mqg, bd, mk, mv, wout, rc0, h0):
    rows = x2d.shape[0]
    ts = TS_MIX
    nt = rows // bsz // ts
    consts = (g, win, cw, cb, wg, bgx, bga, lru, mqg, bd)
    per_b3 = lambda b, i: (b, 0, 0)
    return pl.pallas_call(
        _mixer_a_kernel,
        grid=(bsz, nt),
        in_specs=[pl.BlockSpec((ts, D_MODEL), lambda b, i: (b * nt + i, 0))]
        + [_resident_spec(c) for c in consts]
        + [pl.BlockSpec((1, N_MEM, MEM_W), per_b3)] * 2
        + [_resident_spec(wout),
           pl.BlockSpec((1, SUBLANES, D_RNN), per_b3),
           pl.BlockSpec((1, 1, D_RNN), per_b3)],
        out_specs=[
            pl.BlockSpec((ts, D_MODEL), lambda b, i: (b * nt + i, 0)),
            pl.BlockSpec((1, 1, D_RNN), per_b3),
            pl.BlockSpec((1, SUBLANES, D_RNN), per_b3),
        ],
        out_shape=[
            jax.ShapeDtypeStruct((rows, D_MODEL), F32),
            jax.ShapeDtypeStruct((bsz, 1, D_RNN), F32),
            jax.ShapeDtypeStruct((bsz, SUBLANES, D_RNN), F32),
        ],
        scratch_shapes=[
            pltpu.VMEM((D_RNN // LANES, SUBLANES + ts, LANES), F32),
            pltpu.VMEM((D_RNN // LANES, SUBLANES, LANES), F32),
            pltpu.VMEM((D_RNN // LANES, SUBLANES + ts, LANES), F32),
            pltpu.VMEM((D_RNN // LANES, SUBLANES + ts, LANES), F32),
            pltpu.VMEM((SUBLANES, D_RNN), F32),
            pltpu.VMEM((MEM_HEADS * N_MEM, MEM_W), BF16),
            pltpu.VMEM((MEM_HEADS * N_MEM, MEM_W), BF16),
        ],
        compiler_params=_params("arbitrary", "arbitrary"),
        name="mixer_a",
    )(x2d, *[_operand(c) for c in consts], mk, mv, _operand(wout), rc0, h0)


def _mixer_b(x2d, bsz, sink_tab, g, win, qg, cos_t, sin_t, ksh, vsh, mqg, bd, mk, mv, wout):
    rows = x2d.shape[0]
    ts = TS_MIX
    nt = rows // bsz // ts
    wpt = ts // WINDOW
    cur = lambda b, i: (b * nt + i, 0)
    prev = lambda b, i: (jnp.maximum((b * nt + i) * wpt - 1, 0), 0)
    per_b3 = lambda b, i: (b, 0, 0)
    return pl.pallas_call(
        _mixer_b_kernel,
        grid=(bsz, nt),
        in_specs=[
            pl.BlockSpec(memory_space=pltpu.SMEM),
            pl.BlockSpec((ts, D_MODEL), cur),
            _const_spec(g.shape), _resident_spec(win), _const_spec(qg.shape),
            pl.BlockSpec((ts, LANES), lambda b, i: (i, 0)),
            pl.BlockSpec((ts, LANES), lambda b, i: (i, 0)),
            pl.BlockSpec((ts, KV_W), cur), pl.BlockSpec((WINDOW, KV_W), prev),
            pl.BlockSpec((ts, KV_W), cur), pl.BlockSpec((WINDOW, KV_W), prev),
            _const_spec(mqg.shape), _const_spec(bd.shape),
            pl.BlockSpec((1, N_MEM, MEM_W), per_b3), pl.BlockSpec((1, N_MEM, MEM_W), per_b3),
            _resident_spec(wout),
        ],
        out_specs=pl.BlockSpec((ts, D_MODEL), cur),
        out_shape=jax.ShapeDtypeStruct((rows, D_MODEL), F32),
        scratch_shapes=[
            pltpu.VMEM((MEM_HEADS * N_MEM, MEM_W), BF16),
            pltpu.VMEM((MEM_HEADS * N_MEM, MEM_W), BF16),
        ],
        compiler_params=_params("arbitrary", "arbitrary"),
        name="mixer_b",
    )(sink_tab, x2d, g, _operand(win), qg, cos_t, sin_t, ksh, ksh, vsh, vsh, mqg, bd, mk, mv,
      _operand(wout))


def _ffn(x2d, bsz, layer, g, wup, cw, cb, wdn):
    rows = x2d.shape[0]
    ts = TS_FFN
    nt = rows // ts
    tiles_per_seq = rows // bsz // ts
    last = nt - 1
    return pl.pallas_call(
        functools.partial(_ffn_kernel, tiles_per_seq=tiles_per_seq),
        grid=(nt + 1,),
        in_specs=[pl.BlockSpec((ts, D_MODEL), lambda i: (jnp.minimum(i, last), 0)),
                  _const_spec(g.shape), _layer_spec(wup, layer), _const_spec(cw.shape),
                  _const_spec(cb.shape), _layer_spec(wdn, layer)],
        out_specs=[
            pl.BlockSpec((ts, D_MODEL), lambda i: (jnp.maximum(i - 1, 0), 0)),
            pl.BlockSpec((1, SUBLANES, 2 * D_FF),
                         lambda i: (jnp.minimum(i, last) // tiles_per_seq, 0, 0)),
        ],
        out_shape=[
            jax.ShapeDtypeStruct((rows, D_MODEL), F32),
            jax.ShapeDtypeStruct((bsz, SUBLANES, 2 * D_FF), F32),
        ],
        scratch_shapes=[pltpu.VMEM((4 * FF_CHUNK // LANES, SUBLANES + ts, LANES), F32),
                        pltpu.VMEM((2 * D_FF // LANES, SUBLANES, LANES), F32),
                        pltpu.VMEM((2, ts, D_FF), BF16),
                        pltpu.VMEM((2, ts, D_MODEL), F32)],
        compiler_params=_params("arbitrary"),
        name="ffn",
    )(x2d, g, wup, cw, cb, wdn)


def _dec_in_a(x, g, win, cw, cb, wg, bgx, bga, lru, mqg, bd, rc, h0):
    n = x.shape[0]
    consts = (x, g, win, cw, cb, wg, bgx, bga, lru, mqg, bd)
    buf = lambda j: pl.BlockSpec((None, n, D_RNN), lambda i: (j, 0, 0))
    return pl.pallas_call(
        _dec_in_a_kernel,
        grid=(1,),
        in_specs=[_resident_spec(c) for c in consts]
        + [buf(0), buf(1), buf(2), _const_spec(h0.shape)],
        out_specs=[_const_spec((n, D_RNN)), _const_spec((n, MEM_W)),
                   _const_spec((n, D_RNN)), _const_spec((n, D_RNN))],
        out_shape=[jax.ShapeDtypeStruct((n, D_RNN), F32), jax.ShapeDtypeStruct((n, MEM_W), F32),
                   jax.ShapeDtypeStruct((n, D_RNN), F32), jax.ShapeDtypeStruct((n, D_RNN), F32)],
        compiler_params=_params("arbitrary"),
        name="dec_in_a",
    )(*[_operand(c) for c in consts], rc, rc, rc, h0)


def _dec_in_b(x, g, win, qg, cos_t, sin_t, mqg, bd):
    n = x.shape[0]
    args = (x, g, win, qg, cos_t, sin_t, mqg, bd)
    return pl.pallas_call(
        _dec_in_b_kernel,
        grid=(1,),
        in_specs=[_resident_spec(a) for a in args],
        out_specs=[_const_spec((n, Q_W)), _const_spec((n, MEM_W))],
        out_shape=[jax.ShapeDtypeStruct((n, Q_W), F32), jax.ShapeDtypeStruct((n, MEM_W), F32)],
        compiler_params=_params("arbitrary"),
        name="dec_in_b",
    )(*[_operand(a) for a in args])


def _dec_mem_attn(qn, ckt, cvt, layer):
    n = qn.shape[0]
    sb = DEC_SEQ_BLOCK
    return pl.pallas_call(
        _dec_mem_attn_kernel,
        grid=(n // sb,),
        in_specs=[pl.BlockSpec((sb, MEM_W), lambda i: (i, 0)),
                  pl.BlockSpec((None, sb, MEM_W, N_MEM), lambda i: (layer, i, 0, 0)),
                  pl.BlockSpec((None, sb, MEM_W, N_MEM), lambda i: (layer, i, 0, 0))],
        out_specs=pl.BlockSpec((sb, MEM_W), lambda i: (i, 0)),
        out_shape=jax.ShapeDtypeStruct((n, MEM_W), F32),
        compiler_params=_params("arbitrary"),
        name="dec_mem_attn",
    )(qn, ckt, cvt)


def _dec_swa(q, kbt, vbt, kn, vn, sink_rows):
    n = q.shape[0]
    wb = kbt.shape[2]
    sb = DEC_SEQ_BLOCK
    assert wb <= WINDOW
    return pl.pallas_call(
        _dec_swa_kernel,
        grid=(n // sb,),
        in_specs=[pl.BlockSpec((sb, Q_W), lambda i: (i, 0)),
                  pl.BlockSpec((sb, KV_W, wb), lambda i: (i, 0, 0)),
                  pl.BlockSpec((sb, KV_W, wb), lambda i: (i, 0, 0)),
                  pl.BlockSpec((sb, KV_W), lambda i: (i, 0)),
                  pl.BlockSpec((sb, KV_W), lambda i: (i, 0)),
                  _const_spec(sink_rows.shape)],
        out_specs=pl.BlockSpec((sb, Q_W), lambda i: (i, 0)),
        out_shape=jax.ShapeDtypeStruct((n, Q_W), F32),
        compiler_params=_params("arbitrary"),
        name="dec_swa",
    )(q, kbt, vbt, kn, vn, sink_rows)


def _dec_out_ffn(x, main, mo, wout, wout_layer, layer, g, wup, cw, cb, wdn, st):
    n = x.shape[0]
    nch = N_FF_CHUNKS
    blks = FF_CHUNK // MXU_DIM
    lo = lambda j: (0, j)
    hi = lambda j: (0, nch + j)
    state = lambda r, off: pl.BlockSpec((None, n, FF_CHUNK), lambda j: (r, 0, off + j))
    up_blocks = lambda off: pl.BlockSpec((None, blks, D_MODEL, MXU_DIM),
                                         lambda j: (layer, off + j, 0, 0))
    return pl.pallas_call(
        _dec_out_ffn_kernel,
        grid=(nch,),
        in_specs=[_const_spec(x.shape), _const_spec(main.shape), _const_spec(mo.shape),
                  _layer_spec(wout, wout_layer), _const_spec(g.shape),
                  up_blocks(0), up_blocks(nch),
                  pl.BlockSpec((CONV_F, FF_CHUNK), lo), pl.BlockSpec((CONV_F, FF_CHUNK), hi),
                  pl.BlockSpec((1, FF_CHUNK), lo), pl.BlockSpec((1, FF_CHUNK), hi),
                  pl.BlockSpec((None, wdn.shape[1], FF_CHUNK, MXU_DIM), lambda j: (layer, 0, j, 0)),
                  state(0, 0), state(1, 0), state(0, nch), state(1, nch)],
        out_specs=[_const_spec((n, D_MODEL)),
                   pl.BlockSpec((n, FF_CHUNK), lambda j: (0, j)),
                   pl.BlockSpec((n, FF_CHUNK), lambda j: (0, j))],
        out_shape=[jax.ShapeDtypeStruct((n, D_MODEL), F32),
                   jax.ShapeDtypeStruct((n, D_FF), F32),
                   jax.ShapeDtypeStruct((n, D_FF), F32)],
        scratch_shapes=[pltpu.VMEM((n, D_MODEL), F32), pltpu.VMEM((n, D_MODEL), BF16),
                        pltpu.VMEM((n, D_MODEL), F32)],
        compiler_params=_params("arbitrary"),
        name="dec_out_ffn",
    )(x, main, mo, wout, g, wup, wup, cw, cw, cb, cb, wdn, st, st, st, st)


def _rope_tables(pos):
    half = HEAD_DIM // 2
    inv = ROPE_THETA ** (-jnp.arange(half, dtype=F32) / half)
    ang = pos.astype(F32)[:, None] * inv[None, :]
    cos = jnp.cos(ang)
    sin = jnp.sin(ang)
    reps = LANES // HEAD_DIM
    cos_t = jnp.tile(jnp.concatenate([cos, cos], axis=1), (1, reps))
    sin_t = jnp.tile(jnp.concatenate([-sin, sin], axis=1), (1, reps))
    return cos_t, sin_t


def _block_diag_gates(wx, wa):
    per = MXU_DIM // HEAD_DIM
    eye = jnp.eye(per, dtype=F32)

    def bd(w):
        w4 = w.reshape(RNN_BLOCKS // per, per, HEAD_DIM, HEAD_DIM)
        return jnp.einsum('ckij,kK->ckiKj', w4, eye).reshape(RNN_BLOCKS // per, MXU_DIM, MXU_DIM)

    return jnp.concatenate([bd(wx), bd(wa)], axis=2).astype(BF16)


def kernel(x_prompt, x_sample, state_rglru_h, state_rglru_conv, state_ffn_conv, cache_swa_k, cache_swa_v, cache_mem_k, cache_mem_v, mem_prompt, norm_mix_g, norm_ffn_g, w_in_a, rnn_conv_w, rnn_conv_b, w_gate_x, b_gate_x, w_gate_a, b_gate_a, lru_param, w_in_b, q_norm_g, sinks, kv_norm_g, w_kv, k_norm_g, mem_norm_g, w_mem_kv, mem_q_norm_g, mem_k_norm_g, w_out, w_ffn_up, ffn_conv_w, ffn_conv_b, w_ffn_down):
    bsz, seq, _ = x_prompt.shape
    dbsz = x_sample.shape[0]
    depth = norm_mix_g.shape[0]
    n_a = w_in_a.shape[0]
    assert x_sample.shape[1] == 1
    assert seq % TS_MIX == 0 and seq % TS_FFN == 0 and seq % TS_KV == 0 and TS_MIX % WINDOW == 0

    n_b = w_in_b.shape[0]
    wq = w_in_b[:, :, :Q_W].astype(BF16).reshape(n_b, D_MODEL, N_KV, GROUP, HEAD_DIM)
    wq = wq.transpose(0, 1, 3, 2, 4).reshape(n_b, D_MODEL, Q_W)
    wo_main = w_out[n_a:, :Q_W].astype(BF16).reshape(n_b, N_KV, GROUP, HEAD_DIM, D_MODEL)
    wo_main = wo_main.transpose(0, 2, 1, 3, 4).reshape(n_b, Q_W, D_MODEL)
    bd =(jnp.kron(jnp.eye(MXU_DIM // HEAD_DIM, dtype=F32),
                   jnp.ones((HEAD_DIM, HEAD_DIM), F32)) / HEAD_DIM).astype(BF16)

    row = lambda v: v.reshape(1, -1)
    tile_row = lambda v, n: jnp.tile(v, n).reshape(1, -1)
    w_in_a_b = w_in_a.astype(BF16)
    w_in_b_b = jnp.concatenate([wq, w_in_b[:, :, Q_W:].astype(BF16)], axis=2)
    w_out_b = w_out.astype(BF16)
    w_out_perm_b = jnp.concatenate([wo_main, w_out_b[n_a:, Q_W:]], axis=1)
    w_kv_b = w_kv.astype(BF16)
    w_mem_b = w_mem_kv.astype(BF16)
    wup_b = w_ffn_up.astype(BF16).reshape(depth, D_MODEL, 2 * D_FF // MXU_DIM, MXU_DIM)
    wup_b = wup_b.transpose(0, 2, 1, 3)
    wdn_b = w_ffn_down.astype(BF16).reshape(depth, D_FF, D_MODEL // MXU_DIM, MXU_DIM)
    wdn_b = wdn_b.transpose(0, 2, 1, 3)
    fcw = ffn_conv_w
    fcb = ffn_conv_b.reshape(depth, 1, 2 * D_FF)
    wg_b = jnp.stack([_block_diag_gates(w_gate_x[l], w_gate_a[l]) for l in range(n_a)])
    sink_gk = sinks.reshape(-1, N_KV, GROUP).transpose(0, 2, 1)

    cos_p, sin_p = _rope_tables(jnp.arange(seq, dtype=jnp.int32))
    pos_s = PAST_LEN + jnp.zeros((dbsz,), jnp.int32)
    cos_s, sin_s = _rope_tables(pos_s)

    mem2d = mem_prompt.reshape(bsz * N_MEM, D_MODEL)
    pmk, pmv = _mem_kv(mem2d, mem_norm_g.reshape(depth, 1, D_MODEL), w_mem_b,
                       jnp.tile(mem_k_norm_g, (1, MEM_HEADS)).reshape(depth, 1, MEM_W), bd)
    pmk4 = pmk.reshape(depth, bsz, N_MEM, MEM_W)
    pmv4 = pmv.reshape(depth, bsz, N_MEM, MEM_W)

    x = x_prompt.reshape(bsz * seq, D_MODEL)
    zeros_rc = jnp.zeros((bsz, SUBLANES, D_RNN), F32)
    zeros_h = jnp.zeros((bsz, 1, D_RNN), F32)
    p_h, p_rc, p_fc = [], [], []
    ksh = vsh = None
    for l in range(depth):
        mqg = tile_row(mem_q_norm_g[l], MEM_HEADS)
        if l < n_a:
            x, hl, rct = _mixer_a(
                x, bsz, row(norm_mix_g[l]), (w_in_a_b, l), rnn_conv_w[l], row(rnn_conv_b[l]),
                (wg_b, l), row(b_gate_x[l]), row(b_gate_a[l]), row(lru_param[l]), mqg, bd,
                pmk4[l], pmv4[l], (w_out_b, l), zeros_rc, zeros_h)
            p_h.append(hl.reshape(bsz, D_RNN))
            p_rc.append(rct[:, SUBLANES - (CONV_A - 1):])
        else:
            j = l - n_a
            x = _mixer_b(
                x, bsz, sink_gk[j].reshape(-1), row(norm_mix_g[l]), (w_in_b_b, j),
                tile_row(q_norm_g[j], N_Q), cos_p, sin_p, ksh, vsh, mqg, bd,
                pmk4[l], pmv4[l], (w_out_perm_b, j))
        x, ut = _ffn(x, bsz, l, row(norm_ffn_g[l]), wup_b, fcw[l], fcb[l], wdn_b)
        p_fc.append(ut[:, SUBLANES - (CONV_F - 1):])
        if l == n_a - 1:
            ksh, vsh = _shared_kv(x, row(kv_norm_g), w_kv_b, tile_row(k_norm_g, N_KV), bd,
                                  cos_p, sin_p, TS_KV)
    y_prompt = x.reshape(bsz, seq, D_MODEL)
    keep = min(WINDOW, seq)
    p_k = ksh.reshape(bsz, seq, KV_W)[:, seq - keep:].reshape(bsz, keep, N_KV, HEAD_DIM)
    p_v = vsh.reshape(bsz, seq, KV_W)[:, seq - keep:].reshape(bsz, keep, N_KV, HEAD_DIM)
    p_mem_k = pmk.reshape(depth, bsz, N_MEM, MEM_HEADS, HEAD_DIM)
    p_mem_v = pmv.reshape(depth, bsz, N_MEM, MEM_HEADS, HEAD_DIM)

    xs = x_sample.reshape(dbsz, D_MODEL)
    cmk = cache_mem_k.transpose(0, 1, 3, 4, 2).reshape(depth, dbsz, MEM_W, N_MEM)
    cmv = cache_mem_v.transpose(0, 1, 3, 4, 2).reshape(depth, dbsz, MEM_W, N_MEM)
    wb = cache_swa_k.shape[1]
    ckb = cache_swa_k.transpose(0, 2, 3, 1).reshape(dbsz, KV_W, wb)
    cvb = cache_swa_v.transpose(0, 2, 3, 1).reshape(dbsz, KV_W, wb)
    s_h, s_rc, s_fc = [], [], []
    kn = vn = None
    for l in range(depth):
        mqg = tile_row(mem_q_norm_g[l], MEM_HEADS)
        if l < n_a:
            main, qn, hnew, xrpre = _dec_in_a(
                xs, row(norm_mix_g[l]), (w_in_a_b, l), rnn_conv_w[l], row(rnn_conv_b[l]),
                (wg_b, l), row(b_gate_x[l]), row(b_gate_a[l]), row(lru_param[l]), mqg, bd,
                state_rglru_conv[l].transpose(1, 0, 2), state_rglru_h[l])
            s_h.append(hnew)
            s_rc.append(jnp.concatenate([state_rglru_conv[l][:, 1:], xrpre[:, None, :]], axis=1))
            wo, wo_layer = w_out_b, l
        else:
            j = l - n_a
            q, qn = _dec_in_b(xs, row(norm_mix_g[l]), (w_in_b_b, j), tile_row(q_norm_g[j], N_Q),
                              cos_s, sin_s, mqg, bd)
            sink_rows = jnp.zeros((DEC_HEAD_ROWS, LANES), F32).at[:N_Q].set(
                jnp.broadcast_to(sink_gk[j].reshape(N_Q, 1), (N_Q, LANES)))
            main = _dec_swa(q, ckb, cvb, kn, vn, sink_rows)
            wo, wo_layer = w_out_perm_b, j
        mo = _dec_mem_attn(qn, cmk, cmv, l)
        xs, ug, uv = _dec_out_ffn(xs, main, mo, wo, wo_layer, l, row(norm_ffn_g[l]), wup_b,
                                  fcw[l], fcb[l], wdn_b, state_ffn_conv[l].transpose(1, 0, 2))
        unew = jnp.concatenate([ug, uv], axis=1)
        s_fc.append(jnp.concatenate([state_ffn_conv[l][:, 1:], unew[:, None, :]], axis=1))
        if l == n_a - 1:
            kn, vn = _shared_kv(xs, row(kv_norm_g), w_kv_b, tile_row(k_norm_g, N_KV), bd,
                                cos_s, sin_s, dbsz)
    y_sample = xs.reshape(dbsz, 1, D_MODEL)
    s_k = kn.reshape(dbsz, 1, N_KV, HEAD_DIM)
    s_v = vn.reshape(dbsz, 1, N_KV, HEAD_DIM)

    return (y_prompt, y_sample, jnp.stack(p_h), jnp.stack(p_rc), jnp.stack(p_fc), p_k, p_v,
            p_mem_k, p_mem_v, jnp.stack(s_h), jnp.stack(s_rc), jnp.stack(s_fc), s_k, s_v)
```

```python
import functools
import math

import jax
import jax.numpy as jnp
from jax import lax
from jax.experimental import pallas as pl
from jax.experimental.pallas import tpu as pltpu

F32 = jnp.float32
BF16 = jnp.bfloat16

D_MODEL = 1024
HEAD_DIM = 64
MEM_HEADS = 4
MEM_W = MEM_HEADS * HEAD_DIM
N_MEM = 256
D_RNN = D_MODEL - MEM_W
RNN_BLOCKS = D_RNN // HEAD_DIM
CONV_A = 4
LRU_C = 8.0
N_Q = D_RNN // HEAD_DIM
N_KV = 4
GROUP = N_Q // N_KV
Q_W = N_Q * HEAD_DIM
KV_W = N_KV * HEAD_DIM
WINDOW = 128
ROPE_THETA = 10000.0
D_FF = 3 * D_MODEL
CONV_F = 3
EPS = 1e-6
NEG = -1e30
ATT_SCALE = HEAD_DIM ** -0.5
PAST_LEN = 8192

SUBLANES = 8
LANES = 128
MXU_DIM = 256
VMEM_LIMIT_BYTES = 56 * 1024 * 1024

TS_MIX = 512
TS_FFN = 512
TS_KV = 512
FF_CHUNK = 512
N_FF_CHUNKS = D_FF // FF_CHUNK
DEC_SEQ_BLOCK = 8
CAST_TILE = 1024


def _mm(a, b):
    return jnp.dot(a.astype(BF16), b, preferred_element_type=F32)


def _mm_nt(a, b):
    return lax.dot_general(a.astype(BF16), b, (((1,), (1,)), ((), ())),
                           preferred_element_type=F32)


def _mm_split(a, b):
    hi = a.astype(BF16)
    lo = (a - hi.astype(F32)).astype(BF16)
    return (jnp.dot(hi, b, preferred_element_type=F32)
            + jnp.dot(lo, b, preferred_element_type=F32))


def _rmsnorm(x, g):
    ms = jnp.mean(x * x, axis=-1, keepdims=True)
    return x * lax.rsqrt(ms + EPS) * g


def _head_rmsnorm(x, bd, g):
    parts = []
    for c in range(x.shape[1] // MXU_DIM):
        xc = x[:, c * MXU_DIM:(c + 1) * MXU_DIM]
        ms = _mm_split(xc * xc, bd)
        parts.append(xc * lax.rsqrt(ms + EPS))
    y = parts[0] if len(parts) == 1 else jnp.concatenate(parts, axis=1)
    return y * g


def _tile_lanes(t, width):
    reps = width // t.shape[1]
    return t if reps == 1 else jnp.concatenate([t] * reps, axis=1)


def _rope(x, cos_t, sin_t):
    w = x.shape[1]
    lane = lax.broadcasted_iota(jnp.int32, x.shape, 1)
    first = (lane % HEAD_DIM) < (HEAD_DIM // 2)
    swapped = jnp.where(first, pltpu.roll(x, w - HEAD_DIM // 2, 1),
                        pltpu.roll(x, HEAD_DIM // 2, 1))
    return x * _tile_lanes(cos_t, w) + swapped * _tile_lanes(sin_t, w)


def _gelu(x):
    c = math.sqrt(2.0 / math.pi)
    return x * (0.5 * (1.0 + jnp.tanh(c * (x + 0.044715 * (x * x * x)))))


def _log_sigmoid(x):
    return jnp.minimum(x, 0.0) - jnp.log1p(jnp.exp(-jnp.abs(x)))


def _slab_stage(x, work_ref, work0, tail_ref, tail0):
    ts = x.shape[0]
    for s in range(x.shape[1] // LANES):
        xs = x[:, s * LANES:(s + 1) * LANES]
        buf = work_ref.at[work0 + s]
        buf[0:SUBLANES, :] = tail_ref[tail0 + s]
        buf[SUBLANES:SUBLANES + ts, :] = xs
        tail_ref[tail0 + s] = xs[ts - SUBLANES:]


def _slab_taps(ts, n_slabs, work_ref, work0, w_ref, b_ref, col0):
    k = w_ref.shape[0]
    outs = []
    for s in range(n_slabs):
        lanes = slice(col0 + s * LANES, col0 + (s + 1) * LANES)
        buf = work_ref.at[work0 + s]
        acc = buf[SUBLANES - (k - 1):SUBLANES - (k - 1) + ts, :] * w_ref[0:1, lanes]
        for j in range(1, k):
            off = SUBLANES - (k - 1 - j)
            acc = acc + buf[off:off + ts, :] * w_ref[j:j + 1, lanes]
        outs.append(acc + b_ref[:, lanes])
    return outs


def _sqrt_pos(x):
    return jnp.where(x > 0.0, x * lax.rsqrt(x), 0.0)


def _lru_coeffs(xr, wg_ref, bgx, bga, logsig):
    xb = xr.astype(BF16)
    gxs, gas = [], []
    for c in range(D_RNN // MXU_DIM):
        gg = jnp.dot(xb[:, c * MXU_DIM:(c + 1) * MXU_DIM], wg_ref[c],
                     preferred_element_type=F32)
        gxs.append(gg[:, :MXU_DIM])
        gas.append(gg[:, MXU_DIM:])
    gx = jax.nn.sigmoid(jnp.concatenate(gxs, axis=1) + bgx)
    ga = jax.nn.sigmoid(jnp.concatenate(gas, axis=1) + bga)
    log_a = ga * (LRU_C * logsig)
    a = jnp.exp(log_a)
    mult = _sqrt_pos(-jnp.tanh(log_a) * (a * a + 1.0))
    return a, mult * gx * xr


def _lru_scan(a, b, a_ref, b_ref, hc_ref):
    ts = a.shape[0]
    outs = []
    for s in range(a.shape[1] // LANES):
        lanes = slice(s * LANES, (s + 1) * LANES)
        a_s, b_s = a[:, lanes], b[:, lanes]
        abuf, bbuf = a_ref.at[s], b_ref.at[s]
        d = 1
        while d < SUBLANES:
            abuf[SUBLANES:SUBLANES + ts, :] = a_s
            bbuf[SUBLANES:SUBLANES + ts, :] = b_s
            b_s = a_s * bbuf[SUBLANES - d:SUBLANES - d + ts, :] + b_s
            a_s = a_s * abuf[SUBLANES - d:SUBLANES - d + ts, :]
            d *= 2
        h = hc_ref[:, lanes]
        hs = []
        for q in range(ts // SUBLANES):
            rows = slice(q * SUBLANES, (q + 1) * SUBLANES)
            h = a_s[rows] * h + b_s[rows]
            hs.append(h)
        hc_ref[:, lanes] = jnp.broadcast_to(h[SUBLANES - 1:], (SUBLANES, LANES))
        outs.append(jnp.concatenate(hs, axis=0))
    return jnp.concatenate(outs, axis=1)


def _head_mask(shape, h):
    lane = lax.broadcasted_iota(jnp.int32, shape, 1)
    return (lane >= h * HEAD_DIM) & (lane < (h + 1) * HEAD_DIM)


def _head_masked_stack(blocks, n_heads):
    parts = []
    for h in range(n_heads):
        for blk in blocks:
            parts.append(jnp.where(_head_mask(blk.shape, h), blk, 0.0).astype(BF16))
    return jnp.concatenate(parts, axis=0)


def _mem_attention(qn, kcat, vcat):
    s = _mm_nt(qn, kcat)
    parts = []
    for h in range(MEM_HEADS):
        sh = s[:, h * N_MEM:(h + 1) * N_MEM]
        m = jnp.max(sh, axis=-1, keepdims=True)
        p = jnp.exp(sh - m)
        den = jnp.sum(p, axis=-1, keepdims=True)
        parts.append((p * (1.0 / den)).astype(BF16))
    return jnp.dot(jnp.concatenate(parts, axis=1), vcat, preferred_element_type=F32)


def _cast_blocks_kernel(w_ref, o_ref):
    for b in range(o_ref.shape[0]):
        o_ref[b] = w_ref[:, b * MXU_DIM:(b + 1) * MXU_DIM].astype(BF16)


def _mem_kv_kernel(mem_ref, g_ref, w_ref, kg_ref, bd_ref, k_ref, v_ref):
    h = _mm(_rmsnorm(mem_ref[...], g_ref[0]), w_ref[0])
    k_ref[0] = _head_rmsnorm(h[:, :MEM_W], bd_ref[...], kg_ref[0])
    v_ref[0] = h[:, MEM_W:]


def _shared_kv_kernel(x_ref, g_ref, w_ref, kg_ref, bd_ref, cos_ref, sin_ref,
                      k_ref, v_ref):
    h = _mm(_rmsnorm(x_ref[...], g_ref[...]), w_ref[...])
    k = _head_rmsnorm(h[:, :KV_W], bd_ref[...], kg_ref[...])
    k_ref[...] = _rope(k, cos_ref[...], sin_ref[...])
    v_ref[...] = h[:, KV_W:]


def _mixer_a_kernel(x_ref, g_ref, win_ref, cw_ref, cb_ref, wg_ref, bgx_ref, bga_ref,
                    lru_ref, mqg_ref, bd_ref, mk_ref, mv_ref, wout_ref, rc0_ref, h0_ref,
                    xo_ref, hlast_ref, rctail_ref,
                    conv_s, tail_s, a_s, b_s, hc_s, kcat_s, vcat_s):
    ts = x_ref.shape[0]
    n_slabs = D_RNN // LANES

    @pl.when(pl.program_id(1) == 0)
    def _():
        for s in range(n_slabs):
            tail_s[s] = rc0_ref[0, :, s * LANES:(s + 1) * LANES]
        a_s[:, 0:SUBLANES, :] = jnp.ones((n_slabs, SUBLANES, LANES), F32)
        b_s[:, 0:SUBLANES, :] = jnp.zeros((n_slabs, SUBLANES, LANES), F32)
        hc_s[...] = jnp.broadcast_to(h0_ref[0], hc_s.shape)
        kcat_s[...] = _head_masked_stack([mk_ref[0]], MEM_HEADS)
        vcat_s[...] = _head_masked_stack([mv_ref[0]], MEM_HEADS)

    x = x_ref[...]
    u = _mm(_rmsnorm(x, g_ref[...]), win_ref[...])
    gate = u[:, :D_RNN]
    xr_pre = u[:, D_RNN:2 * D_RNN]
    qm = u[:, 2 * D_RNN:]

    _slab_stage(xr_pre, conv_s, 0, tail_s, 0)
    xr = jnp.concatenate(_slab_taps(ts, n_slabs, conv_s, 0, cw_ref, cb_ref, 0), axis=1)
    rctail_ref[0] = xr_pre[ts - SUBLANES:]

    a, b = _lru_coeffs(xr, wg_ref, bgx_ref[...], bga_ref[...], _log_sigmoid(lru_ref[...]))
    h = _lru_scan(a, b, a_s, b_s, hc_s)
    hlast_ref[0] = h[ts - 1:ts]
    main = h * _gelu(gate)

    qn = _head_rmsnorm(qm, bd_ref[...], mqg_ref[...]) * ATT_SCALE
    mo = _mem_attention(qn, kcat_s[...], vcat_s[...])

    y = _mm(jnp.concatenate([main, mo], axis=1), wout_ref[...])
    xo_ref[...] = x + y


def _mixer_b_kernel(sink_ref, x_ref, g_ref, win_ref, qg_ref, cos_ref, sin_ref,
                    kcur_ref, kprev_ref, vcur_ref, vprev_ref,
                    mqg_ref, bd_ref, mk_ref, mv_ref, wout_ref,
                    xo_ref, kcat_s, vcat_s):
    ts = x_ref.shape[0]
    i = pl.program_id(1)

    @pl.when(i == 0)
    def _():
        kcat_s[...] = _head_masked_stack([mk_ref[0]], MEM_HEADS)
        vcat_s[...] = _head_masked_stack([mv_ref[0]], MEM_HEADS)

    x = x_ref[...]
    u = _mm(_rmsnorm(x, g_ref[...]), win_ref[...])
    q = _head_rmsnorm(u[:, :Q_W], bd_ref[...], qg_ref[...])
    q = (_rope(q, cos_ref[...], sin_ref[...]) * ATT_SCALE).astype(BF16)
    qm = u[:, Q_W:]

    row = lax.broadcasted_iota(jnp.int32, (WINDOW, 2 * WINDOW), 0)
    kj = lax.broadcasted_iota(jnp.int32, (WINDOW, 2 * WINDOW), 1) - WINDOW
    mask_inner = (kj <= row) & (kj >= row - WINDOW)
    mask_first = (kj <= row) & (kj >= jnp.maximum(row - WINDOW, jnp.where(i > 0, -WINDOW, 0)))

    kt = kcur_ref[...]
    vt = vcur_ref[...]

    def block_scores(jb):
        lo, hi = jb * WINDOW, (jb + 1) * WINDOW
        if jb == 0:
            kp, vp = kprev_ref[...], vprev_ref[...]
        else:
            kp, vp = kt[lo - WINDOW:lo], vt[lo - WINDOW:lo]
        kcat = _head_masked_stack([kp, kt[lo:hi]], N_KV)
        vcat = _head_masked_stack([vp, vt[lo:hi]], N_KV)
        qs = jnp.concatenate([q[lo:hi, g * KV_W:(g + 1) * KV_W] for g in range(GROUP)], axis=0)
        s = lax.dot_general(qs, kcat, (((1,), (1,)), ((), ())), preferred_element_type=F32)
        return s, vcat

    n_blocks = ts // WINDOW
    mains = []
    nxt = block_scores(0)
    for jb in range(n_blocks):
        s, vcat = nxt
        if jb + 1 < n_blocks:
            nxt = block_scores(jb + 1)
        mask = mask_first if jb == 0 else mask_inner
        prow = []
        for g in range(GROUP):
            pseg = []
            for kv in range(N_KV):
                seg = s[g * WINDOW:(g + 1) * WINDOW, kv * 2 * WINDOW:(kv + 1) * 2 * WINDOW]
                seg = jnp.where(mask, seg, NEG)
                sink = sink_ref[g * N_KV + kv]
                m = jnp.maximum(jnp.max(seg, axis=-1, keepdims=True), sink)
                p = jnp.exp(seg - m)
                den = jnp.sum(p, axis=-1, keepdims=True) + jnp.exp(sink - m)
                pseg.append((p * (1.0 / den)).astype(BF16))
            prow.append(jnp.concatenate(pseg, axis=1))
        o = jnp.dot(jnp.concatenate(prow, axis=0), vcat, preferred_element_type=F32)
        mains.append(jnp.concatenate([o[g * WINDOW:(g + 1) * WINDOW] for g in range(GROUP)], axis=1))
    main = mains[0] if len(mains) == 1 else jnp.concatenate(mains, axis=0)

    qn = _head_rmsnorm(qm, bd_ref[...], mqg_ref[...]) * ATT_SCALE
    mo = _mem_attention(qn, kcat_s[...], vcat_s[...])

    y = _mm(jnp.concatenate([main, mo], axis=1), wout_ref[...])
    xo_ref[...] = x + y


def _ffn_kernel(x_ref, g_ref, wup_ref, cw_ref, cb_ref, wdn_ref,
                xo_ref, utail_ref, conv_s, tail_s, act_s, xres_s, *, tiles_per_seq):
    ts = x_ref.shape[0]
    slabs = FF_CHUNK // LANES
    blk_slabs = MXU_DIM // LANES
    blks = FF_CHUNK // MXU_DIM
    i = pl.program_id(0)
    slot = i % 2
    pslot = 1 - slot

    @pl.when(i == 0)
    def _():
        act_s[1] = jnp.zeros(act_s.shape[1:], act_s.dtype)
        xres_s[1] = jnp.zeros(xres_s.shape[1:], xres_s.dtype)

    @pl.when(i % tiles_per_seq == 0)
    def _():
        tail_s[...] = jnp.zeros_like(tail_s)

    x = x_ref[...]
    xres_s[slot] = x
    hn = _rmsnorm(x, g_ref[...]).astype(BF16)

    def chunk_cols(j):
        return [(c, ((j % 2) * 2 + half) * slabs) for half, c in enumerate((j, N_FF_CHUNKS + j))]

    def up_project(j):
        for c, work0 in chunk_cols(j):
            for b in range(blks):
                blk = c * blks + b
                u = jnp.dot(hn, wup_ref[blk], preferred_element_type=F32)
                _slab_stage(u, conv_s, work0 + b * blk_slabs, tail_s, c * slabs + b * blk_slabs)
                utail_ref[0, :, blk * MXU_DIM:(blk + 1) * MXU_DIM] = u[ts - SUBLANES:]

    def activation(j):
        cg, cv = [_slab_taps(ts, slabs, conv_s, work0, cw_ref, cb_ref, c * FF_CHUNK)
                  for c, work0 in chunk_cols(j)]
        return jnp.concatenate([_gelu(a) * b for a, b in zip(cg, cv)], axis=1).astype(BF16)

    def down_prev(n):
        return jnp.dot(act_s[pslot], wdn_ref[n], preferred_element_type=F32)

    n_dn = wdn_ref.shape[0]
    outs = [None] * n_dn
    outs[0] = down_prev(0)
    up_project(0)
    for j in range(N_FF_CHUNKS):
        if j + 1 < N_FF_CHUNKS:
            up_project(j + 1)
        else:
            outs[1] = down_prev(1)
        act_s[slot, :, j * FF_CHUNK:(j + 1) * FF_CHUNK] = activation(j)
    for n in range(2, n_dn):
        outs[n] = down_prev(n)
    xo_ref[...] = xres_s[pslot] + jnp.concatenate(outs, axis=1)


def _dec_in_a_kernel(x_ref, g_ref, win_ref, cw_ref, cb_ref, wg_ref, bgx_ref, bga_ref,
                     lru_ref, mqg_ref, bd_ref, b0_ref, b1_ref, b2_ref, h0_ref,
                     main_ref, qn_ref, hnew_ref, xrpre_ref):
    u = _mm(_rmsnorm(x_ref[...], g_ref[...]), win_ref[...])
    gate = u[:, :D_RNN]
    xr_pre = u[:, D_RNN:2 * D_RNN]
    qm = u[:, 2 * D_RNN:]
    xr = b0_ref[...] * cw_ref[0:1, :]
    xr = xr + b1_ref[...] * cw_ref[1:2, :]
    xr = xr + b2_ref[...] * cw_ref[2:3, :]
    xr = xr + xr_pre * cw_ref[3:4, :]
    xr = xr + cb_ref[...]
    a, b = _lru_coeffs(xr, wg_ref, bgx_ref[...], bga_ref[...], _log_sigmoid(lru_ref[...]))
    h = a * h0_ref[...] + b
    main_ref[...] = h * _gelu(gate)
    qn_ref[...] = _head_rmsnorm(qm, bd_ref[...], mqg_ref[...]) * ATT_SCALE
    hnew_ref[...] = h
    xrpre_ref[...] = xr_pre


def _dec_in_b_kernel(x_ref, g_ref, win_ref, qg_ref, cos_ref, sin_ref, mqg_ref, bd_ref,
                     q_ref, qn_ref):
    u = _mm(_rmsnorm(x_ref[...], g_ref[...]), win_ref[...])
    q = _head_rmsnorm(u[:, :Q_W], bd_ref[...], qg_ref[...])
    q_ref[...] = _rope(q, cos_ref[...], sin_ref[...]) * ATT_SCALE
    qn_ref[...] = _head_rmsnorm(u[:, Q_W:], bd_ref[...], mqg_ref[...]) * ATT_SCALE


DEC_HEAD_ROWS = 16


def _own_head_lanes(n_heads, width):
    row = lax.broadcasted_iota(jnp.int32, (DEC_HEAD_ROWS, width), 0)
    lane = lax.broadcasted_iota(jnp.int32, (DEC_HEAD_ROWS, width), 1)
    start = (row % (width // HEAD_DIM)) * HEAD_DIM
    return (lane >= start) & (lane < start + HEAD_DIM) & (row < n_heads)


def _dec_mem_attn_kernel(q_ref, kt_ref, vt_ref, o_ref):
    own = _own_head_lanes(MEM_HEADS, MEM_W)
    for s in range(q_ref.shape[0]):
        q_rows = jnp.broadcast_to(q_ref[s:s + 1, :], (DEC_HEAD_ROWS, MEM_W))
        qbd = jnp.where(own, q_rows, 0.0).astype(BF16)
        sc = jnp.dot(qbd, kt_ref[s].astype(BF16), preferred_element_type=F32)
        m = jnp.max(sc, axis=-1, keepdims=True)
        p = jnp.exp(sc - m)
        den = jnp.sum(p, axis=-1, keepdims=True)
        pn = (p * (1.0 / den)).astype(BF16)
        o_all = lax.dot_general(pn, vt_ref[s].astype(BF16), (((1,), (1,)), ((), ())),
                                preferred_element_type=F32)
        o_ref[s:s + 1, :] = jnp.sum(jnp.where(own, o_all, 0.0), axis=0, keepdims=True)


def _dec_swa_kernel(q_ref, kt_ref, vt_ref, kn_ref, vn_ref, sink_ref, o_ref):
    n_heads = GROUP * N_KV
    own = _own_head_lanes(n_heads, KV_W)
    grp = lax.broadcasted_iota(jnp.int32, (DEC_HEAD_ROWS, KV_W), 0) // N_KV
    sink = sink_ref[:, 0:1]
    for s in range(q_ref.shape[0]):
        q_rows = jnp.zeros((DEC_HEAD_ROWS, KV_W), F32)
        for g in range(GROUP):
            qg = jnp.broadcast_to(q_ref[s:s + 1, g * KV_W:(g + 1) * KV_W], (DEC_HEAD_ROWS, KV_W))
            q_rows = jnp.where(grp == g, qg, q_rows)
        qbd = jnp.where(own, q_rows, 0.0)
        s_buf = jnp.dot(qbd.astype(BF16), kt_ref[s].astype(BF16), preferred_element_type=F32)
        s_new = jnp.sum(qbd * kn_ref[s:s + 1, :], axis=-1, keepdims=True)
        m = jnp.maximum(jnp.maximum(jnp.max(s_buf, axis=-1, keepdims=True), s_new), sink)
        p_buf = jnp.exp(s_buf - m)
        p_new = jnp.exp(s_new - m)
        den = jnp.sum(p_buf, axis=-1, keepdims=True) + p_new + jnp.exp(sink - m)
        r = 1.0 / den
        o_all = lax.dot_general((p_buf * r).astype(BF16), vt_ref[s].astype(BF16),
                                (((1,), (1,)), ((), ())), preferred_element_type=F32)
        o_all = jnp.where(own, o_all + (p_new * r) * vn_ref[s:s + 1, :], 0.0)
        o_sum = o_all + pltpu.roll(o_all, 1, 0)
        o_sum = o_sum + pltpu.roll(o_sum, 2, 0)
        for g in range(GROUP):
            last = (g + 1) * N_KV - 1
            o_ref[s:s + 1, g * KV_W:(g + 1) * KV_W] = o_sum[last:last + 1, :]


def _dec_out_ffn_kernel(x_ref, main_ref, mo_ref, wout_ref, g_ref,
                        wug_ref, wuv_ref, cwg_ref, cwv_ref, cbg_ref, cbv_ref, wdn_ref,
                        sg0_ref, sg1_ref, sv0_ref, sv1_ref,
                        xo_ref, ug_ref, uv_ref,
                        xmid_s, hn_s, acc_s):
    j = pl.program_id(0)

    @pl.when(j == 0)
    def _():
        y = _mm(jnp.concatenate([main_ref[...], mo_ref[...]], axis=1), wout_ref[...])
        xmid = x_ref[...] + y
        xmid_s[...] = xmid
        hn_s[...] = _rmsnorm(xmid, g_ref[...]).astype(BF16)
        acc_s[...] = jnp.zeros_like(acc_s)

    hn = hn_s[...]
    up = lambda w_ref: jnp.concatenate(
        [jnp.dot(hn, w_ref[b], preferred_element_type=F32) for b in range(w_ref.shape[0])], axis=1)
    ug = up(wug_ref)
    uv = up(wuv_ref)
    ug_ref[...] = ug
    uv_ref[...] = uv
    cg = (sg0_ref[...] * cwg_ref[0:1, :] + sg1_ref[...] * cwg_ref[1:2, :]
          + ug * cwg_ref[2:3, :] + cbg_ref[...])
    cv = (sv0_ref[...] * cwv_ref[0:1, :] + sv1_ref[...] * cwv_ref[1:2, :]
          + uv * cwv_ref[2:3, :] + cbv_ref[...])
    act = (_gelu(cg) * cv).astype(BF16)
    for n in range(wdn_ref.shape[0]):
        acc_s[:, n * MXU_DIM:(n + 1) * MXU_DIM] += jnp.dot(act, wdn_ref[n],
                                                           preferred_element_type=F32)

    @pl.when(j == pl.num_programs(0) - 1)
    def _():
        xo_ref[...] = xmid_s[...] + acc_s[...]


def _const_spec(shape):
    nd = len(shape)
    return pl.BlockSpec(shape, lambda *_: (0,) * nd)


def _layer_spec(arr, layer):
    nd = arr.ndim - 1
    return pl.BlockSpec((None,) + arr.shape[1:], lambda *_: (layer,) + (0,) * nd)


def _resident_spec(op):
    return _layer_spec(*op) if isinstance(op, tuple) else _const_spec(op.shape)


def _operand(op):
    return op[0] if isinstance(op, tuple) else op


def _params(*sem):
    return pltpu.CompilerParams(dimension_semantics=sem, vmem_limit_bytes=VMEM_LIMIT_BYTES)


def _cast_blocks(w):
    layers, r, c = w.shape
    tile = CAST_TILE
    per = tile // MXU_DIM
    return pl.pallas_call(
        _cast_blocks_kernel,
        grid=(layers, r // tile, c // tile),
        in_specs=[pl.BlockSpec((None, tile, tile), lambda l, i, j: (l, i, j))],
        out_specs=pl.BlockSpec((None, per, tile, MXU_DIM), lambda l, i, j: (l, j, i, 0)),
        out_shape=jax.ShapeDtypeStruct((layers, c // MXU_DIM, r, MXU_DIM), BF16),
        compiler_params=_params("arbitrary", "arbitrary", "arbitrary"),
        name="cast_blocks",
    )(w)


def _mem_kv(mem2d, g, w, kg, bd):
    depth = w.shape[0]
    rows = mem2d.shape[0]
    out = jax.ShapeDtypeStruct((depth, rows, MEM_W), F32)
    return pl.pallas_call(
        _mem_kv_kernel,
        grid=(depth,),
        in_specs=[
            _const_spec(mem2d.shape),
            pl.BlockSpec((1, 1, D_MODEL), lambda l: (l, 0, 0)),
            pl.BlockSpec((1, D_MODEL, 2 * MEM_W), lambda l: (l, 0, 0)),
            pl.BlockSpec((1, 1, MEM_W), lambda l: (l, 0, 0)),
            _const_spec(bd.shape),
        ],
        out_specs=[pl.BlockSpec((1, rows, MEM_W), lambda l: (l, 0, 0))] * 2,
        out_shape=[out, out],
        compiler_params=_params("arbitrary"),
        name="mem_kv",
    )(mem2d, g, w, kg, bd)


def _shared_kv(x2d, g, w, kg, bd, cos_t, sin_t, ts):
    rows = x2d.shape[0]
    tab_blocks = cos_t.shape[0] // ts
    out = jax.ShapeDtypeStruct((rows, KV_W), F32)
    return pl.pallas_call(
        _shared_kv_kernel,
        grid=(rows // ts,),
        in_specs=[
            pl.BlockSpec((ts, D_MODEL), lambda i: (i, 0)),
            _const_spec(g.shape), _const_spec(w.shape), _const_spec(kg.shape),
            _const_spec(bd.shape),
            pl.BlockSpec((ts, LANES), lambda i: (i % tab_blocks, 0)),
            pl.BlockSpec((ts, LANES), lambda i: (i % tab_blocks, 0)),
        ],
        out_specs=[pl.BlockSpec((ts, KV_W), lambda i: (i, 0))] * 2,
        out_shape=[out, out],
        compiler_params=_params("arbitrary"),
        name="shared_kv",
    )(x2d, g, w, kg, bd, cos_t, sin_t)


def _mixer_a(x2d, bsz, g, win, cw, cb, wg, bgx, bga, lru, mqg, bd, mk, mv, wout, rc0, h0):
    rows = x2d.shape[0]
    ts = TS_MIX
    nt = rows // bsz // ts
    consts = (g, win, cw, cb, wg, bgx, bga, lru, mqg, bd)
    per_b3 = lambda b, i: (b, 0, 0)
    return pl.pallas_call(
        _mixer_a_kernel,
        grid=(bsz, nt),
        in_specs=[pl.BlockSpec((ts, D_MODEL), lambda b, i: (b * nt + i, 0))]
        + [_resident_spec(c) for c in consts]
        + [pl.BlockSpec((1, N_MEM, MEM_W), per_b3)] * 2
        + [_resident_spec(wout),
           pl.BlockSpec((1, SUBLANES, D_RNN), per_b3),
           pl.BlockSpec((1, 1, D_RNN), per_b3)],
        out_specs=[
            pl.BlockSpec((ts, D_MODEL), lambda b, i: (b * nt + i, 0)),
            pl.BlockSpec((1, 1, D_RNN), per_b3),
            pl.BlockSpec((1, SUBLANES, D_RNN), per_b3),
        ],
        out_shape=[
            jax.ShapeDtypeStruct((rows, D_MODEL), F32),
            jax.ShapeDtypeStruct((bsz, 1, D_RNN), F32),
            jax.ShapeDtypeStruct((bsz, SUBLANES, D_RNN), F32),
        ],
        scratch_shapes=[
            pltpu.VMEM((D_RNN // LANES, SUBLANES + ts, LANES), F32),
            pltpu.VMEM((D_RNN // LANES, SUBLANES, LANES), F32),
            pltpu.VMEM((D_RNN // LANES, SUBLANES + ts, LANES), F32),
            pltpu.VMEM((D_RNN // LANES, SUBLANES + ts, LANES), F32),
            pltpu.VMEM((SUBLANES, D_RNN), F32),
            pltpu.VMEM((MEM_HEADS * N_MEM, MEM_W), BF16),
            pltpu.VMEM((MEM_HEADS * N_MEM, MEM_W), BF16),
        ],
        compiler_params=_params("arbitrary", "arbitrary"),
        name="mixer_a",
    )(x2d, *[_operand(c) for c in consts], mk, mv, _operand(wout), rc0, h0)


def _mixer_b(x2d, bsz, sink_tab, g, win, qg, cos_t, sin_t, ksh, vsh, mqg, bd, mk, mv, wout):
    rows = x2d.shape[0]
    ts = TS_MIX
    nt = rows // bsz // ts
    wpt = ts // WINDOW
    cur = lambda b, i: (b * nt + i, 0)
    prev = lambda b, i: (jnp.maximum((b * nt + i) * wpt - 1, 0), 0)
    per_b3 = lambda b, i: (b, 0, 0)
    return pl.pallas_call(
        _mixer_b_kernel,
        grid=(bsz, nt),
        in_specs=[
            pl.BlockSpec(memory_space=pltpu.SMEM),
            pl.BlockSpec((ts, D_MODEL), cur),
            _const_spec(g.shape), _resident_spec(win), _const_spec(qg.shape),
            pl.BlockSpec((ts, LANES), lambda b, i: (i, 0)),
            pl.BlockSpec((ts, LANES), lambda b, i: (i, 0)),
            pl.BlockSpec((ts, KV_W), cur), pl.BlockSpec((WINDOW, KV_W), prev),
            pl.BlockSpec((ts, KV_W), cur), pl.BlockSpec((WINDOW, KV_W), prev),
            _const_spec(mqg.shape), _const_spec(bd.shape),
            pl.BlockSpec((1, N_MEM, MEM_W), per_b3), pl.BlockSpec((1, N_MEM, MEM_W), per_b3),
            _resident_spec(wout),
        ],
        out_specs=pl.BlockSpec((ts, D_MODEL), cur),
        out_shape=jax.ShapeDtypeStruct((rows, D_MODEL), F32),
        scratch_shapes=[
            pltpu.VMEM((MEM_HEADS * N_MEM, MEM_W), BF16),
            pltpu.VMEM((MEM_HEADS * N_MEM, MEM_W), BF16),
        ],
        compiler_params=_params("arbitrary", "arbitrary"),
        name="mixer_b",
    )(sink_tab, x2d, g, _operand(win), qg, cos_t, sin_t, ksh, ksh, vsh, vsh, mqg, bd, mk, mv,
      _operand(wout))


def _ffn(x2d, bsz, layer, g, wup, cw, cb, wdn):
    rows = x2d.shape[0]
    ts = TS_FFN
    nt = rows // ts
    tiles_per_seq = rows // bsz // ts
    last = nt - 1
    return pl.pallas_call(
        functools.partial(_ffn_kernel, tiles_per_seq=tiles_per_seq),
        grid=(nt + 1,),
        in_specs=[pl.BlockSpec((ts, D_MODEL), lambda i: (jnp.minimum(i, last), 0)),
                  _const_spec(g.shape), _layer_spec(wup, layer), _const_spec(cw.shape),
                  _const_spec(cb.shape), _layer_spec(wdn, layer)],
        out_specs=[
            pl.BlockSpec((ts, D_MODEL), lambda i: (jnp.maximum(i - 1, 0), 0)),
            pl.BlockSpec((1, SUBLANES, 2 * D_FF),
                         lambda i: (jnp.minimum(i, last) // tiles_per_seq, 0, 0)),
        ],
        out_shape=[
            jax.ShapeDtypeStruct((rows, D_MODEL), F32),
            jax.ShapeDtypeStruct((bsz, SUBLANES, 2 * D_FF), F32),
        ],
        scratch_shapes=[pltpu.VMEM((4 * FF_CHUNK // LANES, SUBLANES + ts, LANES), F32),
                        pltpu.VMEM((2 * D_FF // LANES, SUBLANES, LANES), F32),
                        pltpu.VMEM((2, ts, D_FF), BF16),
                        pltpu.VMEM((2, ts, D_MODEL), F32)],
        compiler_params=_params("arbitrary"),
        name="ffn",
    )(x2d, g, wup, cw, cb, wdn)


def _dec_in_a(x, g, win, cw, cb, wg, bgx, bga, lru, mqg, bd, rc, h0):
    n = x.shape[0]
    consts = (x, g, win, cw, cb, wg, bgx, bga, lru, mqg, bd)
    buf = lambda j: pl.BlockSpec((None, n, D_RNN), lambda i: (j, 0, 0))
    return pl.pallas_call(
        _dec_in_a_kernel,
        grid=(1,),
        in_specs=[_resident_spec(c) for c in consts]
        + [buf(0), buf(1), buf(2), _const_spec(h0.shape)],
        out_specs=[_const_spec((n, D_RNN)), _const_spec((n, MEM_W)),
                   _const_spec((n, D_RNN)), _const_spec((n, D_RNN))],
        out_shape=[jax.ShapeDtypeStruct((n, D_RNN), F32), jax.ShapeDtypeStruct((n, MEM_W), F32),
                   jax.ShapeDtypeStruct((n, D_RNN), F32), jax.ShapeDtypeStruct((n, D_RNN), F32)],
        compiler_params=_params("arbitrary"),
        name="dec_in_a",
    )(*[_operand(c) for c in consts], rc, rc, rc, h0)


def _dec_in_b(x, g, win, qg, cos_t, sin_t, mqg, bd):
    n = x.shape[0]
    args = (x, g, win, qg, cos_t, sin_t, mqg, bd)
    return pl.pallas_call(
        _dec_in_b_kernel,
        grid=(1,),
        in_specs=[_resident_spec(a) for a in args],
        out_specs=[_const_spec((n, Q_W)), _const_spec((n, MEM_W))],
        out_shape=[jax.ShapeDtypeStruct((n, Q_W), F32), jax.ShapeDtypeStruct((n, MEM_W), F32)],
        compiler_params=_params("arbitrary"),
        name="dec_in_b",
    )(*[_operand(a) for a in args])


def _dec_mem_attn(qn, ckt, cvt, layer):
    n = qn.shape[0]
    sb = DEC_SEQ_BLOCK
    return pl.pallas_call(
        _dec_mem_attn_kernel,
        grid=(n // sb,),
        in_specs=[pl.BlockSpec((sb, MEM_W), lambda i: (i, 0)),
                  pl.BlockSpec((None, sb, MEM_W, N_MEM), lambda i: (layer, i, 0, 0)),
                  pl.BlockSpec((None, sb, MEM_W, N_MEM), lambda i: (layer, i, 0, 0))],
        out_specs=pl.BlockSpec((sb, MEM_W), lambda i: (i, 0)),
        out_shape=jax.ShapeDtypeStruct((n, MEM_W), F32),
        compiler_params=_params("arbitrary"),
        name="dec_mem_attn",
    )(qn, ckt, cvt)


def _dec_swa(q, kbt, vbt, kn, vn, sink_rows):
    n = q.shape[0]
    wb = kbt.shape[2]
    sb = DEC_SEQ_BLOCK
    assert wb <= WINDOW
    return pl.pallas_call(
        _dec_swa_kernel,
        grid=(n // sb,),
        in_specs=[pl.BlockSpec((sb, Q_W), lambda i: (i, 0)),
                  pl.BlockSpec((sb, KV_W, wb), lambda i: (i, 0, 0)),
                  pl.BlockSpec((sb, KV_W, wb), lambda i: (i, 0, 0)),
                  pl.BlockSpec((sb, KV_W), lambda i: (i, 0)),
                  pl.BlockSpec((sb, KV_W), lambda i: (i, 0)),
                  _const_spec(sink_rows.shape)],
        out_specs=pl.BlockSpec((sb, Q_W), lambda i: (i, 0)),
        out_shape=jax.ShapeDtypeStruct((n, Q_W), F32),
        compiler_params=_params("arbitrary"),
        name="dec_swa",
    )(q, kbt, vbt, kn, vn, sink_rows)


def _dec_out_ffn(x, main, mo, wout, wout_layer, layer, g, wup, cw, cb, wdn, st):
    n = x.shape[0]
    nch = N_FF_CHUNKS
    blks = FF_CHUNK // MXU_DIM
    lo = lambda j: (0, j)
    hi = lambda j: (0, nch + j)
    state = lambda r, off: pl.BlockSpec((None, n, FF_CHUNK), lambda j: (r, 0, off + j))
    up_blocks = lambda off: pl.BlockSpec((None, blks, D_MODEL, MXU_DIM),
                                         lambda j: (layer, off + j, 0, 0))
    return pl.pallas_call(
        _dec_out_ffn_kernel,
        grid=(nch,),
        in_specs=[_const_spec(x.shape), _const_spec(main.shape), _const_spec(mo.shape),
                  _layer_spec(wout, wout_layer), _const_spec(g.shape),
                  up_blocks(0), up_blocks(nch),
                  pl.BlockSpec((CONV_F, FF_CHUNK), lo), pl.BlockSpec((CONV_F, FF_CHUNK), hi),
                  pl.BlockSpec((1, FF_CHUNK), lo), pl.BlockSpec((1, FF_CHUNK), hi),
                  pl.BlockSpec((None, wdn.shape[1], FF_CHUNK, MXU_DIM), lambda j: (layer, 0, j, 0)),
                  state(0, 0), state(1, 0), state(0, nch), state(1, nch)],
        out_specs=[_const_spec((n, D_MODEL)),
                   pl.BlockSpec((n, FF_CHUNK), lambda j: (0, j)),
                   pl.BlockSpec((n, FF_CHUNK), lambda j: (0, j))],
        out_shape=[jax.ShapeDtypeStruct((n, D_MODEL), F32),
                   jax.ShapeDtypeStruct((n, D_FF), F32),
                   jax.ShapeDtypeStruct((n, D_FF), F32)],
        scratch_shapes=[pltpu.VMEM((n, D_MODEL), F32), pltpu.VMEM((n, D_MODEL), BF16),
                        pltpu.VMEM((n, D_MODEL), F32)],
        compiler_params=_params("arbitrary"),
        name="dec_out_ffn",
    )(x, main, mo, wout, g, wup, wup, cw, cw, cb, cb, wdn, st, st, st, st)


def _rope_tables(pos):
    half = HEAD_DIM // 2
    inv = ROPE_THETA ** (-jnp.arange(half, dtype=F32) / half)
    ang = pos.astype(F32)[:, None] * inv[None, :]
    cos = jnp.cos(ang)
    sin = jnp.sin(ang)
    reps = LANES // HEAD_DIM
    cos_t = jnp.tile(jnp.concatenate([cos, cos], axis=1), (1, reps))
    sin_t = jnp.tile(jnp.concatenate([-sin, sin], axis=1), (1, reps))
    return cos_t, sin_t


def _block_diag_gates(wx, wa):
    per = MXU_DIM // HEAD_DIM
    eye = jnp.eye(per, dtype=F32)

    def bd(w):
        w4 = w.reshape(RNN_BLOCKS // per, per, HEAD_DIM, HEAD_DIM)
        return jnp.einsum('ckij,kK->ckiKj', w4, eye).reshape(RNN_BLOCKS // per, MXU_DIM, MXU_DIM)

    return jnp.concatenate([bd(wx), bd(wa)], axis=2).astype(BF16)


def kernel(x_prompt, x_sample, state_rglru_h, state_rglru_conv, state_ffn_conv, cache_swa_k, cache_swa_v, cache_mem_k, cache_mem_v, mem_prompt, norm_mix_g, norm_ffn_g, w_in_a, rnn_conv_w, rnn_conv_b, w_gate_x, b_gate_x, w_gate_a, b_gate_a, lru_param, w_in_b, q_norm_g, sinks, kv_norm_g, w_kv, k_norm_g, mem_norm_g, w_mem_kv, mem_q_norm_g, mem_k_norm_g, w_out, w_ffn_up, ffn_conv_w, ffn_conv_b, w_ffn_down):
    bsz, seq, _ = x_prompt.shape
    dbsz = x_sample.shape[0]
    depth = norm_mix_g.shape[0]
    n_a = w_in_a.shape[0]
    assert x_sample.shape[1] == 1
    assert seq % TS_MIX == 0 and seq % TS_FFN == 0 and seq % TS_KV == 0 and TS_MIX % WINDOW == 0

    n_b = w_in_b.shape[0]
    wq = w_in_b[:, :, :Q_W].astype(BF16).reshape(n_b, D_MODEL, N_KV, GROUP, HEAD_DIM)
    wq = wq.transpose(0, 1, 3, 2, 4).reshape(n_b, D_MODEL, Q_W)
    wo_main = w_out[n_a:, :Q_W].astype(BF16).reshape(n_b, N_KV, GROUP, HEAD_DIM, D_MODEL)
    wo_main = wo_main.transpose(0, 2, 1, 3, 4).reshape(n_b, Q_W, D_MODEL)
    bd =(jnp.kron(jnp.eye(MXU_DIM // HEAD_DIM, dtype=F32),
                   jnp.ones((HEAD_DIM, HEAD_DIM), F32)) / HEAD_DIM).astype(BF16)

    row = lambda v: v.reshape(1, -1)
    tile_row = lambda v, n: jnp.tile(v, n).reshape(1, -1)
    w_in_a_b = w_in_a.astype(BF16)
    w_in_b_b = jnp.concatenate([wq, w_in_b[:, :, Q_W:].astype(BF16)], axis=2)
    w_out_b = w_out.astype(BF16)
    w_out_perm_b = jnp.concatenate([wo_main, w_out_b[n_a:, Q_W:]], axis=1)
    w_kv_b = w_kv.astype(BF16)
    w_mem_b = w_mem_kv.astype(BF16)
    wup_b = _cast_blocks(w_ffn_up)
    wdn_b = _cast_blocks(w_ffn_down)
    fcw = ffn_conv_w
    fcb = ffn_conv_b.reshape(depth, 1, 2 * D_FF)
    wg_b = jnp.stack([_block_diag_gates(w_gate_x[l], w_gate_a[l]) for l in range(n_a)])
    sink_gk = sinks.reshape(-1, N_KV, GROUP).transpose(0, 2, 1)

    cos_p, sin_p = _rope_tables(jnp.arange(seq, dtype=jnp.int32))
    pos_s = PAST_LEN + jnp.zeros((dbsz,), jnp.int32)
    cos_s, sin_s = _rope_tables(pos_s)

    mem2d = mem_prompt.reshape(bsz * N_MEM, D_MODEL)
    pmk, pmv = _mem_kv(mem2d, mem_norm_g.reshape(depth, 1, D_MODEL), w_mem_b,
                       jnp.tile(mem_k_norm_g, (1, MEM_HEADS)).reshape(depth, 1, MEM_W), bd)
    pmk4 = pmk.reshape(depth, bsz, N_MEM, MEM_W)
    pmv4 = pmv.reshape(depth, bsz, N_MEM, MEM_W)

    x = x_prompt.reshape(bsz * seq, D_MODEL)
    zeros_rc = jnp.zeros((bsz, SUBLANES, D_RNN), F32)
    zeros_h = jnp.zeros((bsz, 1, D_RNN), F32)
    p_h, p_rc, p_fc = [], [], []
    ksh = vsh = None
    for l in range(depth):
        mqg = tile_row(mem_q_norm_g[l], MEM_HEADS)
        if l < n_a:
            x, hl, rct = _mixer_a(
                x, bsz, row(norm_mix_g[l]), (w_in_a_b, l), rnn_conv_w[l], row(rnn_conv_b[l]),
                (wg_b, l), row(b_gate_x[l]), row(b_gate_a[l]), row(lru_param[l]), mqg, bd,
                pmk4[l], pmv4[l], (w_out_b, l), zeros_rc, zeros_h)
            p_h.append(hl.reshape(bsz, D_RNN))
            p_rc.append(rct[:, SUBLANES - (CONV_A - 1):])
        else:
            j = l - n_a
            x = _mixer_b(
                x, bsz, sink_gk[j].reshape(-1), row(norm_mix_g[l]), (w_in_b_b, j),
                tile_row(q_norm_g[j], N_Q), cos_p, sin_p, ksh, vsh, mqg, bd,
                pmk4[l], pmv4[l], (w_out_perm_b, j))
        x, ut = _ffn(x, bsz, l, row(norm_ffn_g[l]), wup_b, fcw[l], fcb[l], wdn_b)
        p_fc.append(ut[:, SUBLANES - (CONV_F - 1):])
        if l == n_a - 1:
            ksh, vsh = _shared_kv(x, row(kv_norm_g), w_kv_b, tile_row(k_norm_g, N_KV), bd,
                                  cos_p, sin_p, TS_KV)
    y_prompt = x.reshape(bsz, seq, D_MODEL)
    keep = min(WINDOW, seq)
    p_k = ksh.reshape(bsz, seq, KV_W)[:, seq - keep:].reshape(bsz, keep, N_KV, HEAD_DIM)
    p_v = vsh.reshape(bsz, seq, KV_W)[:, seq - keep:].reshape(bsz, keep, N_KV, HEAD_DIM)
    p_mem_k = pmk.reshape(depth, bsz, N_MEM, MEM_HEADS, HEAD_DIM)
    p_mem_v = pmv.reshape(depth, bsz, N_MEM, MEM_HEADS, HEAD_DIM)

    xs = x_sample.reshape(dbsz, D_MODEL)
    cmk = cache_mem_k.transpose(0, 1, 3, 4, 2).reshape(depth, dbsz, MEM_W, N_MEM)
    cmv = cache_mem_v.transpose(0, 1, 3, 4, 2).reshape(depth, dbsz, MEM_W, N_MEM)
    wb = cache_swa_k.shape[1]
    ckb = cache_swa_k.transpose(0, 2, 3, 1).reshape(dbsz, KV_W, wb)
    cvb = cache_swa_v.transpose(0, 2, 3, 1).reshape(dbsz, KV_W, wb)
    s_h, s_rc, s_fc = [], [], []
    kn = vn = None
    for l in range(depth):
        mqg = tile_row(mem_q_norm_g[l], MEM_HEADS)
        if l < n_a:
            main, qn, hnew, xrpre = _dec_in_a(
                xs, row(norm_mix_g[l]), (w_in_a_b, l), rnn_conv_w[l], row(rnn_conv_b[l]),
                (wg_b, l), row(b_gate_x[l]), row(b_gate_a[l]), row(lru_param[l]), mqg, bd,
                state_rglru_conv[l].transpose(1, 0, 2), state_rglru_h[l])
            s_h.append(hnew)
            s_rc.append(jnp.concatenate([state_rglru_conv[l][:, 1:], xrpre[:, None, :]], axis=1))
            wo, wo_layer = w_out_b, l
        else:
            j = l - n_a
            q, qn = _dec_in_b(xs, row(norm_mix_g[l]), (w_in_b_b, j), tile_row(q_norm_g[j], N_Q),
                              cos_s, sin_s, mqg, bd)
            sink_rows = jnp.zeros((DEC_HEAD_ROWS, LANES), F32).at[:N_Q].set(
                jnp.broadcast_to(sink_gk[j].reshape(N_Q, 1), (N_Q, LANES)))
            main = _dec_swa(q, ckb, cvb, kn, vn, sink_rows)
            wo, wo_layer = w_out_perm_b, j
        mo = _dec_mem_attn(qn, cmk, cmv, l)
        xs, ug, uv = _dec_out_ffn(xs, main, mo, wo, wo_layer, l, row(norm_ffn_g[l]), wup_b,
                                  fcw[l], fcb[l], wdn_b, state_ffn_conv[l].transpose(1, 0, 2))
        unew = jnp.concatenate([ug, uv], axis=1)
        s_fc.append(jnp.concatenate([state_ffn_conv[l][:, 1:], unew[:, None, :]], axis=1))
        if l == n_a - 1:
            kn, vn = _shared_kv(xs, row(kv_norm_g), w_kv_b, tile_row(k_norm_g, N_KV), bd,
                                cos_s, sin_s, dbsz)
    y_sample = xs.reshape(dbsz, 1, D_MODEL)
    s_k = kn.reshape(dbsz, 1, N_KV, HEAD_DIM)
    s_v = vn.reshape(dbsz, 1, N_KV, HEAD_DIM)

    return (y_prompt, y_sample, jnp.stack(p_h), jnp.stack(p_rc), jnp.stack(p_fc), p_k, p_v,
            p_mem_k, p_mem_v, jnp.stack(s_h), jnp.stack(s_rc), jnp.stack(s_fc), s_k, s_v)
```

```python
import functools
import math

import jax
import jax.numpy as jnp
from jax import lax
from jax.experimental import pallas as pl
from jax.experimental.pallas import tpu as pltpu

F32 = jnp.float32
BF16 = jnp.bfloat16

D_MODEL = 1024
HEAD_DIM = 64
MEM_HEADS = 4
MEM_W = MEM_HEADS * HEAD_DIM
N_MEM = 256
D_RNN = D_MODEL - MEM_W
RNN_BLOCKS = D_RNN // HEAD_DIM
CONV_A = 4
LRU_C = 8.0
N_Q = D_RNN // HEAD_DIM
N_KV = 4
GROUP = N_Q // N_KV
Q_W = N_Q * HEAD_DIM
KV_W = N_KV * HEAD_DIM
WINDOW = 128
ROPE_THETA = 10000.0
D_FF = 3 * D_MODEL
CONV_F = 3
EPS = 1e-6
NEG = -1e30
ATT_SCALE = HEAD_DIM ** -0.5
PAST_LEN = 8192

SUBLANES = 8
LANES = 128
MXU_DIM = 256
VMEM_LIMIT_BYTES = 56 * 1024 * 1024

TS_MIX = 512
TS_FFN = 512
TS_KV = 512
FF_CHUNK = 512
N_FF_CHUNKS = D_FF // FF_CHUNK
DEC_SEQ_BLOCK = 8
CAST_TILE = 1024


def _mm(a, b):
    return jnp.dot(a.astype(BF16), b, preferred_element_type=F32)


def _mm_nt(a, b):
    return lax.dot_general(a.astype(BF16), b, (((1,), (1,)), ((), ())),
                           preferred_element_type=F32)


def _mm_split(a, b):
    hi = a.astype(BF16)
    lo = (a - hi.astype(F32)).astype(BF16)
    return (jnp.dot(hi, b, preferred_element_type=F32)
            + jnp.dot(lo, b, preferred_element_type=F32))


def _rmsnorm(x, g):
    ms = jnp.mean(x * x, axis=-1, keepdims=True)
    return x * lax.rsqrt(ms + EPS) * g


def _head_rmsnorm(x, bd, g):
    parts = []
    for c in range(x.shape[1] // MXU_DIM):
        xc = x[:, c * MXU_DIM:(c + 1) * MXU_DIM]
        ms = _mm_split(xc * xc, bd)
        parts.append(xc * lax.rsqrt(ms + EPS))
    y = parts[0] if len(parts) == 1 else jnp.concatenate(parts, axis=1)
    return y * g


def _tile_lanes(t, width):
    reps = width // t.shape[1]
    return t if reps == 1 else jnp.concatenate([t] * reps, axis=1)


def _rope(x, cos_t, sin_t):
    w = x.shape[1]
    lane = lax.broadcasted_iota(jnp.int32, x.shape, 1)
    first = (lane % HEAD_DIM) < (HEAD_DIM // 2)
    swapped = jnp.where(first, pltpu.roll(x, w - HEAD_DIM // 2, 1),
                        pltpu.roll(x, HEAD_DIM // 2, 1))
    return x * _tile_lanes(cos_t, w) + swapped * _tile_lanes(sin_t, w)


def _gelu(x):
    c = math.sqrt(2.0 / math.pi)
    return x * (0.5 * (1.0 + jnp.tanh(c * (x + 0.044715 * (x * x * x)))))


def _log_sigmoid(x):
    return jnp.minimum(x, 0.0) - jnp.log1p(jnp.exp(-jnp.abs(x)))


def _slab_stage(x, work_ref, work0, tail_ref, tail0):
    ts = x.shape[0]
    for s in range(x.shape[1] // LANES):
        xs = x[:, s * LANES:(s + 1) * LANES]
        buf = work_ref.at[work0 + s]
        buf[0:SUBLANES, :] = tail_ref[tail0 + s]
        buf[SUBLANES:SUBLANES + ts, :] = xs
        tail_ref[tail0 + s] = xs[ts - SUBLANES:]


def _slab_taps(ts, n_slabs, work_ref, work0, w_ref, b_ref, col0):
    k = w_ref.shape[0]
    outs = []
    for s in range(n_slabs):
        lanes = slice(col0 + s * LANES, col0 + (s + 1) * LANES)
        buf = work_ref.at[work0 + s]
        acc = buf[SUBLANES - (k - 1):SUBLANES - (k - 1) + ts, :] * w_ref[0:1, lanes]
        for j in range(1, k):
            off = SUBLANES - (k - 1 - j)
            acc = acc + buf[off:off + ts, :] * w_ref[j:j + 1, lanes]
        outs.append(acc + b_ref[:, lanes])
    return outs


def _sqrt_pos(x):
    return jnp.where(x > 0.0, x * lax.rsqrt(x), 0.0)


def _lru_coeffs(xr, wg_ref, bgx, bga, logsig):
    xb = xr.astype(BF16)
    gxs, gas = [], []
    for c in range(D_RNN // MXU_DIM):
        gg = jnp.dot(xb[:, c * MXU_DIM:(c + 1) * MXU_DIM], wg_ref[c],
                     preferred_element_type=F32)
        gxs.append(gg[:, :MXU_DIM])
        gas.append(gg[:, MXU_DIM:])
    gx = jax.nn.sigmoid(jnp.concatenate(gxs, axis=1) + bgx)
    ga = jax.nn.sigmoid(jnp.concatenate(gas, axis=1) + bga)
    log_a = ga * (LRU_C * logsig)
    a = jnp.exp(log_a)
    mult = _sqrt_pos(-jnp.tanh(log_a) * (a * a + 1.0))
    return a, mult * gx * xr


def _lru_scan(a, b, a_ref, b_ref, hc_ref):
    ts = a.shape[0]
    outs = []
    for s in range(a.shape[1] // LANES):
        lanes = slice(s * LANES, (s + 1) * LANES)
        a_s, b_s = a[:, lanes], b[:, lanes]
        abuf, bbuf = a_ref.at[s], b_ref.at[s]
        d = 1
        while d < SUBLANES:
            abuf[SUBLANES:SUBLANES + ts, :] = a_s
            bbuf[SUBLANES:SUBLANES + ts, :] = b_s
            b_s = a_s * bbuf[SUBLANES - d:SUBLANES - d + ts, :] + b_s
            a_s = a_s * abuf[SUBLANES - d:SUBLANES - d + ts, :]
            d *= 2
        h = hc_ref[:, lanes]
        hs = []
        for q in range(ts // SUBLANES):
            rows = slice(q * SUBLANES, (q + 1) * SUBLANES)
            h = a_s[rows] * h + b_s[rows]
            hs.append(h)
        hc_ref[:, lanes] = jnp.broadcast_to(h[SUBLANES - 1:], (SUBLANES, LANES))
        outs.append(jnp.concatenate(hs, axis=0))
    return jnp.concatenate(outs, axis=1)


def _head_mask(shape, h):
    lane = lax.broadcasted_iota(jnp.int32, shape, 1)
    return (lane >= h * HEAD_DIM) & (lane < (h + 1) * HEAD_DIM)


def _head_masked_stack(blocks, n_heads):
    parts = []
    for h in range(n_heads):
        for blk in blocks:
            parts.append(jnp.where(_head_mask(blk.shape, h), blk, 0.0).astype(BF16))
    return jnp.concatenate(parts, axis=0)


def _mem_attention(qn, kcat, vcat):
    s = _mm_nt(qn, kcat)
    parts = []
    for h in range(MEM_HEADS):
        sh = s[:, h * N_MEM:(h + 1) * N_MEM]
        m = jnp.max(sh, axis=-1, keepdims=True)
        p = jnp.exp(sh - m)
        den = jnp.sum(p, axis=-1, keepdims=True)
        parts.append((p * (1.0 / den)).astype(BF16))
    return jnp.dot(jnp.concatenate(parts, axis=1), vcat, preferred_element_type=F32)


def _cast_blocks_kernel(w_ref, o_ref):
    for b in range(o_ref.shape[0]):
        o_ref[b] = w_ref[:, b * MXU_DIM:(b + 1) * MXU_DIM].astype(BF16)


def _mem_kv_kernel(mem_ref, g_ref, w_ref, kg_ref, bd_ref, k_ref, v_ref):
    h = _mm(_rmsnorm(mem_ref[...], g_ref[0]), w_ref[0])
    k_ref[0] = _head_rmsnorm(h[:, :MEM_W], bd_ref[...], kg_ref[0])
    v_ref[0] = h[:, MEM_W:]


def _shared_kv_kernel(x_ref, g_ref, w_ref, kg_ref, bd_ref, cos_ref, sin_ref,
                      k_ref, v_ref):
    h = _mm(_rmsnorm(x_ref[...], g_ref[...]), w_ref[...])
    k = _head_rmsnorm(h[:, :KV_W], bd_ref[...], kg_ref[...])
    k_ref[...] = _rope(k, cos_ref[...], sin_ref[...])
    v_ref[...] = h[:, KV_W:]


def _mixer_a_kernel(x_ref, g_ref, win_ref, cw_ref, cb_ref, wg_ref, bgx_ref, bga_ref,
                    lru_ref, mqg_ref, bd_ref, mk_ref, mv_ref, wout_ref, rc0_ref, h0_ref,
                    xo_ref, hlast_ref, rctail_ref,
                    conv_s, tail_s, a_s, b_s, hc_s, kcat_s, vcat_s, mixed_s, xres_s,
                    *, tiles_per_seq, n_tiles):
    ts = x_ref.shape[0]
    n_slabs = D_RNN // LANES
    i = pl.program_id(0)
    slot = i % 2
    pslot = 1 - slot

    @pl.when(i == 0)
    def _():
        mixed_s[1] = jnp.zeros(mixed_s.shape[1:], mixed_s.dtype)
        xres_s[1] = jnp.zeros(xres_s.shape[1:], xres_s.dtype)

    @pl.when(i % tiles_per_seq == 0)
    def _():
        for s in range(n_slabs):
            tail_s[s] = rc0_ref[0, :, s * LANES:(s + 1) * LANES]
        a_s[:, 0:SUBLANES, :] = jnp.ones((n_slabs, SUBLANES, LANES), F32)
        b_s[:, 0:SUBLANES, :] = jnp.zeros((n_slabs, SUBLANES, LANES), F32)
        hc_s[...] = jnp.broadcast_to(h0_ref[0], hc_s.shape)
        kcat_s[...] = _head_masked_stack([mk_ref[0]], MEM_HEADS)
        vcat_s[...] = _head_masked_stack([mv_ref[0]], MEM_HEADS)

    x = x_ref[...]
    xres_s[slot] = x
    hn = _rmsnorm(x, g_ref[...]).astype(BF16)

    def in_proj(lo, hi):
        return jnp.dot(hn, win_ref[:, lo:hi], preferred_element_type=F32)

    xr_pre = in_proj(D_RNN, 2 * D_RNN)
    _slab_stage(xr_pre, conv_s, 0, tail_s, 0)
    xr = jnp.concatenate(_slab_taps(ts, n_slabs, conv_s, 0, cw_ref, cb_ref, 0), axis=1)
    out_prev = lambda n: jnp.dot(mixed_s[pslot], wout_ref[n], preferred_element_type=F32)
    y0 = out_prev(0)
    qm = in_proj(2 * D_RNN, 2 * D_RNN + MEM_W)
    a, b = _lru_coeffs(xr, wg_ref, bgx_ref[...], bga_ref[...], _log_sigmoid(lru_ref[...]))
    y1 = out_prev(1)
    qn = _head_rmsnorm(qm, bd_ref[...], mqg_ref[...]) * ATT_SCALE
    gate = in_proj(0, D_RNN)
    y2 = out_prev(2)
    mo = _mem_attention(qn, kcat_s[...], vcat_s[...])
    y3 = out_prev(3)
    h = _lru_scan(a, b, a_s, b_s, hc_s)
    main = h * _gelu(gate)
    mixed_s[slot] = jnp.concatenate([main, mo], axis=1).astype(BF16)
    xo_ref[...] = xres_s[pslot] + jnp.concatenate([y0, y1, y2, y3], axis=1)

    @pl.when(i < n_tiles)
    def _():
        rctail_ref[0] = xr_pre[ts - SUBLANES:]
        hlast_ref[0] = h[ts - 1:ts]


def _mixer_b_kernel(sink_ref, x_ref, g_ref, win_ref, qg_ref, cos_ref, sin_ref,
                    kcur_ref, kprev_ref, vcur_ref, vprev_ref,
                    mqg_ref, bd_ref, mk_ref, mv_ref, wout_ref,
                    xo_ref, kcat_s, vcat_s, mixed_s, xres_s, *, tiles_per_seq):
    ts = x_ref.shape[0]
    i = pl.program_id(0)
    seq_tile = i % tiles_per_seq
    slot = i % 2
    pslot = 1 - slot

    @pl.when(i == 0)
    def _():
        mixed_s[1] = jnp.zeros(mixed_s.shape[1:], mixed_s.dtype)
        xres_s[1] = jnp.zeros(xres_s.shape[1:], xres_s.dtype)

    @pl.when(seq_tile == 0)
    def _():
        kcat_s[...] = _head_masked_stack([mk_ref[0]], MEM_HEADS)
        vcat_s[...] = _head_masked_stack([mv_ref[0]], MEM_HEADS)

    x = x_ref[...]
    xres_s[slot] = x
    u = _mm(_rmsnorm(x, g_ref[...]), win_ref[...])
    q = _head_rmsnorm(u[:, :Q_W], bd_ref[...], qg_ref[...])
    q = (_rope(q, cos_ref[...], sin_ref[...]) * ATT_SCALE).astype(BF16)
    qm = u[:, Q_W:]

    row = lax.broadcasted_iota(jnp.int32, (WINDOW, 2 * WINDOW), 0)
    kj = lax.broadcasted_iota(jnp.int32, (WINDOW, 2 * WINDOW), 1) - WINDOW
    mask_inner = (kj <= row) & (kj >= row - WINDOW)
    mask_first = (kj <= row) & (kj >= jnp.maximum(row - WINDOW,
                                                  jnp.where(seq_tile > 0, -WINDOW, 0)))

    kt = kcur_ref[...]
    vt = vcur_ref[...]

    def block_scores(jb):
        lo, hi = jb * WINDOW, (jb + 1) * WINDOW
        if jb == 0:
            kp, vp = kprev_ref[...], vprev_ref[...]
        else:
            kp, vp = kt[lo - WINDOW:lo], vt[lo - WINDOW:lo]
        kcat = _head_masked_stack([kp, kt[lo:hi]], N_KV)
        vcat = _head_masked_stack([vp, vt[lo:hi]], N_KV)
        qs = jnp.concatenate([q[lo:hi, g * KV_W:(g + 1) * KV_W] for g in range(GROUP)], axis=0)
        s = lax.dot_general(qs, kcat, (((1,), (1,)), ((), ())), preferred_element_type=F32)
        return s, vcat

    n_blocks = ts // WINDOW
    mains = []
    nxt = block_scores(0)
    n_out = wout_ref.shape[0]
    y_prev = []
    for jb in range(n_blocks):
        s, vcat = nxt
        if jb + 1 < n_blocks:
            nxt = block_scores(jb + 1)
        for n in range(jb * n_out // n_blocks, (jb + 1) * n_out // n_blocks):
            y_prev.append(jnp.dot(mixed_s[pslot], wout_ref[n], preferred_element_type=F32))
        mask = mask_first if jb == 0 else mask_inner
        prow = []
        for g in range(GROUP):
            pseg = []
            for kv in range(N_KV):
                seg = s[g * WINDOW:(g + 1) * WINDOW, kv * 2 * WINDOW:(kv + 1) * 2 * WINDOW]
                seg = jnp.where(mask, seg, NEG)
                sink = sink_ref[g * N_KV + kv]
                m = jnp.maximum(jnp.max(seg, axis=-1, keepdims=True), sink)
                p = jnp.exp(seg - m)
                den = jnp.sum(p, axis=-1, keepdims=True) + jnp.exp(sink - m)
                pseg.append((p * (1.0 / den)).astype(BF16))
            prow.append(jnp.concatenate(pseg, axis=1))
        o = jnp.dot(jnp.concatenate(prow, axis=0), vcat, preferred_element_type=F32)
        mains.append(jnp.concatenate([o[g * WINDOW:(g + 1) * WINDOW] for g in range(GROUP)], axis=1))
    main = mains[0] if len(mains) == 1 else jnp.concatenate(mains, axis=0)

    qn = _head_rmsnorm(qm, bd_ref[...], mqg_ref[...]) * ATT_SCALE
    mo = _mem_attention(qn, kcat_s[...], vcat_s[...])

    mixed_s[slot] = jnp.concatenate([main, mo], axis=1).astype(BF16)
    xo_ref[...] = xres_s[pslot] + jnp.concatenate(y_prev, axis=1)


def _ffn_kernel(x_ref, g_ref, wup_ref, cw_ref, cb_ref, wdn_ref,
                xo_ref, utail_ref, conv_s, tail_s, act_s, xres_s, *, tiles_per_seq):
    ts = x_ref.shape[0]
    slabs = FF_CHUNK // LANES
    blk_slabs = MXU_DIM // LANES
    blks = FF_CHUNK // MXU_DIM
    i = pl.program_id(0)
    slot = i % 2
    pslot = 1 - slot

    @pl.when(i == 0)
    def _():
        act_s[1] = jnp.zeros(act_s.shape[1:], act_s.dtype)
        xres_s[1] = jnp.zeros(xres_s.shape[1:], xres_s.dtype)

    @pl.when(i % tiles_per_seq == 0)
    def _():
        tail_s[...] = jnp.zeros_like(tail_s)

    x = x_ref[...]
    xres_s[slot] = x
    hn = _rmsnorm(x, g_ref[...]).astype(BF16)

    def chunk_cols(j):
        return [(c, ((j % 2) * 2 + half) * slabs) for half, c in enumerate((j, N_FF_CHUNKS + j))]

    def up_project(j):
        for c, work0 in chunk_cols(j):
            for b in range(blks):
                blk = c * blks + b
                u = jnp.dot(hn, wup_ref[blk], preferred_element_type=F32)
                _slab_stage(u, conv_s, work0 + b * blk_slabs, tail_s, c * slabs + b * blk_slabs)
                utail_ref[0, :, blk * MXU_DIM:(blk + 1) * MXU_DIM] = u[ts - SUBLANES:]

    def activation(j):
        cg, cv = [_slab_taps(ts, slabs, conv_s, work0, cw_ref, cb_ref, c * FF_CHUNK)
                  for c, work0 in chunk_cols(j)]
        return jnp.concatenate([_gelu(a) * b for a, b in zip(cg, cv)], axis=1).astype(BF16)

    def down_prev(n):
        return jnp.dot(act_s[pslot], wdn_ref[n], preferred_element_type=F32)

    n_dn = wdn_ref.shape[0]
    outs = [None] * n_dn
    outs[0] = down_prev(0)
    up_project(0)
    for j in range(N_FF_CHUNKS):
        if j + 1 < N_FF_CHUNKS:
            up_project(j + 1)
        else:
            outs[1] = down_prev(1)
        act_s[slot, :, j * FF_CHUNK:(j + 1) * FF_CHUNK] = activation(j)
    for n in range(2, n_dn):
        outs[n] = down_prev(n)
    xo_ref[...] = xres_s[pslot] + jnp.concatenate(outs, axis=1)


def _dec_in_a_kernel(x_ref, g_ref, win_ref, cw_ref, cb_ref, wg_ref, bgx_ref, bga_ref,
                     lru_ref, mqg_ref, bd_ref, b0_ref, b1_ref, b2_ref, h0_ref,
                     main_ref, qn_ref, hnew_ref, xrpre_ref):
    u = _mm(_rmsnorm(x_ref[...], g_ref[...]), win_ref[...])
    gate = u[:, :D_RNN]
    xr_pre = u[:, D_RNN:2 * D_RNN]
    qm = u[:, 2 * D_RNN:]
    xr = b0_ref[...] * cw_ref[0:1, :]
    xr = xr + b1_ref[...] * cw_ref[1:2, :]
    xr = xr + b2_ref[...] * cw_ref[2:3, :]
    xr = xr + xr_pre * cw_ref[3:4, :]
    xr = xr + cb_ref[...]
    a, b = _lru_coeffs(xr, wg_ref, bgx_ref[...], bga_ref[...], _log_sigmoid(lru_ref[...]))
    h = a * h0_ref[...] + b
    main_ref[...] = h * _gelu(gate)
    qn_ref[...] = _head_rmsnorm(qm, bd_ref[...], mqg_ref[...]) * ATT_SCALE
    hnew_ref[...] = h
    xrpre_ref[...] = xr_pre


def _dec_in_b_kernel(x_ref, g_ref, win_ref, qg_ref, cos_ref, sin_ref, mqg_ref, bd_ref,
                     q_ref, qn_ref):
    u = _mm(_rmsnorm(x_ref[...], g_ref[...]), win_ref[...])
    q = _head_rmsnorm(u[:, :Q_W], bd_ref[...], qg_ref[...])
    q_ref[...] = _rope(q, cos_ref[...], sin_ref[...]) * ATT_SCALE
    qn_ref[...] = _head_rmsnorm(u[:, Q_W:], bd_ref[...], mqg_ref[...]) * ATT_SCALE


DEC_HEAD_ROWS = 16


def _own_head_lanes(n_heads, width):
    row = lax.broadcasted_iota(jnp.int32, (DEC_HEAD_ROWS, width), 0)
    lane = lax.broadcasted_iota(jnp.int32, (DEC_HEAD_ROWS, width), 1)
    start = (row % (width // HEAD_DIM)) * HEAD_DIM
    return (lane >= start) & (lane < start + HEAD_DIM) & (row < n_heads)


def _dec_mem_attn_kernel(q_ref, kt_ref, vt_ref, o_ref):
    own = _own_head_lanes(MEM_HEADS, MEM_W)
    for s in range(q_ref.shape[0]):
        q_rows = jnp.broadcast_to(q_ref[s:s + 1, :], (DEC_HEAD_ROWS, MEM_W))
        qbd = jnp.where(own, q_rows, 0.0).astype(BF16)
        sc = jnp.dot(qbd, kt_ref[s].astype(BF16), preferred_element_type=F32)
        m = jnp.max(sc, axis=-1, keepdims=True)
        p = jnp.exp(sc - m)
        den = jnp.sum(p, axis=-1, keepdims=True)
        pn = (p * (1.0 / den)).astype(BF16)
        o_all = lax.dot_general(pn, vt_ref[s].astype(BF16), (((1,), (1,)), ((), ())),
                                preferred_element_type=F32)
        o_ref[s:s + 1, :] = jnp.sum(jnp.where(own, o_all, 0.0), axis=0, keepdims=True)


def _dec_swa_kernel(q_ref, kt_ref, vt_ref, kn_ref, vn_ref, sink_ref, o_ref):
    n_heads = GROUP * N_KV
    own = _own_head_lanes(n_heads, KV_W)
    grp = lax.broadcasted_iota(jnp.int32, (DEC_HEAD_ROWS, KV_W), 0) // N_KV
    sink = sink_ref[:, 0:1]
    for s in range(q_ref.shape[0]):
        q_rows = jnp.zeros((DEC_HEAD_ROWS, KV_W), F32)
        for g in range(GROUP):
            qg = jnp.broadcast_to(q_ref[s:s + 1, g * KV_W:(g + 1) * KV_W], (DEC_HEAD_ROWS, KV_W))
            q_rows = jnp.where(grp == g, qg, q_rows)
        qbd = jnp.where(own, q_rows, 0.0)
        s_buf = jnp.dot(qbd.astype(BF16), kt_ref[s].astype(BF16), preferred_element_type=F32)
        s_new = jnp.sum(qbd * kn_ref[s:s + 1, :], axis=-1, keepdims=True)
        m = jnp.maximum(jnp.maximum(jnp.max(s_buf, axis=-1, keepdims=True), s_new), sink)
        p_buf = jnp.exp(s_buf - m)
        p_new = jnp.exp(s_new - m)
        den = jnp.sum(p_buf, axis=-1, keepdims=True) + p_new + jnp.exp(sink - m)
        r = 1.0 / den
        o_all = lax.dot_general((p_buf * r).astype(BF16), vt_ref[s].astype(BF16),
                                (((1,), (1,)), ((), ())), preferred_element_type=F32)
        o_all = jnp.where(own, o_all + (p_new * r) * vn_ref[s:s + 1, :], 0.0)
        o_sum = o_all + pltpu.roll(o_all, 1, 0)
        o_sum = o_sum + pltpu.roll(o_sum, 2, 0)
        for g in range(GROUP):
            last = (g + 1) * N_KV - 1
            o_ref[s:s + 1, g * KV_W:(g + 1) * KV_W] = o_sum[last:last + 1, :]


def _dec_out_ffn_kernel(x_ref, main_ref, mo_ref, wout_ref, g_ref,
                        wug_ref, wuv_ref, cwg_ref, cwv_ref, cbg_ref, cbv_ref, wdn_ref,
                        sg0_ref, sg1_ref, sv0_ref, sv1_ref,
                        xo_ref, ug_ref, uv_ref,
                        xmid_s, hn_s, acc_s):
    j = pl.program_id(0)

    @pl.when(j == 0)
    def _():
        mixed = jnp.concatenate([main_ref[...], mo_ref[...]], axis=1).astype(BF16)
        y = jnp.concatenate([jnp.dot(mixed, wout_ref[n], preferred_element_type=F32)
                             for n in range(wout_ref.shape[0])], axis=1)
        xmid = x_ref[...] + y
        xmid_s[...] = xmid
        hn_s[...] = _rmsnorm(xmid, g_ref[...]).astype(BF16)
        acc_s[...] = jnp.zeros_like(acc_s)

    hn = hn_s[...]
    up = lambda w_ref: jnp.concatenate(
        [jnp.dot(hn, w_ref[b], preferred_element_type=F32) for b in range(w_ref.shape[0])], axis=1)
    ug = up(wug_ref)
    uv = up(wuv_ref)
    ug_ref[...] = ug
    uv_ref[...] = uv
    cg = (sg0_ref[...] * cwg_ref[0:1, :] + sg1_ref[...] * cwg_ref[1:2, :]
          + ug * cwg_ref[2:3, :] + cbg_ref[...])
    cv = (sv0_ref[...] * cwv_ref[0:1, :] + sv1_ref[...] * cwv_ref[1:2, :]
          + uv * cwv_ref[2:3, :] + cbv_ref[...])
    act = (_gelu(cg) * cv).astype(BF16)
    for n in range(wdn_ref.shape[0]):
        acc_s[:, n * MXU_DIM:(n + 1) * MXU_DIM] += jnp.dot(act, wdn_ref[n],
                                                           preferred_element_type=F32)

    @pl.when(j == pl.num_programs(0) - 1)
    def _():
        xo_ref[...] = xmid_s[...] + acc_s[...]


def _const_spec(shape):
    nd = len(shape)
    return pl.BlockSpec(shape, lambda *_: (0,) * nd)


def _layer_spec(arr, layer):
    nd = arr.ndim - 1
    return pl.BlockSpec((None,) + arr.shape[1:], lambda *_: (layer,) + (0,) * nd)


def _resident_spec(op):
    return _layer_spec(*op) if isinstance(op, tuple) else _const_spec(op.shape)


def _operand(op):
    return op[0] if isinstance(op, tuple) else op


def _params(*sem):
    return pltpu.CompilerParams(dimension_semantics=sem, vmem_limit_bytes=VMEM_LIMIT_BYTES)


def _cast_blocks(w):
    layers, r, c = w.shape
    tile = CAST_TILE
    per = tile // MXU_DIM
    return pl.pallas_call(
        _cast_blocks_kernel,
        grid=(layers, r // tile, c // tile),
        in_specs=[pl.BlockSpec((None, tile, tile), lambda l, i, j: (l, i, j))],
        out_specs=pl.BlockSpec((None, per, tile, MXU_DIM), lambda l, i, j: (l, j, i, 0)),
        out_shape=jax.ShapeDtypeStruct((layers, c // MXU_DIM, r, MXU_DIM), BF16),
        compiler_params=_params("arbitrary", "arbitrary", "arbitrary"),
        name="cast_blocks",
    )(w)


def _mem_kv(mem2d, g, w, kg, bd):
    depth = w.shape[0]
    rows = mem2d.shape[0]
    out = jax.ShapeDtypeStruct((depth, rows, MEM_W), F32)
    return pl.pallas_call(
        _mem_kv_kernel,
        grid=(depth,),
        in_specs=[
            _const_spec(mem2d.shape),
            pl.BlockSpec((1, 1, D_MODEL), lambda l: (l, 0, 0)),
            pl.BlockSpec((1, D_MODEL, 2 * MEM_W), lambda l: (l, 0, 0)),
            pl.BlockSpec((1, 1, MEM_W), lambda l: (l, 0, 0)),
            _const_spec(bd.shape),
        ],
        out_specs=[pl.BlockSpec((1, rows, MEM_W), lambda l: (l, 0, 0))] * 2,
        out_shape=[out, out],
        compiler_params=_params("arbitrary"),
        name="mem_kv",
    )(mem2d, g, w, kg, bd)


def _shared_kv(x2d, g, w, kg, bd, cos_t, sin_t, ts):
    rows = x2d.shape[0]
    tab_blocks = cos_t.shape[0] // ts
    out = jax.ShapeDtypeStruct((rows, KV_W), F32)
    return pl.pallas_call(
        _shared_kv_kernel,
        grid=(rows // ts,),
        in_specs=[
            pl.BlockSpec((ts, D_MODEL), lambda i: (i, 0)),
            _const_spec(g.shape), _const_spec(w.shape), _const_spec(kg.shape),
            _const_spec(bd.shape),
            pl.BlockSpec((ts, LANES), lambda i: (i % tab_blocks, 0)),
            pl.BlockSpec((ts, LANES), lambda i: (i % tab_blocks, 0)),
        ],
        out_specs=[pl.BlockSpec((ts, KV_W), lambda i: (i, 0))] * 2,
        out_shape=[out, out],
        compiler_params=_params("arbitrary"),
        name="shared_kv",
    )(x2d, g, w, kg, bd, cos_t, sin_t)


def _mixer_a(x2d, bsz, g, win, cw, cb, wg, bgx, bga, lru, mqg, bd, mk, mv, wout, rc0, h0):
    rows = x2d.shape[0]
    ts = TS_MIX
    nt = rows // ts
    tps = rows // bsz // ts
    consts = (g, win, cw, cb, wg, bgx, bga, lru, mqg, bd)
    tile = lambda i: jnp.minimum(i, nt - 1)
    per_b3 = lambda i: (tile(i) // tps, 0, 0)
    return pl.pallas_call(
        functools.partial(_mixer_a_kernel, tiles_per_seq=tps, n_tiles=nt),
        grid=(nt + 1,),
        in_specs=[pl.BlockSpec((ts, D_MODEL), lambda i: (tile(i), 0))]
        + [_resident_spec(c) for c in consts]
        + [pl.BlockSpec((1, N_MEM, MEM_W), per_b3)] * 2
        + [_resident_spec(wout),
           pl.BlockSpec((1, SUBLANES, D_RNN), per_b3),
           pl.BlockSpec((1, 1, D_RNN), per_b3)],
        out_specs=[
            pl.BlockSpec((ts, D_MODEL), lambda i: (jnp.maximum(i - 1, 0), 0)),
            pl.BlockSpec((1, 1, D_RNN), per_b3),
            pl.BlockSpec((1, SUBLANES, D_RNN), per_b3),
        ],
        out_shape=[
            jax.ShapeDtypeStruct((rows, D_MODEL), F32),
            jax.ShapeDtypeStruct((bsz, 1, D_RNN), F32),
            jax.ShapeDtypeStruct((bsz, SUBLANES, D_RNN), F32),
        ],
        scratch_shapes=[
            pltpu.VMEM((D_RNN // LANES, SUBLANES + ts, LANES), F32),
            pltpu.VMEM((D_RNN // LANES, SUBLANES, LANES), F32),
            pltpu.VMEM((D_RNN // LANES, SUBLANES + ts, LANES), F32),
            pltpu.VMEM((D_RNN // LANES, SUBLANES + ts, LANES), F32),
            pltpu.VMEM((SUBLANES, D_RNN), F32),
            pltpu.VMEM((MEM_HEADS * N_MEM, MEM_W), BF16),
            pltpu.VMEM((MEM_HEADS * N_MEM, MEM_W), BF16),
            pltpu.VMEM((2, ts, D_MODEL), BF16),
            pltpu.VMEM((2, ts, D_MODEL), F32),
        ],
        compiler_params=_params("arbitrary"),
        name="mixer_a",
    )(x2d, *[_operand(c) for c in consts], mk, mv, _operand(wout), rc0, h0)


def _mixer_b(x2d, bsz, sink_tab, g, win, qg, cos_t, sin_t, ksh, vsh, mqg, bd, mk, mv, wout):
    rows = x2d.shape[0]
    ts = TS_MIX
    nt = rows // ts
    tps = rows // bsz // ts
    wpt = ts // WINDOW
    tile = lambda i: jnp.minimum(i, nt - 1)
    cur = lambda i: (tile(i), 0)
    prev = lambda i: (jnp.maximum(tile(i) * wpt - 1, 0), 0)
    table = lambda i: (tile(i) % tps, 0)
    per_b3 = lambda i: (tile(i) // tps, 0, 0)
    return pl.pallas_call(
        functools.partial(_mixer_b_kernel, tiles_per_seq=tps),
        grid=(nt + 1,),
        in_specs=[
            pl.BlockSpec(memory_space=pltpu.SMEM),
            pl.BlockSpec((ts, D_MODEL), cur),
            _const_spec(g.shape), _resident_spec(win), _const_spec(qg.shape),
            pl.BlockSpec((ts, LANES), table),
            pl.BlockSpec((ts, LANES), table),
            pl.BlockSpec((ts, KV_W), cur), pl.BlockSpec((WINDOW, KV_W), prev),
            pl.BlockSpec((ts, KV_W), cur), pl.BlockSpec((WINDOW, KV_W), prev),
            _const_spec(mqg.shape), _const_spec(bd.shape),
            pl.BlockSpec((1, N_MEM, MEM_W), per_b3), pl.BlockSpec((1, N_MEM, MEM_W), per_b3),
            _resident_spec(wout),
        ],
        out_specs=pl.BlockSpec((ts, D_MODEL), lambda i: (jnp.maximum(i - 1, 0), 0)),
        out_shape=jax.ShapeDtypeStruct((rows, D_MODEL), F32),
        scratch_shapes=[
            pltpu.VMEM((MEM_HEADS * N_MEM, MEM_W), BF16),
            pltpu.VMEM((MEM_HEADS * N_MEM, MEM_W), BF16),
            pltpu.VMEM((2, ts, D_MODEL), BF16),
            pltpu.VMEM((2, ts, D_MODEL), F32),
        ],
        compiler_params=_params("arbitrary"),
        name="mixer_b",
    )(sink_tab, x2d, g, _operand(win), qg, cos_t, sin_t, ksh, ksh, vsh, vsh, mqg, bd, mk, mv,
      _operand(wout))


def _ffn(x2d, bsz, layer, g, wup, cw, cb, wdn):
    rows = x2d.shape[0]
    ts = TS_FFN
    nt = rows // ts
    tiles_per_seq = rows // bsz // ts
    last = nt - 1
    return pl.pallas_call(
        functools.partial(_ffn_kernel, tiles_per_seq=tiles_per_seq),
        grid=(nt + 1,),
        in_specs=[pl.BlockSpec((ts, D_MODEL), lambda i: (jnp.minimum(i, last), 0)),
                  _const_spec(g.shape), _layer_spec(wup, layer), _const_spec(cw.shape),
                  _const_spec(cb.shape), _layer_spec(wdn, layer)],
        out_specs=[
            pl.BlockSpec((ts, D_MODEL), lambda i: (jnp.maximum(i - 1, 0), 0)),
            pl.BlockSpec((1, SUBLANES, 2 * D_FF),
                         lambda i: (jnp.minimum(i, last) // tiles_per_seq, 0, 0)),
        ],
        out_shape=[
            jax.ShapeDtypeStruct((rows, D_MODEL), F32),
            jax.ShapeDtypeStruct((bsz, SUBLANES, 2 * D_FF), F32),
        ],
        scratch_shapes=[pltpu.VMEM((4 * FF_CHUNK // LANES, SUBLANES + ts, LANES), F32),
                        pltpu.VMEM((2 * D_FF // LANES, SUBLANES, LANES), F32),
                        pltpu.VMEM((2, ts, D_FF), BF16),
                        pltpu.VMEM((2, ts, D_MODEL), F32)],
        compiler_params=_params("arbitrary"),
        name="ffn",
    )(x2d, g, wup, cw, cb, wdn)


def _dec_in_a(x, g, win, cw, cb, wg, bgx, bga, lru, mqg, bd, rc, h0):
    n = x.shape[0]
    consts = (x, g, win, cw, cb, wg, bgx, bga, lru, mqg, bd)
    buf = lambda j: pl.BlockSpec((None, n, D_RNN), lambda i: (j, 0, 0))
    return pl.pallas_call(
        _dec_in_a_kernel,
        grid=(1,),
        in_specs=[_resident_spec(c) for c in consts]
        + [buf(0), buf(1), buf(2), _const_spec(h0.shape)],
        out_specs=[_const_spec((n, D_RNN)), _const_spec((n, MEM_W)),
                   _const_spec((n, D_RNN)), _const_spec((n, D_RNN))],
        out_shape=[jax.ShapeDtypeStruct((n, D_RNN), F32), jax.ShapeDtypeStruct((n, MEM_W), F32),
                   jax.ShapeDtypeStruct((n, D_RNN), F32), jax.ShapeDtypeStruct((n, D_RNN), F32)],
        compiler_params=_params("arbitrary"),
        name="dec_in_a",
    )(*[_operand(c) for c in consts], rc, rc, rc, h0)


def _dec_in_b(x, g, win, qg, cos_t, sin_t, mqg, bd):
    n = x.shape[0]
    args = (x, g, win, qg, cos_t, sin_t, mqg, bd)
    return pl.pallas_call(
        _dec_in_b_kernel,
        grid=(1,),
        in_specs=[_resident_spec(a) for a in args],
        out_specs=[_const_spec((n, Q_W)), _const_spec((n, MEM_W))],
        out_shape=[jax.ShapeDtypeStruct((n, Q_W), F32), jax.ShapeDtypeStruct((n, MEM_W), F32)],
        compiler_params=_params("arbitrary"),
        name="dec_in_b",
    )(*[_operand(a) for a in args])


def _dec_mem_attn(qn, ckt, cvt, layer):
    n = qn.shape[0]
    sb = DEC_SEQ_BLOCK
    return pl.pallas_call(
        _dec_mem_attn_kernel,
        grid=(n // sb,),
        in_specs=[pl.BlockSpec((sb, MEM_W), lambda i: (i, 0)),
                  pl.BlockSpec((None, sb, MEM_W, N_MEM), lambda i: (layer, i, 0, 0)),
                  pl.BlockSpec((None, sb, MEM_W, N_MEM), lambda i: (layer, i, 0, 0))],
        out_specs=pl.BlockSpec((sb, MEM_W), lambda i: (i, 0)),
        out_shape=jax.ShapeDtypeStruct((n, MEM_W), F32),
        compiler_params=_params("arbitrary"),
        name="dec_mem_attn",
    )(qn, ckt, cvt)


def _dec_swa(q, kbt, vbt, kn, vn, sink_rows):
    n = q.shape[0]
    wb = kbt.shape[2]
    sb = DEC_SEQ_BLOCK
    assert wb <= WINDOW
    return pl.pallas_call(
        _dec_swa_kernel,
        grid=(n // sb,),
        in_specs=[pl.BlockSpec((sb, Q_W), lambda i: (i, 0)),
                  pl.BlockSpec((sb, KV_W, wb), lambda i: (i, 0, 0)),
                  pl.BlockSpec((sb, KV_W, wb), lambda i: (i, 0, 0)),
                  pl.BlockSpec((sb, KV_W), lambda i: (i, 0)),
                  pl.BlockSpec((sb, KV_W), lambda i: (i, 0)),
                  _const_spec(sink_rows.shape)],
        out_specs=pl.BlockSpec((sb, Q_W), lambda i: (i, 0)),
        out_shape=jax.ShapeDtypeStruct((n, Q_W), F32),
        compiler_params=_params("arbitrary"),
        name="dec_swa",
    )(q, kbt, vbt, kn, vn, sink_rows)


def _dec_out_ffn(x, main, mo, wout, wout_layer, layer, g, wup, cw, cb, wdn, st):
    n = x.shape[0]
    nch = N_FF_CHUNKS
    blks = FF_CHUNK // MXU_DIM
    lo = lambda j: (0, j)
    hi = lambda j: (0, nch + j)
    state = lambda r, off: pl.BlockSpec((None, n, FF_CHUNK), lambda j: (r, 0, off + j))
    up_blocks = lambda off: pl.BlockSpec((None, blks, D_MODEL, MXU_DIM),
                                         lambda j: (layer, off + j, 0, 0))
    return pl.pallas_call(
        _dec_out_ffn_kernel,
        grid=(nch,),
        in_specs=[_const_spec(x.shape), _const_spec(main.shape), _const_spec(mo.shape),
                  _layer_spec(wout, wout_layer), _const_spec(g.shape),
                  up_blocks(0), up_blocks(nch),
                  pl.BlockSpec((CONV_F, FF_CHUNK), lo), pl.BlockSpec((CONV_F, FF_CHUNK), hi),
                  pl.BlockSpec((1, FF_CHUNK), lo), pl.BlockSpec((1, FF_CHUNK), hi),
                  pl.BlockSpec((None, wdn.shape[1], FF_CHUNK, MXU_DIM), lambda j: (layer, 0, j, 0)),
                  state(0, 0), state(1, 0), state(0, nch), state(1, nch)],
        out_specs=[_const_spec((n, D_MODEL)),
                   pl.BlockSpec((n, FF_CHUNK), lambda j: (0, j)),
                   pl.BlockSpec((n, FF_CHUNK), lambda j: (0, j))],
        out_shape=[jax.ShapeDtypeStruct((n, D_MODEL), F32),
                   jax.ShapeDtypeStruct((n, D_FF), F32),
                   jax.ShapeDtypeStruct((n, D_FF), F32)],
        scratch_shapes=[pltpu.VMEM((n, D_MODEL), F32), pltpu.VMEM((n, D_MODEL), BF16),
                        pltpu.VMEM((n, D_MODEL), F32)],
        compiler_params=_params("arbitrary"),
        name="dec_out_ffn",
    )(x, main, mo, wout, g, wup, wup, cw, cw, cb, cb, wdn, st, st, st, st)


def _rope_tables(pos):
    half = HEAD_DIM // 2
    inv = ROPE_THETA ** (-jnp.arange(half, dtype=F32) / half)
    ang = pos.astype(F32)[:, None] * inv[None, :]
    cos = jnp.cos(ang)
    sin = jnp.sin(ang)
    reps = LANES // HEAD_DIM
    cos_t = jnp.tile(jnp.concatenate([cos, cos], axis=1), (1, reps))
    sin_t = jnp.tile(jnp.concatenate([-sin, sin], axis=1), (1, reps))
    return cos_t, sin_t


def _block_diag_gates(wx, wa):
    per = MXU_DIM // HEAD_DIM
    eye = jnp.eye(per, dtype=F32)

    def bd(w):
        w4 = w.reshape(RNN_BLOCKS // per, per, HEAD_DIM, HEAD_DIM)
        return jnp.einsum('ckij,kK->ckiKj', w4, eye).reshape(RNN_BLOCKS // per, MXU_DIM, MXU_DIM)

    return jnp.concatenate([bd(wx), bd(wa)], axis=2).astype(BF16)


def kernel(x_prompt, x_sample, state_rglru_h, state_rglru_conv, state_ffn_conv, cache_swa_k, cache_swa_v, cache_mem_k, cache_mem_v, mem_prompt, norm_mix_g, norm_ffn_g, w_in_a, rnn_conv_w, rnn_conv_b, w_gate_x, b_gate_x, w_gate_a, b_gate_a, lru_param, w_in_b, q_norm_g, sinks, kv_norm_g, w_kv, k_norm_g, mem_norm_g, w_mem_kv, mem_q_norm_g, mem_k_norm_g, w_out, w_ffn_up, ffn_conv_w, ffn_conv_b, w_ffn_down):
    bsz, seq, _ = x_prompt.shape
    dbsz = x_sample.shape[0]
    depth = norm_mix_g.shape[0]
    n_a = w_in_a.shape[0]
    assert x_sample.shape[1] == 1
    assert seq % TS_MIX == 0 and seq % TS_FFN == 0 and seq % TS_KV == 0 and TS_MIX % WINDOW == 0

    n_b = w_in_b.shape[0]
    wq = w_in_b[:, :, :Q_W].astype(BF16).reshape(n_b, D_MODEL, N_KV, GROUP, HEAD_DIM)
    wq = wq.transpose(0, 1, 3, 2, 4).reshape(n_b, D_MODEL, Q_W)
    wo_main = w_out[n_a:, :Q_W].reshape(n_b, N_KV, GROUP, HEAD_DIM, D_MODEL)
    wo_main = wo_main.transpose(0, 2, 1, 3, 4).reshape(n_b, Q_W, D_MODEL)
    bd =(jnp.kron(jnp.eye(MXU_DIM // HEAD_DIM, dtype=F32),
                   jnp.ones((HEAD_DIM, HEAD_DIM), F32)) / HEAD_DIM).astype(BF16)

    row = lambda v: v.reshape(1, -1)
    tile_row = lambda v, n: jnp.tile(v, n).reshape(1, -1)
    w_in_a_b = w_in_a.astype(BF16)
    w_in_b_b = jnp.concatenate([wq, w_in_b[:, :, Q_W:].astype(BF16)], axis=2)
    w_out_b = _cast_blocks(w_out)
    w_out_perm_b = _cast_blocks(jnp.concatenate([wo_main, w_out[n_a:, Q_W:]], axis=1))
    w_kv_b = w_kv.astype(BF16)
    w_mem_b = w_mem_kv.astype(BF16)
    wup_b = _cast_blocks(w_ffn_up)
    wdn_b = _cast_blocks(w_ffn_down)
    fcw = ffn_conv_w
    fcb = ffn_conv_b.reshape(depth, 1, 2 * D_FF)
    wg_b = jnp.stack([_block_diag_gates(w_gate_x[l], w_gate_a[l]) for l in range(n_a)])
    sink_gk = sinks.reshape(-1, N_KV, GROUP).transpose(0, 2, 1)

    cos_p, sin_p = _rope_tables(jnp.arange(seq, dtype=jnp.int32))
    pos_s = PAST_LEN + jnp.zeros((dbsz,), jnp.int32)
    cos_s, sin_s = _rope_tables(pos_s)

    mem2d = mem_prompt.reshape(bsz * N_MEM, D_MODEL)
    pmk, pmv = _mem_kv(mem2d, mem_norm_g.reshape(depth, 1, D_MODEL), w_mem_b,
                       jnp.tile(mem_k_norm_g, (1, MEM_HEADS)).reshape(depth, 1, MEM_W), bd)
    pmk4 = pmk.reshape(depth, bsz, N_MEM, MEM_W)
    pmv4 = pmv.reshape(depth, bsz, N_MEM, MEM_W)

    x = x_prompt.reshape(bsz * seq, D_MODEL)
    zeros_rc = jnp.zeros((bsz, SUBLANES, D_RNN), F32)
    zeros_h = jnp.zeros((bsz, 1, D_RNN), F32)
    p_h, p_rc, p_fc = [], [], []
    ksh = vsh = None
    for l in range(depth):
        mqg = tile_row(mem_q_norm_g[l], MEM_HEADS)
        if l < n_a:
            x, hl, rct = _mixer_a(
                x, bsz, row(norm_mix_g[l]), (w_in_a_b, l), rnn_conv_w[l], row(rnn_conv_b[l]),
                (wg_b, l), row(b_gate_x[l]), row(b_gate_a[l]), row(lru_param[l]), mqg, bd,
                pmk4[l], pmv4[l], (w_out_b, l), zeros_rc, zeros_h)
            p_h.append(hl.reshape(bsz, D_RNN))
            p_rc.append(rct[:, SUBLANES - (CONV_A - 1):])
        else:
            j = l - n_a
            x = _mixer_b(
                x, bsz, sink_gk[j].reshape(-1), row(norm_mix_g[l]), (w_in_b_b, j),
                tile_row(q_norm_g[j], N_Q), cos_p, sin_p, ksh, vsh, mqg, bd,
                pmk4[l], pmv4[l], (w_out_perm_b, j))
        x, ut = _ffn(x, bsz, l, row(norm_ffn_g[l]), wup_b, fcw[l], fcb[l], wdn_b)
        p_fc.append(ut[:, SUBLANES - (CONV_F - 1):])
        if l == n_a - 1:
            ksh, vsh = _shared_kv(x, row(kv_norm_g), w_kv_b, tile_row(k_norm_g, N_KV), bd,
                                  cos_p, sin_p, TS_KV)
    y_prompt = x.reshape(bsz, seq, D_MODEL)
    keep = min(WINDOW, seq)
    p_k = ksh.reshape(bsz, seq, KV_W)[:, seq - keep:].reshape(bsz, keep, N_KV, HEAD_DIM)
    p_v = vsh.reshape(bsz, seq, KV_W)[:, seq - keep:].reshape(bsz, keep, N_KV, HEAD_DIM)
    p_mem_k = pmk.reshape(depth, bsz, N_MEM, MEM_HEADS, HEAD_DIM)
    p_mem_v = pmv.reshape(depth, bsz, N_MEM, MEM_HEADS, HEAD_DIM)

    xs = x_sample.reshape(dbsz, D_MODEL)
    cmk = cache_mem_k.transpose(0, 1, 3, 4, 2).reshape(depth, dbsz, MEM_W, N_MEM)
    cmv = cache_mem_v.transpose(0, 1, 3, 4, 2).reshape(depth, dbsz, MEM_W, N_MEM)
    wb = cache_swa_k.shape[1]
    ckb = cache_swa_k.transpose(0, 2, 3, 1).reshape(dbsz, KV_W, wb)
    cvb = cache_swa_v.transpose(0, 2, 3, 1).reshape(dbsz, KV_W, wb)
    s_h, s_rc, s_fc = [], [], []
    kn = vn = None
    for l in range(depth):
        mqg = tile_row(mem_q_norm_g[l], MEM_HEADS)
        if l < n_a:
            main, qn, hnew, xrpre = _dec_in_a(
                xs, row(norm_mix_g[l]), (w_in_a_b, l), rnn_conv_w[l], row(rnn_conv_b[l]),
                (wg_b, l), row(b_gate_x[l]), row(b_gate_a[l]), row(lru_param[l]), mqg, bd,
                state_rglru_conv[l].transpose(1, 0, 2), state_rglru_h[l])
            s_h.append(hnew)
            s_rc.append(jnp.concatenate([state_rglru_conv[l][:, 1:], xrpre[:, None, :]], axis=1))
            wo, wo_layer = w_out_b, l
        else:
            j = l - n_a
            q, qn = _dec_in_b(xs, row(norm_mix_g[l]), (w_in_b_b, j), tile_row(q_norm_g[j], N_Q),
                              cos_s, sin_s, mqg, bd)
            sink_rows = jnp.zeros((DEC_HEAD_ROWS, LANES), F32).at[:N_Q].set(
                jnp.broadcast_to(sink_gk[j].reshape(N_Q, 1), (N_Q, LANES)))
            main = _dec_swa(q, ckb, cvb, kn, vn, sink_rows)
            wo, wo_layer = w_out_perm_b, j
        mo = _dec_mem_attn(qn, cmk, cmv, l)
        xs, ug, uv = _dec_out_ffn(xs, main, mo, wo, wo_layer, l, row(norm_ffn_g[l]), wup_b,
                                  fcw[l], fcb[l], wdn_b, state_ffn_conv[l].transpose(1, 0, 2))
        unew = jnp.concatenate([ug, uv], axis=1)
        s_fc.append(jnp.concatenate([state_ffn_conv[l][:, 1:], unew[:, None, :]], axis=1))
        if l == n_a - 1:
            kn, vn = _shared_kv(xs, row(kv_norm_g), w_kv_b, tile_row(k_norm_g, N_KV), bd,
                                cos_s, sin_s, dbsz)
    y_sample = xs.reshape(dbsz, 1, D_MODEL)
    s_k = kn.reshape(dbsz, 1, N_KV, HEAD_DIM)
    s_v = vn.reshape(dbsz, 1, N_KV, HEAD_DIM)

    return (y_prompt, y_sample, jnp.stack(p_h), jnp.stack(p_rc), jnp.stack(p_fc), p_k, p_v,
            p_mem_k, p_mem_v, jnp.stack(s_h), jnp.stack(s_rc), jnp.stack(s_fc), s_k, s_v)
```

```python
import functools
import math

import jax
import jax.numpy as jnp
from jax import lax
from jax.experimental import pallas as pl
from jax.experimental.pallas import tpu as pltpu

F32 = jnp.float32
BF16 = jnp.bfloat16

D_MODEL = 1024
HEAD_DIM = 64
MEM_HEADS = 4
MEM_W = MEM_HEADS * HEAD_DIM
N_MEM = 256
D_RNN = D_MODEL - MEM_W
RNN_BLOCKS = D_RNN // HEAD_DIM
CONV_A = 4
LRU_C = 8.0
N_Q = D_RNN // HEAD_DIM
N_KV = 4
GROUP = N_Q // N_KV
Q_W = N_Q * HEAD_DIM
KV_W = N_KV * HEAD_DIM
WINDOW = 128
ROPE_THETA = 10000.0
D_FF = 3 * D_MODEL
CONV_F = 3
EPS = 1e-6
NEG = -1e30
ATT_SCALE = HEAD_DIM ** -0.5
PAST_LEN = 8192

SUBLANES = 8
LANES = 128
MXU_DIM = 256
VMEM_LIMIT_BYTES = 56 * 1024 * 1024

TS_MIX = 512
TS_FFN = 512
TS_KV = 1024
FF_CHUNK = 512
N_FF_CHUNKS = D_FF // FF_CHUNK
DEC_SEQ_BLOCK = 16
CAST_TILE = 1024


def _mm(a, b):
    return jnp.dot(a.astype(BF16), b, preferred_element_type=F32)


def _mm_nt(a, b):
    return lax.dot_general(a.astype(BF16), b, (((1,), (1,)), ((), ())),
                           preferred_element_type=F32)


def _mm_split(a, b):
    hi = a.astype(BF16)
    lo = (a - hi.astype(F32)).astype(BF16)
    return (jnp.dot(hi, b, preferred_element_type=F32)
            + jnp.dot(lo, b, preferred_element_type=F32))


def _rmsnorm(x, g):
    ms = jnp.mean(x * x, axis=-1, keepdims=True)
    return x * lax.rsqrt(ms + EPS) * g


def _head_rmsnorm(x, bd, g):
    parts = []
    for c in range(x.shape[1] // MXU_DIM):
        xc = x[:, c * MXU_DIM:(c + 1) * MXU_DIM]
        ms = _mm_split(xc * xc, bd)
        parts.append(xc * lax.rsqrt(ms + EPS))
    y = parts[0] if len(parts) == 1 else jnp.concatenate(parts, axis=1)
    return y * g


def _tile_lanes(t, width):
    reps = width // t.shape[1]
    return t if reps == 1 else jnp.concatenate([t] * reps, axis=1)


def _rope(x, cos_t, sin_t):
    w = x.shape[1]
    lane = lax.broadcasted_iota(jnp.int32, x.shape, 1)
    first = (lane % HEAD_DIM) < (HEAD_DIM // 2)
    swapped = jnp.where(first, pltpu.roll(x, w - HEAD_DIM // 2, 1),
                        pltpu.roll(x, HEAD_DIM // 2, 1))
    return x * _tile_lanes(cos_t, w) + swapped * _tile_lanes(sin_t, w)


def _gelu(x):
    c = math.sqrt(2.0 / math.pi)
    return x * (0.5 * (1.0 + jnp.tanh(c * (x + 0.044715 * (x * x * x)))))


def _log_sigmoid(x):
    return jnp.minimum(x, 0.0) - jnp.log1p(jnp.exp(-jnp.abs(x)))


def _slab_stage(x, work_ref, work0, tail_ref, tail0):
    ts = x.shape[0]
    for s in range(x.shape[1] // LANES):
        xs = x[:, s * LANES:(s + 1) * LANES]
        buf = work_ref.at[work0 + s]
        buf[0:SUBLANES, :] = tail_ref[tail0 + s]
        buf[SUBLANES:SUBLANES + ts, :] = xs
        tail_ref[tail0 + s] = xs[ts - SUBLANES:]


def _slab_taps(ts, n_slabs, work_ref, work0, w_ref, b_ref, col0):
    k = w_ref.shape[0]
    outs = []
    for s in range(n_slabs):
        lanes = slice(col0 + s * LANES, col0 + (s + 1) * LANES)
        buf = work_ref.at[work0 + s]
        acc = buf[SUBLANES - (k - 1):SUBLANES - (k - 1) + ts, :] * w_ref[0:1, lanes]
        for j in range(1, k):
            off = SUBLANES - (k - 1 - j)
            acc = acc + buf[off:off + ts, :] * w_ref[j:j + 1, lanes]
        outs.append(acc + b_ref[:, lanes])
    return outs


def _sqrt_pos(x):
    return jnp.where(x > 0.0, x * lax.rsqrt(x), 0.0)


def _lru_coeffs(xr, wg_ref, bgx, bga, logsig):
    xb = xr.astype(BF16)
    gxs, gas = [], []
    for c in range(D_RNN // MXU_DIM):
        gg = jnp.dot(xb[:, c * MXU_DIM:(c + 1) * MXU_DIM], wg_ref[c],
                     preferred_element_type=F32)
        gxs.append(gg[:, :MXU_DIM])
        gas.append(gg[:, MXU_DIM:])
    gx = jax.nn.sigmoid(jnp.concatenate(gxs, axis=1) + bgx)
    ga = jax.nn.sigmoid(jnp.concatenate(gas, axis=1) + bga)
    log_a = ga * (LRU_C * logsig)
    a = jnp.exp(log_a)
    mult = _sqrt_pos(-jnp.tanh(log_a) * (a * a + 1.0))
    return a, mult * gx * xr


def _lru_scan(a, b, a_ref, b_ref, hc_ref):
    ts = a.shape[0]
    outs = []
    for s in range(a.shape[1] // LANES):
        lanes = slice(s * LANES, (s + 1) * LANES)
        a_s, b_s = a[:, lanes], b[:, lanes]
        abuf, bbuf = a_ref.at[s], b_ref.at[s]
        d = 1
        while d < SUBLANES:
            abuf[SUBLANES:SUBLANES + ts, :] = a_s
            bbuf[SUBLANES:SUBLANES + ts, :] = b_s
            b_s = a_s * bbuf[SUBLANES - d:SUBLANES - d + ts, :] + b_s
            a_s = a_s * abuf[SUBLANES - d:SUBLANES - d + ts, :]
            d *= 2
        h = hc_ref[:, lanes]
        hs = []
        for q in range(ts // SUBLANES):
            rows = slice(q * SUBLANES, (q + 1) * SUBLANES)
            h = a_s[rows] * h + b_s[rows]
            hs.append(h)
        hc_ref[:, lanes] = jnp.broadcast_to(h[SUBLANES - 1:], (SUBLANES, LANES))
        outs.append(jnp.concatenate(hs, axis=0))
    return jnp.concatenate(outs, axis=1)


def _head_mask(shape, h):
    lane = lax.broadcasted_iota(jnp.int32, shape, 1)
    return (lane >= h * HEAD_DIM) & (lane < (h + 1) * HEAD_DIM)


def _head_masked_stack(blocks, n_heads):
    parts = []
    for h in range(n_heads):
        for blk in blocks:
            parts.append(jnp.where(_head_mask(blk.shape, h), blk, 0.0).astype(BF16))
    return jnp.concatenate(parts, axis=0)


def _mem_attention(qn, kcat, vcat):
    s = _mm_nt(qn, kcat)
    parts = []
    for h in range(MEM_HEADS):
        sh = s[:, h * N_MEM:(h + 1) * N_MEM]
        m = jnp.max(sh, axis=-1, keepdims=True)
        p = jnp.exp(sh - m)
        den = jnp.sum(p, axis=-1, keepdims=True)
        parts.append((p * (1.0 / den)).astype(BF16))
    return jnp.dot(jnp.concatenate(parts, axis=1), vcat, preferred_element_type=F32)


def _cast_blocks_kernel(w_ref, o_ref):
    for b in range(o_ref.shape[0]):
        o_ref[b] = w_ref[:, b * MXU_DIM:(b + 1) * MXU_DIM].astype(BF16)


def _mem_kv_kernel(mem_ref, g_ref, w_ref, kg_ref, bd_ref, k_ref, v_ref):
    h = _mm(_rmsnorm(mem_ref[...], g_ref[0]), w_ref[0])
    k_ref[0] = _head_rmsnorm(h[:, :MEM_W], bd_ref[...], kg_ref[0])
    v_ref[0] = h[:, MEM_W:]


def _shared_kv_kernel(x_ref, g_ref, w_ref, kg_ref, bd_ref, cos_ref, sin_ref,
                      k_ref, v_ref):
    h = _mm(_rmsnorm(x_ref[...], g_ref[...]), w_ref[...])
    k = _head_rmsnorm(h[:, :KV_W], bd_ref[...], kg_ref[...])
    k_ref[...] = _rope(k, cos_ref[...], sin_ref[...])
    v_ref[...] = h[:, KV_W:]


def _mixer_a_kernel(x_ref, g_ref, win_ref, cw_ref, cb_ref, wg_ref, bgx_ref, bga_ref,
                    lru_ref, mqg_ref, bd_ref, mk_ref, mv_ref, wout_ref, rc0_ref, h0_ref,
                    xo_ref, hlast_ref, rctail_ref,
                    conv_s, tail_s, a_s, b_s, hc_s, kcat_s, vcat_s):
    ts = x_ref.shape[0]
    n_slabs = D_RNN // LANES

    @pl.when(pl.program_id(1) == 0)
    def _():
        for s in range(n_slabs):
            tail_s[s] = rc0_ref[0, :, s * LANES:(s + 1) * LANES]
        a_s[:, 0:SUBLANES, :] = jnp.ones((n_slabs, SUBLANES, LANES), F32)
        b_s[:, 0:SUBLANES, :] = jnp.zeros((n_slabs, SUBLANES, LANES), F32)
        hc_s[...] = jnp.broadcast_to(h0_ref[0], hc_s.shape)
        kcat_s[...] = _head_masked_stack([mk_ref[0]], MEM_HEADS)
        vcat_s[...] = _head_masked_stack([mv_ref[0]], MEM_HEADS)

    x = x_ref[...]
    hn = _rmsnorm(x, g_ref[...]).astype(BF16)

    def in_proj(lo, hi):
        return jnp.dot(hn, win_ref[:, lo:hi], preferred_element_type=F32)

    xr_pre = in_proj(D_RNN, 2 * D_RNN)
    _slab_stage(xr_pre, conv_s, 0, tail_s, 0)
    xr = jnp.concatenate(_slab_taps(ts, n_slabs, conv_s, 0, cw_ref, cb_ref, 0), axis=1)
    rctail_ref[0] = xr_pre[ts - SUBLANES:]
    qm = in_proj(2 * D_RNN, 2 * D_RNN + MEM_W)
    a, b = _lru_coeffs(xr, wg_ref, bgx_ref[...], bga_ref[...], _log_sigmoid(lru_ref[...]))
    qn = _head_rmsnorm(qm, bd_ref[...], mqg_ref[...]) * ATT_SCALE
    gate = in_proj(0, D_RNN)
    mo = _mem_attention(qn, kcat_s[...], vcat_s[...])
    h = _lru_scan(a, b, a_s, b_s, hc_s)
    hlast_ref[0] = h[ts - 1:ts]
    main = h * _gelu(gate)

    y = _mm(jnp.concatenate([main, mo], axis=1), wout_ref[...])
    xo_ref[...] = x + y


def _mixer_b_kernel(sink_ref, x_ref, g_ref, win_ref, qg_ref, cos_ref, sin_ref,
                    kcur_ref, kprev_ref, vcur_ref, vprev_ref,
                    mqg_ref, bd_ref, mk_ref, mv_ref, wout_ref,
                    xo_ref, kcat_s, vcat_s):
    ts = x_ref.shape[0]
    i = pl.program_id(1)

    @pl.when(i == 0)
    def _():
        kcat_s[...] = _head_masked_stack([mk_ref[0]], MEM_HEADS)
        vcat_s[...] = _head_masked_stack([mv_ref[0]], MEM_HEADS)

    x = x_ref[...]
    u = _mm(_rmsnorm(x, g_ref[...]), win_ref[...])
    q = _head_rmsnorm(u[:, :Q_W], bd_ref[...], qg_ref[...])
    q = (_rope(q, cos_ref[...], sin_ref[...]) * ATT_SCALE).astype(BF16)
    qm = u[:, Q_W:]

    row = lax.broadcasted_iota(jnp.int32, (WINDOW, 2 * WINDOW), 0)
    kj = lax.broadcasted_iota(jnp.int32, (WINDOW, 2 * WINDOW), 1) - WINDOW
    mask_inner = (kj <= row) & (kj >= row - WINDOW)
    mask_first = (kj <= row) & (kj >= jnp.maximum(row - WINDOW, jnp.where(i > 0, -WINDOW, 0)))

    kt = kcur_ref[...]
    vt = vcur_ref[...]

    def block_scores(jb):
        lo, hi = jb * WINDOW, (jb + 1) * WINDOW
        if jb == 0:
            kp, vp = kprev_ref[...], vprev_ref[...]
        else:
            kp, vp = kt[lo - WINDOW:lo], vt[lo - WINDOW:lo]
        kcat = _head_masked_stack([kp, kt[lo:hi]], N_KV)
        vcat = _head_masked_stack([vp, vt[lo:hi]], N_KV)
        qs = jnp.concatenate([q[lo:hi, g * KV_W:(g + 1) * KV_W] for g in range(GROUP)], axis=0)
        s = lax.dot_general(qs, kcat, (((1,), (1,)), ((), ())), preferred_element_type=F32)
        return s, vcat

    n_blocks = ts // WINDOW
    mains = []
    nxt = block_scores(0)
    for jb in range(n_blocks):
        s, vcat = nxt
        if jb + 1 < n_blocks:
            nxt = block_scores(jb + 1)
        mask = mask_first if jb == 0 else mask_inner
        prow = []
        for g in range(GROUP):
            pseg = []
            for kv in range(N_KV):
                seg = s[g * WINDOW:(g + 1) * WINDOW, kv * 2 * WINDOW:(kv + 1) * 2 * WINDOW]
                seg = jnp.where(mask, seg, NEG)
                sink = sink_ref[g * N_KV + kv]
                m = jnp.maximum(jnp.max(seg, axis=-1, keepdims=True), sink)
                p = jnp.exp(seg - m)
                den = jnp.sum(p, axis=-1, keepdims=True) + jnp.exp(sink - m)
                pseg.append((p * (1.0 / den)).astype(BF16))
            prow.append(jnp.concatenate(pseg, axis=1))
        o = jnp.dot(jnp.concatenate(prow, axis=0), vcat, preferred_element_type=F32)
        mains.append(jnp.concatenate([o[g * WINDOW:(g + 1) * WINDOW] for g in range(GROUP)], axis=1))
    main = mains[0] if len(mains) == 1 else jnp.concatenate(mains, axis=0)

    qn = _head_rmsnorm(qm, bd_ref[...], mqg_ref[...]) * ATT_SCALE
    mo = _mem_attention(qn, kcat_s[...], vcat_s[...])

    y = _mm(jnp.concatenate([main, mo], axis=1), wout_ref[...])
    xo_ref[...] = x + y


def _ffn_kernel(x_ref, g_ref, wup_ref, cw_ref, cb_ref, wdn_ref,
                xo_ref, utail_ref, conv_s, tail_s, act_s, xres_s, *, tiles_per_seq):
    ts = x_ref.shape[0]
    slabs = FF_CHUNK // LANES
    blk_slabs = MXU_DIM // LANES
    blks = FF_CHUNK // MXU_DIM
    i = pl.program_id(0)
    slot = i % 2
    pslot = 1 - slot

    @pl.when(i == 0)
    def _():
        act_s[1] = jnp.zeros(act_s.shape[1:], act_s.dtype)
        xres_s[1] = jnp.zeros(xres_s.shape[1:], xres_s.dtype)

    @pl.when(i % tiles_per_seq == 0)
    def _():
        tail_s[...] = jnp.zeros_like(tail_s)

    x = x_ref[...]
    xres_s[slot] = x
    hn = _rmsnorm(x, g_ref[...]).astype(BF16)

    def chunk_cols(j):
        return [(c, ((j % 2) * 2 + half) * slabs) for half, c in enumerate((j, N_FF_CHUNKS + j))]

    def up_project(j):
        for c, work0 in chunk_cols(j):
            for b in range(blks):
                blk = c * blks + b
                u = jnp.dot(hn, wup_ref[blk], preferred_element_type=F32)
                _slab_stage(u, conv_s, work0 + b * blk_slabs, tail_s, c * slabs + b * blk_slabs)
                utail_ref[0, :, blk * MXU_DIM:(blk + 1) * MXU_DIM] = u[ts - SUBLANES:]

    def activation(j):
        cg, cv = [_slab_taps(ts, slabs, conv_s, work0, cw_ref, cb_ref, c * FF_CHUNK)
                  for c, work0 in chunk_cols(j)]
        return jnp.concatenate([_gelu(a) * b for a, b in zip(cg, cv)], axis=1).astype(BF16)

    def down_prev(n):
        return jnp.dot(act_s[pslot], wdn_ref[n], preferred_element_type=F32)

    n_dn = wdn_ref.shape[0]
    outs = [None] * n_dn
    outs[0] = down_prev(0)
    up_project(0)
    for j in range(N_FF_CHUNKS):
        if j + 1 < N_FF_CHUNKS:
            up_project(j + 1)
        else:
            outs[1] = down_prev(1)
        act_s[slot, :, j * FF_CHUNK:(j + 1) * FF_CHUNK] = activation(j)
    for n in range(2, n_dn):
        outs[n] = down_prev(n)
    xo_ref[...] = xres_s[pslot] + jnp.concatenate(outs, axis=1)


def _dec_in_a_kernel(x_ref, g_ref, win_ref, cw_ref, cb_ref, wg_ref, bgx_ref, bga_ref,
                     lru_ref, mqg_ref, bd_ref, b0_ref, b1_ref, b2_ref, h0_ref,
                     main_ref, qn_ref, hnew_ref, xrpre_ref):
    u = _mm(_rmsnorm(x_ref[...], g_ref[...]), win_ref[...])
    gate = u[:, :D_RNN]
    xr_pre = u[:, D_RNN:2 * D_RNN]
    qm = u[:, 2 * D_RNN:]
    xr = b0_ref[...] * cw_ref[0:1, :]
    xr = xr + b1_ref[...] * cw_ref[1:2, :]
    xr = xr + b2_ref[...] * cw_ref[2:3, :]
    xr = xr + xr_pre * cw_ref[3:4, :]
    xr = xr + cb_ref[...]
    a, b = _lru_coeffs(xr, wg_ref, bgx_ref[...], bga_ref[...], _log_sigmoid(lru_ref[...]))
    h = a * h0_ref[...] + b
    main_ref[...] = h * _gelu(gate)
    qn_ref[...] = _head_rmsnorm(qm, bd_ref[...], mqg_ref[...]) * ATT_SCALE
    hnew_ref[...] = h
    xrpre_ref[...] = xr_pre


def _dec_in_b_kernel(x_ref, g_ref, win_ref, qg_ref, cos_ref, sin_ref, mqg_ref, bd_ref,
                     q_ref, qn_ref):
    u = _mm(_rmsnorm(x_ref[...], g_ref[...]), win_ref[...])
    q = _head_rmsnorm(u[:, :Q_W], bd_ref[...], qg_ref[...])
    q_ref[...] = _rope(q, cos_ref[...], sin_ref[...]) * ATT_SCALE
    qn_ref[...] = _head_rmsnorm(u[:, Q_W:], bd_ref[...], mqg_ref[...]) * ATT_SCALE


DEC_HEAD_ROWS = 16


def _own_head_lanes(n_heads, width):
    row = lax.broadcasted_iota(jnp.int32, (DEC_HEAD_ROWS, width), 0)
    lane = lax.broadcasted_iota(jnp.int32, (DEC_HEAD_ROWS, width), 1)
    start = (row % (width // HEAD_DIM)) * HEAD_DIM
    return (lane >= start) & (lane < start + HEAD_DIM) & (row < n_heads)


def _dec_mem_attn_kernel(q_ref, kt_ref, vt_ref, o_ref):
    own = _own_head_lanes(MEM_HEADS, MEM_W)
    for s in range(q_ref.shape[0]):
        q_rows = jnp.broadcast_to(q_ref[s:s + 1, :], (DEC_HEAD_ROWS, MEM_W))
        qbd = jnp.where(own, q_rows, 0.0).astype(BF16)
        sc = jnp.dot(qbd, kt_ref[s].astype(BF16), preferred_element_type=F32)
        m = jnp.max(sc, axis=-1, keepdims=True)
        p = jnp.exp(sc - m)
        den = jnp.sum(p, axis=-1, keepdims=True)
        pn = (p * (1.0 / den)).astype(BF16)
        o_all = lax.dot_general(pn, vt_ref[s].astype(BF16), (((1,), (1,)), ((), ())),
                                preferred_element_type=F32)
        o_ref[s:s + 1, :] = jnp.sum(jnp.where(own, o_all, 0.0), axis=0, keepdims=True)


def _dec_swa_kernel(q_ref, kt_ref, vt_ref, kn_ref, vn_ref, sink_ref, o_ref):
    n_heads = GROUP * N_KV
    own = _own_head_lanes(n_heads, KV_W)
    grp = lax.broadcasted_iota(jnp.int32, (DEC_HEAD_ROWS, KV_W), 0) // N_KV
    sink = sink_ref[:, 0:1]
    for s in range(q_ref.shape[0]):
        q_rows = jnp.zeros((DEC_HEAD_ROWS, KV_W), F32)
        for g in range(GROUP):
            qg = jnp.broadcast_to(q_ref[s:s + 1, g * KV_W:(g + 1) * KV_W], (DEC_HEAD_ROWS, KV_W))
            q_rows = jnp.where(grp == g, qg, q_rows)
        qbd = jnp.where(own, q_rows, 0.0)
        s_buf = jnp.dot(qbd.astype(BF16), kt_ref[s].astype(BF16), preferred_element_type=F32)
        s_new = jnp.sum(qbd * kn_ref[s:s + 1, :], axis=-1, keepdims=True)
        m = jnp.maximum(jnp.maximum(jnp.max(s_buf, axis=-1, keepdims=True), s_new), sink)
        p_buf = jnp.exp(s_buf - m)
        p_new = jnp.exp(s_new - m)
        den = jnp.sum(p_buf, axis=-1, keepdims=True) + p_new + jnp.exp(sink - m)
        r = 1.0 / den
        o_all = lax.dot_general((p_buf * r).astype(BF16), vt_ref[s].astype(BF16),
                                (((1,), (1,)), ((), ())), preferred_element_type=F32)
        o_all = jnp.where(own, o_all + (p_new * r) * vn_ref[s:s + 1, :], 0.0)
        o_sum = o_all + pltpu.roll(o_all, 1, 0)
        o_sum = o_sum + pltpu.roll(o_sum, 2, 0)
        for g in range(GROUP):
            last = (g + 1) * N_KV - 1
            o_ref[s:s + 1, g * KV_W:(g + 1) * KV_W] = o_sum[last:last + 1, :]


def _dec_out_ffn_kernel(x_ref, main_ref, mo_ref, wout_ref, g_ref,
                        wug_ref, wuv_ref, cwg_ref, cwv_ref, cbg_ref, cbv_ref, wdn_ref,
                        sg0_ref, sg1_ref, sv0_ref, sv1_ref,
                        xo_ref, ug_ref, uv_ref,
                        xmid_s, hn_s, acc_s):
    j = pl.program_id(0)

    @pl.when(j == 0)
    def _():
        y = _mm(jnp.concatenate([main_ref[...], mo_ref[...]], axis=1), wout_ref[...])
        xmid = x_ref[...] + y
        xmid_s[...] = xmid
        hn_s[...] = _rmsnorm(xmid, g_ref[...]).astype(BF16)
        acc_s[...] = jnp.zeros_like(acc_s)

    hn = hn_s[...]
    up = lambda w_ref: jnp.concatenate(
        [jnp.dot(hn, w_ref[b], preferred_element_type=F32) for b in range(w_ref.shape[0])], axis=1)
    ug = up(wug_ref)
    uv = up(wuv_ref)
    ug_ref[...] = ug
    uv_ref[...] = uv
    cg = (sg0_ref[...] * cwg_ref[0:1, :] + sg1_ref[...] * cwg_ref[1:2, :]
          + ug * cwg_ref[2:3, :] + cbg_ref[...])
    cv = (sv0_ref[...] * cwv_ref[0:1, :] + sv1_ref[...] * cwv_ref[1:2, :]
          + uv * cwv_ref[2:3, :] + cbv_ref[...])
    act = (_gelu(cg) * cv).astype(BF16)
    for n in range(wdn_ref.shape[0]):
        acc_s[:, n * MXU_DIM:(n + 1) * MXU_DIM] += jnp.dot(act, wdn_ref[n],
                                                           preferred_element_type=F32)

    @pl.when(j == pl.num_programs(0) - 1)
    def _():
        xo_ref[...] = xmid_s[...] + acc_s[...]


def _const_spec(shape):
    nd = len(shape)
    return pl.BlockSpec(shape, lambda *_: (0,) * nd)


def _layer_spec(arr, layer):
    nd = arr.ndim - 1
    return pl.BlockSpec((None,) + arr.shape[1:], lambda *_: (layer,) + (0,) * nd)


def _resident_spec(op):
    return _layer_spec(*op) if isinstance(op, tuple) else _const_spec(op.shape)


def _operand(op):
    return op[0] if isinstance(op, tuple) else op


def _params(*sem):
    return pltpu.CompilerParams(dimension_semantics=sem, vmem_limit_bytes=VMEM_LIMIT_BYTES)


def _cast_blocks(w):
    layers, r, c = w.shape
    tile = CAST_TILE
    per = tile // MXU_DIM
    return pl.pallas_call(
        _cast_blocks_kernel,
        grid=(layers, r // tile, c // tile),
        in_specs=[pl.BlockSpec((None, tile, tile), lambda l, i, j: (l, i, j))],
        out_specs=pl.BlockSpec((None, per, tile, MXU_DIM), lambda l, i, j: (l, j, i, 0)),
        out_shape=jax.ShapeDtypeStruct((layers, c // MXU_DIM, r, MXU_DIM), BF16),
        compiler_params=_params("arbitrary", "arbitrary", "arbitrary"),
        name="cast_blocks",
    )(w)


def _mem_kv(mem2d, g, w, kg, bd):
    depth = w.shape[0]
    rows = mem2d.shape[0]
    out = jax.ShapeDtypeStruct((depth, rows, MEM_W), F32)
    return pl.pallas_call(
        _mem_kv_kernel,
        grid=(depth,),
        in_specs=[
            _const_spec(mem2d.shape),
            pl.BlockSpec((1, 1, D_MODEL), lambda l: (l, 0, 0)),
            pl.BlockSpec((1, D_MODEL, 2 * MEM_W), lambda l: (l, 0, 0)),
            pl.BlockSpec((1, 1, MEM_W), lambda l: (l, 0, 0)),
            _const_spec(bd.shape),
        ],
        out_specs=[pl.BlockSpec((1, rows, MEM_W), lambda l: (l, 0, 0))] * 2,
        out_shape=[out, out],
        compiler_params=_params("arbitrary"),
        name="mem_kv",
    )(mem2d, g, w, kg, bd)


def _shared_kv(x2d, g, w, kg, bd, cos_t, sin_t, ts):
    rows = x2d.shape[0]
    tab_blocks = cos_t.shape[0] // ts
    out = jax.ShapeDtypeStruct((rows, KV_W), F32)
    return pl.pallas_call(
        _shared_kv_kernel,
        grid=(rows // ts,),
        in_specs=[
            pl.BlockSpec((ts, D_MODEL), lambda i: (i, 0)),
            _const_spec(g.shape), _const_spec(w.shape), _const_spec(kg.shape),
            _const_spec(bd.shape),
            pl.BlockSpec((ts, LANES), lambda i: (i % tab_blocks, 0)),
            pl.BlockSpec((ts, LANES), lambda i: (i % tab_blocks, 0)),
        ],
        out_specs=[pl.BlockSpec((ts, KV_W), lambda i: (i, 0))] * 2,
        out_shape=[out, out],
        compiler_params=_params("arbitrary"),
        name="shared_kv",
    )(x2d, g, w, kg, bd, cos_t, sin_t)


def _mixer_a(x2d, bsz, g, win, cw, cb, wg, bgx, bga, lru, mqg, bd, mk, mv, wout, rc0, h0):
    rows = x2d.shape[0]
    ts = TS_MIX
    nt = rows // bsz // ts
    consts = (g, win, cw, cb, wg, bgx, bga, lru, mqg, bd)
    per_b3 = lambda b, i: (b, 0, 0)
    return pl.pallas_call(
        _mixer_a_kernel,
        grid=(bsz, nt),
        in_specs=[pl.BlockSpec((ts, D_MODEL), lambda b, i: (b * nt + i, 0))]
        + [_resident_spec(c) for c in consts]
        + [pl.BlockSpec((1, N_MEM, MEM_W), per_b3)] * 2
        + [_resident_spec(wout),
           pl.BlockSpec((1, SUBLANES, D_RNN), per_b3),
           pl.BlockSpec((1, 1, D_RNN), per_b3)],
        out_specs=[
            pl.BlockSpec((ts, D_MODEL), lambda b, i: (b * nt + i, 0)),
            pl.BlockSpec((1, 1, D_RNN), per_b3),
            pl.BlockSpec((1, SUBLANES, D_RNN), per_b3),
        ],
        out_shape=[
            jax.ShapeDtypeStruct((rows, D_MODEL), F32),
            jax.ShapeDtypeStruct((bsz, 1, D_RNN), F32),
            jax.ShapeDtypeStruct((bsz, SUBLANES, D_RNN), F32),
        ],
        scratch_shapes=[
            pltpu.VMEM((D_RNN // LANES, SUBLANES + ts, LANES), F32),
            pltpu.VMEM((D_RNN // LANES, SUBLANES, LANES), F32),
            pltpu.VMEM((D_RNN // LANES, SUBLANES + ts, LANES), F32),
            pltpu.VMEM((D_RNN // LANES, SUBLANES + ts, LANES), F32),
            pltpu.VMEM((SUBLANES, D_RNN), F32),
            pltpu.VMEM((MEM_HEADS * N_MEM, MEM_W), BF16),
            pltpu.VMEM((MEM_HEADS * N_MEM, MEM_W), BF16),
        ],
        compiler_params=_params("arbitrary", "arbitrary"),
        name="mixer_a",
    )(x2d, *[_operand(c) for c in consts], mk, mv, _operand(wout), rc0, h0)


def _mixer_b(x2d, bsz, sink_tab, g, win, qg, cos_t, sin_t, ksh, vsh, mqg, bd, mk, mv, wout):
    rows = x2d.shape[0]
    ts = TS_MIX
    nt = rows // bsz // ts
    wpt = ts // WINDOW
    cur = lambda b, i: (b * nt + i, 0)
    prev = lambda b, i: (jnp.maximum((b * nt + i) * wpt - 1, 0), 0)
    table = lambda b, i: (i, 0)
    per_b3 = lambda b, i: (b, 0, 0)
    return pl.pallas_call(
        _mixer_b_kernel,
        grid=(bsz, nt),
        in_specs=[
            pl.BlockSpec(memory_space=pltpu.SMEM),
            pl.BlockSpec((ts, D_MODEL), cur),
            _const_spec(g.shape), _resident_spec(win), _const_spec(qg.shape),
            pl.BlockSpec((ts, LANES), table),
            pl.BlockSpec((ts, LANES), table),
            pl.BlockSpec((ts, KV_W), cur), pl.BlockSpec((WINDOW, KV_W), prev),
            pl.BlockSpec((ts, KV_W), cur), pl.BlockSpec((WINDOW, KV_W), prev),
            _const_spec(mqg.shape), _const_spec(bd.shape),
            pl.BlockSpec((1, N_MEM, MEM_W), per_b3), pl.BlockSpec((1, N_MEM, MEM_W), per_b3),
            _resident_spec(wout),
        ],
        out_specs=pl.BlockSpec((ts, D_MODEL), cur),
        out_shape=jax.ShapeDtypeStruct((rows, D_MODEL), F32),
        scratch_shapes=[
            pltpu.VMEM((MEM_HEADS * N_MEM, MEM_W), BF16),
            pltpu.VMEM((MEM_HEADS * N_MEM, MEM_W), BF16),
        ],
        compiler_params=_params("arbitrary", "arbitrary"),
        name="mixer_b",
    )(sink_tab, x2d, g, _operand(win), qg, cos_t, sin_t, ksh, ksh, vsh, vsh, mqg, bd, mk, mv,
      _operand(wout))


def _ffn(x2d, bsz, layer, g, wup, cw, cb, wdn):
    rows = x2d.shape[0]
    ts = TS_FFN
    nt = rows // ts
    tiles_per_seq = rows // bsz // ts
    last = nt - 1
    return pl.pallas_call(
        functools.partial(_ffn_kernel, tiles_per_seq=tiles_per_seq),
        grid=(nt + 1,),
        in_specs=[pl.BlockSpec((ts, D_MODEL), lambda i: (jnp.minimum(i, last), 0)),
                  _const_spec(g.shape), _layer_spec(wup, layer), _const_spec(cw.shape),
                  _const_spec(cb.shape), _layer_spec(wdn, layer)],
        out_specs=[
            pl.BlockSpec((ts, D_MODEL), lambda i: (jnp.maximum(i - 1, 0), 0)),
            pl.BlockSpec((1, SUBLANES, 2 * D_FF),
                         lambda i: (jnp.minimum(i, last) // tiles_per_seq, 0, 0)),
        ],
        out_shape=[
            jax.ShapeDtypeStruct((rows, D_MODEL), F32),
            jax.ShapeDtypeStruct((bsz, SUBLANES, 2 * D_FF), F32),
        ],
        scratch_shapes=[pltpu.VMEM((4 * FF_CHUNK // LANES, SUBLANES + ts, LANES), F32),
                        pltpu.VMEM((2 * D_FF // LANES, SUBLANES, LANES), F32),
                        pltpu.VMEM((2, ts, D_FF), BF16),
                        pltpu.VMEM((2, ts, D_MODEL), F32)],
        compiler_params=_params("arbitrary"),
        name="ffn",
    )(x2d, g, wup, cw, cb, wdn)


def _dec_in_a(x, g, win, cw, cb, wg, bgx, bga, lru, mqg, bd, rc, h0):
    n = x.shape[0]
    consts = (x, g, win, cw, cb, wg, bgx, bga, lru, mqg, bd)
    buf = lambda j: pl.BlockSpec((None, n, D_RNN), lambda i: (j, 0, 0))
    return pl.pallas_call(
        _dec_in_a_kernel,
        grid=(1,),
        in_specs=[_resident_spec(c) for c in consts]
        + [buf(0), buf(1), buf(2), _const_spec(h0.shape)],
        out_specs=[_const_spec((n, D_RNN)), _const_spec((n, MEM_W)),
                   _const_spec((n, D_RNN)), _const_spec((n, D_RNN))],
        out_shape=[jax.ShapeDtypeStruct((n, D_RNN), F32), jax.ShapeDtypeStruct((n, MEM_W), F32),
                   jax.ShapeDtypeStruct((n, D_RNN), F32), jax.ShapeDtypeStruct((n, D_RNN), F32)],
        compiler_params=_params("arbitrary"),
        name="dec_in_a",
    )(*[_operand(c) for c in consts], rc, rc, rc, h0)


def _dec_in_b(x, g, win, qg, cos_t, sin_t, mqg, bd):
    n = x.shape[0]
    args = (x, g, win, qg, cos_t, sin_t, mqg, bd)
    return pl.pallas_call(
        _dec_in_b_kernel,
        grid=(1,),
        in_specs=[_resident_spec(a) for a in args],
        out_specs=[_const_spec((n, Q_W)), _const_spec((n, MEM_W))],
        out_shape=[jax.ShapeDtypeStruct((n, Q_W), F32), jax.ShapeDtypeStruct((n, MEM_W), F32)],
        compiler_params=_params("arbitrary"),
        name="dec_in_b",
    )(*[_operand(a) for a in args])


def _dec_mem_attn(qn, ckt, cvt, layer):
    n = qn.shape[0]
    sb = DEC_SEQ_BLOCK
    return pl.pallas_call(
        _dec_mem_attn_kernel,
        grid=(n // sb,),
        in_specs=[pl.BlockSpec((sb, MEM_W), lambda i: (i, 0)),
                  pl.BlockSpec((None, sb, MEM_W, N_MEM), lambda i: (layer, i, 0, 0)),
                  pl.BlockSpec((None, sb, MEM_W, N_MEM), lambda i: (layer, i, 0, 0))],
        out_specs=pl.BlockSpec((sb, MEM_W), lambda i: (i, 0)),
        out_shape=jax.ShapeDtypeStruct((n, MEM_W), F32),
        compiler_params=_params("arbitrary"),
        name="dec_mem_attn",
    )(qn, ckt, cvt)


def _dec_swa(q, kbt, vbt, kn, vn, sink_rows):
    n = q.shape[0]
    wb = kbt.shape[2]
    sb = DEC_SEQ_BLOCK
    assert wb <= WINDOW
    return pl.pallas_call(
        _dec_swa_kernel,
        grid=(n // sb,),
        in_specs=[pl.BlockSpec((sb, Q_W), lambda i: (i, 0)),
                  pl.BlockSpec((sb, KV_W, wb), lambda i: (i, 0, 0)),
                  pl.BlockSpec((sb, KV_W, wb), lambda i: (i, 0, 0)),
                  pl.BlockSpec((sb, KV_W), lambda i: (i, 0)),
                  pl.BlockSpec((sb, KV_W), lambda i: (i, 0)),
                  _const_spec(sink_rows.shape)],
        out_specs=pl.BlockSpec((sb, Q_W), lambda i: (i, 0)),
        out_shape=jax.ShapeDtypeStruct((n, Q_W), F32),
        compiler_params=_params("arbitrary"),
        name="dec_swa",
    )(q, kbt, vbt, kn, vn, sink_rows)


def _dec_out_ffn(x, main, mo, wout, wout_layer, layer, g, wup, cw, cb, wdn, st):
    n = x.shape[0]
    nch = N_FF_CHUNKS
    blks = FF_CHUNK // MXU_DIM
    lo = lambda j: (0, j)
    hi = lambda j: (0, nch + j)
    state = lambda r, off: pl.BlockSpec((None, n, FF_CHUNK), lambda j: (r, 0, off + j))
    up_blocks = lambda off: pl.BlockSpec((None, blks, D_MODEL, MXU_DIM),
                                         lambda j: (layer, off + j, 0, 0))
    return pl.pallas_call(
        _dec_out_ffn_kernel,
        grid=(nch,),
        in_specs=[_const_spec(x.shape), _const_spec(main.shape), _const_spec(mo.shape),
                  _layer_spec(wout, wout_layer), _const_spec(g.shape),
                  up_blocks(0), up_blocks(nch),
                  pl.BlockSpec((CONV_F, FF_CHUNK), lo), pl.BlockSpec((CONV_F, FF_CHUNK), hi),
                  pl.BlockSpec((1, FF_CHUNK), lo), pl.BlockSpec((1, FF_CHUNK), hi),
                  pl.BlockSpec((None, wdn.shape[1], FF_CHUNK, MXU_DIM), lambda j: (layer, 0, j, 0)),
                  state(0, 0), state(1, 0), state(0, nch), state(1, nch)],
        out_specs=[_const_spec((n, D_MODEL)),
                   pl.BlockSpec((n, FF_CHUNK), lambda j: (0, j)),
                   pl.BlockSpec((n, FF_CHUNK), lambda j: (0, j))],
        out_shape=[jax.ShapeDtypeStruct((n, D_MODEL), F32),
                   jax.ShapeDtypeStruct((n, D_FF), F32),
                   jax.ShapeDtypeStruct((n, D_FF), F32)],
        scratch_shapes=[pltpu.VMEM((n, D_MODEL), F32), pltpu.VMEM((n, D_MODEL), BF16),
                        pltpu.VMEM((n, D_MODEL), F32)],
        compiler_params=_params("arbitrary"),
        name="dec_out_ffn",
    )(x, main, mo, wout, g, wup, wup, cw, cw, cb, cb, wdn, st, st, st, st)


def _rope_tables(pos):
    half = HEAD_DIM // 2
    inv = ROPE_THETA ** (-jnp.arange(half, dtype=F32) / half)
    ang = pos.astype(F32)[:, None] * inv[None, :]
    cos = jnp.cos(ang)
    sin = jnp.sin(ang)
    reps = LANES // HEAD_DIM
    cos_t = jnp.tile(jnp.concatenate([cos, cos], axis=1), (1, reps))
    sin_t = jnp.tile(jnp.concatenate([-sin, sin], axis=1), (1, reps))
    return cos_t, sin_t


def _block_diag_gates(wx, wa):
    per = MXU_DIM // HEAD_DIM
    eye = jnp.eye(per, dtype=F32)

    def bd(w):
        w4 = w.reshape(RNN_BLOCKS // per, per, HEAD_DIM, HEAD_DIM)
        return jnp.einsum('ckij,kK->ckiKj', w4, eye).reshape(RNN_BLOCKS // per, MXU_DIM, MXU_DIM)

    return jnp.concatenate([bd(wx), bd(wa)], axis=2).astype(BF16)


def kernel(x_prompt, x_sample, state_rglru_h, state_rglru_conv, state_ffn_conv, cache_swa_k, cache_swa_v, cache_mem_k, cache_mem_v, mem_prompt, norm_mix_g, norm_ffn_g, w_in_a, rnn_conv_w, rnn_conv_b, w_gate_x, b_gate_x, w_gate_a, b_gate_a, lru_param, w_in_b, q_norm_g, sinks, kv_norm_g, w_kv, k_norm_g, mem_norm_g, w_mem_kv, mem_q_norm_g, mem_k_norm_g, w_out, w_ffn_up, ffn_conv_w, ffn_conv_b, w_ffn_down):
    bsz, seq, _ = x_prompt.shape
    dbsz = x_sample.shape[0]
    depth = norm_mix_g.shape[0]
    n_a = w_in_a.shape[0]
    assert x_sample.shape[1] == 1
    assert seq % TS_MIX == 0 and seq % TS_FFN == 0 and seq % TS_KV == 0 and TS_MIX % WINDOW == 0

    n_b = w_in_b.shape[0]
    wq = w_in_b[:, :, :Q_W].astype(BF16).reshape(n_b, D_MODEL, N_KV, GROUP, HEAD_DIM)
    wq = wq.transpose(0, 1, 3, 2, 4).reshape(n_b, D_MODEL, Q_W)
    wo_main = w_out[n_a:, :Q_W].astype(BF16).reshape(n_b, N_KV, GROUP, HEAD_DIM, D_MODEL)
    wo_main = wo_main.transpose(0, 2, 1, 3, 4).reshape(n_b, Q_W, D_MODEL)
    bd =(jnp.kron(jnp.eye(MXU_DIM // HEAD_DIM, dtype=F32),
                   jnp.ones((HEAD_DIM, HEAD_DIM), F32)) / HEAD_DIM).astype(BF16)

    row = lambda v: v.reshape(1, -1)
    tile_row = lambda v, n: jnp.tile(v, n).reshape(1, -1)
    w_in_a_b = w_in_a.astype(BF16)
    w_in_b_b = jnp.concatenate([wq, w_in_b[:, :, Q_W:].astype(BF16)], axis=2)
    w_out_b = w_out.astype(BF16)
    w_out_perm_b = jnp.concatenate([wo_main, w_out_b[n_a:, Q_W:]], axis=1)
    w_kv_b = w_kv.astype(BF16)
    w_mem_b = w_mem_kv.astype(BF16)
    wup_b = _cast_blocks(w_ffn_up)
    wdn_b = _cast_blocks(w_ffn_down)
    fcw = ffn_conv_w
    fcb = ffn_conv_b.reshape(depth, 1, 2 * D_FF)
    wg_b = jnp.stack([_block_diag_gates(w_gate_x[l], w_gate_a[l]) for l in range(n_a)])
    sink_gk = sinks.reshape(-1, N_KV, GROUP).transpose(0, 2, 1)

    cos_p, sin_p = _rope_tables(jnp.arange(seq, dtype=jnp.int32))
    pos_s = PAST_LEN + jnp.zeros((dbsz,), jnp.int32)
    cos_s, sin_s = _rope_tables(pos_s)

    mem2d = mem_prompt.reshape(bsz * N_MEM, D_MODEL)
    pmk, pmv = _mem_kv(mem2d, mem_norm_g.reshape(depth, 1, D_MODEL), w_mem_b,
                       jnp.tile(mem_k_norm_g, (1, MEM_HEADS)).reshape(depth, 1, MEM_W), bd)
    pmk4 = pmk.reshape(depth, bsz, N_MEM, MEM_W)
    pmv4 = pmv.reshape(depth, bsz, N_MEM, MEM_W)

    x = x_prompt.reshape(bsz * seq, D_MODEL)
    zeros_rc = jnp.zeros((bsz, SUBLANES, D_RNN), F32)
    zeros_h = jnp.zeros((bsz, 1, D_RNN), F32)
    p_h, p_rc, p_fc = [], [], []
    ksh = vsh = None
    for l in range(depth):
        mqg = tile_row(mem_q_norm_g[l], MEM_HEADS)
        if l < n_a:
            x, hl, rct = _mixer_a(
                x, bsz, row(norm_mix_g[l]), (w_in_a_b, l), rnn_conv_w[l], row(rnn_conv_b[l]),
                (wg_b, l), row(b_gate_x[l]), row(b_gate_a[l]), row(lru_param[l]), mqg, bd,
                pmk4[l], pmv4[l], (w_out_b, l), zeros_rc, zeros_h)
            p_h.append(hl.reshape(bsz, D_RNN))
            p_rc.append(rct[:, SUBLANES - (CONV_A - 1):])
        else:
            j = l - n_a
            x = _mixer_b(
                x, bsz, sink_gk[j].reshape(-1), row(norm_mix_g[l]), (w_in_b_b, j),
                tile_row(q_norm_g[j], N_Q), cos_p, sin_p, ksh, vsh, mqg, bd,
                pmk4[l], pmv4[l], (w_out_perm_b, j))
        x, ut = _ffn(x, bsz, l, row(norm_ffn_g[l]), wup_b, fcw[l], fcb[l], wdn_b)
        p_fc.append(ut[:, SUBLANES - (CONV_F - 1):])
        if l == n_a - 1:
            ksh, vsh = _shared_kv(x, row(kv_norm_g), w_kv_b, tile_row(k_norm_g, N_KV), bd,
                                  cos_p, sin_p, TS_KV)
    y_prompt = x.reshape(bsz, seq, D_MODEL)
    keep = min(WINDOW, seq)
    p_k = ksh.reshape(bsz, seq, KV_W)[:, seq - keep:].reshape(bsz, keep, N_KV, HEAD_DIM)
    p_v = vsh.reshape(bsz, seq, KV_W)[:, seq - keep:].reshape(bsz, keep, N_KV, HEAD_DIM)
    p_mem_k = pmk.reshape(depth, bsz, N_MEM, MEM_HEADS, HEAD_DIM)
    p_mem_v = pmv.reshape(depth, bsz, N_MEM, MEM_HEADS, HEAD_DIM)

    xs = x_sample.reshape(dbsz, D_MODEL)
    cmk = cache_mem_k.transpose(0, 1, 3, 4, 2).reshape(depth, dbsz, MEM_W, N_MEM)
    cmv = cache_mem_v.transpose(0, 1, 3, 4, 2).reshape(depth, dbsz, MEM_W, N_MEM)
    wb = cache_swa_k.shape[1]
    ckb = cache_swa_k.transpose(0, 2, 3, 1).reshape(dbsz, KV_W, wb)
    cvb = cache_swa_v.transpose(0, 2, 3, 1).reshape(dbsz, KV_W, wb)
    s_h, s_rc, s_fc = [], [], []
    kn = vn = None
    for l in range(depth):
        mqg = tile_row(mem_q_norm_g[l], MEM_HEADS)
        if l < n_a:
            main, qn, hnew, xrpre = _dec_in_a(
                xs, row(norm_mix_g[l]), (w_in_a_b, l), rnn_conv_w[l], row(rnn_conv_b[l]),
                (wg_b, l), row(b_gate_x[l]), row(b_gate_a[l]), row(lru_param[l]), mqg, bd,
                state_rglru_conv[l].transpose(1, 0, 2), state_rglru_h[l])
            s_h.append(hnew)
            s_rc.append(jnp.concatenate([state_rglru_conv[l][:, 1:], xrpre[:, None, :]], axis=1))
            wo, wo_layer = w_out_b, l
        else:
            j = l - n_a
            q, qn = _dec_in_b(xs, row(norm_mix_g[l]), (w_in_b_b, j), tile_row(q_norm_g[j], N_Q),
                              cos_s, sin_s, mqg, bd)
            sink_rows = jnp.zeros((DEC_HEAD_ROWS, LANES), F32).at[:N_Q].set(
                jnp.broadcast_to(sink_gk[j].reshape(N_Q, 1), (N_Q, LANES)))
            main = _dec_swa(q, ckb, cvb, kn, vn, sink_rows)
            wo, wo_layer = w_out_perm_b, j
        mo = _dec_mem_attn(qn, cmk, cmv, l)
        xs, ug, uv = _dec_out_ffn(xs, main, mo, wo, wo_layer, l, row(norm_ffn_g[l]), wup_b,
                                  fcw[l], fcb[l], wdn_b, state_ffn_conv[l].transpose(1, 0, 2))
        unew = jnp.concatenate([ug, uv], axis=1)
        s_fc.append(jnp.concatenate([state_ffn_conv[l][:, 1:], unew[:, None, :]], axis=1))
        if l == n_a - 1:
            kn, vn = _shared_kv(xs, row(kv_norm_g), w_kv_b, tile_row(k_norm_g, N_KV), bd,
                                cos_s, sin_s, dbsz)
    y_sample = xs.reshape(dbsz, 1, D_MODEL)
    s_k = kn.reshape(dbsz, 1, N_KV, HEAD_DIM)
    s_v = vn.reshape(dbsz, 1, N_KV, HEAD_DIM)

    return (y_prompt, y_sample, jnp.stack(p_h), jnp.stack(p_rc), jnp.stack(p_fc), p_k, p_v,
            p_mem_k, p_mem_v, jnp.stack(s_h), jnp.stack(s_rc), jnp.stack(s_fc), s_k, s_v)
```

```python
import functools
import math

import jax
import jax.numpy as jnp
from jax import lax
from jax.experimental import pallas as pl
from jax.experimental.pallas import tpu as pltpu

F32 = jnp.float32
BF16 = jnp.bfloat16

D_MODEL = 1024
HEAD_DIM = 64
MEM_HEADS = 4
MEM_W = MEM_HEADS * HEAD_DIM
N_MEM = 256
D_RNN = D_MODEL - MEM_W
RNN_BLOCKS = D_RNN // HEAD_DIM
CONV_A = 4
LRU_C = 8.0
N_Q = D_RNN // HEAD_DIM
N_KV = 4
GROUP = N_Q // N_KV
Q_W = N_Q * HEAD_DIM
KV_W = N_KV * HEAD_DIM
WINDOW = 128
ROPE_THETA = 10000.0
D_FF = 3 * D_MODEL
CONV_F = 3
EPS = 1e-6
NEG = -1e30
ATT_SCALE = HEAD_DIM ** -0.5
PAST_LEN = 8192

SUBLANES = 8
LANES = 128
MXU_DIM = 256
VMEM_LIMIT_BYTES = 56 * 1024 * 1024

TS_MIX = 512
TS_FFN = 512
TS_KV = 1024
FF_CHUNK = 512
N_FF_CHUNKS = D_FF // FF_CHUNK
DEC_SEQ_BLOCK = 16
CAST_TILE = 1024


def _mm(a, b):
    return jnp.dot(a.astype(BF16), b, preferred_element_type=F32)


def _mm_nt(a, b):
    return lax.dot_general(a.astype(BF16), b, (((1,), (1,)), ((), ())),
                           preferred_element_type=F32)


def _mm_split(a, b):
    hi = a.astype(BF16)
    lo = (a - hi.astype(F32)).astype(BF16)
    return (jnp.dot(hi, b, preferred_element_type=F32)
            + jnp.dot(lo, b, preferred_element_type=F32))


def _rmsnorm(x, g):
    ms = jnp.mean(x * x, axis=-1, keepdims=True)
    return x * lax.rsqrt(ms + EPS) * g


def _head_rmsnorm(x, bd, g, f32_stat=True):
    mean_sq = _mm_split if f32_stat else _mm
    parts = []
    for c in range(x.shape[1] // MXU_DIM):
        xc = x[:, c * MXU_DIM:(c + 1) * MXU_DIM]
        ms = mean_sq(xc * xc, bd)
        parts.append(xc * lax.rsqrt(ms + EPS))
    y = parts[0] if len(parts) == 1 else jnp.concatenate(parts, axis=1)
    return y * g


def _tile_lanes(t, width):
    reps = width // t.shape[1]
    return t if reps == 1 else jnp.concatenate([t] * reps, axis=1)


def _rope(x, cos_t, sin_t):
    w = x.shape[1]
    lane = lax.broadcasted_iota(jnp.int32, x.shape, 1)
    first = (lane % HEAD_DIM) < (HEAD_DIM // 2)
    swapped = jnp.where(first, pltpu.roll(x, w - HEAD_DIM // 2, 1),
                        pltpu.roll(x, HEAD_DIM // 2, 1))
    return x * _tile_lanes(cos_t, w) + swapped * _tile_lanes(sin_t, w)


def _gelu(x):
    c = math.sqrt(2.0 / math.pi)
    return x * (0.5 * (1.0 + jnp.tanh(c * (x + 0.044715 * (x * x * x)))))


def _log_sigmoid(x):
    return jnp.minimum(x, 0.0) - jnp.log1p(jnp.exp(-jnp.abs(x)))


def _slab_stage(x, work_ref, work0, tail_ref, tail0):
    ts = x.shape[0]
    for s in range(x.shape[1] // LANES):
        xs = x[:, s * LANES:(s + 1) * LANES]
        buf = work_ref.at[work0 + s]
        buf[0:SUBLANES, :] = tail_ref[tail0 + s]
        buf[SUBLANES:SUBLANES + ts, :] = xs
        tail_ref[tail0 + s] = xs[ts - SUBLANES:]


def _slab_taps(ts, n_slabs, work_ref, work0, w_ref, b_ref, col0):
    k = w_ref.shape[0]
    outs = []
    for s in range(n_slabs):
        lanes = slice(col0 + s * LANES, col0 + (s + 1) * LANES)
        buf = work_ref.at[work0 + s]
        acc = buf[SUBLANES - (k - 1):SUBLANES - (k - 1) + ts, :] * w_ref[0:1, lanes]
        for j in range(1, k):
            off = SUBLANES - (k - 1 - j)
            acc = acc + buf[off:off + ts, :] * w_ref[j:j + 1, lanes]
        outs.append(acc + b_ref[:, lanes])
    return outs


def _sqrt_pos(x):
    return jnp.where(x > 0.0, x * lax.rsqrt(x), 0.0)


def _lru_coeffs(xr, wg_ref, bgx, bga, logsig):
    xb = xr.astype(BF16)
    gxs, gas = [], []
    for c in range(D_RNN // MXU_DIM):
        gg = jnp.dot(xb[:, c * MXU_DIM:(c + 1) * MXU_DIM], wg_ref[c],
                     preferred_element_type=F32)
        gxs.append(gg[:, :MXU_DIM])
        gas.append(gg[:, MXU_DIM:])
    gx = jax.nn.sigmoid(jnp.concatenate(gxs, axis=1) + bgx)
    ga = jax.nn.sigmoid(jnp.concatenate(gas, axis=1) + bga)
    log_a = ga * (LRU_C * logsig)
    a = jnp.exp(log_a)
    mult = _sqrt_pos(-jnp.tanh(log_a) * (a * a + 1.0))
    return a, mult * gx * xr


def _lru_scan(a, b, a_ref, b_ref, hc_ref):
    ts = a.shape[0]
    outs = []
    for s in range(a.shape[1] // LANES):
        lanes = slice(s * LANES, (s + 1) * LANES)
        a_s, b_s = a[:, lanes], b[:, lanes]
        abuf, bbuf = a_ref.at[s], b_ref.at[s]
        d = 1
        while d < SUBLANES:
            abuf[SUBLANES:SUBLANES + ts, :] = a_s
            bbuf[SUBLANES:SUBLANES + ts, :] = b_s
            b_s = a_s * bbuf[SUBLANES - d:SUBLANES - d + ts, :] + b_s
            a_s = a_s * abuf[SUBLANES - d:SUBLANES - d + ts, :]
            d *= 2
        h = hc_ref[:, lanes]
        hs = []
        for q in range(ts // SUBLANES):
            rows = slice(q * SUBLANES, (q + 1) * SUBLANES)
            h = a_s[rows] * h + b_s[rows]
            hs.append(h)
        hc_ref[:, lanes] = jnp.broadcast_to(h[SUBLANES - 1:], (SUBLANES, LANES))
        outs.append(jnp.concatenate(hs, axis=0))
    return jnp.concatenate(outs, axis=1)


def _head_mask(shape, h):
    lane = lax.broadcasted_iota(jnp.int32, shape, 1)
    return (lane >= h * HEAD_DIM) & (lane < (h + 1) * HEAD_DIM)


def _head_masked_stack(blocks, n_heads):
    parts = []
    for h in range(n_heads):
        for blk in blocks:
            parts.append(jnp.where(_head_mask(blk.shape, h), blk, 0.0).astype(BF16))
    return jnp.concatenate(parts, axis=0)


def _mem_attention(qn, kcat, vcat):
    return _mem_softmax_pv(_mm_nt(qn, kcat), vcat)


def _mem_softmax_pv(s, vcat):
    parts = []
    for h in range(MEM_HEADS):
        sh = s[:, h * N_MEM:(h + 1) * N_MEM]
        m = jnp.max(sh, axis=-1, keepdims=True)
        p = jnp.exp(sh - m)
        den = jnp.sum(p, axis=-1, keepdims=True)
        parts.append((p * (1.0 / den)).astype(BF16))
    return jnp.dot(jnp.concatenate(parts, axis=1), vcat, preferred_element_type=F32)


def _cast_blocks_kernel(w_ref, o_ref):
    for b in range(o_ref.shape[0]):
        o_ref[b] = w_ref[:, b * MXU_DIM:(b + 1) * MXU_DIM].astype(BF16)


def _mem_kv_kernel(mem_ref, g_ref, w_ref, kg_ref, bd_ref, k_ref, v_ref):
    h = _mm(_rmsnorm(mem_ref[...], g_ref[0]), w_ref[0])
    k_ref[0] = _head_rmsnorm(h[:, :MEM_W], bd_ref[...], kg_ref[0])
    v_ref[0] = h[:, MEM_W:]


def _shared_kv_kernel(x_ref, g_ref, w_ref, kg_ref, bd_ref, cos_ref, sin_ref,
                      k_ref, v_ref):
    h = _mm(_rmsnorm(x_ref[...], g_ref[...]), w_ref[...])
    k = _head_rmsnorm(h[:, :KV_W], bd_ref[...], kg_ref[...])
    k_ref[...] = _rope(k, cos_ref[...], sin_ref[...])
    v_ref[...] = h[:, KV_W:]


def _mixer_a_kernel(x_ref, g_ref, win_ref, cw_ref, cb_ref, wg_ref, bgx_ref, bga_ref,
                    lru_ref, mqg_ref, bd_ref, mk_ref, mv_ref, wout_ref, rc0_ref, h0_ref,
                    xo_ref, hlast_ref, rctail_ref,
                    conv_s, tail_s, a_s, b_s, hc_s, kcat_s, vcat_s):
    ts = x_ref.shape[0]
    n_slabs = D_RNN // LANES

    @pl.when(pl.program_id(1) == 0)
    def _():
        for s in range(n_slabs):
            tail_s[s] = rc0_ref[0, :, s * LANES:(s + 1) * LANES]
        a_s[:, 0:SUBLANES, :] = jnp.ones((n_slabs, SUBLANES, LANES), F32)
        b_s[:, 0:SUBLANES, :] = jnp.zeros((n_slabs, SUBLANES, LANES), F32)
        hc_s[...] = jnp.broadcast_to(h0_ref[0], hc_s.shape)
        kcat_s[...] = _head_masked_stack([mk_ref[0]], MEM_HEADS)
        vcat_s[...] = _head_masked_stack([mv_ref[0]], MEM_HEADS)

    x = x_ref[...]
    hn = _rmsnorm(x, g_ref[...]).astype(BF16)

    def in_proj(lo, hi):
        return jnp.dot(hn, win_ref[:, lo:hi], preferred_element_type=F32)

    xr_pre = in_proj(D_RNN, 2 * D_RNN)
    _slab_stage(xr_pre, conv_s, 0, tail_s, 0)
    xr = jnp.concatenate(_slab_taps(ts, n_slabs, conv_s, 0, cw_ref, cb_ref, 0), axis=1)
    rctail_ref[0] = xr_pre[ts - SUBLANES:]
    qm = in_proj(2 * D_RNN, 2 * D_RNN + MEM_W)
    a, b = _lru_coeffs(xr, wg_ref, bgx_ref[...], bga_ref[...], _log_sigmoid(lru_ref[...]))
    qn = _head_rmsnorm(qm, bd_ref[...], mqg_ref[...]) * ATT_SCALE
    gate = in_proj(0, D_RNN)
    mo = _mem_attention(qn, kcat_s[...], vcat_s[...])
    h = _lru_scan(a, b, a_s, b_s, hc_s)
    hlast_ref[0] = h[ts - 1:ts]
    main = h * _gelu(gate)

    y = _mm(jnp.concatenate([main, mo], axis=1), wout_ref[...])
    xo_ref[...] = x + y


def _mixer_b_kernel(sink_ref, x_ref, g_ref, win_ref, qg_ref, cos_ref, sin_ref,
                    kcur_ref, kprev_ref, vcur_ref, vprev_ref,
                    mqg_ref, bd_ref, mk_ref, mv_ref, wout_ref,
                    xo_ref, kcat_s, vcat_s):
    ts = x_ref.shape[0]
    i = pl.program_id(1)

    @pl.when(i == 0)
    def _():
        kcat_s[...] = _head_masked_stack([mk_ref[0]], MEM_HEADS)
        vcat_s[...] = _head_masked_stack([mv_ref[0]], MEM_HEADS)

    x = x_ref[...]
    hn = _rmsnorm(x, g_ref[...]).astype(BF16)

    def in_proj(lo, hi):
        return jnp.dot(hn, win_ref[:, lo:hi], preferred_element_type=F32)

    q = _head_rmsnorm(in_proj(0, Q_W), bd_ref[...], qg_ref[...], f32_stat=False)
    q = (_rope(q, cos_ref[...], sin_ref[...]) * ATT_SCALE).astype(BF16)

    row = lax.broadcasted_iota(jnp.int32, (WINDOW, 2 * WINDOW), 0)
    kj = lax.broadcasted_iota(jnp.int32, (WINDOW, 2 * WINDOW), 1) - WINDOW
    mask_inner = (kj <= row) & (kj >= row - WINDOW)
    mask_first = (kj <= row) & (kj >= jnp.maximum(row - WINDOW, jnp.where(i > 0, -WINDOW, 0)))

    kt = kcur_ref[...]
    vt = vcur_ref[...]

    def block_scores(jb):
        lo, hi = jb * WINDOW, (jb + 1) * WINDOW
        if jb == 0:
            kp, vp = kprev_ref[...], vprev_ref[...]
        else:
            kp, vp = kt[lo - WINDOW:lo], vt[lo - WINDOW:lo]
        kcat = _head_masked_stack([kp, kt[lo:hi]], N_KV)
        vcat = _head_masked_stack([vp, vt[lo:hi]], N_KV)
        qs = jnp.concatenate([q[lo:hi, g * KV_W:(g + 1) * KV_W] for g in range(GROUP)], axis=0)
        s = lax.dot_general(qs, kcat, (((1,), (1,)), ((), ())), preferred_element_type=F32)
        return s, vcat

    n_blocks = ts // WINDOW
    mains = []
    nxt = block_scores(0)
    qm = in_proj(Q_W, Q_W + MEM_W)
    qn = _head_rmsnorm(qm, bd_ref[...], mqg_ref[...], f32_stat=False) * ATT_SCALE
    s_mem = _mm_nt(qn, kcat_s[...])
    for jb in range(n_blocks):
        s, vcat = nxt
        if jb + 1 < n_blocks:
            nxt = block_scores(jb + 1)
        mask = mask_first if jb == 0 else mask_inner
        prow = []
        for g in range(GROUP):
            pseg = []
            for kv in range(N_KV):
                seg = s[g * WINDOW:(g + 1) * WINDOW, kv * 2 * WINDOW:(kv + 1) * 2 * WINDOW]
                seg = jnp.where(mask, seg, NEG)
                sink = sink_ref[g * N_KV + kv]
                m = jnp.maximum(jnp.max(seg, axis=-1, keepdims=True), sink)
                p = jnp.exp(seg - m)
                den = jnp.sum(p, axis=-1, keepdims=True) + jnp.exp(sink - m)
                pseg.append((p * (1.0 / den)).astype(BF16))
            prow.append(jnp.concatenate(pseg, axis=1))
        o = jnp.dot(jnp.concatenate(prow, axis=0), vcat, preferred_element_type=F32)
        mains.append(jnp.concatenate([o[g * WINDOW:(g + 1) * WINDOW] for g in range(GROUP)], axis=1))
    main = mains[0] if len(mains) == 1 else jnp.concatenate(mains, axis=0)

    mo = _mem_softmax_pv(s_mem, vcat_s[...])

    y = _mm(jnp.concatenate([main, mo], axis=1), wout_ref[...])
    xo_ref[...] = x + y


def _ffn_kernel(x_ref, g_ref, wup_ref, cw_ref, cb_ref, wdn_ref,
                xo_ref, utail_ref, conv_s, tail_s, act_s, xres_s, *, tiles_per_seq):
    ts = x_ref.shape[0]
    slabs = FF_CHUNK // LANES
    blk_slabs = MXU_DIM // LANES
    blks = FF_CHUNK // MXU_DIM
    i = pl.program_id(0)
    slot = i % 2
    pslot = 1 - slot

    @pl.when(i == 0)
    def _():
        act_s[1] = jnp.zeros(act_s.shape[1:], act_s.dtype)
        xres_s[1] = jnp.zeros(xres_s.shape[1:], xres_s.dtype)

    @pl.when(i % tiles_per_seq == 0)
    def _():
        tail_s[...] = jnp.zeros_like(tail_s)

    x = x_ref[...]
    xres_s[slot] = x
    hn = _rmsnorm(x, g_ref[...]).astype(BF16)

    def chunk_cols(j):
        return [(c, ((j % 2) * 2 + half) * slabs) for half, c in enumerate((j, N_FF_CHUNKS + j))]

    def up_project(j):
        for c, work0 in chunk_cols(j):
            for b in range(blks):
                blk = c * blks + b
                u = jnp.dot(hn, wup_ref[blk], preferred_element_type=F32)
                _slab_stage(u, conv_s, work0 + b * blk_slabs, tail_s, c * slabs + b * blk_slabs)
                utail_ref[0, :, blk * MXU_DIM:(blk + 1) * MXU_DIM] = u[ts - SUBLANES:]

    def activation(j):
        cg, cv = [_slab_taps(ts, slabs, conv_s, work0, cw_ref, cb_ref, c * FF_CHUNK)
                  for c, work0 in chunk_cols(j)]
        return jnp.concatenate([_gelu(a) * b for a, b in zip(cg, cv)], axis=1).astype(BF16)

    def down_prev(n):
        return jnp.dot(act_s[pslot], wdn_ref[n], preferred_element_type=F32)

    n_dn = wdn_ref.shape[0]
    outs = [None] * n_dn
    outs[0] = down_prev(0)
    up_project(0)
    for j in range(N_FF_CHUNKS):
        if j + 1 < N_FF_CHUNKS:
            up_project(j + 1)
        else:
            outs[1] = down_prev(1)
        act_s[slot, :, j * FF_CHUNK:(j + 1) * FF_CHUNK] = activation(j)
    for n in range(2, n_dn):
        outs[n] = down_prev(n)
    xo_ref[...] = xres_s[pslot] + jnp.concatenate(outs, axis=1)


def _dec_in_a_kernel(x_ref, g_ref, win_ref, cw_ref, cb_ref, wg_ref, bgx_ref, bga_ref,
                     lru_ref, mqg_ref, bd_ref, b0_ref, b1_ref, b2_ref, h0_ref,
                     main_ref, qn_ref, hnew_ref, xrpre_ref):
    u = _mm(_rmsnorm(x_ref[...], g_ref[...]), win_ref[...])
    gate = u[:, :D_RNN]
    xr_pre = u[:, D_RNN:2 * D_RNN]
    qm = u[:, 2 * D_RNN:]
    xr = b0_ref[...] * cw_ref[0:1, :]
    xr = xr + b1_ref[...] * cw_ref[1:2, :]
    xr = xr + b2_ref[...] * cw_ref[2:3, :]
    xr = xr + xr_pre * cw_ref[3:4, :]
    xr = xr + cb_ref[...]
    a, b = _lru_coeffs(xr, wg_ref, bgx_ref[...], bga_ref[...], _log_sigmoid(lru_ref[...]))
    h = a * h0_ref[...] + b
    main_ref[...] = h * _gelu(gate)
    qn_ref[...] = _head_rmsnorm(qm, bd_ref[...], mqg_ref[...]) * ATT_SCALE
    hnew_ref[...] = h
    xrpre_ref[...] = xr_pre


def _dec_in_b_kernel(x_ref, g_ref, win_ref, qg_ref, cos_ref, sin_ref, mqg_ref, bd_ref,
                     q_ref, qn_ref):
    u = _mm(_rmsnorm(x_ref[...], g_ref[...]), win_ref[...])
    q = _head_rmsnorm(u[:, :Q_W], bd_ref[...], qg_ref[...])
    q_ref[...] = _rope(q, cos_ref[...], sin_ref[...]) * ATT_SCALE
    qn_ref[...] = _head_rmsnorm(u[:, Q_W:], bd_ref[...], mqg_ref[...]) * ATT_SCALE


DEC_HEAD_ROWS = 16


def _own_head_lanes(n_heads, width):
    row = lax.broadcasted_iota(jnp.int32, (DEC_HEAD_ROWS, width), 0)
    lane = lax.broadcasted_iota(jnp.int32, (DEC_HEAD_ROWS, width), 1)
    start = (row % (width // HEAD_DIM)) * HEAD_DIM
    return (lane >= start) & (lane < start + HEAD_DIM) & (row < n_heads)


def _dec_mem_attn_kernel(q_ref, kt_ref, vt_ref, o_ref):
    own = _own_head_lanes(MEM_HEADS, MEM_W)
    for s in range(q_ref.shape[0]):
        q_rows = jnp.broadcast_to(q_ref[s:s + 1, :], (DEC_HEAD_ROWS, MEM_W))
        qbd = jnp.where(own, q_rows, 0.0).astype(BF16)
        sc = jnp.dot(qbd, kt_ref[s].astype(BF16), preferred_element_type=F32)
        m = jnp.max(sc, axis=-1, keepdims=True)
        p = jnp.exp(sc - m)
        den = jnp.sum(p, axis=-1, keepdims=True)
        pn = (p * (1.0 / den)).astype(BF16)
        o_all = lax.dot_general(pn, vt_ref[s].astype(BF16), (((1,), (1,)), ((), ())),
                                preferred_element_type=F32)
        o_ref[s:s + 1, :] = jnp.sum(jnp.where(own, o_all, 0.0), axis=0, keepdims=True)


def _dec_swa_kernel(q_ref, kt_ref, vt_ref, kn_ref, vn_ref, sink_ref, o_ref):
    n_heads = GROUP * N_KV
    own = _own_head_lanes(n_heads, KV_W)
    grp = lax.broadcasted_iota(jnp.int32, (DEC_HEAD_ROWS, KV_W), 0) // N_KV
    sink = sink_ref[:, 0:1]
    for s in range(q_ref.shape[0]):
        q_rows = jnp.zeros((DEC_HEAD_ROWS, KV_W), F32)
        for g in range(GROUP):
            qg = jnp.broadcast_to(q_ref[s:s + 1, g * KV_W:(g + 1) * KV_W], (DEC_HEAD_ROWS, KV_W))
            q_rows = jnp.where(grp == g, qg, q_rows)
        qbd = jnp.where(own, q_rows, 0.0)
        s_buf = jnp.dot(qbd.astype(BF16), kt_ref[s].astype(BF16), preferred_element_type=F32)
        s_new = jnp.sum(qbd * kn_ref[s:s + 1, :], axis=-1, keepdims=True)
        m = jnp.maximum(jnp.maximum(jnp.max(s_buf, axis=-1, keepdims=True), s_new), sink)
        p_buf = jnp.exp(s_buf - m)
        p_new = jnp.exp(s_new - m)
        den = jnp.sum(p_buf, axis=-1, keepdims=True) + p_new + jnp.exp(sink - m)
        r = 1.0 / den
        o_all = lax.dot_general((p_buf * r).astype(BF16), vt_ref[s].astype(BF16),
                                (((1,), (1,)), ((), ())), preferred_element_type=F32)
        o_all = jnp.where(own, o_all + (p_new * r) * vn_ref[s:s + 1, :], 0.0)
        o_sum = o_all + pltpu.roll(o_all, 1, 0)
        o_sum = o_sum + pltpu.roll(o_sum, 2, 0)
        for g in range(GROUP):
            last = (g + 1) * N_KV - 1
            o_ref[s:s + 1, g * KV_W:(g + 1) * KV_W] = o_sum[last:last + 1, :]


def _dec_out_ffn_kernel(x_ref, main_ref, mo_ref, wout_ref, g_ref,
                        wug_ref, wuv_ref, cwg_ref, cwv_ref, cbg_ref, cbv_ref, wdn_ref,
                        sg0_ref, sg1_ref, sv0_ref, sv1_ref,
                        xo_ref, ug_ref, uv_ref,
                        xmid_s, hn_s, acc_s):
    j = pl.program_id(0)

    @pl.when(j == 0)
    def _():
        y = _mm(jnp.concatenate([main_ref[...], mo_ref[...]], axis=1), wout_ref[...])
        xmid = x_ref[...] + y
        xmid_s[...] = xmid
        hn_s[...] = _rmsnorm(xmid, g_ref[...]).astype(BF16)
        acc_s[...] = jnp.zeros_like(acc_s)

    hn = hn_s[...]
    up = lambda w_ref: jnp.concatenate(
        [jnp.dot(hn, w_ref[b], preferred_element_type=F32) for b in range(w_ref.shape[0])], axis=1)
    ug = up(wug_ref)
    uv = up(wuv_ref)
    ug_ref[...] = ug
    uv_ref[...] = uv
    cg = (sg0_ref[...] * cwg_ref[0:1, :] + sg1_ref[...] * cwg_ref[1:2, :]
          + ug * cwg_ref[2:3, :] + cbg_ref[...])
    cv = (sv0_ref[...] * cwv_ref[0:1, :] + sv1_ref[...] * cwv_ref[1:2, :]
          + uv * cwv_ref[2:3, :] + cbv_ref[...])
    act = (_gelu(cg) * cv).astype(BF16)
    for n in range(wdn_ref.shape[0]):
        acc_s[:, n * MXU_DIM:(n + 1) * MXU_DIM] += jnp.dot(act, wdn_ref[n],
                                                           preferred_element_type=F32)

    @pl.when(j == pl.num_programs(0) - 1)
    def _():
        xo_ref[...] = xmid_s[...] + acc_s[...]


def _const_spec(shape):
    nd = len(shape)
    return pl.BlockSpec(shape, lambda *_: (0,) * nd)


def _layer_spec(arr, layer):
    nd = arr.ndim - 1
    return pl.BlockSpec((None,) + arr.shape[1:], lambda *_: (layer,) + (0,) * nd)


def _resident_spec(op):
    return _layer_spec(*op) if isinstance(op, tuple) else _const_spec(op.shape)


def _operand(op):
    return op[0] if isinstance(op, tuple) else op


def _params(*sem):
    return pltpu.CompilerParams(dimension_semantics=sem, vmem_limit_bytes=VMEM_LIMIT_BYTES)


def _cast_blocks(w):
    layers, r, c = w.shape
    tile = CAST_TILE
    per = tile // MXU_DIM
    return pl.pallas_call(
        _cast_blocks_kernel,
        grid=(layers, r // tile, c // tile),
        in_specs=[pl.BlockSpec((None, tile, tile), lambda l, i, j: (l, i, j))],
        out_specs=pl.BlockSpec((None, per, tile, MXU_DIM), lambda l, i, j: (l, j, i, 0)),
        out_shape=jax.ShapeDtypeStruct((layers, c // MXU_DIM, r, MXU_DIM), BF16),
        compiler_params=_params("arbitrary", "arbitrary", "arbitrary"),
        name="cast_blocks",
    )(w)


def _mem_kv(mem2d, g, w, kg, bd):
    depth = w.shape[0]
    rows = mem2d.shape[0]
    out = jax.ShapeDtypeStruct((depth, rows, MEM_W), F32)
    return pl.pallas_call(
        _mem_kv_kernel,
        grid=(depth,),
        in_specs=[
            _const_spec(mem2d.shape),
            pl.BlockSpec((1, 1, D_MODEL), lambda l: (l, 0, 0)),
            pl.BlockSpec((1, D_MODEL, 2 * MEM_W), lambda l: (l, 0, 0)),
            pl.BlockSpec((1, 1, MEM_W), lambda l: (l, 0, 0)),
            _const_spec(bd.shape),
        ],
        out_specs=[pl.BlockSpec((1, rows, MEM_W), lambda l: (l, 0, 0))] * 2,
        out_shape=[out, out],
        compiler_params=_params("arbitrary"),
        name="mem_kv",
    )(mem2d, g, w, kg, bd)


def _shared_kv(x2d, g, w, kg, bd, cos_t, sin_t, ts):
    rows = x2d.shape[0]
    tab_blocks = cos_t.shape[0] // ts
    out = jax.ShapeDtypeStruct((rows, KV_W), F32)
    return pl.pallas_call(
        _shared_kv_kernel,
        grid=(rows // ts,),
        in_specs=[
            pl.BlockSpec((ts, D_MODEL), lambda i: (i, 0)),
            _const_spec(g.shape), _const_spec(w.shape), _const_spec(kg.shape),
            _const_spec(bd.shape),
            pl.BlockSpec((ts, LANES), lambda i: (i % tab_blocks, 0)),
            pl.BlockSpec((ts, LANES), lambda i: (i % tab_blocks, 0)),
        ],
        out_specs=[pl.BlockSpec((ts, KV_W), lambda i: (i, 0))] * 2,
        out_shape=[out, out],
        compiler_params=_params("arbitrary"),
        name="shared_kv",
    )(x2d, g, w, kg, bd, cos_t, sin_t)


def _mixer_a(x2d, bsz, g, win, cw, cb, wg, bgx, bga, lru, mqg, bd, mk, mv, wout, rc0, h0):
    rows = x2d.shape[0]
    ts = TS_MIX
    nt = rows // bsz // ts
    consts = (g, win, cw, cb, wg, bgx, bga, lru, mqg, bd)
    per_b3 = lambda b, i: (b, 0, 0)
    return pl.pallas_call(
        _mixer_a_kernel,
        grid=(bsz, nt),
        in_specs=[pl.BlockSpec((ts, D_MODEL), lambda b, i: (b * nt + i, 0))]
        + [_resident_spec(c) for c in consts]
        + [pl.BlockSpec((1, N_MEM, MEM_W), per_b3)] * 2
        + [_resident_spec(wout),
           pl.BlockSpec((1, SUBLANES, D_RNN), per_b3),
           pl.BlockSpec((1, 1, D_RNN), per_b3)],
        out_specs=[
            pl.BlockSpec((ts, D_MODEL), lambda b, i: (b * nt + i, 0)),
            pl.BlockSpec((1, 1, D_RNN), per_b3),
            pl.BlockSpec((1, SUBLANES, D_RNN), per_b3),
        ],
        out_shape=[
            jax.ShapeDtypeStruct((rows, D_MODEL), F32),
            jax.ShapeDtypeStruct((bsz, 1, D_RNN), F32),
            jax.ShapeDtypeStruct((bsz, SUBLANES, D_RNN), F32),
        ],
        scratch_shapes=[
            pltpu.VMEM((D_RNN // LANES, SUBLANES + ts, LANES), F32),
            pltpu.VMEM((D_RNN // LANES, SUBLANES, LANES), F32),
            pltpu.VMEM((D_RNN // LANES, SUBLANES + ts, LANES), F32),
            pltpu.VMEM((D_RNN // LANES, SUBLANES + ts, LANES), F32),
            pltpu.VMEM((SUBLANES, D_RNN), F32),
            pltpu.VMEM((MEM_HEADS * N_MEM, MEM_W), BF16),
            pltpu.VMEM((MEM_HEADS * N_MEM, MEM_W), BF16),
        ],
        compiler_params=_params("arbitrary", "arbitrary"),
        name="mixer_a",
    )(x2d, *[_operand(c) for c in consts], mk, mv, _operand(wout), rc0, h0)


def _mixer_b(x2d, bsz, sink_tab, g, win, qg, cos_t, sin_t, ksh, vsh, mqg, bd, mk, mv, wout):
    rows = x2d.shape[0]
    ts = TS_MIX
    nt = rows // bsz // ts
    wpt = ts // WINDOW
    cur = lambda b, i: (b * nt + i, 0)
    prev = lambda b, i: (jnp.maximum((b * nt + i) * wpt - 1, 0), 0)
    table = lambda b, i: (i, 0)
    per_b3 = lambda b, i: (b, 0, 0)
    return pl.pallas_call(
        _mixer_b_kernel,
        grid=(bsz, nt),
        in_specs=[
            pl.BlockSpec(memory_space=pltpu.SMEM),
            pl.BlockSpec((ts, D_MODEL), cur),
            _const_spec(g.shape), _resident_spec(win), _const_spec(qg.shape),
            pl.BlockSpec((ts, LANES), table),
            pl.BlockSpec((ts, LANES), table),
            pl.BlockSpec((ts, KV_W), cur), pl.BlockSpec((WINDOW, KV_W), prev),
            pl.BlockSpec((ts, KV_W), cur), pl.BlockSpec((WINDOW, KV_W), prev),
            _const_spec(mqg.shape), _const_spec(bd.shape),
            pl.BlockSpec((1, N_MEM, MEM_W), per_b3), pl.BlockSpec((1, N_MEM, MEM_W), per_b3),
            _resident_spec(wout),
        ],
        out_specs=pl.BlockSpec((ts, D_MODEL), cur),
        out_shape=jax.ShapeDtypeStruct((rows, D_MODEL), F32),
        scratch_shapes=[
            pltpu.VMEM((MEM_HEADS * N_MEM, MEM_W), BF16),
            pltpu.VMEM((MEM_HEADS * N_MEM, MEM_W), BF16),
        ],
        compiler_params=_params("arbitrary", "arbitrary"),
        name="mixer_b",
    )(sink_tab, x2d, g, _operand(win), qg, cos_t, sin_t, ksh, ksh, vsh, vsh, mqg, bd, mk, mv,
      _operand(wout))


def _ffn(x2d, bsz, layer, g, wup, cw, cb, wdn):
    rows = x2d.shape[0]
    ts = TS_FFN
    nt = rows // ts
    tiles_per_seq = rows // bsz // ts
    last = nt - 1
    return pl.pallas_call(
        functools.partial(_ffn_kernel, tiles_per_seq=tiles_per_seq),
        grid=(nt + 1,),
        in_specs=[pl.BlockSpec((ts, D_MODEL), lambda i: (jnp.minimum(i, last), 0)),
                  _const_spec(g.shape), _layer_spec(wup, layer), _const_spec(cw.shape),
                  _const_spec(cb.shape), _layer_spec(wdn, layer)],
        out_specs=[
            pl.BlockSpec((ts, D_MODEL), lambda i: (jnp.maximum(i - 1, 0), 0)),
            pl.BlockSpec((1, SUBLANES, 2 * D_FF),
                         lambda i: (jnp.minimum(i, last) // tiles_per_seq, 0, 0)),
        ],
        out_shape=[
            jax.ShapeDtypeStruct((rows, D_MODEL), F32),
            jax.ShapeDtypeStruct((bsz, SUBLANES, 2 * D_FF), F32),
        ],
        scratch_shapes=[pltpu.VMEM((4 * FF_CHUNK // LANES, SUBLANES + ts, LANES), F32),
                        pltpu.VMEM((2 * D_FF // LANES, SUBLANES, LANES), F32),
                        pltpu.VMEM((2, ts, D_FF), BF16),
                        pltpu.VMEM((2, ts, D_MODEL), F32)],
        compiler_params=_params("arbitrary"),
        name="ffn",
    )(x2d, g, wup, cw, cb, wdn)


def _dec_in_a(x, g, win, cw, cb, wg, bgx, bga, lru, mqg, bd, rc, h0):
    n = x.shape[0]
    consts = (x, g, win, cw, cb, wg, bgx, bga, lru, mqg, bd)
    buf = lambda j: pl.BlockSpec((None, n, D_RNN), lambda i: (j, 0, 0))
    return pl.pallas_call(
        _dec_in_a_kernel,
        grid=(1,),
        in_specs=[_resident_spec(c) for c in consts]
        + [buf(0), buf(1), buf(2), _const_spec(h0.shape)],
        out_specs=[_const_spec((n, D_RNN)), _const_spec((n, MEM_W)),
                   _const_spec((n, D_RNN)), _const_spec((n, D_RNN))],
        out_shape=[jax.ShapeDtypeStruct((n, D_RNN), F32), jax.ShapeDtypeStruct((n, MEM_W), F32),
                   jax.ShapeDtypeStruct((n, D_RNN), F32), jax.ShapeDtypeStruct((n, D_RNN), F32)],
        compiler_params=_params("arbitrary"),
        name="dec_in_a",
    )(*[_operand(c) for c in consts], rc, rc, rc, h0)


def _dec_in_b(x, g, win, qg, cos_t, sin_t, mqg, bd):
    n = x.shape[0]
    args = (x, g, win, qg, cos_t, sin_t, mqg, bd)
    return pl.pallas_call(
        _dec_in_b_kernel,
        grid=(1,),
        in_specs=[_resident_spec(a) for a in args],
        out_specs=[_const_spec((n, Q_W)), _const_spec((n, MEM_W))],
        out_shape=[jax.ShapeDtypeStruct((n, Q_W), F32), jax.ShapeDtypeStruct((n, MEM_W), F32)],
        compiler_params=_params("arbitrary"),
        name="dec_in_b",
    )(*[_operand(a) for a in args])


def _dec_mem_attn(qn, ckt, cvt, layer):
    n = qn.shape[0]
    sb = DEC_SEQ_BLOCK
    return pl.pallas_call(
        _dec_mem_attn_kernel,
        grid=(n // sb,),
        in_specs=[pl.BlockSpec((sb, MEM_W), lambda i: (i, 0)),
                  pl.BlockSpec((None, sb, MEM_W, N_MEM), lambda i: (layer, i, 0, 0)),
                  pl.BlockSpec((None, sb, MEM_W, N_MEM), lambda i: (layer, i, 0, 0))],
        out_specs=pl.BlockSpec((sb, MEM_W), lambda i: (i, 0)),
        out_shape=jax.ShapeDtypeStruct((n, MEM_W), F32),
        compiler_params=_params("arbitrary"),
        name="dec_mem_attn",
    )(qn, ckt, cvt)


def _dec_swa(q, kbt, vbt, kn, vn, sink_rows):
    n = q.shape[0]
    wb = kbt.shape[2]
    sb = DEC_SEQ_BLOCK
    assert wb <= WINDOW
    return pl.pallas_call(
        _dec_swa_kernel,
        grid=(n // sb,),
        in_specs=[pl.BlockSpec((sb, Q_W), lambda i: (i, 0)),
                  pl.BlockSpec((sb, KV_W, wb), lambda i: (i, 0, 0)),
                  pl.BlockSpec((sb, KV_W, wb), lambda i: (i, 0, 0)),
                  pl.BlockSpec((sb, KV_W), lambda i: (i, 0)),
                  pl.BlockSpec((sb, KV_W), lambda i: (i, 0)),
                  _const_spec(sink_rows.shape)],
        out_specs=pl.BlockSpec((sb, Q_W), lambda i: (i, 0)),
        out_shape=jax.ShapeDtypeStruct((n, Q_W), F32),
        compiler_params=_params("arbitrary"),
        name="dec_swa",
    )(q, kbt, vbt, kn, vn, sink_rows)


def _dec_out_ffn(x, main, mo, wout, wout_layer, layer, g, wup, cw, cb, wdn, st):
    n = x.shape[0]
    nch = N_FF_CHUNKS
    blks = FF_CHUNK // MXU_DIM
    lo = lambda j: (0, j)
    hi = lambda j: (0, nch + j)
    state = lambda r, off: pl.BlockSpec((None, n, FF_CHUNK), lambda j: (r, 0, off + j))
    up_blocks = lambda off: pl.BlockSpec((None, blks, D_MODEL, MXU_DIM),
                                         lambda j: (layer, off + j, 0, 0))
    return pl.pallas_call(
        _dec_out_ffn_kernel,
        grid=(nch,),
        in_specs=[_const_spec(x.shape), _const_spec(main.shape), _const_spec(mo.shape),
                  _layer_spec(wout, wout_layer), _const_spec(g.shape),
                  up_blocks(0), up_blocks(nch),
                  pl.BlockSpec((CONV_F, FF_CHUNK), lo), pl.BlockSpec((CONV_F, FF_CHUNK), hi),
                  pl.BlockSpec((1, FF_CHUNK), lo), pl.BlockSpec((1, FF_CHUNK), hi),
                  pl.BlockSpec((None, wdn.shape[1], FF_CHUNK, MXU_DIM), lambda j: (layer, 0, j, 0)),
                  state(0, 0), state(1, 0), state(0, nch), state(1, nch)],
        out_specs=[_const_spec((n, D_MODEL)),
                   pl.BlockSpec((n, FF_CHUNK), lambda j: (0, j)),
                   pl.BlockSpec((n, FF_CHUNK), lambda j: (0, j))],
        out_shape=[jax.ShapeDtypeStruct((n, D_MODEL), F32),
                   jax.ShapeDtypeStruct((n, D_FF), F32),
                   jax.ShapeDtypeStruct((n, D_FF), F32)],
        scratch_shapes=[pltpu.VMEM((n, D_MODEL), F32), pltpu.VMEM((n, D_MODEL), BF16),
                        pltpu.VMEM((n, D_MODEL), F32)],
        compiler_params=_params("arbitrary"),
        name="dec_out_ffn",
    )(x, main, mo, wout, g, wup, wup, cw, cw, cb, cb, wdn, st, st, st, st)


def _rope_tables(pos):
    half = HEAD_DIM // 2
    inv = ROPE_THETA ** (-jnp.arange(half, dtype=F32) / half)
    ang = pos.astype(F32)[:, None] * inv[None, :]
    cos = jnp.cos(ang)
    sin = jnp.sin(ang)
    reps = LANES // HEAD_DIM
    cos_t = jnp.tile(jnp.concatenate([cos, cos], axis=1), (1, reps))
    sin_t = jnp.tile(jnp.concatenate([-sin, sin], axis=1), (1, reps))
    return cos_t, sin_t


def _block_diag_gates(wx, wa):
    per = MXU_DIM // HEAD_DIM
    eye = jnp.eye(per, dtype=F32)

    def bd(w):
        w4 = w.reshape(RNN_BLOCKS // per, per, HEAD_DIM, HEAD_DIM)
        return jnp.einsum('ckij,kK->ckiKj', w4, eye).reshape(RNN_BLOCKS // per, MXU_DIM, MXU_DIM)

    return jnp.concatenate([bd(wx), bd(wa)], axis=2).astype(BF16)


def kernel(x_prompt, x_sample, state_rglru_h, state_rglru_conv, state_ffn_conv, cache_swa_k, cache_swa_v, cache_mem_k, cache_mem_v, mem_prompt, norm_mix_g, norm_ffn_g, w_in_a, rnn_conv_w, rnn_conv_b, w_gate_x, b_gate_x, w_gate_a, b_gate_a, lru_param, w_in_b, q_norm_g, sinks, kv_norm_g, w_kv, k_norm_g, mem_norm_g, w_mem_kv, mem_q_norm_g, mem_k_norm_g, w_out, w_ffn_up, ffn_conv_w, ffn_conv_b, w_ffn_down):
    bsz, seq, _ = x_prompt.shape
    dbsz = x_sample.shape[0]
    depth = norm_mix_g.shape[0]
    n_a = w_in_a.shape[0]
    assert x_sample.shape[1] == 1
    assert seq % TS_MIX == 0 and seq % TS_FFN == 0 and seq % TS_KV == 0 and TS_MIX % WINDOW == 0

    n_b = w_in_b.shape[0]
    wq = w_in_b[:, :, :Q_W].astype(BF16).reshape(n_b, D_MODEL, N_KV, GROUP, HEAD_DIM)
    wq = wq.transpose(0, 1, 3, 2, 4).reshape(n_b, D_MODEL, Q_W)
    wo_main = w_out[n_a:, :Q_W].astype(BF16).reshape(n_b, N_KV, GROUP, HEAD_DIM, D_MODEL)
    wo_main = wo_main.transpose(0, 2, 1, 3, 4).reshape(n_b, Q_W, D_MODEL)
    bd =(jnp.kron(jnp.eye(MXU_DIM // HEAD_DIM, dtype=F32),
                   jnp.ones((HEAD_DIM, HEAD_DIM), F32)) / HEAD_DIM).astype(BF16)

    row = lambda v: v.reshape(1, -1)
    tile_row = lambda v, n: jnp.tile(v, n).reshape(1, -1)
    w_in_a_b = w_in_a.astype(BF16)
    w_in_b_b = jnp.concatenate([wq, w_in_b[:, :, Q_W:].astype(BF16)], axis=2)
    w_out_b = w_out.astype(BF16)
    w_out_perm_b = jnp.concatenate([wo_main, w_out_b[n_a:, Q_W:]], axis=1)
    w_kv_b = w_kv.astype(BF16)
    w_mem_b = w_mem_kv.astype(BF16)
    wup_b = _cast_blocks(w_ffn_up)
    wdn_b = _cast_blocks(w_ffn_down)
    fcw = ffn_conv_w
    fcb = ffn_conv_b.reshape(depth, 1, 2 * D_FF)
    wg_b = jnp.stack([_block_diag_gates(w_gate_x[l], w_gate_a[l]) for l in range(n_a)])
    sink_gk = sinks.reshape(-1, N_KV, GROUP).transpose(0, 2, 1)

    cos_p, sin_p = _rope_tables(jnp.arange(seq, dtype=jnp.int32))
    pos_s = PAST_LEN + jnp.zeros((dbsz,), jnp.int32)
    cos_s, sin_s = _rope_tables(pos_s)

    mem2d = mem_prompt.reshape(bsz * N_MEM, D_MODEL)
    pmk, pmv = _mem_kv(mem2d, mem_norm_g.reshape(depth, 1, D_MODEL), w_mem_b,
                       jnp.tile(mem_k_norm_g, (1, MEM_HEADS)).reshape(depth, 1, MEM_W), bd)
    pmk4 = pmk.reshape(depth, bsz, N_MEM, MEM_W)
    pmv4 = pmv.reshape(depth, bsz, N_MEM, MEM_W)

    x = x_prompt.reshape(bsz * seq, D_MODEL)
    zeros_rc = jnp.zeros((bsz, SUBLANES, D_RNN), F32)
    zeros_h = jnp.zeros((bsz, 1, D_RNN), F32)
    p_h, p_rc, p_fc = [], [], []
    ksh = vsh = None
    for l in range(depth):
        mqg = tile_row(mem_q_norm_g[l], MEM_HEADS)
        if l < n_a:
            x, hl, rct = _mixer_a(
                x, bsz, row(norm_mix_g[l]), (w_in_a_b, l), rnn_conv_w[l], row(rnn_conv_b[l]),
                (wg_b, l), row(b_gate_x[l]), row(b_gate_a[l]), row(lru_param[l]), mqg, bd,
                pmk4[l], pmv4[l], (w_out_b, l), zeros_rc, zeros_h)
            p_h.append(hl.reshape(bsz, D_RNN))
            p_rc.append(rct[:, SUBLANES - (CONV_A - 1):])
        else:
            j = l - n_a
            x = _mixer_b(
                x, bsz, sink_gk[j].reshape(-1), row(norm_mix_g[l]), (w_in_b_b, j),
                tile_row(q_norm_g[j], N_Q), cos_p, sin_p, ksh, vsh, mqg, bd,
                pmk4[l], pmv4[l], (w_out_perm_b, j))
        x, ut = _ffn(x, bsz, l, row(norm_ffn_g[l]), wup_b, fcw[l], fcb[l], wdn_b)
        p_fc.append(ut[:, SUBLANES - (CONV_F - 1):])
        if l == n_a - 1:
            ksh, vsh = _shared_kv(x, row(kv_norm_g), w_kv_b, tile_row(k_norm_g, N_KV), bd,
                                  cos_p, sin_p, TS_KV)
    y_prompt = x.reshape(bsz, seq, D_MODEL)
    keep = min(WINDOW, seq)
    p_k = ksh.reshape(bsz, seq, KV_W)[:, seq - keep:].reshape(bsz, keep, N_KV, HEAD_DIM)
    p_v = vsh.reshape(bsz, seq, KV_W)[:, seq - keep:].reshape(bsz, keep, N_KV, HEAD_DIM)
    p_mem_k = pmk.reshape(depth, bsz, N_MEM, MEM_HEADS, HEAD_DIM)
    p_mem_v = pmv.reshape(depth, bsz, N_MEM, MEM_HEADS, HEAD_DIM)

    xs = x_sample.reshape(dbsz, D_MODEL)
    cmk = cache_mem_k.transpose(0, 1, 3, 4, 2).reshape(depth, dbsz, MEM_W, N_MEM)
    cmv = cache_mem_v.transpose(0, 1, 3, 4, 2).reshape(depth, dbsz, MEM_W, N_MEM)
    wb = cache_swa_k.shape[1]
    ckb = cache_swa_k.transpose(0, 2, 3, 1).reshape(dbsz, KV_W, wb)
    cvb = cache_swa_v.transpose(0, 2, 3, 1).reshape(dbsz, KV_W, wb)
    s_h, s_rc, s_fc = [], [], []
    kn = vn = None
    for l in range(depth):
        mqg = tile_row(mem_q_norm_g[l], MEM_HEADS)
        if l < n_a:
            main, qn, hnew, xrpre = _dec_in_a(
                xs, row(norm_mix_g[l]), (w_in_a_b, l), rnn_conv_w[l], row(rnn_conv_b[l]),
                (wg_b, l), row(b_gate_x[l]), row(b_gate_a[l]), row(lru_param[l]), mqg, bd,
                state_rglru_conv[l].transpose(1, 0, 2), state_rglru_h[l])
            s_h.append(hnew)
            s_rc.append(jnp.concatenate([state_rglru_conv[l][:, 1:], xrpre[:, None, :]], axis=1))
            wo, wo_layer = w_out_b, l
        else:
            j = l - n_a
            q, qn = _dec_in_b(xs, row(norm_mix_g[l]), (w_in_b_b, j), tile_row(q_norm_g[j], N_Q),
                              cos_s, sin_s, mqg, bd)
            sink_rows = jnp.zeros((DEC_HEAD_ROWS, LANES), F32).at[:N_Q].set(
                jnp.broadcast_to(sink_gk[j].reshape(N_Q, 1), (N_Q, LANES)))
            main = _dec_swa(q, ckb, cvb, kn, vn, sink_rows)
            wo, wo_layer = w_out_perm_b, j
        mo = _dec_mem_attn(qn, cmk, cmv, l)
        xs, ug, uv = _dec_out_ffn(xs, main, mo, wo, wo_layer, l, row(norm_ffn_g[l]), wup_b,
                                  fcw[l], fcb[l], wdn_b, state_ffn_conv[l].transpose(1, 0, 2))
        unew = jnp.concatenate([ug, uv], axis=1)
        s_fc.append(jnp.concatenate([state_ffn_conv[l][:, 1:], unew[:, None, :]], axis=1))
        if l == n_a - 1:
            kn, vn = _shared_kv(xs, row(kv_norm_g), w_kv_b, tile_row(k_norm_g, N_KV), bd,
                                cos_s, sin_s, dbsz)
    y_sample = xs.reshape(dbsz, 1, D_MODEL)
    s_k = kn.reshape(dbsz, 1, N_KV, HEAD_DIM)
    s_v = vn.reshape(dbsz, 1, N_KV, HEAD_DIM)

    return (y_prompt, y_sample, jnp.stack(p_h), jnp.stack(p_rc), jnp.stack(p_fc), p_k, p_v,
            p_mem_k, p_mem_v, jnp.stack(s_h), jnp.stack(s_rc), jnp.stack(s_fc), s_k, s_v)
```

```python
import functools
import math

import jax
import jax.numpy as jnp
from jax import lax
from jax.experimental import pallas as pl
from jax.experimental.pallas import tpu as pltpu

F32 = jnp.float32
BF16 = jnp.bfloat16

D_MODEL = 1024
HEAD_DIM = 64
MEM_HEADS = 4
MEM_W = MEM_HEADS * HEAD_DIM
N_MEM = 256
D_RNN = D_MODEL - MEM_W
RNN_BLOCKS = D_RNN // HEAD_DIM
CONV_A = 4
LRU_C = 8.0
N_Q = D_RNN // HEAD_DIM
N_KV = 4
GROUP = N_Q // N_KV
Q_W = N_Q * HEAD_DIM
KV_W = N_KV * HEAD_DIM
WINDOW = 128
ROPE_THETA = 10000.0
D_FF = 3 * D_MODEL
CONV_F = 3
EPS = 1e-6
NEG = -1e30
ATT_SCALE = HEAD_DIM ** -0.5
PAST_LEN = 8192

SUBLANES = 8
LANES = 128
MXU_DIM = 256
VMEM_LIMIT_BYTES = 56 * 1024 * 1024

TS_MIX = 512
TS_FFN = 512
TS_KV = 1024
FF_CHUNK = 512
N_FF_CHUNKS = D_FF // FF_CHUNK
DEC_SEQ_BLOCK = 16
CAST_TILE = 1024


def _mm(a, b):
    return jnp.dot(a.astype(BF16), b, preferred_element_type=F32)


def _mm_nt(a, b):
    return lax.dot_general(a.astype(BF16), b, (((1,), (1,)), ((), ())),
                           preferred_element_type=F32)


def _mm_split(a, b):
    hi = a.astype(BF16)
    lo = (a - hi.astype(F32)).astype(BF16)
    return (jnp.dot(hi, b, preferred_element_type=F32)
            + jnp.dot(lo, b, preferred_element_type=F32))


def _rmsnorm(x, g):
    ms = jnp.mean(x * x, axis=-1, keepdims=True)
    return x * lax.rsqrt(ms + EPS) * g


def _head_rmsnorm(x, bd, g, f32_stat=True):
    mean_sq = _mm_split if f32_stat else _mm
    parts = []
    for c in range(x.shape[1] // MXU_DIM):
        xc = x[:, c * MXU_DIM:(c + 1) * MXU_DIM]
        ms = mean_sq(xc * xc, bd)
        parts.append(xc * lax.rsqrt(ms + EPS))
    y = parts[0] if len(parts) == 1 else jnp.concatenate(parts, axis=1)
    return y * g


def _tile_lanes(t, width):
    reps = width // t.shape[1]
    return t if reps == 1 else jnp.concatenate([t] * reps, axis=1)


def _rope(x, cos_t, sin_t):
    w = x.shape[1]
    lane = lax.broadcasted_iota(jnp.int32, x.shape, 1)
    first = (lane % HEAD_DIM) < (HEAD_DIM // 2)
    swapped = jnp.where(first, pltpu.roll(x, w - HEAD_DIM // 2, 1),
                        pltpu.roll(x, HEAD_DIM // 2, 1))
    return x * _tile_lanes(cos_t, w) + swapped * _tile_lanes(sin_t, w)


def _gelu(x):
    c = math.sqrt(2.0 / math.pi)
    return x * (0.5 * (1.0 + jnp.tanh(c * (x + 0.044715 * (x * x * x)))))


def _log_sigmoid(x):
    return jnp.minimum(x, 0.0) - jnp.log1p(jnp.exp(-jnp.abs(x)))


def _slab_stage(x, work_ref, work0, tail_ref, tail0):
    ts = x.shape[0]
    for s in range(x.shape[1] // LANES):
        xs = x[:, s * LANES:(s + 1) * LANES]
        buf = work_ref.at[work0 + s]
        buf[0:SUBLANES, :] = tail_ref[tail0 + s]
        buf[SUBLANES:SUBLANES + ts, :] = xs
        tail_ref[tail0 + s] = xs[ts - SUBLANES:]


def _slab_taps(ts, n_slabs, work_ref, work0, w_ref, b_ref, col0):
    k = w_ref.shape[0]
    outs = []
    for s in range(n_slabs):
        lanes = slice(col0 + s * LANES, col0 + (s + 1) * LANES)
        buf = work_ref.at[work0 + s]
        acc = buf[SUBLANES - (k - 1):SUBLANES - (k - 1) + ts, :] * w_ref[0:1, lanes]
        for j in range(1, k):
            off = SUBLANES - (k - 1 - j)
            acc = acc + buf[off:off + ts, :] * w_ref[j:j + 1, lanes]
        outs.append(acc + b_ref[:, lanes])
    return outs


def _sqrt_pos(x):
    return jnp.where(x > 0.0, x * lax.rsqrt(x), 0.0)


def _lru_coeffs(xr, wg_ref, bgx, bga, logsig):
    xb = xr.astype(BF16)
    gxs, gas = [], []
    for c in range(D_RNN // MXU_DIM):
        gg = jnp.dot(xb[:, c * MXU_DIM:(c + 1) * MXU_DIM], wg_ref[c],
                     preferred_element_type=F32)
        gxs.append(gg[:, :MXU_DIM])
        gas.append(gg[:, MXU_DIM:])
    gx = jax.nn.sigmoid(jnp.concatenate(gxs, axis=1) + bgx)
    ga = jax.nn.sigmoid(jnp.concatenate(gas, axis=1) + bga)
    log_a = ga * (LRU_C * logsig)
    a = jnp.exp(log_a)
    mult = _sqrt_pos(-jnp.tanh(log_a) * (a * a + 1.0))
    return a, mult * gx * xr


def _lru_scan(a, b, a_ref, b_ref, hc_ref):
    ts = a.shape[0]
    outs = []
    for s in range(a.shape[1] // LANES):
        lanes = slice(s * LANES, (s + 1) * LANES)
        a_s, b_s = a[:, lanes], b[:, lanes]
        abuf, bbuf = a_ref.at[s], b_ref.at[s]
        d = 1
        while d < SUBLANES:
            abuf[SUBLANES:SUBLANES + ts, :] = a_s
            bbuf[SUBLANES:SUBLANES + ts, :] = b_s
            b_s = a_s * bbuf[SUBLANES - d:SUBLANES - d + ts, :] + b_s
            a_s = a_s * abuf[SUBLANES - d:SUBLANES - d + ts, :]
            d *= 2
        h = hc_ref[:, lanes]
        hs = []
        for q in range(ts // SUBLANES):
            rows = slice(q * SUBLANES, (q + 1) * SUBLANES)
            h = a_s[rows] * h + b_s[rows]
            hs.append(h)
        hc_ref[:, lanes] = jnp.broadcast_to(h[SUBLANES - 1:], (SUBLANES, LANES))
        outs.append(jnp.concatenate(hs, axis=0))
    return jnp.concatenate(outs, axis=1)


def _head_mask(shape, h):
    lane = lax.broadcasted_iota(jnp.int32, shape, 1)
    return (lane >= h * HEAD_DIM) & (lane < (h + 1) * HEAD_DIM)


def _head_masked_stack(blocks, n_heads):
    parts = []
    for h in range(n_heads):
        for blk in blocks:
            parts.append(jnp.where(_head_mask(blk.shape, h), blk, 0.0).astype(BF16))
    return jnp.concatenate(parts, axis=0)


def _mem_attention(qn, kcat, vcat):
    return _mem_softmax_pv(_mm_nt(qn, kcat), vcat)


def _mem_softmax_pv(s, vcat):
    parts = []
    for h in range(MEM_HEADS):
        sh = s[:, h * N_MEM:(h + 1) * N_MEM]
        m = jnp.max(sh, axis=-1, keepdims=True)
        p = jnp.exp(sh - m)
        den = jnp.sum(p, axis=-1, keepdims=True)
        parts.append((p * (1.0 / den)).astype(BF16))
    return jnp.dot(jnp.concatenate(parts, axis=1), vcat, preferred_element_type=F32)


def _cast_blocks_kernel(w_ref, o_ref):
    for b in range(o_ref.shape[0]):
        o_ref[b] = w_ref[:, b * MXU_DIM:(b + 1) * MXU_DIM].astype(BF16)


def _mem_kv_kernel(mem_ref, g_ref, w_ref, kg_ref, bd_ref, k_ref, v_ref):
    h = _mm(_rmsnorm(mem_ref[...], g_ref[0]), w_ref[0])
    k_ref[0] = _head_rmsnorm(h[:, :MEM_W], bd_ref[...], kg_ref[0])
    v_ref[0] = h[:, MEM_W:]


def _shared_kv_kernel(x_ref, g_ref, w_ref, kg_ref, bd_ref, cos_ref, sin_ref,
                      k_ref, v_ref):
    h = _mm(_rmsnorm(x_ref[...], g_ref[...]), w_ref[...])
    k = _head_rmsnorm(h[:, :KV_W], bd_ref[...], kg_ref[...])
    k_ref[...] = _rope(k, cos_ref[...], sin_ref[...])
    v_ref[...] = h[:, KV_W:]


def _mixer_a_kernel(x_ref, g_ref, win_ref, cw_ref, cb_ref, wg_ref, bgx_ref, bga_ref,
                    lru_ref, mqg_ref, bd_ref, mk_ref, mv_ref, wout_ref, rc0_ref, h0_ref,
                    xo_ref, hlast_ref, rctail_ref,
                    conv_s, tail_s, a_s, b_s, hc_s, kcat_s, vcat_s):
    ts = x_ref.shape[0]
    n_slabs = D_RNN // LANES

    @pl.when(pl.program_id(1) == 0)
    def _():
        for s in range(n_slabs):
            tail_s[s] = rc0_ref[0, :, s * LANES:(s + 1) * LANES]
        a_s[:, 0:SUBLANES, :] = jnp.ones((n_slabs, SUBLANES, LANES), F32)
        b_s[:, 0:SUBLANES, :] = jnp.zeros((n_slabs, SUBLANES, LANES), F32)
        hc_s[...] = jnp.broadcast_to(h0_ref[0], hc_s.shape)
        kcat_s[...] = _head_masked_stack([mk_ref[0]], MEM_HEADS)
        vcat_s[...] = _head_masked_stack([mv_ref[0]], MEM_HEADS)

    x = x_ref[...]
    hn = _rmsnorm(x, g_ref[...]).astype(BF16)

    def in_proj(lo, hi):
        return jnp.dot(hn, win_ref[:, lo:hi], preferred_element_type=F32)

    xr_pre = in_proj(D_RNN, 2 * D_RNN)
    _slab_stage(xr_pre, conv_s, 0, tail_s, 0)
    xr = jnp.concatenate(_slab_taps(ts, n_slabs, conv_s, 0, cw_ref, cb_ref, 0), axis=1)
    rctail_ref[0] = xr_pre[ts - SUBLANES:]
    qm = in_proj(2 * D_RNN, 2 * D_RNN + MEM_W)
    a, b = _lru_coeffs(xr, wg_ref, bgx_ref[...], bga_ref[...], _log_sigmoid(lru_ref[...]))
    qn = _head_rmsnorm(qm, bd_ref[...], mqg_ref[...]) * ATT_SCALE
    gate = in_proj(0, D_RNN)
    mo = _mem_attention(qn, kcat_s[...], vcat_s[...])
    h = _lru_scan(a, b, a_s, b_s, hc_s)
    hlast_ref[0] = h[ts - 1:ts]
    main = h * _gelu(gate)

    y = _mm(jnp.concatenate([main, mo], axis=1), wout_ref[...])
    xo_ref[...] = x + y


def _mixer_b_kernel(sink_ref, x_ref, g_ref, win_ref, qg_ref, cos_ref, sin_ref,
                    kcur_ref, kprev_ref, vcur_ref, vprev_ref,
                    mqg_ref, bd_ref, mk_ref, mv_ref, wout_ref,
                    xo_ref, kcat_s, vcat_s):
    ts = x_ref.shape[0]
    i = pl.program_id(1)

    @pl.when(i == 0)
    def _():
        kcat_s[...] = _head_masked_stack([mk_ref[0]], MEM_HEADS)
        vcat_s[...] = _head_masked_stack([mv_ref[0]], MEM_HEADS)

    x = x_ref[...]
    hn = _rmsnorm(x, g_ref[...]).astype(BF16)

    def in_proj(lo, hi):
        return jnp.dot(hn, win_ref[:, lo:hi], preferred_element_type=F32)

    q = _head_rmsnorm(in_proj(0, Q_W), bd_ref[...], qg_ref[...], f32_stat=False)
    q = (_rope(q, cos_ref[...], sin_ref[...]) * ATT_SCALE).astype(BF16)

    row = lax.broadcasted_iota(jnp.int32, (WINDOW, 2 * WINDOW), 0)
    kj = lax.broadcasted_iota(jnp.int32, (WINDOW, 2 * WINDOW), 1) - WINDOW
    mask_inner = (kj <= row) & (kj >= row - WINDOW)
    mask_first = (kj <= row) & (kj >= jnp.maximum(row - WINDOW, jnp.where(i > 0, -WINDOW, 0)))

    kt = kcur_ref[...]
    vt = vcur_ref[...]

    def block_scores(jb):
        lo, hi = jb * WINDOW, (jb + 1) * WINDOW
        if jb == 0:
            kp, vp = kprev_ref[...], vprev_ref[...]
        else:
            kp, vp = kt[lo - WINDOW:lo], vt[lo - WINDOW:lo]
        kcat = _head_masked_stack([kp, kt[lo:hi]], N_KV)
        vcat = _head_masked_stack([vp, vt[lo:hi]], N_KV)
        qs = jnp.concatenate([q[lo:hi, g * KV_W:(g + 1) * KV_W] for g in range(GROUP)], axis=0)
        s = lax.dot_general(qs, kcat, (((1,), (1,)), ((), ())), preferred_element_type=F32)
        return s, vcat

    n_blocks = ts // WINDOW
    mains = []
    nxt = block_scores(0)
    qm = in_proj(Q_W, Q_W + MEM_W)
    qn = _head_rmsnorm(qm, bd_ref[...], mqg_ref[...], f32_stat=False) * ATT_SCALE
    s_mem = _mm_nt(qn, kcat_s[...])
    for jb in range(n_blocks):
        s, vcat = nxt
        if jb + 1 < n_blocks:
            nxt = block_scores(jb + 1)
        mask = mask_first if jb == 0 else mask_inner
        prow = []
        for g in range(GROUP):
            pseg = []
            for kv in range(N_KV):
                seg = s[g * WINDOW:(g + 1) * WINDOW, kv * 2 * WINDOW:(kv + 1) * 2 * WINDOW]
                seg = jnp.where(mask, seg, NEG)
                sink = sink_ref[g * N_KV + kv]
                m = jnp.maximum(jnp.max(seg, axis=-1, keepdims=True), sink)
                p = jnp.exp(seg - m)
                den = jnp.sum(p, axis=-1, keepdims=True) + jnp.exp(sink - m)
                pseg.append((p * (1.0 / den)).astype(BF16))
            prow.append(jnp.concatenate(pseg, axis=1))
        o = jnp.dot(jnp.concatenate(prow, axis=0), vcat, preferred_element_type=F32)
        mains.append(jnp.concatenate([o[g * WINDOW:(g + 1) * WINDOW] for g in range(GROUP)], axis=1))
    main = mains[0] if len(mains) == 1 else jnp.concatenate(mains, axis=0)

    mo = _mem_softmax_pv(s_mem, vcat_s[...])

    y = _mm(jnp.concatenate([main, mo], axis=1), wout_ref[...])
    xo_ref[...] = x + y


def _ffn_kernel(x_ref, g_ref, wup_ref, cw_ref, cb_ref, wdn_ref,
                xo_ref, utail_ref, conv_s, tail_s, act_s, xres_s, *, tiles_per_seq):
    ts = x_ref.shape[0]
    slabs = FF_CHUNK // LANES
    blk_slabs = MXU_DIM // LANES
    blks = FF_CHUNK // MXU_DIM
    i = pl.program_id(0)
    slot = i % 2
    pslot = 1 - slot

    @pl.when(i == 0)
    def _():
        act_s[1] = jnp.zeros(act_s.shape[1:], act_s.dtype)
        xres_s[1] = jnp.zeros(xres_s.shape[1:], xres_s.dtype)

    @pl.when(i % tiles_per_seq == 0)
    def _():
        tail_s[...] = jnp.zeros_like(tail_s)

    x = x_ref[...]
    xres_s[slot] = x
    hn = _rmsnorm(x, g_ref[...]).astype(BF16)

    def chunk_cols(j):
        return [(c, ((j % 2) * 2 + half) * slabs) for half, c in enumerate((j, N_FF_CHUNKS + j))]

    def up_project(j):
        for c, work0 in chunk_cols(j):
            for b in range(blks):
                blk = c * blks + b
                u = jnp.dot(hn, wup_ref[blk], preferred_element_type=F32)
                _slab_stage(u, conv_s, work0 + b * blk_slabs, tail_s, c * slabs + b * blk_slabs)
                utail_ref[0, :, blk * MXU_DIM:(blk + 1) * MXU_DIM] = u[ts - SUBLANES:]

    def activation(j):
        cg, cv = [_slab_taps(ts, slabs, conv_s, work0, cw_ref, cb_ref, c * FF_CHUNK)
                  for c, work0 in chunk_cols(j)]
        return jnp.concatenate([_gelu(a) * b for a, b in zip(cg, cv)], axis=1).astype(BF16)

    def down_prev(n):
        return jnp.dot(act_s[pslot], wdn_ref[n], preferred_element_type=F32)

    n_dn = wdn_ref.shape[0]
    outs = [None] * n_dn
    outs[0] = down_prev(0)
    up_project(0)
    for j in range(N_FF_CHUNKS):
        if j + 1 < N_FF_CHUNKS:
            up_project(j + 1)
        else:
            outs[1] = down_prev(1)
        act_s[slot, :, j * FF_CHUNK:(j + 1) * FF_CHUNK] = activation(j)
    for n in range(2, n_dn):
        outs[n] = down_prev(n)
    xo_ref[...] = xres_s[pslot] + jnp.concatenate(outs, axis=1)


def _dec_in_a_kernel(x_ref, g_ref, win_ref, cw_ref, cb_ref, wg_ref, bgx_ref, bga_ref,
                     lru_ref, mqg_ref, bd_ref, b0_ref, b1_ref, b2_ref, h0_ref,
                     main_ref, qn_ref, hnew_ref, xrpre_ref):
    u = _mm(_rmsnorm(x_ref[...], g_ref[...]), win_ref[...])
    gate = u[:, :D_RNN]
    xr_pre = u[:, D_RNN:2 * D_RNN]
    qm = u[:, 2 * D_RNN:]
    xr = b0_ref[...] * cw_ref[0:1, :]
    xr = xr + b1_ref[...] * cw_ref[1:2, :]
    xr = xr + b2_ref[...] * cw_ref[2:3, :]
    xr = xr + xr_pre * cw_ref[3:4, :]
    xr = xr + cb_ref[...]
    a, b = _lru_coeffs(xr, wg_ref, bgx_ref[...], bga_ref[...], _log_sigmoid(lru_ref[...]))
    h = a * h0_ref[...] + b
    main_ref[...] = h * _gelu(gate)
    qn_ref[...] = _head_rmsnorm(qm, bd_ref[...], mqg_ref[...]) * ATT_SCALE
    hnew_ref[...] = h
    xrpre_ref[...] = xr_pre


def _dec_in_b_kernel(x_ref, g_ref, win_ref, qg_ref, cos_ref, sin_ref, mqg_ref, bd_ref,
                     q_ref, qn_ref):
    u = _mm(_rmsnorm(x_ref[...], g_ref[...]), win_ref[...])
    q = _head_rmsnorm(u[:, :Q_W], bd_ref[...], qg_ref[...])
    q_ref[...] = _rope(q, cos_ref[...], sin_ref[...]) * ATT_SCALE
    qn_ref[...] = _head_rmsnorm(u[:, Q_W:], bd_ref[...], mqg_ref[...]) * ATT_SCALE


DEC_HEAD_ROWS = 16


def _own_head_lanes(n_heads, width):
    row = lax.broadcasted_iota(jnp.int32, (DEC_HEAD_ROWS, width), 0)
    lane = lax.broadcasted_iota(jnp.int32, (DEC_HEAD_ROWS, width), 1)
    start = (row % (width // HEAD_DIM)) * HEAD_DIM
    return (lane >= start) & (lane < start + HEAD_DIM) & (row < n_heads)


def _dec_mem_attn_kernel(q_ref, kt_ref, vt_ref, o_ref):
    own = _own_head_lanes(MEM_HEADS, MEM_W)
    for s in range(q_ref.shape[0]):
        q_rows = jnp.broadcast_to(q_ref[s:s + 1, :], (DEC_HEAD_ROWS, MEM_W))
        qbd = jnp.where(own, q_rows, 0.0).astype(BF16)
        sc = jnp.dot(qbd, kt_ref[s].astype(BF16), preferred_element_type=F32)
        m = jnp.max(sc, axis=-1, keepdims=True)
        p = jnp.exp(sc - m)
        den = jnp.sum(p, axis=-1, keepdims=True)
        pn = (p * (1.0 / den)).astype(BF16)
        o_all = lax.dot_general(pn, vt_ref[s].astype(BF16), (((1,), (1,)), ((), ())),
                                preferred_element_type=F32)
        o_ref[s:s + 1, :] = jnp.sum(jnp.where(own, o_all, 0.0), axis=0, keepdims=True)


def _dec_swa_kernel(q_ref, kt_ref, vt_ref, kn_ref, vn_ref, sink_ref, o_ref):
    n_heads = GROUP * N_KV
    own = _own_head_lanes(n_heads, KV_W)
    grp = lax.broadcasted_iota(jnp.int32, (DEC_HEAD_ROWS, KV_W), 0) // N_KV
    sink = sink_ref[:, 0:1]
    for s in range(q_ref.shape[0]):
        q_rows = jnp.zeros((DEC_HEAD_ROWS, KV_W), F32)
        for g in range(GROUP):
            qg = jnp.broadcast_to(q_ref[s:s + 1, g * KV_W:(g + 1) * KV_W], (DEC_HEAD_ROWS, KV_W))
            q_rows = jnp.where(grp == g, qg, q_rows)
        qbd = jnp.where(own, q_rows, 0.0)
        s_buf = jnp.dot(qbd.astype(BF16), kt_ref[s].astype(BF16), preferred_element_type=F32)
        s_new = jnp.sum(qbd * kn_ref[s:s + 1, :], axis=-1, keepdims=True)
        m = jnp.maximum(jnp.maximum(jnp.max(s_buf, axis=-1, keepdims=True), s_new), sink)
        p_buf = jnp.exp(s_buf - m)
        p_new = jnp.exp(s_new - m)
        den = jnp.sum(p_buf, axis=-1, keepdims=True) + p_new + jnp.exp(sink - m)
        r = 1.0 / den
        o_all = lax.dot_general((p_buf * r).astype(BF16), vt_ref[s].astype(BF16),
                                (((1,), (1,)), ((), ())), preferred_element_type=F32)
        o_all = jnp.where(own, o_all + (p_new * r) * vn_ref[s:s + 1, :], 0.0)
        o_sum = o_all + pltpu.roll(o_all, 1, 0)
        o_sum = o_sum + pltpu.roll(o_sum, 2, 0)
        for g in range(GROUP):
            last = (g + 1) * N_KV - 1
            o_ref[s:s + 1, g * KV_W:(g + 1) * KV_W] = o_sum[last:last + 1, :]


def _dec_out_ffn_kernel(x_ref, main_ref, mo_ref, wout_ref, g_ref,
                        wug_ref, wuv_ref, cwg_ref, cwv_ref, cbg_ref, cbv_ref, wdn_ref,
                        sg_ref, sv_ref, snew_in_ref,
                        xo_ref, snew_ref,
                        xmid_s, hn_s, acc_s):
    del snew_in_ref
    j = pl.program_id(0)

    @pl.when(j == 0)
    def _():
        y = _mm(jnp.concatenate([main_ref[...], mo_ref[...]], axis=1), wout_ref[...])
        xmid = x_ref[...] + y
        xmid_s[...] = xmid
        hn_s[...] = _rmsnorm(xmid, g_ref[...]).astype(BF16)
        acc_s[...] = jnp.zeros_like(acc_s)

    hn = hn_s[...]
    up = lambda w_ref: jnp.concatenate(
        [jnp.dot(hn, w_ref[b], preferred_element_type=F32) for b in range(w_ref.shape[0])], axis=1)
    ug = up(wug_ref)
    uv = up(wuv_ref)
    sg0, sg1 = sg_ref[:, 0, :], sg_ref[:, 1, :]
    sv0, sv1 = sv_ref[:, 0, :], sv_ref[:, 1, :]
    gate_cols = pl.ds(pl.multiple_of(j * FF_CHUNK, FF_CHUNK), FF_CHUNK)
    value_cols = pl.ds(pl.multiple_of(D_FF + j * FF_CHUNK, FF_CHUNK), FF_CHUNK)
    snew_ref[:, 0, gate_cols] = sg1
    snew_ref[:, 1, gate_cols] = ug
    snew_ref[:, 0, value_cols] = sv1
    snew_ref[:, 1, value_cols] = uv
    cg = sg0 * cwg_ref[0:1, :] + sg1 * cwg_ref[1:2, :] + ug * cwg_ref[2:3, :] + cbg_ref[...]
    cv = sv0 * cwv_ref[0:1, :] + sv1 * cwv_ref[1:2, :] + uv * cwv_ref[2:3, :] + cbv_ref[...]
    act = (_gelu(cg) * cv).astype(BF16)
    for n in range(wdn_ref.shape[0]):
        acc_s[:, n * MXU_DIM:(n + 1) * MXU_DIM] += jnp.dot(act, wdn_ref[n],
                                                           preferred_element_type=F32)

    @pl.when(j == pl.num_programs(0) - 1)
    def _():
        xo_ref[...] = xmid_s[...] + acc_s[...]


def _const_spec(shape):
    nd = len(shape)
    return pl.BlockSpec(shape, lambda *_: (0,) * nd)


def _layer_spec(arr, layer):
    nd = arr.ndim - 1
    return pl.BlockSpec((None,) + arr.shape[1:], lambda *_: (layer,) + (0,) * nd)


def _resident_spec(op):
    return _layer_spec(*op) if isinstance(op, tuple) else _const_spec(op.shape)


def _operand(op):
    return op[0] if isinstance(op, tuple) else op


def _params(*sem):
    return pltpu.CompilerParams(dimension_semantics=sem, vmem_limit_bytes=VMEM_LIMIT_BYTES)


def _cast_blocks(w):
    layers, r, c = w.shape
    tile = CAST_TILE
    per = tile // MXU_DIM
    return pl.pallas_call(
        _cast_blocks_kernel,
        grid=(layers, r // tile, c // tile),
        in_specs=[pl.BlockSpec((None, tile, tile), lambda l, i, j: (l, i, j))],
        out_specs=pl.BlockSpec((None, per, tile, MXU_DIM), lambda l, i, j: (l, j, i, 0)),
        out_shape=jax.ShapeDtypeStruct((layers, c // MXU_DIM, r, MXU_DIM), BF16),
        compiler_params=_params("arbitrary", "arbitrary", "arbitrary"),
        name="cast_blocks",
    )(w)


def _mem_kv(mem2d, g, w, kg, bd):
    depth = w.shape[0]
    rows = mem2d.shape[0]
    out = jax.ShapeDtypeStruct((depth, rows, MEM_W), F32)
    return pl.pallas_call(
        _mem_kv_kernel,
        grid=(depth,),
        in_specs=[
            _const_spec(mem2d.shape),
            pl.BlockSpec((1, 1, D_MODEL), lambda l: (l, 0, 0)),
            pl.BlockSpec((1, D_MODEL, 2 * MEM_W), lambda l: (l, 0, 0)),
            pl.BlockSpec((1, 1, MEM_W), lambda l: (l, 0, 0)),
            _const_spec(bd.shape),
        ],
        out_specs=[pl.BlockSpec((1, rows, MEM_W), lambda l: (l, 0, 0))] * 2,
        out_shape=[out, out],
        compiler_params=_params("arbitrary"),
        name="mem_kv",
    )(mem2d, g, w, kg, bd)


def _shared_kv(x2d, g, w, kg, bd, cos_t, sin_t, ts):
    rows = x2d.shape[0]
    tab_blocks = cos_t.shape[0] // ts
    out = jax.ShapeDtypeStruct((rows, KV_W), F32)
    return pl.pallas_call(
        _shared_kv_kernel,
        grid=(rows // ts,),
        in_specs=[
            pl.BlockSpec((ts, D_MODEL), lambda i: (i, 0)),
            _const_spec(g.shape), _const_spec(w.shape), _const_spec(kg.shape),
            _const_spec(bd.shape),
            pl.BlockSpec((ts, LANES), lambda i: (i % tab_blocks, 0)),
            pl.BlockSpec((ts, LANES), lambda i: (i % tab_blocks, 0)),
        ],
        out_specs=[pl.BlockSpec((ts, KV_W), lambda i: (i, 0))] * 2,
        out_shape=[out, out],
        compiler_params=_params("arbitrary"),
        name="shared_kv",
    )(x2d, g, w, kg, bd, cos_t, sin_t)


def _mixer_a(x2d, bsz, g, win, cw, cb, wg, bgx, bga, lru, mqg, bd, mk, mv, wout, rc0, h0):
    rows = x2d.shape[0]
    ts = TS_MIX
    nt = rows // bsz // ts
    consts = (g, win, cw, cb, wg, bgx, bga, lru, mqg, bd)
    per_b3 = lambda b, i: (b, 0, 0)
    return pl.pallas_call(
        _mixer_a_kernel,
        grid=(bsz, nt),
        in_specs=[pl.BlockSpec((ts, D_MODEL), lambda b, i: (b * nt + i, 0))]
        + [_resident_spec(c) for c in consts]
        + [pl.BlockSpec((1, N_MEM, MEM_W), per_b3)] * 2
        + [_resident_spec(wout),
           pl.BlockSpec((1, SUBLANES, D_RNN), per_b3),
           pl.BlockSpec((1, 1, D_RNN), per_b3)],
        out_specs=[
            pl.BlockSpec((ts, D_MODEL), lambda b, i: (b * nt + i, 0)),
            pl.BlockSpec((1, 1, D_RNN), per_b3),
            pl.BlockSpec((1, SUBLANES, D_RNN), per_b3),
        ],
        out_shape=[
            jax.ShapeDtypeStruct((rows, D_MODEL), F32),
            jax.ShapeDtypeStruct((bsz, 1, D_RNN), F32),
            jax.ShapeDtypeStruct((bsz, SUBLANES, D_RNN), F32),
        ],
        scratch_shapes=[
            pltpu.VMEM((D_RNN // LANES, SUBLANES + ts, LANES), F32),
            pltpu.VMEM((D_RNN // LANES, SUBLANES, LANES), F32),
            pltpu.VMEM((D_RNN // LANES, SUBLANES + ts, LANES), F32),
            pltpu.VMEM((D_RNN // LANES, SUBLANES + ts, LANES), F32),
            pltpu.VMEM((SUBLANES, D_RNN), F32),
            pltpu.VMEM((MEM_HEADS * N_MEM, MEM_W), BF16),
            pltpu.VMEM((MEM_HEADS * N_MEM, MEM_W), BF16),
        ],
        compiler_params=_params("arbitrary", "arbitrary"),
        name="mixer_a",
    )(x2d, *[_operand(c) for c in consts], mk, mv, _operand(wout), rc0, h0)


def _mixer_b(x2d, bsz, sink_tab, g, win, qg, cos_t, sin_t, ksh, vsh, mqg, bd, mk, mv, wout):
    rows = x2d.shape[0]
    ts = TS_MIX
    nt = rows // bsz // ts
    wpt = ts // WINDOW
    cur = lambda b, i: (b * nt + i, 0)
    prev = lambda b, i: (jnp.maximum((b * nt + i) * wpt - 1, 0), 0)
    table = lambda b, i: (i, 0)
    per_b3 = lambda b, i: (b, 0, 0)
    return pl.pallas_call(
        _mixer_b_kernel,
        grid=(bsz, nt),
        in_specs=[
            pl.BlockSpec(memory_space=pltpu.SMEM),
            pl.BlockSpec((ts, D_MODEL), cur),
            _const_spec(g.shape), _resident_spec(win), _const_spec(qg.shape),
            pl.BlockSpec((ts, LANES), table),
            pl.BlockSpec((ts, LANES), table),
            pl.BlockSpec((ts, KV_W), cur), pl.BlockSpec((WINDOW, KV_W), prev),
            pl.BlockSpec((ts, KV_W), cur), pl.BlockSpec((WINDOW, KV_W), prev),
            _const_spec(mqg.shape), _const_spec(bd.shape),
            pl.BlockSpec((1, N_MEM, MEM_W), per_b3), pl.BlockSpec((1, N_MEM, MEM_W), per_b3),
            _resident_spec(wout),
        ],
        out_specs=pl.BlockSpec((ts, D_MODEL), cur),
        out_shape=jax.ShapeDtypeStruct((rows, D_MODEL), F32),
        scratch_shapes=[
            pltpu.VMEM((MEM_HEADS * N_MEM, MEM_W), BF16),
            pltpu.VMEM((MEM_HEADS * N_MEM, MEM_W), BF16),
        ],
        compiler_params=_params("arbitrary", "arbitrary"),
        name="mixer_b",
    )(sink_tab, x2d, g, _operand(win), qg, cos_t, sin_t, ksh, ksh, vsh, vsh, mqg, bd, mk, mv,
      _operand(wout))


def _ffn(x2d, bsz, layer, g, wup, cw, cb, wdn):
    rows = x2d.shape[0]
    ts = TS_FFN
    nt = rows // ts
    tiles_per_seq = rows // bsz // ts
    last = nt - 1
    return pl.pallas_call(
        functools.partial(_ffn_kernel, tiles_per_seq=tiles_per_seq),
        grid=(nt + 1,),
        in_specs=[pl.BlockSpec((ts, D_MODEL), lambda i: (jnp.minimum(i, last), 0)),
                  _const_spec(g.shape), _layer_spec(wup, layer), _const_spec(cw.shape),
                  _const_spec(cb.shape), _layer_spec(wdn, layer)],
        out_specs=[
            pl.BlockSpec((ts, D_MODEL), lambda i: (jnp.maximum(i - 1, 0), 0)),
            pl.BlockSpec((1, SUBLANES, 2 * D_FF),
                         lambda i: (jnp.minimum(i, last) // tiles_per_seq, 0, 0)),
        ],
        out_shape=[
            jax.ShapeDtypeStruct((rows, D_MODEL), F32),
            jax.ShapeDtypeStruct((bsz, SUBLANES, 2 * D_FF), F32),
        ],
        scratch_shapes=[pltpu.VMEM((4 * FF_CHUNK // LANES, SUBLANES + ts, LANES), F32),
                        pltpu.VMEM((2 * D_FF // LANES, SUBLANES, LANES), F32),
                        pltpu.VMEM((2, ts, D_FF), BF16),
                        pltpu.VMEM((2, ts, D_MODEL), F32)],
        compiler_params=_params("arbitrary"),
        name="ffn",
    )(x2d, g, wup, cw, cb, wdn)


def _dec_in_a(x, g, win, cw, cb, wg, bgx, bga, lru, mqg, bd, rc, h0):
    n = x.shape[0]
    consts = (x, g, win, cw, cb, wg, bgx, bga, lru, mqg, bd)
    buf = lambda j: pl.BlockSpec((None, n, D_RNN), lambda i: (j, 0, 0))
    return pl.pallas_call(
        _dec_in_a_kernel,
        grid=(1,),
        in_specs=[_resident_spec(c) for c in consts]
        + [buf(0), buf(1), buf(2), _const_spec(h0.shape)],
        out_specs=[_const_spec((n, D_RNN)), _const_spec((n, MEM_W)),
                   _const_spec((n, D_RNN)), _const_spec((n, D_RNN))],
        out_shape=[jax.ShapeDtypeStruct((n, D_RNN), F32), jax.ShapeDtypeStruct((n, MEM_W), F32),
                   jax.ShapeDtypeStruct((n, D_RNN), F32), jax.ShapeDtypeStruct((n, D_RNN), F32)],
        compiler_params=_params("arbitrary"),
        name="dec_in_a",
    )(*[_operand(c) for c in consts], rc, rc, rc, h0)


def _dec_in_b(x, g, win, qg, cos_t, sin_t, mqg, bd):
    n = x.shape[0]
    args = (x, g, win, qg, cos_t, sin_t, mqg, bd)
    return pl.pallas_call(
        _dec_in_b_kernel,
        grid=(1,),
        in_specs=[_resident_spec(a) for a in args],
        out_specs=[_const_spec((n, Q_W)), _const_spec((n, MEM_W))],
        out_shape=[jax.ShapeDtypeStruct((n, Q_W), F32), jax.ShapeDtypeStruct((n, MEM_W), F32)],
        compiler_params=_params("arbitrary"),
        name="dec_in_b",
    )(*[_operand(a) for a in args])


def _dec_mem_attn(qn, ckt, cvt, layer):
    n = qn.shape[0]
    sb = DEC_SEQ_BLOCK
    return pl.pallas_call(
        _dec_mem_attn_kernel,
        grid=(n // sb,),
        in_specs=[pl.BlockSpec((sb, MEM_W), lambda i: (i, 0)),
                  pl.BlockSpec((None, sb, MEM_W, N_MEM), lambda i: (layer, i, 0, 0)),
                  pl.BlockSpec((None, sb, MEM_W, N_MEM), lambda i: (layer, i, 0, 0))],
        out_specs=pl.BlockSpec((sb, MEM_W), lambda i: (i, 0)),
        out_shape=jax.ShapeDtypeStruct((n, MEM_W), F32),
        compiler_params=_params("arbitrary"),
        name="dec_mem_attn",
    )(qn, ckt, cvt)


def _dec_swa(q, kbt, vbt, kn, vn, sink_rows):
    n = q.shape[0]
    wb = kbt.shape[2]
    sb = DEC_SEQ_BLOCK
    assert wb <= WINDOW
    return pl.pallas_call(
        _dec_swa_kernel,
        grid=(n // sb,),
        in_specs=[pl.BlockSpec((sb, Q_W), lambda i: (i, 0)),
                  pl.BlockSpec((sb, KV_W, wb), lambda i: (i, 0, 0)),
                  pl.BlockSpec((sb, KV_W, wb), lambda i: (i, 0, 0)),
                  pl.BlockSpec((sb, KV_W), lambda i: (i, 0)),
                  pl.BlockSpec((sb, KV_W), lambda i: (i, 0)),
                  _const_spec(sink_rows.shape)],
        out_specs=pl.BlockSpec((sb, Q_W), lambda i: (i, 0)),
        out_shape=jax.ShapeDtypeStruct((n, Q_W), F32),
        compiler_params=_params("arbitrary"),
        name="dec_swa",
    )(q, kbt, vbt, kn, vn, sink_rows)


def _dec_out_ffn(x, main, mo, wout, wout_layer, layer, g, wup, cw, cb, wdn, st, st_new):
    n = x.shape[0]
    nch = N_FF_CHUNKS
    blks = FF_CHUNK // MXU_DIM
    lo = lambda j: (0, j)
    hi = lambda j: (0, nch + j)
    state = lambda off: pl.BlockSpec((None, n, CONV_F - 1, FF_CHUNK),
                                     lambda j: (layer, 0, 0, off + j))
    up_blocks = lambda off: pl.BlockSpec((None, blks, D_MODEL, MXU_DIM),
                                         lambda j: (layer, off + j, 0, 0))
    operands = (x, main, mo, wout, g, wup, wup, cw, cw, cb, cb, wdn, st, st, st_new)
    return pl.pallas_call(
        _dec_out_ffn_kernel,
        grid=(nch,),
        in_specs=[_const_spec(x.shape), _const_spec(main.shape), _const_spec(mo.shape),
                  _layer_spec(wout, wout_layer), _const_spec(g.shape),
                  up_blocks(0), up_blocks(nch),
                  pl.BlockSpec((CONV_F, FF_CHUNK), lo), pl.BlockSpec((CONV_F, FF_CHUNK), hi),
                  pl.BlockSpec((1, FF_CHUNK), lo), pl.BlockSpec((1, FF_CHUNK), hi),
                  pl.BlockSpec((None, wdn.shape[1], FF_CHUNK, MXU_DIM), lambda j: (layer, 0, j, 0)),
                  state(0), state(nch),
                  pl.BlockSpec(memory_space=pl.ANY)],
        out_specs=[_const_spec((n, D_MODEL)),
                   pl.BlockSpec((None, n, CONV_F - 1, 2 * D_FF), lambda j: (layer, 0, 0, 0))],
        out_shape=[jax.ShapeDtypeStruct((n, D_MODEL), F32),
                   jax.ShapeDtypeStruct(st_new.shape, F32)],
        input_output_aliases={len(operands) - 1: 1},
        scratch_shapes=[pltpu.VMEM((n, D_MODEL), F32), pltpu.VMEM((n, D_MODEL), BF16),
                        pltpu.VMEM((n, D_MODEL), F32)],
        compiler_params=_params("arbitrary"),
        name="dec_out_ffn",
    )(*operands)


def _rope_tables(pos):
    half = HEAD_DIM // 2
    inv = ROPE_THETA ** (-jnp.arange(half, dtype=F32) / half)
    ang = pos.astype(F32)[:, None] * inv[None, :]
    cos = jnp.cos(ang)
    sin = jnp.sin(ang)
    reps = LANES // HEAD_DIM
    cos_t = jnp.tile(jnp.concatenate([cos, cos], axis=1), (1, reps))
    sin_t = jnp.tile(jnp.concatenate([-sin, sin], axis=1), (1, reps))
    return cos_t, sin_t


def _block_diag_gates(wx, wa):
    per = MXU_DIM // HEAD_DIM
    eye = jnp.eye(per, dtype=F32)

    def bd(w):
        w4 = w.reshape(RNN_BLOCKS // per, per, HEAD_DIM, HEAD_DIM)
        return jnp.einsum('ckij,kK->ckiKj', w4, eye).reshape(RNN_BLOCKS // per, MXU_DIM, MXU_DIM)

    return jnp.concatenate([bd(wx), bd(wa)], axis=2).astype(BF16)


def kernel(x_prompt, x_sample, state_rglru_h, state_rglru_conv, state_ffn_conv, cache_swa_k, cache_swa_v, cache_mem_k, cache_mem_v, mem_prompt, norm_mix_g, norm_ffn_g, w_in_a, rnn_conv_w, rnn_conv_b, w_gate_x, b_gate_x, w_gate_a, b_gate_a, lru_param, w_in_b, q_norm_g, sinks, kv_norm_g, w_kv, k_norm_g, mem_norm_g, w_mem_kv, mem_q_norm_g, mem_k_norm_g, w_out, w_ffn_up, ffn_conv_w, ffn_conv_b, w_ffn_down):
    bsz, seq, _ = x_prompt.shape
    dbsz = x_sample.shape[0]
    depth = norm_mix_g.shape[0]
    n_a = w_in_a.shape[0]
    assert x_sample.shape[1] == 1
    assert seq % TS_MIX == 0 and seq % TS_FFN == 0 and seq % TS_KV == 0 and TS_MIX % WINDOW == 0

    n_b = w_in_b.shape[0]
    wq = w_in_b[:, :, :Q_W].astype(BF16).reshape(n_b, D_MODEL, N_KV, GROUP, HEAD_DIM)
    wq = wq.transpose(0, 1, 3, 2, 4).reshape(n_b, D_MODEL, Q_W)
    wo_main = w_out[n_a:, :Q_W].astype(BF16).reshape(n_b, N_KV, GROUP, HEAD_DIM, D_MODEL)
    wo_main = wo_main.transpose(0, 2, 1, 3, 4).reshape(n_b, Q_W, D_MODEL)
    bd =(jnp.kron(jnp.eye(MXU_DIM // HEAD_DIM, dtype=F32),
                   jnp.ones((HEAD_DIM, HEAD_DIM), F32)) / HEAD_DIM).astype(BF16)

    row = lambda v: v.reshape(1, -1)
    tile_row = lambda v, n: jnp.tile(v, n).reshape(1, -1)
    w_in_a_b = w_in_a.astype(BF16)
    w_in_b_b = jnp.concatenate([wq, w_in_b[:, :, Q_W:].astype(BF16)], axis=2)
    w_out_b = w_out.astype(BF16)
    w_out_perm_b = jnp.concatenate([wo_main, w_out_b[n_a:, Q_W:]], axis=1)
    w_kv_b = w_kv.astype(BF16)
    w_mem_b = w_mem_kv.astype(BF16)
    wup_b = _cast_blocks(w_ffn_up)
    wdn_b = _cast_blocks(w_ffn_down)
    fcw = ffn_conv_w
    fcb = ffn_conv_b.reshape(depth, 1, 2 * D_FF)
    wg_b = jnp.stack([_block_diag_gates(w_gate_x[l], w_gate_a[l]) for l in range(n_a)])
    sink_gk = sinks.reshape(-1, N_KV, GROUP).transpose(0, 2, 1)

    cos_p, sin_p = _rope_tables(jnp.arange(seq, dtype=jnp.int32))
    pos_s = PAST_LEN + jnp.zeros((dbsz,), jnp.int32)
    cos_s, sin_s = _rope_tables(pos_s)

    mem2d = mem_prompt.reshape(bsz * N_MEM, D_MODEL)
    pmk, pmv = _mem_kv(mem2d, mem_norm_g.reshape(depth, 1, D_MODEL), w_mem_b,
                       jnp.tile(mem_k_norm_g, (1, MEM_HEADS)).reshape(depth, 1, MEM_W), bd)
    pmk4 = pmk.reshape(depth, bsz, N_MEM, MEM_W)
    pmv4 = pmv.reshape(depth, bsz, N_MEM, MEM_W)

    x = x_prompt.reshape(bsz * seq, D_MODEL)
    zeros_rc = jnp.zeros((bsz, SUBLANES, D_RNN), F32)
    zeros_h = jnp.zeros((bsz, 1, D_RNN), F32)
    p_h, p_rc, p_fc = [], [], []
    ksh = vsh = None
    for l in range(depth):
        mqg = tile_row(mem_q_norm_g[l], MEM_HEADS)
        if l < n_a:
            x, hl, rct = _mixer_a(
                x, bsz, row(norm_mix_g[l]), (w_in_a_b, l), rnn_conv_w[l], row(rnn_conv_b[l]),
                (wg_b, l), row(b_gate_x[l]), row(b_gate_a[l]), row(lru_param[l]), mqg, bd,
                pmk4[l], pmv4[l], (w_out_b, l), zeros_rc, zeros_h)
            p_h.append(hl.reshape(bsz, D_RNN))
            p_rc.append(rct[:, SUBLANES - (CONV_A - 1):])
        else:
            j = l - n_a
            x = _mixer_b(
                x, bsz, sink_gk[j].reshape(-1), row(norm_mix_g[l]), (w_in_b_b, j),
                tile_row(q_norm_g[j], N_Q), cos_p, sin_p, ksh, vsh, mqg, bd,
                pmk4[l], pmv4[l], (w_out_perm_b, j))
        x, ut = _ffn(x, bsz, l, row(norm_ffn_g[l]), wup_b, fcw[l], fcb[l], wdn_b)
        p_fc.append(ut[:, SUBLANES - (CONV_F - 1):])
        if l == n_a - 1:
            ksh, vsh = _shared_kv(x, row(kv_norm_g), w_kv_b, tile_row(k_norm_g, N_KV), bd,
                                  cos_p, sin_p, TS_KV)
    y_prompt = x.reshape(bsz, seq, D_MODEL)
    keep = min(WINDOW, seq)
    p_k = ksh.reshape(bsz, seq, KV_W)[:, seq - keep:].reshape(bsz, keep, N_KV, HEAD_DIM)
    p_v = vsh.reshape(bsz, seq, KV_W)[:, seq - keep:].reshape(bsz, keep, N_KV, HEAD_DIM)
    p_mem_k = pmk.reshape(depth, bsz, N_MEM, MEM_HEADS, HEAD_DIM)
    p_mem_v = pmv.reshape(depth, bsz, N_MEM, MEM_HEADS, HEAD_DIM)

    xs = x_sample.reshape(dbsz, D_MODEL)
    cmk = cache_mem_k.transpose(0, 1, 3, 4, 2).reshape(depth, dbsz, MEM_W, N_MEM)
    cmv = cache_mem_v.transpose(0, 1, 3, 4, 2).reshape(depth, dbsz, MEM_W, N_MEM)
    wb = cache_swa_k.shape[1]
    ckb = cache_swa_k.transpose(0, 2, 3, 1).reshape(dbsz, KV_W, wb)
    cvb = cache_swa_v.transpose(0, 2, 3, 1).reshape(dbsz, KV_W, wb)
    s_h, s_rc = [], []
    s_fc = jnp.zeros_like(state_ffn_conv)
    kn = vn = None
    for l in range(depth):
        mqg = tile_row(mem_q_norm_g[l], MEM_HEADS)
        if l < n_a:
            main, qn, hnew, xrpre = _dec_in_a(
                xs, row(norm_mix_g[l]), (w_in_a_b, l), rnn_conv_w[l], row(rnn_conv_b[l]),
                (wg_b, l), row(b_gate_x[l]), row(b_gate_a[l]), row(lru_param[l]), mqg, bd,
                state_rglru_conv[l].transpose(1, 0, 2), state_rglru_h[l])
            s_h.append(hnew)
            s_rc.append(jnp.concatenate([state_rglru_conv[l][:, 1:], xrpre[:, None, :]], axis=1))
            wo, wo_layer = w_out_b, l
        else:
            j = l - n_a
            q, qn = _dec_in_b(xs, row(norm_mix_g[l]), (w_in_b_b, j), tile_row(q_norm_g[j], N_Q),
                              cos_s, sin_s, mqg, bd)
            sink_rows = jnp.zeros((DEC_HEAD_ROWS, LANES), F32).at[:N_Q].set(
                jnp.broadcast_to(sink_gk[j].reshape(N_Q, 1), (N_Q, LANES)))
            main = _dec_swa(q, ckb, cvb, kn, vn, sink_rows)
            wo, wo_layer = w_out_perm_b, j
        mo = _dec_mem_attn(qn, cmk, cmv, l)
        xs, s_fc = _dec_out_ffn(xs, main, mo, wo, wo_layer, l, row(norm_ffn_g[l]), wup_b,
                                fcw[l], fcb[l], wdn_b, state_ffn_conv, s_fc)
        if l == n_a - 1:
            kn, vn = _shared_kv(xs, row(kv_norm_g), w_kv_b, tile_row(k_norm_g, N_KV), bd,
                                cos_s, sin_s, dbsz)
    y_sample = xs.reshape(dbsz, 1, D_MODEL)
    s_k = kn.reshape(dbsz, 1, N_KV, HEAD_DIM)
    s_v = vn.reshape(dbsz, 1, N_KV, HEAD_DIM)

    return (y_prompt, y_sample, jnp.stack(p_h), jnp.stack(p_rc), jnp.stack(p_fc), p_k, p_v,
            p_mem_k, p_mem_v, jnp.stack(s_h), jnp.stack(s_rc), s_fc, s_k, s_v)
```

```python
import functools
import math

import jax
import jax.numpy as jnp
from jax import lax
from jax.experimental import pallas as pl
from jax.experimental.pallas import tpu as pltpu

F32 = jnp.float32
BF16 = jnp.bfloat16

D_MODEL = 1024
HEAD_DIM = 64
MEM_HEADS = 4
MEM_W = MEM_HEADS * HEAD_DIM
N_MEM = 256
D_RNN = D_MODEL - MEM_W
RNN_BLOCKS = D_RNN // HEAD_DIM
CONV_A = 4
LRU_C = 8.0
N_Q = D_RNN // HEAD_DIM
N_KV = 4
GROUP = N_Q // N_KV
Q_W = N_Q * HEAD_DIM
KV_W = N_KV * HEAD_DIM
WINDOW = 128
ROPE_THETA = 10000.0
D_FF = 3 * D_MODEL
CONV_F = 3
EPS = 1e-6
NEG = -1e30
ATT_SCALE = HEAD_DIM ** -0.5
PAST_LEN = 8192

SUBLANES = 8
LANES = 128
MXU_DIM = 256
VMEM_LIMIT_BYTES = 56 * 1024 * 1024

TS_MIX = 512
TS_FFN = 512
TS_KV = 1024
FF_CHUNK = 512
N_FF_CHUNKS = D_FF // FF_CHUNK
DEC_SEQ_BLOCK = 16
CAST_TILE = 1024


def _mm(a, b):
    return jnp.dot(a.astype(BF16), b, preferred_element_type=F32)


def _mm_nt(a, b):
    return lax.dot_general(a.astype(BF16), b, (((1,), (1,)), ((), ())),
                           preferred_element_type=F32)


def _mm_split(a, b):
    hi = a.astype(BF16)
    lo = (a - hi.astype(F32)).astype(BF16)
    return (jnp.dot(hi, b, preferred_element_type=F32)
            + jnp.dot(lo, b, preferred_element_type=F32))


def _rmsnorm(x, g):
    ms = jnp.mean(x * x, axis=-1, keepdims=True)
    return x * lax.rsqrt(ms + EPS) * g


def _head_rmsnorm(x, bd, g, f32_stat=True):
    mean_sq = _mm_split if f32_stat else _mm
    parts = []
    for c in range(x.shape[1] // MXU_DIM):
        xc = x[:, c * MXU_DIM:(c + 1) * MXU_DIM]
        ms = mean_sq(xc * xc, bd)
        parts.append(xc * lax.rsqrt(ms + EPS))
    y = parts[0] if len(parts) == 1 else jnp.concatenate(parts, axis=1)
    return y * g


def _tile_lanes(t, width):
    reps = width // t.shape[1]
    return t if reps == 1 else jnp.concatenate([t] * reps, axis=1)


def _rope(x, cos_t, sin_t):
    w = x.shape[1]
    lane = lax.broadcasted_iota(jnp.int32, x.shape, 1)
    first = (lane % HEAD_DIM) < (HEAD_DIM // 2)
    swapped = jnp.where(first, pltpu.roll(x, w - HEAD_DIM // 2, 1),
                        pltpu.roll(x, HEAD_DIM // 2, 1))
    return x * _tile_lanes(cos_t, w) + swapped * _tile_lanes(sin_t, w)


def _gelu(x):
    c = math.sqrt(2.0 / math.pi)
    return x * (0.5 * (1.0 + jnp.tanh(c * (x + 0.044715 * (x * x * x)))))


def _log_sigmoid(x):
    return jnp.minimum(x, 0.0) - jnp.log1p(jnp.exp(-jnp.abs(x)))


def _slab_stage(x, work_ref, work0, tail_ref, tail0):
    ts = x.shape[0]
    for s in range(x.shape[1] // LANES):
        xs = x[:, s * LANES:(s + 1) * LANES]
        buf = work_ref.at[work0 + s]
        buf[0:SUBLANES, :] = tail_ref[tail0 + s]
        buf[SUBLANES:SUBLANES + ts, :] = xs
        tail_ref[tail0 + s] = xs[ts - SUBLANES:]


def _slab_taps(ts, n_slabs, work_ref, work0, w_ref, b_ref, col0):
    k = w_ref.shape[0]
    outs = []
    for s in range(n_slabs):
        lanes = slice(col0 + s * LANES, col0 + (s + 1) * LANES)
        buf = work_ref.at[work0 + s]
        acc = buf[SUBLANES - (k - 1):SUBLANES - (k - 1) + ts, :] * w_ref[0:1, lanes]
        for j in range(1, k):
            off = SUBLANES - (k - 1 - j)
            acc = acc + buf[off:off + ts, :] * w_ref[j:j + 1, lanes]
        outs.append(acc + b_ref[:, lanes])
    return outs


def _sqrt_pos(x):
    return jnp.where(x > 0.0, x * lax.rsqrt(x), 0.0)


def _lru_coeffs(xr, wg_ref, bgx, bga, logsig):
    xb = xr.astype(BF16)
    gxs, gas = [], []
    for c in range(D_RNN // MXU_DIM):
        gg = jnp.dot(xb[:, c * MXU_DIM:(c + 1) * MXU_DIM], wg_ref[c],
                     preferred_element_type=F32)
        gxs.append(gg[:, :MXU_DIM])
        gas.append(gg[:, MXU_DIM:])
    gx = jax.nn.sigmoid(jnp.concatenate(gxs, axis=1) + bgx)
    ga = jax.nn.sigmoid(jnp.concatenate(gas, axis=1) + bga)
    log_a = ga * (LRU_C * logsig)
    a = jnp.exp(log_a)
    mult = _sqrt_pos(-jnp.tanh(log_a) * (a * a + 1.0))
    return a, mult * gx * xr


def _lru_scan(a, b, a_ref, b_ref, hc_ref):
    ts = a.shape[0]
    outs = []
    for s in range(a.shape[1] // LANES):
        lanes = slice(s * LANES, (s + 1) * LANES)
        a_s, b_s = a[:, lanes], b[:, lanes]
        abuf, bbuf = a_ref.at[s], b_ref.at[s]
        d = 1
        while d < SUBLANES:
            abuf[SUBLANES:SUBLANES + ts, :] = a_s
            bbuf[SUBLANES:SUBLANES + ts, :] = b_s
            b_s = a_s * bbuf[SUBLANES - d:SUBLANES - d + ts, :] + b_s
            a_s = a_s * abuf[SUBLANES - d:SUBLANES - d + ts, :]
            d *= 2
        h = hc_ref[:, lanes]
        hs = []
        for q in range(ts // SUBLANES):
            rows = slice(q * SUBLANES, (q + 1) * SUBLANES)
            h = a_s[rows] * h + b_s[rows]
            hs.append(h)
        hc_ref[:, lanes] = jnp.broadcast_to(h[SUBLANES - 1:], (SUBLANES, LANES))
        outs.append(jnp.concatenate(hs, axis=0))
    return jnp.concatenate(outs, axis=1)


def _head_mask(shape, h):
    lane = lax.broadcasted_iota(jnp.int32, shape, 1)
    return (lane >= h * HEAD_DIM) & (lane < (h + 1) * HEAD_DIM)


def _head_masked_stack(blocks, n_heads):
    parts = []
    for h in range(n_heads):
        for blk in blocks:
            parts.append(jnp.where(_head_mask(blk.shape, h), blk, 0.0).astype(BF16))
    return jnp.concatenate(parts, axis=0)


def _mem_attention(qn, kcat, vcat):
    return _mem_softmax_pv(_mm_nt(qn, kcat), vcat)


def _mem_softmax_pv(s, vcat):
    parts = []
    for h in range(MEM_HEADS):
        sh = s[:, h * N_MEM:(h + 1) * N_MEM]
        m = jnp.max(sh, axis=-1, keepdims=True)
        p = jnp.exp(sh - m)
        den = jnp.sum(p, axis=-1, keepdims=True)
        parts.append((p * (1.0 / den)).astype(BF16))
    return jnp.dot(jnp.concatenate(parts, axis=1), vcat, preferred_element_type=F32)


def _cast_blocks_kernel(w_ref, o_ref):
    for b in range(o_ref.shape[0]):
        o_ref[b] = w_ref[:, b * MXU_DIM:(b + 1) * MXU_DIM].astype(BF16)


def _mem_kv_kernel(mem_ref, g_ref, w_ref, kg_ref, bd_ref, k_ref, v_ref, kt_ref, vt_ref):
    h = _mm(_rmsnorm(mem_ref[...], g_ref[0]), w_ref[0])
    k = _head_rmsnorm(h[:, :MEM_W], bd_ref[...], kg_ref[0])
    v = h[:, MEM_W:]
    k_ref[0] = k
    v_ref[0] = v
    for b in range(kt_ref.shape[1]):
        rows = slice(b * N_MEM, (b + 1) * N_MEM)
        kt_ref[0, b] = k[rows].T
        vt_ref[0, b] = v[rows].T


def _shared_kv_kernel(x_ref, g_ref, w_ref, kg_ref, bd_ref, cos_ref, sin_ref,
                      k_ref, v_ref):
    h = _mm(_rmsnorm(x_ref[...], g_ref[...]), w_ref[...])
    k = _head_rmsnorm(h[:, :KV_W], bd_ref[...], kg_ref[...])
    k_ref[...] = _rope(k, cos_ref[...], sin_ref[...])
    v_ref[...] = h[:, KV_W:]


def _mixer_a_kernel(x_ref, g_ref, win_ref, cw_ref, cb_ref, wg_ref, bgx_ref, bga_ref,
                    lru_ref, mqg_ref, bd_ref, mk_ref, mv_ref, wout_ref, rc0_ref, h0_ref,
                    xo_ref, hlast_ref, rctail_ref,
                    conv_s, tail_s, a_s, b_s, hc_s, kcat_s, vcat_s):
    ts = x_ref.shape[0]
    n_slabs = D_RNN // LANES

    @pl.when(pl.program_id(1) == 0)
    def _():
        for s in range(n_slabs):
            tail_s[s] = rc0_ref[0, :, s * LANES:(s + 1) * LANES]
        a_s[:, 0:SUBLANES, :] = jnp.ones((n_slabs, SUBLANES, LANES), F32)
        b_s[:, 0:SUBLANES, :] = jnp.zeros((n_slabs, SUBLANES, LANES), F32)
        hc_s[...] = jnp.broadcast_to(h0_ref[0], hc_s.shape)
        kcat_s[...] = _head_masked_stack([mk_ref[0]], MEM_HEADS)
        vcat_s[...] = _head_masked_stack([mv_ref[0]], MEM_HEADS)

    x = x_ref[...]
    hn = _rmsnorm(x, g_ref[...]).astype(BF16)

    def in_proj(lo, hi):
        return jnp.dot(hn, win_ref[:, lo:hi], preferred_element_type=F32)

    xr_pre = in_proj(D_RNN, 2 * D_RNN)
    _slab_stage(xr_pre, conv_s, 0, tail_s, 0)
    xr = jnp.concatenate(_slab_taps(ts, n_slabs, conv_s, 0, cw_ref, cb_ref, 0), axis=1)
    rctail_ref[0] = xr_pre[ts - SUBLANES:]
    qm = in_proj(2 * D_RNN, 2 * D_RNN + MEM_W)
    a, b = _lru_coeffs(xr, wg_ref, bgx_ref[...], bga_ref[...], _log_sigmoid(lru_ref[...]))
    qn = _head_rmsnorm(qm, bd_ref[...], mqg_ref[...]) * ATT_SCALE
    gate = in_proj(0, D_RNN)
    mo = _mem_attention(qn, kcat_s[...], vcat_s[...])
    h = _lru_scan(a, b, a_s, b_s, hc_s)
    hlast_ref[0] = h[ts - 1:ts]
    main = h * _gelu(gate)

    y = _mm(jnp.concatenate([main, mo], axis=1), wout_ref[...])
    xo_ref[...] = x + y


def _mixer_b_kernel(sink_ref, x_ref, g_ref, win_ref, qg_ref, cos_ref, sin_ref,
                    kcur_ref, kprev_ref, vcur_ref, vprev_ref,
                    mqg_ref, bd_ref, mk_ref, mv_ref, wout_ref,
                    xo_ref, kcat_s, vcat_s):
    ts = x_ref.shape[0]
    i = pl.program_id(1)

    @pl.when(i == 0)
    def _():
        kcat_s[...] = _head_masked_stack([mk_ref[0]], MEM_HEADS)
        vcat_s[...] = _head_masked_stack([mv_ref[0]], MEM_HEADS)

    x = x_ref[...]
    hn = _rmsnorm(x, g_ref[...]).astype(BF16)

    def in_proj(lo, hi):
        return jnp.dot(hn, win_ref[:, lo:hi], preferred_element_type=F32)

    q = _head_rmsnorm(in_proj(0, Q_W), bd_ref[...], qg_ref[...], f32_stat=False)
    q = (_rope(q, cos_ref[...], sin_ref[...]) * ATT_SCALE).astype(BF16)

    row = lax.broadcasted_iota(jnp.int32, (WINDOW, 2 * WINDOW), 0)
    kj = lax.broadcasted_iota(jnp.int32, (WINDOW, 2 * WINDOW), 1) - WINDOW
    mask_inner = (kj <= row) & (kj >= row - WINDOW)
    mask_first = (kj <= row) & (kj >= jnp.maximum(row - WINDOW, jnp.where(i > 0, -WINDOW, 0)))

    kt = kcur_ref[...]
    vt = vcur_ref[...]

    def block_scores(jb):
        lo, hi = jb * WINDOW, (jb + 1) * WINDOW
        if jb == 0:
            kp, vp = kprev_ref[...], vprev_ref[...]
        else:
            kp, vp = kt[lo - WINDOW:lo], vt[lo - WINDOW:lo]
        kcat = _head_masked_stack([kp, kt[lo:hi]], N_KV)
        vcat = _head_masked_stack([vp, vt[lo:hi]], N_KV)
        qs = jnp.concatenate([q[lo:hi, g * KV_W:(g + 1) * KV_W] for g in range(GROUP)], axis=0)
        s = lax.dot_general(qs, kcat, (((1,), (1,)), ((), ())), preferred_element_type=F32)
        return s, vcat

    n_blocks = ts // WINDOW
    mains = []
    nxt = block_scores(0)
    qm = in_proj(Q_W, Q_W + MEM_W)
    qn = _head_rmsnorm(qm, bd_ref[...], mqg_ref[...], f32_stat=False) * ATT_SCALE
    s_mem = _mm_nt(qn, kcat_s[...])
    for jb in range(n_blocks):
        s, vcat = nxt
        if jb + 1 < n_blocks:
            nxt = block_scores(jb + 1)
        mask = mask_first if jb == 0 else mask_inner
        prow = []
        for g in range(GROUP):
            pseg = []
            for kv in range(N_KV):
                seg = s[g * WINDOW:(g + 1) * WINDOW, kv * 2 * WINDOW:(kv + 1) * 2 * WINDOW]
                seg = jnp.where(mask, seg, NEG)
                sink = sink_ref[g * N_KV + kv]
                m = jnp.maximum(jnp.max(seg, axis=-1, keepdims=True), sink)
                p = jnp.exp(seg - m)
                den = jnp.sum(p, axis=-1, keepdims=True) + jnp.exp(sink - m)
                pseg.append((p * (1.0 / den)).astype(BF16))
            prow.append(jnp.concatenate(pseg, axis=1))
        o = jnp.dot(jnp.concatenate(prow, axis=0), vcat, preferred_element_type=F32)
        mains.append(jnp.concatenate([o[g * WINDOW:(g + 1) * WINDOW] for g in range(GROUP)], axis=1))
    main = mains[0] if len(mains) == 1 else jnp.concatenate(mains, axis=0)

    mo = _mem_softmax_pv(s_mem, vcat_s[...])

    y = _mm(jnp.concatenate([main, mo], axis=1), wout_ref[...])
    xo_ref[...] = x + y


def _ffn_kernel(x_ref, g_ref, wup_ref, cw_ref, cb_ref, wdn_ref,
                xo_ref, utail_ref, conv_s, tail_s, act_s, xres_s, *, tiles_per_seq):
    ts = x_ref.shape[0]
    slabs = FF_CHUNK // LANES
    blk_slabs = MXU_DIM // LANES
    blks = FF_CHUNK // MXU_DIM
    i = pl.program_id(0)
    slot = i % 2
    pslot = 1 - slot

    @pl.when(i == 0)
    def _():
        act_s[1] = jnp.zeros(act_s.shape[1:], act_s.dtype)
        xres_s[1] = jnp.zeros(xres_s.shape[1:], xres_s.dtype)

    @pl.when(i % tiles_per_seq == 0)
    def _():
        tail_s[...] = jnp.zeros_like(tail_s)

    x = x_ref[...]
    xres_s[slot] = x
    hn = _rmsnorm(x, g_ref[...]).astype(BF16)

    def chunk_cols(j):
        return [(c, ((j % 2) * 2 + half) * slabs) for half, c in enumerate((j, N_FF_CHUNKS + j))]

    def up_project(j):
        for c, work0 in chunk_cols(j):
            for b in range(blks):
                blk = c * blks + b
                u = jnp.dot(hn, wup_ref[blk], preferred_element_type=F32)
                _slab_stage(u, conv_s, work0 + b * blk_slabs, tail_s, c * slabs + b * blk_slabs)
                utail_ref[0, :, blk * MXU_DIM:(blk + 1) * MXU_DIM] = u[ts - SUBLANES:]

    def activation(j):
        cg, cv = [_slab_taps(ts, slabs, conv_s, work0, cw_ref, cb_ref, c * FF_CHUNK)
                  for c, work0 in chunk_cols(j)]
        return jnp.concatenate([_gelu(a) * b for a, b in zip(cg, cv)], axis=1).astype(BF16)

    def down_prev(n):
        return jnp.dot(act_s[pslot], wdn_ref[n], preferred_element_type=F32)

    n_dn = wdn_ref.shape[0]
    outs = [None] * n_dn
    outs[0] = down_prev(0)
    up_project(0)
    for j in range(N_FF_CHUNKS):
        if j + 1 < N_FF_CHUNKS:
            up_project(j + 1)
        else:
            outs[1] = down_prev(1)
        act_s[slot, :, j * FF_CHUNK:(j + 1) * FF_CHUNK] = activation(j)
    for n in range(2, n_dn):
        outs[n] = down_prev(n)
    xo_ref[...] = xres_s[pslot] + jnp.concatenate(outs, axis=1)


def _dec_in_a_kernel(x_ref, g_ref, win_ref, cw_ref, cb_ref, wg_ref, bgx_ref, bga_ref,
                     lru_ref, mqg_ref, bd_ref, b0_ref, b1_ref, b2_ref, h0_ref,
                     main_ref, qn_ref, hnew_ref, xrpre_ref):
    u = _mm(_rmsnorm(x_ref[...], g_ref[...]), win_ref[...])
    gate = u[:, :D_RNN]
    xr_pre = u[:, D_RNN:2 * D_RNN]
    qm = u[:, 2 * D_RNN:]
    xr = b0_ref[...] * cw_ref[0:1, :]
    xr = xr + b1_ref[...] * cw_ref[1:2, :]
    xr = xr + b2_ref[...] * cw_ref[2:3, :]
    xr = xr + xr_pre * cw_ref[3:4, :]
    xr = xr + cb_ref[...]
    a, b = _lru_coeffs(xr, wg_ref, bgx_ref[...], bga_ref[...], _log_sigmoid(lru_ref[...]))
    h = a * h0_ref[...] + b
    main_ref[...] = h * _gelu(gate)
    qn_ref[...] = _head_rmsnorm(qm, bd_ref[...], mqg_ref[...]) * ATT_SCALE
    hnew_ref[...] = h
    xrpre_ref[...] = xr_pre


def _dec_in_b_kernel(x_ref, g_ref, win_ref, qg_ref, cos_ref, sin_ref, mqg_ref, bd_ref,
                     q_ref, qn_ref):
    u = _mm(_rmsnorm(x_ref[...], g_ref[...]), win_ref[...])
    q = _head_rmsnorm(u[:, :Q_W], bd_ref[...], qg_ref[...])
    q_ref[...] = _rope(q, cos_ref[...], sin_ref[...]) * ATT_SCALE
    qn_ref[...] = _head_rmsnorm(u[:, Q_W:], bd_ref[...], mqg_ref[...]) * ATT_SCALE


DEC_HEAD_ROWS = 16


def _own_head_lanes(n_heads, width):
    row = lax.broadcasted_iota(jnp.int32, (DEC_HEAD_ROWS, width), 0)
    lane = lax.broadcasted_iota(jnp.int32, (DEC_HEAD_ROWS, width), 1)
    start = (row % (width // HEAD_DIM)) * HEAD_DIM
    return (lane >= start) & (lane < start + HEAD_DIM) & (row < n_heads)


def _dec_mem_attn_kernel(q_ref, kt_ref, vt_ref, o_ref):
    own = _own_head_lanes(MEM_HEADS, MEM_W)
    for s in range(q_ref.shape[0]):
        q_rows = jnp.broadcast_to(q_ref[s:s + 1, :], (DEC_HEAD_ROWS, MEM_W))
        qbd = jnp.where(own, q_rows, 0.0).astype(BF16)
        sc = jnp.dot(qbd, kt_ref[s].astype(BF16), preferred_element_type=F32)
        m = jnp.max(sc, axis=-1, keepdims=True)
        p = jnp.exp(sc - m)
        den = jnp.sum(p, axis=-1, keepdims=True)
        pn = (p * (1.0 / den)).astype(BF16)
        o_all = lax.dot_general(pn, vt_ref[s].astype(BF16), (((1,), (1,)), ((), ())),
                                preferred_element_type=F32)
        o_ref[s:s + 1, :] = jnp.sum(jnp.where(own, o_all, 0.0), axis=0, keepdims=True)


def _dec_swa_kernel(q_ref, kt_ref, vt_ref, kn_ref, vn_ref, sink_ref, o_ref):
    n_heads = GROUP * N_KV
    own = _own_head_lanes(n_heads, KV_W)
    grp = lax.broadcasted_iota(jnp.int32, (DEC_HEAD_ROWS, KV_W), 0) // N_KV
    sink = sink_ref[:, 0:1]
    for s in range(q_ref.shape[0]):
        q_rows = jnp.zeros((DEC_HEAD_ROWS, KV_W), F32)
        for g in range(GROUP):
            qg = jnp.broadcast_to(q_ref[s:s + 1, g * KV_W:(g + 1) * KV_W], (DEC_HEAD_ROWS, KV_W))
            q_rows = jnp.where(grp == g, qg, q_rows)
        qbd = jnp.where(own, q_rows, 0.0)
        s_buf = jnp.dot(qbd.astype(BF16), kt_ref[s].astype(BF16), preferred_element_type=F32)
        s_new = jnp.sum(qbd * kn_ref[s:s + 1, :], axis=-1, keepdims=True)
        m = jnp.maximum(jnp.maximum(jnp.max(s_buf, axis=-1, keepdims=True), s_new), sink)
        p_buf = jnp.exp(s_buf - m)
        p_new = jnp.exp(s_new - m)
        den = jnp.sum(p_buf, axis=-1, keepdims=True) + p_new + jnp.exp(sink - m)
        r = 1.0 / den
        o_all = lax.dot_general((p_buf * r).astype(BF16), vt_ref[s].astype(BF16),
                                (((1,), (1,)), ((), ())), preferred_element_type=F32)
        o_all = jnp.where(own, o_all + (p_new * r) * vn_ref[s:s + 1, :], 0.0)
        o_sum = o_all + pltpu.roll(o_all, 1, 0)
        o_sum = o_sum + pltpu.roll(o_sum, 2, 0)
        for g in range(GROUP):
            last = (g + 1) * N_KV - 1
            o_ref[s:s + 1, g * KV_W:(g + 1) * KV_W] = o_sum[last:last + 1, :]


def _dec_out_ffn_kernel(x_ref, main_ref, mo_ref, wout_ref, g_ref,
                        wug_ref, wuv_ref, cwg_ref, cwv_ref, cbg_ref, cbv_ref, wdn_ref,
                        sg_ref, sv_ref, snew_in_ref,
                        xo_ref, snew_ref,
                        xmid_s, hn_s, acc_s):
    del snew_in_ref
    j = pl.program_id(0)

    @pl.when(j == 0)
    def _():
        y = _mm(jnp.concatenate([main_ref[...], mo_ref[...]], axis=1), wout_ref[...])
        xmid = x_ref[...] + y
        xmid_s[...] = xmid
        hn_s[...] = _rmsnorm(xmid, g_ref[...]).astype(BF16)
        acc_s[...] = jnp.zeros_like(acc_s)

    hn = hn_s[...]
    up = lambda w_ref: jnp.concatenate(
        [jnp.dot(hn, w_ref[b], preferred_element_type=F32) for b in range(w_ref.shape[0])], axis=1)
    ug = up(wug_ref)
    uv = up(wuv_ref)
    sg0, sg1 = sg_ref[:, 0, :], sg_ref[:, 1, :]
    sv0, sv1 = sv_ref[:, 0, :], sv_ref[:, 1, :]
    gate_cols = pl.ds(pl.multiple_of(j * FF_CHUNK, FF_CHUNK), FF_CHUNK)
    value_cols = pl.ds(pl.multiple_of(D_FF + j * FF_CHUNK, FF_CHUNK), FF_CHUNK)
    snew_ref[:, 0, gate_cols] = sg1
    snew_ref[:, 1, gate_cols] = ug
    snew_ref[:, 0, value_cols] = sv1
    snew_ref[:, 1, value_cols] = uv
    cg = sg0 * cwg_ref[0:1, :] + sg1 * cwg_ref[1:2, :] + ug * cwg_ref[2:3, :] + cbg_ref[...]
    cv = sv0 * cwv_ref[0:1, :] + sv1 * cwv_ref[1:2, :] + uv * cwv_ref[2:3, :] + cbv_ref[...]
    act = (_gelu(cg) * cv).astype(BF16)
    for n in range(wdn_ref.shape[0]):
        acc_s[:, n * MXU_DIM:(n + 1) * MXU_DIM] += jnp.dot(act, wdn_ref[n],
                                                           preferred_element_type=F32)

    @pl.when(j == pl.num_programs(0) - 1)
    def _():
        xo_ref[...] = xmid_s[...] + acc_s[...]


def _const_spec(shape):
    nd = len(shape)
    return pl.BlockSpec(shape, lambda *_: (0,) * nd)


def _layer_spec(arr, layer):
    nd = arr.ndim - 1
    return pl.BlockSpec((None,) + arr.shape[1:], lambda *_: (layer,) + (0,) * nd)


def _resident_spec(op):
    return _layer_spec(*op) if isinstance(op, tuple) else _const_spec(op.shape)


def _operand(op):
    return op[0] if isinstance(op, tuple) else op


def _params(*sem):
    return pltpu.CompilerParams(dimension_semantics=sem, vmem_limit_bytes=VMEM_LIMIT_BYTES)


def _cast_blocks(w):
    layers, r, c = w.shape
    tile = CAST_TILE
    per = tile // MXU_DIM
    return pl.pallas_call(
        _cast_blocks_kernel,
        grid=(layers, r // tile, c // tile),
        in_specs=[pl.BlockSpec((None, tile, tile), lambda l, i, j: (l, i, j))],
        out_specs=pl.BlockSpec((None, per, tile, MXU_DIM), lambda l, i, j: (l, j, i, 0)),
        out_shape=jax.ShapeDtypeStruct((layers, c // MXU_DIM, r, MXU_DIM), BF16),
        compiler_params=_params("arbitrary", "arbitrary", "arbitrary"),
        name="cast_blocks",
    )(w)


def _mem_kv(mem2d, g, w, kg, bd):
    depth = w.shape[0]
    rows = mem2d.shape[0]
    bsz = rows // N_MEM
    out = jax.ShapeDtypeStruct((depth, rows, MEM_W), F32)
    out_t = jax.ShapeDtypeStruct((depth, bsz, MEM_W, N_MEM), F32)
    return pl.pallas_call(
        _mem_kv_kernel,
        grid=(depth,),
        in_specs=[
            _const_spec(mem2d.shape),
            pl.BlockSpec((1, 1, D_MODEL), lambda l: (l, 0, 0)),
            pl.BlockSpec((1, D_MODEL, 2 * MEM_W), lambda l: (l, 0, 0)),
            pl.BlockSpec((1, 1, MEM_W), lambda l: (l, 0, 0)),
            _const_spec(bd.shape),
        ],
        out_specs=[pl.BlockSpec((1, rows, MEM_W), lambda l: (l, 0, 0))] * 2
        + [pl.BlockSpec((1, bsz, MEM_W, N_MEM), lambda l: (l, 0, 0, 0))] * 2,
        out_shape=[out, out, out_t, out_t],
        compiler_params=_params("arbitrary"),
        name="mem_kv",
    )(mem2d, g, w, kg, bd)


def _shared_kv(x2d, g, w, kg, bd, cos_t, sin_t, ts):
    rows = x2d.shape[0]
    tab_blocks = cos_t.shape[0] // ts
    out = jax.ShapeDtypeStruct((rows, KV_W), F32)
    return pl.pallas_call(
        _shared_kv_kernel,
        grid=(rows // ts,),
        in_specs=[
            pl.BlockSpec((ts, D_MODEL), lambda i: (i, 0)),
            _const_spec(g.shape), _const_spec(w.shape), _const_spec(kg.shape),
            _const_spec(bd.shape),
            pl.BlockSpec((ts, LANES), lambda i: (i % tab_blocks, 0)),
            pl.BlockSpec((ts, LANES), lambda i: (i % tab_blocks, 0)),
        ],
        out_specs=[pl.BlockSpec((ts, KV_W), lambda i: (i, 0))] * 2,
        out_shape=[out, out],
        compiler_params=_params("arbitrary"),
        name="shared_kv",
    )(x2d, g, w, kg, bd, cos_t, sin_t)


def _mixer_a(x2d, bsz, g, win, cw, cb, wg, bgx, bga, lru, mqg, bd, mk, mv, wout, rc0, h0):
    rows = x2d.shape[0]
    ts = TS_MIX
    nt = rows // bsz // ts
    consts = (g, win, cw, cb, wg, bgx, bga, lru, mqg, bd)
    per_b3 = lambda b, i: (b, 0, 0)
    return pl.pallas_call(
        _mixer_a_kernel,
        grid=(bsz, nt),
        in_specs=[pl.BlockSpec((ts, D_MODEL), lambda b, i: (b * nt + i, 0))]
        + [_resident_spec(c) for c in consts]
        + [pl.BlockSpec((1, N_MEM, MEM_W), per_b3)] * 2
        + [_resident_spec(wout),
           pl.BlockSpec((1, SUBLANES, D_RNN), per_b3),
           pl.BlockSpec((1, 1, D_RNN), per_b3)],
        out_specs=[
            pl.BlockSpec((ts, D_MODEL), lambda b, i: (b * nt + i, 0)),
            pl.BlockSpec((1, 1, D_RNN), per_b3),
            pl.BlockSpec((1, SUBLANES, D_RNN), per_b3),
        ],
        out_shape=[
            jax.ShapeDtypeStruct((rows, D_MODEL), F32),
            jax.ShapeDtypeStruct((bsz, 1, D_RNN), F32),
            jax.ShapeDtypeStruct((bsz, SUBLANES, D_RNN), F32),
        ],
        scratch_shapes=[
            pltpu.VMEM((D_RNN // LANES, SUBLANES + ts, LANES), F32),
            pltpu.VMEM((D_RNN // LANES, SUBLANES, LANES), F32),
            pltpu.VMEM((D_RNN // LANES, SUBLANES + ts, LANES), F32),
            pltpu.VMEM((D_RNN // LANES, SUBLANES + ts, LANES), F32),
            pltpu.VMEM((SUBLANES, D_RNN), F32),
            pltpu.VMEM((MEM_HEADS * N_MEM, MEM_W), BF16),
            pltpu.VMEM((MEM_HEADS * N_MEM, MEM_W), BF16),
        ],
        compiler_params=_params("arbitrary", "arbitrary"),
        name="mixer_a",
    )(x2d, *[_operand(c) for c in consts], mk, mv, _operand(wout), rc0, h0)


def _mixer_b(x2d, bsz, sink_tab, g, win, qg, cos_t, sin_t, ksh, vsh, mqg, bd, mk, mv, wout):
    rows = x2d.shape[0]
    ts = TS_MIX
    nt = rows // bsz // ts
    wpt = ts // WINDOW
    cur = lambda b, i: (b * nt + i, 0)
    prev = lambda b, i: (jnp.maximum((b * nt + i) * wpt - 1, 0), 0)
    table = lambda b, i: (i, 0)
    per_b3 = lambda b, i: (b, 0, 0)
    return pl.pallas_call(
        _mixer_b_kernel,
        grid=(bsz, nt),
        in_specs=[
            pl.BlockSpec(memory_space=pltpu.SMEM),
            pl.BlockSpec((ts, D_MODEL), cur),
            _const_spec(g.shape), _resident_spec(win), _const_spec(qg.shape),
            pl.BlockSpec((ts, LANES), table),
            pl.BlockSpec((ts, LANES), table),
            pl.BlockSpec((ts, KV_W), cur), pl.BlockSpec((WINDOW, KV_W), prev),
            pl.BlockSpec((ts, KV_W), cur), pl.BlockSpec((WINDOW, KV_W), prev),
            _const_spec(mqg.shape), _const_spec(bd.shape),
            pl.BlockSpec((1, N_MEM, MEM_W), per_b3), pl.BlockSpec((1, N_MEM, MEM_W), per_b3),
            _resident_spec(wout),
        ],
        out_specs=pl.BlockSpec((ts, D_MODEL), cur),
        out_shape=jax.ShapeDtypeStruct((rows, D_MODEL), F32),
        scratch_shapes=[
            pltpu.VMEM((MEM_HEADS * N_MEM, MEM_W), BF16),
            pltpu.VMEM((MEM_HEADS * N_MEM, MEM_W), BF16),
        ],
        compiler_params=_params("arbitrary", "arbitrary"),
        name="mixer_b",
    )(sink_tab, x2d, g, _operand(win), qg, cos_t, sin_t, ksh, ksh, vsh, vsh, mqg, bd, mk, mv,
      _operand(wout))


def _ffn(x2d, bsz, layer, g, wup, cw, cb, wdn):
    rows = x2d.shape[0]
    ts = TS_FFN
    nt = rows // ts
    tiles_per_seq = rows // bsz // ts
    last = nt - 1
    return pl.pallas_call(
        functools.partial(_ffn_kernel, tiles_per_seq=tiles_per_seq),
        grid=(nt + 1,),
        in_specs=[pl.BlockSpec((ts, D_MODEL), lambda i: (jnp.minimum(i, last), 0)),
                  _const_spec(g.shape), _layer_spec(wup, layer), _const_spec(cw.shape),
                  _const_spec(cb.shape), _layer_spec(wdn, layer)],
        out_specs=[
            pl.BlockSpec((ts, D_MODEL), lambda i: (jnp.maximum(i - 1, 0), 0)),
            pl.BlockSpec((1, SUBLANES, 2 * D_FF),
                         lambda i: (jnp.minimum(i, last) // tiles_per_seq, 0, 0)),
        ],
        out_shape=[
            jax.ShapeDtypeStruct((rows, D_MODEL), F32),
            jax.ShapeDtypeStruct((bsz, SUBLANES, 2 * D_FF), F32),
        ],
        scratch_shapes=[pltpu.VMEM((4 * FF_CHUNK // LANES, SUBLANES + ts, LANES), F32),
                        pltpu.VMEM((2 * D_FF // LANES, SUBLANES, LANES), F32),
                        pltpu.VMEM((2, ts, D_FF), BF16),
                        pltpu.VMEM((2, ts, D_MODEL), F32)],
        compiler_params=_params("arbitrary"),
        name="ffn",
    )(x2d, g, wup, cw, cb, wdn)


def _dec_in_a(x, g, win, cw, cb, wg, bgx, bga, lru, mqg, bd, rc, h0):
    n = x.shape[0]
    consts = (x, g, win, cw, cb, wg, bgx, bga, lru, mqg, bd)
    buf = lambda j: pl.BlockSpec((None, n, D_RNN), lambda i: (j, 0, 0))
    return pl.pallas_call(
        _dec_in_a_kernel,
        grid=(1,),
        in_specs=[_resident_spec(c) for c in consts]
        + [buf(0), buf(1), buf(2), _const_spec(h0.shape)],
        out_specs=[_const_spec((n, D_RNN)), _const_spec((n, MEM_W)),
                   _const_spec((n, D_RNN)), _const_spec((n, D_RNN))],
        out_shape=[jax.ShapeDtypeStruct((n, D_RNN), F32), jax.ShapeDtypeStruct((n, MEM_W), F32),
                   jax.ShapeDtypeStruct((n, D_RNN), F32), jax.ShapeDtypeStruct((n, D_RNN), F32)],
        compiler_params=_params("arbitrary"),
        name="dec_in_a",
    )(*[_operand(c) for c in consts], rc, rc, rc, h0)


def _dec_in_b(x, g, win, qg, cos_t, sin_t, mqg, bd):
    n = x.shape[0]
    args = (x, g, win, qg, cos_t, sin_t, mqg, bd)
    return pl.pallas_call(
        _dec_in_b_kernel,
        grid=(1,),
        in_specs=[_resident_spec(a) for a in args],
        out_specs=[_const_spec((n, Q_W)), _const_spec((n, MEM_W))],
        out_shape=[jax.ShapeDtypeStruct((n, Q_W), F32), jax.ShapeDtypeStruct((n, MEM_W), F32)],
        compiler_params=_params("arbitrary"),
        name="dec_in_b",
    )(*[_operand(a) for a in args])


def _dec_mem_attn(qn, ckt, cvt, layer):
    n = qn.shape[0]
    sb = DEC_SEQ_BLOCK
    return pl.pallas_call(
        _dec_mem_attn_kernel,
        grid=(n // sb,),
        in_specs=[pl.BlockSpec((sb, MEM_W), lambda i: (i, 0)),
                  pl.BlockSpec((None, sb, MEM_W, N_MEM), lambda i: (layer, i, 0, 0)),
                  pl.BlockSpec((None, sb, MEM_W, N_MEM), lambda i: (layer, i, 0, 0))],
        out_specs=pl.BlockSpec((sb, MEM_W), lambda i: (i, 0)),
        out_shape=jax.ShapeDtypeStruct((n, MEM_W), F32),
        compiler_params=_params("arbitrary"),
        name="dec_mem_attn",
    )(qn, ckt, cvt)


def _dec_swa(q, kbt, vbt, kn, vn, sink_rows):
    n = q.shape[0]
    wb = kbt.shape[2]
    sb = DEC_SEQ_BLOCK
    assert wb <= WINDOW
    return pl.pallas_call(
        _dec_swa_kernel,
        grid=(n // sb,),
        in_specs=[pl.BlockSpec((sb, Q_W), lambda i: (i, 0)),
                  pl.BlockSpec((sb, KV_W, wb), lambda i: (i, 0, 0)),
                  pl.BlockSpec((sb, KV_W, wb), lambda i: (i, 0, 0)),
                  pl.BlockSpec((sb, KV_W), lambda i: (i, 0)),
                  pl.BlockSpec((sb, KV_W), lambda i: (i, 0)),
                  _const_spec(sink_rows.shape)],
        out_specs=pl.BlockSpec((sb, Q_W), lambda i: (i, 0)),
        out_shape=jax.ShapeDtypeStruct((n, Q_W), F32),
        compiler_params=_params("arbitrary"),
        name="dec_swa",
    )(q, kbt, vbt, kn, vn, sink_rows)


def _dec_out_ffn(x, main, mo, wout, wout_layer, layer, g, wup, cw, cb, wdn, st, st_new):
    n = x.shape[0]
    nch = N_FF_CHUNKS
    blks = FF_CHUNK // MXU_DIM
    lo = lambda j: (0, j)
    hi = lambda j: (0, nch + j)
    state = lambda off: pl.BlockSpec((None, n, CONV_F - 1, FF_CHUNK),
                                     lambda j: (layer, 0, 0, off + j))
    up_blocks = lambda off: pl.BlockSpec((None, blks, D_MODEL, MXU_DIM),
                                         lambda j: (layer, off + j, 0, 0))
    operands = (x, main, mo, wout, g, wup, wup, cw, cw, cb, cb, wdn, st, st, st_new)
    return pl.pallas_call(
        _dec_out_ffn_kernel,
        grid=(nch,),
        in_specs=[_const_spec(x.shape), _const_spec(main.shape), _const_spec(mo.shape),
                  _layer_spec(wout, wout_layer), _const_spec(g.shape),
                  up_blocks(0), up_blocks(nch),
                  pl.BlockSpec((CONV_F, FF_CHUNK), lo), pl.BlockSpec((CONV_F, FF_CHUNK), hi),
                  pl.BlockSpec((1, FF_CHUNK), lo), pl.BlockSpec((1, FF_CHUNK), hi),
                  pl.BlockSpec((None, wdn.shape[1], FF_CHUNK, MXU_DIM), lambda j: (layer, 0, j, 0)),
                  state(0), state(nch),
                  pl.BlockSpec(memory_space=pl.ANY)],
        out_specs=[_const_spec((n, D_MODEL)),
                   pl.BlockSpec((None, n, CONV_F - 1, 2 * D_FF), lambda j: (layer, 0, 0, 0))],
        out_shape=[jax.ShapeDtypeStruct((n, D_MODEL), F32),
                   jax.ShapeDtypeStruct(st_new.shape, F32)],
        input_output_aliases={len(operands) - 1: 1},
        scratch_shapes=[pltpu.VMEM((n, D_MODEL), F32), pltpu.VMEM((n, D_MODEL), BF16),
                        pltpu.VMEM((n, D_MODEL), F32)],
        compiler_params=_params("arbitrary"),
        name="dec_out_ffn",
    )(*operands)


def _rope_tables(pos):
    half = HEAD_DIM // 2
    inv = ROPE_THETA ** (-jnp.arange(half, dtype=F32) / half)
    ang = pos.astype(F32)[:, None] * inv[None, :]
    cos = jnp.cos(ang)
    sin = jnp.sin(ang)
    reps = LANES // HEAD_DIM
    cos_t = jnp.tile(jnp.concatenate([cos, cos], axis=1), (1, reps))
    sin_t = jnp.tile(jnp.concatenate([-sin, sin], axis=1), (1, reps))
    return cos_t, sin_t


def _block_diag_gates(wx, wa):
    per = MXU_DIM // HEAD_DIM
    eye = jnp.eye(per, dtype=F32)

    def bd(w):
        w4 = w.reshape(RNN_BLOCKS // per, per, HEAD_DIM, HEAD_DIM)
        return jnp.einsum('ckij,kK->ckiKj', w4, eye).reshape(RNN_BLOCKS // per, MXU_DIM, MXU_DIM)

    return jnp.concatenate([bd(wx), bd(wa)], axis=2).astype(BF16)


def kernel(x_prompt, x_sample, state_rglru_h, state_rglru_conv, state_ffn_conv, cache_swa_k, cache_swa_v, cache_mem_k, cache_mem_v, mem_prompt, norm_mix_g, norm_ffn_g, w_in_a, rnn_conv_w, rnn_conv_b, w_gate_x, b_gate_x, w_gate_a, b_gate_a, lru_param, w_in_b, q_norm_g, sinks, kv_norm_g, w_kv, k_norm_g, mem_norm_g, w_mem_kv, mem_q_norm_g, mem_k_norm_g, w_out, w_ffn_up, ffn_conv_w, ffn_conv_b, w_ffn_down):
    bsz, seq, _ = x_prompt.shape
    dbsz = x_sample.shape[0]
    depth = norm_mix_g.shape[0]
    n_a = w_in_a.shape[0]
    assert x_sample.shape[1] == 1
    assert seq % TS_MIX == 0 and seq % TS_FFN == 0 and seq % TS_KV == 0 and TS_MIX % WINDOW == 0

    n_b = w_in_b.shape[0]
    wq = w_in_b[:, :, :Q_W].astype(BF16).reshape(n_b, D_MODEL, N_KV, GROUP, HEAD_DIM)
    wq = wq.transpose(0, 1, 3, 2, 4).reshape(n_b, D_MODEL, Q_W)
    wo_main = w_out[n_a:, :Q_W].astype(BF16).reshape(n_b, N_KV, GROUP, HEAD_DIM, D_MODEL)
    wo_main = wo_main.transpose(0, 2, 1, 3, 4).reshape(n_b, Q_W, D_MODEL)
    bd =(jnp.kron(jnp.eye(MXU_DIM // HEAD_DIM, dtype=F32),
                   jnp.ones((HEAD_DIM, HEAD_DIM), F32)) / HEAD_DIM).astype(BF16)

    row = lambda v: v.reshape(1, -1)
    tile_row = lambda v, n: jnp.tile(v, n).reshape(1, -1)
    w_in_a_b = w_in_a.astype(BF16)
    w_in_b_b = jnp.concatenate([wq, w_in_b[:, :, Q_W:].astype(BF16)], axis=2)
    w_out_b = w_out.astype(BF16)
    w_out_perm_b = jnp.concatenate([wo_main, w_out_b[n_a:, Q_W:]], axis=1)
    w_kv_b = w_kv.astype(BF16)
    w_mem_b = w_mem_kv.astype(BF16)
    wup_b = _cast_blocks(w_ffn_up)
    wdn_b = _cast_blocks(w_ffn_down)
    fcw = ffn_conv_w
    fcb = ffn_conv_b.reshape(depth, 1, 2 * D_FF)
    wg_b = jnp.stack([_block_diag_gates(w_gate_x[l], w_gate_a[l]) for l in range(n_a)])
    sink_gk = sinks.reshape(-1, N_KV, GROUP).transpose(0, 2, 1)

    cos_p, sin_p = _rope_tables(jnp.arange(seq, dtype=jnp.int32))
    pos_s = PAST_LEN + jnp.zeros((dbsz,), jnp.int32)
    cos_s, sin_s = _rope_tables(pos_s)

    mem2d = mem_prompt.reshape(bsz * N_MEM, D_MODEL)
    pmk, pmv, pmk_t, pmv_t = _mem_kv(
        mem2d, mem_norm_g.reshape(depth, 1, D_MODEL), w_mem_b,
        jnp.tile(mem_k_norm_g, (1, MEM_HEADS)).reshape(depth, 1, MEM_W), bd)
    pmk4 = pmk.reshape(depth, bsz, N_MEM, MEM_W)
    pmv4 = pmv.reshape(depth, bsz, N_MEM, MEM_W)

    x = x_prompt.reshape(bsz * seq, D_MODEL)
    zeros_rc = jnp.zeros((bsz, SUBLANES, D_RNN), F32)
    zeros_h = jnp.zeros((bsz, 1, D_RNN), F32)
    p_h, p_rc, p_fc = [], [], []
    ksh = vsh = None
    for l in range(depth):
        mqg = tile_row(mem_q_norm_g[l], MEM_HEADS)
        if l < n_a:
            x, hl, rct = _mixer_a(
                x, bsz, row(norm_mix_g[l]), (w_in_a_b, l), rnn_conv_w[l], row(rnn_conv_b[l]),
                (wg_b, l), row(b_gate_x[l]), row(b_gate_a[l]), row(lru_param[l]), mqg, bd,
                pmk4[l], pmv4[l], (w_out_b, l), zeros_rc, zeros_h)
            p_h.append(hl.reshape(bsz, D_RNN))
            p_rc.append(rct[:, SUBLANES - (CONV_A - 1):])
        else:
            j = l - n_a
            x = _mixer_b(
                x, bsz, sink_gk[j].reshape(-1), row(norm_mix_g[l]), (w_in_b_b, j),
                tile_row(q_norm_g[j], N_Q), cos_p, sin_p, ksh, vsh, mqg, bd,
                pmk4[l], pmv4[l], (w_out_perm_b, j))
        x, ut = _ffn(x, bsz, l, row(norm_ffn_g[l]), wup_b, fcw[l], fcb[l], wdn_b)
        p_fc.append(ut[:, SUBLANES - (CONV_F - 1):])
        if l == n_a - 1:
            ksh, vsh = _shared_kv(x, row(kv_norm_g), w_kv_b, tile_row(k_norm_g, N_KV), bd,
                                  cos_p, sin_p, TS_KV)
    y_prompt = x.reshape(bsz, seq, D_MODEL)
    keep = min(WINDOW, seq)
    p_k = ksh.reshape(bsz, seq, KV_W)[:, seq - keep:].reshape(bsz, keep, N_KV, HEAD_DIM)
    p_v = vsh.reshape(bsz, seq, KV_W)[:, seq - keep:].reshape(bsz, keep, N_KV, HEAD_DIM)
    to_cache = lambda t: t.reshape(depth, bsz, MEM_HEADS, HEAD_DIM, N_MEM).transpose(0, 1, 4, 2, 3)
    p_mem_k = to_cache(pmk_t)
    p_mem_v = to_cache(pmv_t)

    xs = x_sample.reshape(dbsz, D_MODEL)
    cmk = cache_mem_k.transpose(0, 1, 3, 4, 2).reshape(depth, dbsz, MEM_W, N_MEM)
    cmv = cache_mem_v.transpose(0, 1, 3, 4, 2).reshape(depth, dbsz, MEM_W, N_MEM)
    wb = cache_swa_k.shape[1]
    ckb = cache_swa_k.transpose(0, 2, 3, 1).reshape(dbsz, KV_W, wb)
    cvb = cache_swa_v.transpose(0, 2, 3, 1).reshape(dbsz, KV_W, wb)
    s_h, s_rc = [], []
    s_fc = jnp.zeros_like(state_ffn_conv)
    kn = vn = None
    for l in range(depth):
        mqg = tile_row(mem_q_norm_g[l], MEM_HEADS)
        if l < n_a:
            main, qn, hnew, xrpre = _dec_in_a(
                xs, row(norm_mix_g[l]), (w_in_a_b, l), rnn_conv_w[l], row(rnn_conv_b[l]),
                (wg_b, l), row(b_gate_x[l]), row(b_gate_a[l]), row(lru_param[l]), mqg, bd,
                state_rglru_conv[l].transpose(1, 0, 2), state_rglru_h[l])
            s_h.append(hnew)
            s_rc.append(jnp.concatenate([state_rglru_conv[l][:, 1:], xrpre[:, None, :]], axis=1))
            wo, wo_layer = w_out_b, l
        else:
            j = l - n_a
            q, qn = _dec_in_b(xs, row(norm_mix_g[l]), (w_in_b_b, j), tile_row(q_norm_g[j], N_Q),
                              cos_s, sin_s, mqg, bd)
            sink_rows = jnp.zeros((DEC_HEAD_ROWS, LANES), F32).at[:N_Q].set(
                jnp.broadcast_to(sink_gk[j].reshape(N_Q, 1), (N_Q, LANES)))
            main = _dec_swa(q, ckb, cvb, kn, vn, sink_rows)
            wo, wo_layer = w_out_perm_b, j
        mo = _dec_mem_attn(qn, cmk, cmv, l)
        xs, s_fc = _dec_out_ffn(xs, main, mo, wo, wo_layer, l, row(norm_ffn_g[l]), wup_b,
                                fcw[l], fcb[l], wdn_b, state_ffn_conv, s_fc)
        if l == n_a - 1:
            kn, vn = _shared_kv(xs, row(kv_norm_g), w_kv_b, tile_row(k_norm_g, N_KV), bd,
                                cos_s, sin_s, dbsz)
    y_sample = xs.reshape(dbsz, 1, D_MODEL)
    s_k = kn.reshape(dbsz, 1, N_KV, HEAD_DIM)
    s_v = vn.reshape(dbsz, 1, N_KV, HEAD_DIM)

    return (y_prompt, y_sample, jnp.stack(p_h), jnp.stack(p_rc), jnp.stack(p_fc), p_k, p_v,
            p_mem_k, p_mem_v, jnp.stack(s_h), jnp.stack(s_rc), s_fc, s_k, s_v)
```

```python
import functools
import math

import jax
import jax.numpy as jnp
from jax import lax
from jax.experimental import pallas as pl
from jax.experimental.pallas import tpu as pltpu

F32 = jnp.float32
BF16 = jnp.bfloat16

D_MODEL = 1024
HEAD_DIM = 64
MEM_HEADS = 4
MEM_W = MEM_HEADS * HEAD_DIM
N_MEM = 256
D_RNN = D_MODEL - MEM_W
RNN_BLOCKS = D_RNN // HEAD_DIM
CONV_A = 4
LRU_C = 8.0
N_Q = D_RNN // HEAD_DIM
N_KV = 4
GROUP = N_Q // N_KV
Q_W = N_Q * HEAD_DIM
KV_W = N_KV * HEAD_DIM
WINDOW = 128
ROPE_THETA = 10000.0
D_FF = 3 * D_MODEL
CONV_F = 3
EPS = 1e-6
NEG = -1e30
ATT_SCALE = HEAD_DIM ** -0.5
PAST_LEN = 8192

SUBLANES = 8
LANES = 128
MXU_DIM = 256
VMEM_LIMIT_BYTES = 56 * 1024 * 1024

TS_MIX = 1024
TS_FFN = 512
TS_KV = 1024
FF_CHUNK = 512
N_FF_CHUNKS = D_FF // FF_CHUNK
DEC_SEQ_BLOCK = 16
CAST_TILE = 1024


def _mm(a, b):
    return jnp.dot(a.astype(BF16), b, preferred_element_type=F32)


def _mm_nt(a, b):
    return lax.dot_general(a.astype(BF16), b, (((1,), (1,)), ((), ())),
                           preferred_element_type=F32)


def _mm_split(a, b):
    hi = a.astype(BF16)
    lo = (a - hi.astype(F32)).astype(BF16)
    return (jnp.dot(hi, b, preferred_element_type=F32)
            + jnp.dot(lo, b, preferred_element_type=F32))


def _rmsnorm(x, g):
    ms = jnp.mean(x * x, axis=-1, keepdims=True)
    return x * lax.rsqrt(ms + EPS) * g


def _head_rmsnorm(x, bd, g, f32_stat=True):
    mean_sq = _mm_split if f32_stat else _mm
    parts = []
    for c in range(x.shape[1] // MXU_DIM):
        xc = x[:, c * MXU_DIM:(c + 1) * MXU_DIM]
        ms = mean_sq(xc * xc, bd)
        parts.append(xc * lax.rsqrt(ms + EPS))
    y = parts[0] if len(parts) == 1 else jnp.concatenate(parts, axis=1)
    return y * g


def _tile_lanes(t, width):
    reps = width // t.shape[1]
    return t if reps == 1 else jnp.concatenate([t] * reps, axis=1)


def _rope(x, cos_t, sin_t):
    w = x.shape[1]
    lane = lax.broadcasted_iota(jnp.int32, x.shape, 1)
    first = (lane % HEAD_DIM) < (HEAD_DIM // 2)
    swapped = jnp.where(first, pltpu.roll(x, w - HEAD_DIM // 2, 1),
                        pltpu.roll(x, HEAD_DIM // 2, 1))
    return x * _tile_lanes(cos_t, w) + swapped * _tile_lanes(sin_t, w)


def _gelu(x):
    c = math.sqrt(2.0 / math.pi)
    return x * (0.5 * (1.0 + jnp.tanh(c * (x + 0.044715 * (x * x * x)))))


def _log_sigmoid(x):
    return jnp.minimum(x, 0.0) - jnp.log1p(jnp.exp(-jnp.abs(x)))


def _slab_stage(x, work_ref, work0, tail_ref, tail0):
    ts = x.shape[0]
    for s in range(x.shape[1] // LANES):
        xs = x[:, s * LANES:(s + 1) * LANES]
        buf = work_ref.at[work0 + s]
        buf[0:SUBLANES, :] = tail_ref[tail0 + s]
        buf[SUBLANES:SUBLANES + ts, :] = xs
        tail_ref[tail0 + s] = xs[ts - SUBLANES:]


def _slab_taps(ts, n_slabs, work_ref, work0, w_ref, b_ref, col0):
    k = w_ref.shape[0]
    outs = []
    for s in range(n_slabs):
        lanes = slice(col0 + s * LANES, col0 + (s + 1) * LANES)
        buf = work_ref.at[work0 + s]
        acc = buf[SUBLANES - (k - 1):SUBLANES - (k - 1) + ts, :] * w_ref[0:1, lanes]
        for j in range(1, k):
            off = SUBLANES - (k - 1 - j)
            acc = acc + buf[off:off + ts, :] * w_ref[j:j + 1, lanes]
        outs.append(acc + b_ref[:, lanes])
    return outs


def _sqrt_pos(x):
    return jnp.where(x > 0.0, x * lax.rsqrt(x), 0.0)


def _lru_coeffs(xr, wg_ref, bgx, bga, logsig):
    xb = xr.astype(BF16)
    gxs, gas = [], []
    for c in range(D_RNN // MXU_DIM):
        gg = jnp.dot(xb[:, c * MXU_DIM:(c + 1) * MXU_DIM], wg_ref[c],
                     preferred_element_type=F32)
        gxs.append(gg[:, :MXU_DIM])
        gas.append(gg[:, MXU_DIM:])
    gx = jax.nn.sigmoid(jnp.concatenate(gxs, axis=1) + bgx)
    ga = jax.nn.sigmoid(jnp.concatenate(gas, axis=1) + bga)
    log_a = ga * (LRU_C * logsig)
    a = jnp.exp(log_a)
    mult = _sqrt_pos(-jnp.tanh(log_a) * (a * a + 1.0))
    return a, mult * gx * xr


def _lru_scan(a, b, a_ref, b_ref, hc_ref):
    ts = a.shape[0]
    outs = []
    for s in range(a.shape[1] // LANES):
        lanes = slice(s * LANES, (s + 1) * LANES)
        a_s, b_s = a[:, lanes], b[:, lanes]
        abuf, bbuf = a_ref.at[s], b_ref.at[s]
        d = 1
        while d < SUBLANES:
            abuf[SUBLANES:SUBLANES + ts, :] = a_s
            bbuf[SUBLANES:SUBLANES + ts, :] = b_s
            b_s = a_s * bbuf[SUBLANES - d:SUBLANES - d + ts, :] + b_s
            a_s = a_s * abuf[SUBLANES - d:SUBLANES - d + ts, :]
            d *= 2
        h = hc_ref[:, lanes]
        hs = []
        for q in range(ts // SUBLANES):
            rows = slice(q * SUBLANES, (q + 1) * SUBLANES)
            h = a_s[rows] * h + b_s[rows]
            hs.append(h)
        hc_ref[:, lanes] = jnp.broadcast_to(h[SUBLANES - 1:], (SUBLANES, LANES))
        outs.append(jnp.concatenate(hs, axis=0))
    return jnp.concatenate(outs, axis=1)


def _head_mask(shape, h):
    lane = lax.broadcasted_iota(jnp.int32, shape, 1)
    return (lane >= h * HEAD_DIM) & (lane < (h + 1) * HEAD_DIM)


def _head_masked_stack(blocks, n_heads):
    parts = []
    for h in range(n_heads):
        for blk in blocks:
            parts.append(jnp.where(_head_mask(blk.shape, h), blk, 0.0).astype(BF16))
    return jnp.concatenate(parts, axis=0)


def _mem_attention(qn, kcat, vcat):
    return _mem_softmax_pv(_mm_nt(qn, kcat), vcat)


def _mem_softmax_pv(s, vcat):
    parts = []
    for h in range(MEM_HEADS):
        sh = s[:, h * N_MEM:(h + 1) * N_MEM]
        m = jnp.max(sh, axis=-1, keepdims=True)
        p = jnp.exp(sh - m)
        den = jnp.sum(p, axis=-1, keepdims=True)
        parts.append((p * (1.0 / den)).astype(BF16))
    return jnp.dot(jnp.concatenate(parts, axis=1), vcat, preferred_element_type=F32)


def _cast_blocks_kernel(w_ref, o_ref):
    for b in range(o_ref.shape[0]):
        o_ref[b] = w_ref[:, b * MXU_DIM:(b + 1) * MXU_DIM].astype(BF16)


def _mem_kv_kernel(mem_ref, g_ref, w_ref, kg_ref, bd_ref, k_ref, v_ref, kt_ref, vt_ref):
    h = _mm(_rmsnorm(mem_ref[...], g_ref[0]), w_ref[0])
    k = _head_rmsnorm(h[:, :MEM_W], bd_ref[...], kg_ref[0])
    v = h[:, MEM_W:]
    k_ref[0] = k
    v_ref[0] = v
    for b in range(kt_ref.shape[1]):
        rows = slice(b * N_MEM, (b + 1) * N_MEM)
        kt_ref[0, b] = k[rows].T
        vt_ref[0, b] = v[rows].T


def _shared_kv_kernel(x_ref, g_ref, w_ref, kg_ref, bd_ref, cos_ref, sin_ref,
                      k_ref, v_ref):
    h = _mm(_rmsnorm(x_ref[...], g_ref[...]), w_ref[...])
    k = _head_rmsnorm(h[:, :KV_W], bd_ref[...], kg_ref[...])
    k_ref[...] = _rope(k, cos_ref[...], sin_ref[...])
    v_ref[...] = h[:, KV_W:]


def _mixer_a_kernel(x_ref, g_ref, win_ref, cw_ref, cb_ref, wg_ref, bgx_ref, bga_ref,
                    lru_ref, mqg_ref, bd_ref, mk_ref, mv_ref, wout_ref, rc0_ref, h0_ref,
                    xo_ref, hlast_ref, rctail_ref,
                    conv_s, tail_s, a_s, b_s, hc_s, kcat_s, vcat_s):
    ts = x_ref.shape[0]
    n_slabs = D_RNN // LANES

    @pl.when(pl.program_id(1) == 0)
    def _():
        for s in range(n_slabs):
            tail_s[s] = rc0_ref[0, :, s * LANES:(s + 1) * LANES]
        a_s[:, 0:SUBLANES, :] = jnp.ones((n_slabs, SUBLANES, LANES), F32)
        b_s[:, 0:SUBLANES, :] = jnp.zeros((n_slabs, SUBLANES, LANES), F32)
        hc_s[...] = jnp.broadcast_to(h0_ref[0], hc_s.shape)
        kcat_s[...] = _head_masked_stack([mk_ref[0]], MEM_HEADS)
        vcat_s[...] = _head_masked_stack([mv_ref[0]], MEM_HEADS)

    x = x_ref[...]
    hn = _rmsnorm(x, g_ref[...]).astype(BF16)

    def in_proj(lo, hi):
        return jnp.dot(hn, win_ref[:, lo:hi], preferred_element_type=F32)

    xr_pre = in_proj(D_RNN, 2 * D_RNN)
    _slab_stage(xr_pre, conv_s, 0, tail_s, 0)
    xr = jnp.concatenate(_slab_taps(ts, n_slabs, conv_s, 0, cw_ref, cb_ref, 0), axis=1)
    rctail_ref[0] = xr_pre[ts - SUBLANES:]
    qm = in_proj(2 * D_RNN, 2 * D_RNN + MEM_W)
    a, b = _lru_coeffs(xr, wg_ref, bgx_ref[...], bga_ref[...], _log_sigmoid(lru_ref[...]))
    qn = _head_rmsnorm(qm, bd_ref[...], mqg_ref[...]) * ATT_SCALE
    gate = in_proj(0, D_RNN)
    mo = _mem_attention(qn, kcat_s[...], vcat_s[...])
    h = _lru_scan(a, b, a_s, b_s, hc_s)
    hlast_ref[0] = h[ts - 1:ts]
    main = h * _gelu(gate)

    y = _mm(jnp.concatenate([main, mo], axis=1), wout_ref[...])
    xo_ref[...] = x + y


def _mixer_b_kernel(sink_ref, x_ref, g_ref, win_ref, qg_ref, cos_ref, sin_ref,
                    kcur_ref, kprev_ref, vcur_ref, vprev_ref,
                    mqg_ref, bd_ref, mk_ref, mv_ref, wout_ref,
                    xo_ref, kcat_s, vcat_s):
    ts = x_ref.shape[0]
    i = pl.program_id(1)

    @pl.when(i == 0)
    def _():
        kcat_s[...] = _head_masked_stack([mk_ref[0]], MEM_HEADS)
        vcat_s[...] = _head_masked_stack([mv_ref[0]], MEM_HEADS)

    x = x_ref[...]
    hn = _rmsnorm(x, g_ref[...]).astype(BF16)

    def in_proj(lo, hi):
        return jnp.dot(hn, win_ref[:, lo:hi], preferred_element_type=F32)

    q = _head_rmsnorm(in_proj(0, Q_W), bd_ref[...], qg_ref[...], f32_stat=False)
    q = (_rope(q, cos_ref[...], sin_ref[...]) * ATT_SCALE).astype(BF16)

    row = lax.broadcasted_iota(jnp.int32, (WINDOW, 2 * WINDOW), 0)
    kj = lax.broadcasted_iota(jnp.int32, (WINDOW, 2 * WINDOW), 1) - WINDOW
    mask_inner = (kj <= row) & (kj >= row - WINDOW)
    mask_first = (kj <= row) & (kj >= jnp.maximum(row - WINDOW, jnp.where(i > 0, -WINDOW, 0)))

    kt = kcur_ref[...]
    vt = vcur_ref[...]

    def block_scores(jb):
        lo, hi = jb * WINDOW, (jb + 1) * WINDOW
        if jb == 0:
            kp, vp = kprev_ref[...], vprev_ref[...]
        else:
            kp, vp = kt[lo - WINDOW:lo], vt[lo - WINDOW:lo]
        kcat = _head_masked_stack([kp, kt[lo:hi]], N_KV)
        vcat = _head_masked_stack([vp, vt[lo:hi]], N_KV)
        qs = jnp.concatenate([q[lo:hi, g * KV_W:(g + 1) * KV_W] for g in range(GROUP)], axis=0)
        s = lax.dot_general(qs, kcat, (((1,), (1,)), ((), ())), preferred_element_type=F32)
        return s, vcat

    n_blocks = ts // WINDOW
    mains = []
    nxt = block_scores(0)
    qm = in_proj(Q_W, Q_W + MEM_W)
    qn = _head_rmsnorm(qm, bd_ref[...], mqg_ref[...], f32_stat=False) * ATT_SCALE
    s_mem = _mm_nt(qn, kcat_s[...])
    for jb in range(n_blocks):
        s, vcat = nxt
        if jb + 1 < n_blocks:
            nxt = block_scores(jb + 1)
        mask = mask_first if jb == 0 else mask_inner
        prow = []
        for g in range(GROUP):
            pseg = []
            for kv in range(N_KV):
                seg = s[g * WINDOW:(g + 1) * WINDOW, kv * 2 * WINDOW:(kv + 1) * 2 * WINDOW]
                seg = jnp.where(mask, seg, NEG)
                sink = sink_ref[g * N_KV + kv]
                m = jnp.maximum(jnp.max(seg, axis=-1, keepdims=True), sink)
                p = jnp.exp(seg - m)
                den = jnp.sum(p, axis=-1, keepdims=True) + jnp.exp(sink - m)
                pseg.append((p * (1.0 / den)).astype(BF16))
            prow.append(jnp.concatenate(pseg, axis=1))
        o = jnp.dot(jnp.concatenate(prow, axis=0), vcat, preferred_element_type=F32)
        mains.append(jnp.concatenate([o[g * WINDOW:(g + 1) * WINDOW] for g in range(GROUP)], axis=1))
    main = mains[0] if len(mains) == 1 else jnp.concatenate(mains, axis=0)

    mo = _mem_softmax_pv(s_mem, vcat_s[...])

    y = _mm(jnp.concatenate([main, mo], axis=1), wout_ref[...])
    xo_ref[...] = x + y


def _ffn_kernel(x_ref, g_ref, wup_ref, cw_ref, cb_ref, wdn_ref,
                xo_ref, utail_ref, conv_s, tail_s, act_s, xres_s, *, tiles_per_seq):
    ts = x_ref.shape[0]
    slabs = FF_CHUNK // LANES
    blk_slabs = MXU_DIM // LANES
    blks = FF_CHUNK // MXU_DIM
    i = pl.program_id(0)
    slot = i % 2
    pslot = 1 - slot

    @pl.when(i == 0)
    def _():
        act_s[1] = jnp.zeros(act_s.shape[1:], act_s.dtype)
        xres_s[1] = jnp.zeros(xres_s.shape[1:], xres_s.dtype)

    @pl.when(i % tiles_per_seq == 0)
    def _():
        tail_s[...] = jnp.zeros_like(tail_s)

    x = x_ref[...]
    xres_s[slot] = x
    hn = _rmsnorm(x, g_ref[...]).astype(BF16)

    def chunk_cols(j):
        return [(c, ((j % 2) * 2 + half) * slabs) for half, c in enumerate((j, N_FF_CHUNKS + j))]

    def up_project(j):
        for c, work0 in chunk_cols(j):
            for b in range(blks):
                blk = c * blks + b
                u = jnp.dot(hn, wup_ref[blk], preferred_element_type=F32)
                _slab_stage(u, conv_s, work0 + b * blk_slabs, tail_s, c * slabs + b * blk_slabs)
                utail_ref[0, :, blk * MXU_DIM:(blk + 1) * MXU_DIM] = u[ts - SUBLANES:]

    def activation(j):
        cg, cv = [_slab_taps(ts, slabs, conv_s, work0, cw_ref, cb_ref, c * FF_CHUNK)
                  for c, work0 in chunk_cols(j)]
        return jnp.concatenate([_gelu(a) * b for a, b in zip(cg, cv)], axis=1).astype(BF16)

    def down_prev(n):
        return jnp.dot(act_s[pslot], wdn_ref[n], preferred_element_type=F32)

    n_dn = wdn_ref.shape[0]
    outs = [None] * n_dn
    outs[0] = down_prev(0)
    up_project(0)
    for j in range(N_FF_CHUNKS):
        if j + 1 < N_FF_CHUNKS:
            up_project(j + 1)
        else:
            outs[1] = down_prev(1)
        act_s[slot, :, j * FF_CHUNK:(j + 1) * FF_CHUNK] = activation(j)
    for n in range(2, n_dn):
        outs[n] = down_prev(n)
    xo_ref[...] = xres_s[pslot] + jnp.concatenate(outs, axis=1)


def _dec_in_a_kernel(x_ref, g_ref, win_ref, cw_ref, cb_ref, wg_ref, bgx_ref, bga_ref,
                     lru_ref, mqg_ref, bd_ref, b0_ref, b1_ref, b2_ref, h0_ref,
                     main_ref, qn_ref, hnew_ref, xrpre_ref):
    u = _mm(_rmsnorm(x_ref[...], g_ref[...]), win_ref[...])
    gate = u[:, :D_RNN]
    xr_pre = u[:, D_RNN:2 * D_RNN]
    qm = u[:, 2 * D_RNN:]
    xr = b0_ref[...] * cw_ref[0:1, :]
    xr = xr + b1_ref[...] * cw_ref[1:2, :]
    xr = xr + b2_ref[...] * cw_ref[2:3, :]
    xr = xr + xr_pre * cw_ref[3:4, :]
    xr = xr + cb_ref[...]
    a, b = _lru_coeffs(xr, wg_ref, bgx_ref[...], bga_ref[...], _log_sigmoid(lru_ref[...]))
    h = a * h0_ref[...] + b
    main_ref[...] = h * _gelu(gate)
    qn_ref[...] = _head_rmsnorm(qm, bd_ref[...], mqg_ref[...]) * ATT_SCALE
    hnew_ref[...] = h
    xrpre_ref[...] = xr_pre


def _dec_in_b_kernel(x_ref, g_ref, win_ref, qg_ref, cos_ref, sin_ref, mqg_ref, bd_ref,
                     q_ref, qn_ref):
    u = _mm(_rmsnorm(x_ref[...], g_ref[...]), win_ref[...])
    q = _head_rmsnorm(u[:, :Q_W], bd_ref[...], qg_ref[...])
    q_ref[...] = _rope(q, cos_ref[...], sin_ref[...]) * ATT_SCALE
    qn_ref[...] = _head_rmsnorm(u[:, Q_W:], bd_ref[...], mqg_ref[...]) * ATT_SCALE


DEC_HEAD_ROWS = 16


def _own_head_lanes(n_heads, width):
    row = lax.broadcasted_iota(jnp.int32, (DEC_HEAD_ROWS, width), 0)
    lane = lax.broadcasted_iota(jnp.int32, (DEC_HEAD_ROWS, width), 1)
    start = (row % (width // HEAD_DIM)) * HEAD_DIM
    return (lane >= start) & (lane < start + HEAD_DIM) & (row < n_heads)


def _dec_mem_attn_kernel(q_ref, kt_ref, vt_ref, o_ref):
    own = _own_head_lanes(MEM_HEADS, MEM_W)
    for s in range(q_ref.shape[0]):
        q_rows = jnp.broadcast_to(q_ref[s:s + 1, :], (DEC_HEAD_ROWS, MEM_W))
        qbd = jnp.where(own, q_rows, 0.0).astype(BF16)
        sc = jnp.dot(qbd, kt_ref[s].astype(BF16), preferred_element_type=F32)
        m = jnp.max(sc, axis=-1, keepdims=True)
        p = jnp.exp(sc - m)
        den = jnp.sum(p, axis=-1, keepdims=True)
        pn = (p * (1.0 / den)).astype(BF16)
        o_all = lax.dot_general(pn, vt_ref[s].astype(BF16), (((1,), (1,)), ((), ())),
                                preferred_element_type=F32)
        o_ref[s:s + 1, :] = jnp.sum(jnp.where(own, o_all, 0.0), axis=0, keepdims=True)


def _dec_swa_kernel(q_ref, kt_ref, vt_ref, kn_ref, vn_ref, sink_ref, o_ref):
    n_heads = GROUP * N_KV
    own = _own_head_lanes(n_heads, KV_W)
    grp = lax.broadcasted_iota(jnp.int32, (DEC_HEAD_ROWS, KV_W), 0) // N_KV
    sink = sink_ref[:, 0:1]
    for s in range(q_ref.shape[0]):
        q_rows = jnp.zeros((DEC_HEAD_ROWS, KV_W), F32)
        for g in range(GROUP):
            qg = jnp.broadcast_to(q_ref[s:s + 1, g * KV_W:(g + 1) * KV_W], (DEC_HEAD_ROWS, KV_W))
            q_rows = jnp.where(grp == g, qg, q_rows)
        qbd = jnp.where(own, q_rows, 0.0)
        s_buf = jnp.dot(qbd.astype(BF16), kt_ref[s].astype(BF16), preferred_element_type=F32)
        s_new = jnp.sum(qbd * kn_ref[s:s + 1, :], axis=-1, keepdims=True)
        m = jnp.maximum(jnp.maximum(jnp.max(s_buf, axis=-1, keepdims=True), s_new), sink)
        p_buf = jnp.exp(s_buf - m)
        p_new = jnp.exp(s_new - m)
        den = jnp.sum(p_buf, axis=-1, keepdims=True) + p_new + jnp.exp(sink - m)
        r = 1.0 / den
        o_all = lax.dot_general((p_buf * r).astype(BF16), vt_ref[s].astype(BF16),
                                (((1,), (1,)), ((), ())), preferred_element_type=F32)
        o_all = jnp.where(own, o_all + (p_new * r) * vn_ref[s:s + 1, :], 0.0)
        o_sum = o_all + pltpu.roll(o_all, 1, 0)
        o_sum = o_sum + pltpu.roll(o_sum, 2, 0)
        for g in range(GROUP):
            last = (g + 1) * N_KV - 1
            o_ref[s:s + 1, g * KV_W:(g + 1) * KV_W] = o_sum[last:last + 1, :]


def _dec_out_ffn_kernel(x_ref, main_ref, mo_ref, wout_ref, g_ref,
                        wug_ref, wuv_ref, cwg_ref, cwv_ref, cbg_ref, cbv_ref, wdn_ref,
                        sg_ref, sv_ref, snew_in_ref,
                        xo_ref, snew_ref,
                        xmid_s, hn_s, acc_s):
    del snew_in_ref
    j = pl.program_id(0)

    @pl.when(j == 0)
    def _():
        y = _mm(jnp.concatenate([main_ref[...], mo_ref[...]], axis=1), wout_ref[...])
        xmid = x_ref[...] + y
        xmid_s[...] = xmid
        hn_s[...] = _rmsnorm(xmid, g_ref[...]).astype(BF16)
        acc_s[...] = jnp.zeros_like(acc_s)

    hn = hn_s[...]
    up = lambda w_ref: jnp.concatenate(
        [jnp.dot(hn, w_ref[b], preferred_element_type=F32) for b in range(w_ref.shape[0])], axis=1)
    ug = up(wug_ref)
    uv = up(wuv_ref)
    sg0, sg1 = sg_ref[:, 0, :], sg_ref[:, 1, :]
    sv0, sv1 = sv_ref[:, 0, :], sv_ref[:, 1, :]
    gate_cols = pl.ds(pl.multiple_of(j * FF_CHUNK, FF_CHUNK), FF_CHUNK)
    value_cols = pl.ds(pl.multiple_of(D_FF + j * FF_CHUNK, FF_CHUNK), FF_CHUNK)
    snew_ref[:, 0, gate_cols] = sg1
    snew_ref[:, 1, gate_cols] = ug
    snew_ref[:, 0, value_cols] = sv1
    snew_ref[:, 1, value_cols] = uv
    cg = sg0 * cwg_ref[0:1, :] + sg1 * cwg_ref[1:2, :] + ug * cwg_ref[2:3, :] + cbg_ref[...]
    cv = sv0 * cwv_ref[0:1, :] + sv1 * cwv_ref[1:2, :] + uv * cwv_ref[2:3, :] + cbv_ref[...]
    act = (_gelu(cg) * cv).astype(BF16)
    for n in range(wdn_ref.shape[0]):
        acc_s[:, n * MXU_DIM:(n + 1) * MXU_DIM] += jnp.dot(act, wdn_ref[n],
                                                           preferred_element_type=F32)

    @pl.when(j == pl.num_programs(0) - 1)
    def _():
        xo_ref[...] = xmid_s[...] + acc_s[...]


def _const_spec(shape):
    nd = len(shape)
    return pl.BlockSpec(shape, lambda *_: (0,) * nd)


def _layer_spec(arr, layer):
    nd = arr.ndim - 1
    return pl.BlockSpec((None,) + arr.shape[1:], lambda *_: (layer,) + (0,) * nd)


def _resident_spec(op):
    return _layer_spec(*op) if isinstance(op, tuple) else _const_spec(op.shape)


def _operand(op):
    return op[0] if isinstance(op, tuple) else op


def _params(*sem):
    return pltpu.CompilerParams(dimension_semantics=sem, vmem_limit_bytes=VMEM_LIMIT_BYTES)


def _cast_blocks(w):
    layers, r, c = w.shape
    tile = CAST_TILE
    per = tile // MXU_DIM
    return pl.pallas_call(
        _cast_blocks_kernel,
        grid=(layers, r // tile, c // tile),
        in_specs=[pl.BlockSpec((None, tile, tile), lambda l, i, j: (l, i, j))],
        out_specs=pl.BlockSpec((None, per, tile, MXU_DIM), lambda l, i, j: (l, j, i, 0)),
        out_shape=jax.ShapeDtypeStruct((layers, c // MXU_DIM, r, MXU_DIM), BF16),
        compiler_params=_params("arbitrary", "arbitrary", "arbitrary"),
        name="cast_blocks",
    )(w)


def _mem_kv(mem2d, g, w, kg, bd):
    depth = w.shape[0]
    rows = mem2d.shape[0]
    bsz = rows // N_MEM
    out = jax.ShapeDtypeStruct((depth, rows, MEM_W), F32)
    out_t = jax.ShapeDtypeStruct((depth, bsz, MEM_W, N_MEM), F32)
    return pl.pallas_call(
        _mem_kv_kernel,
        grid=(depth,),
        in_specs=[
            _const_spec(mem2d.shape),
            pl.BlockSpec((1, 1, D_MODEL), lambda l: (l, 0, 0)),
            pl.BlockSpec((1, D_MODEL, 2 * MEM_W), lambda l: (l, 0, 0)),
            pl.BlockSpec((1, 1, MEM_W), lambda l: (l, 0, 0)),
            _const_spec(bd.shape),
        ],
        out_specs=[pl.BlockSpec((1, rows, MEM_W), lambda l: (l, 0, 0))] * 2
        + [pl.BlockSpec((1, bsz, MEM_W, N_MEM), lambda l: (l, 0, 0, 0))] * 2,
        out_shape=[out, out, out_t, out_t],
        compiler_params=_params("arbitrary"),
        name="mem_kv",
    )(mem2d, g, w, kg, bd)


def _shared_kv(x2d, g, w, kg, bd, cos_t, sin_t, ts):
    rows = x2d.shape[0]
    tab_blocks = cos_t.shape[0] // ts
    out = jax.ShapeDtypeStruct((rows, KV_W), F32)
    return pl.pallas_call(
        _shared_kv_kernel,
        grid=(rows // ts,),
        in_specs=[
            pl.BlockSpec((ts, D_MODEL), lambda i: (i, 0)),
            _const_spec(g.shape), _const_spec(w.shape), _const_spec(kg.shape),
            _const_spec(bd.shape),
            pl.BlockSpec((ts, LANES), lambda i: (i % tab_blocks, 0)),
            pl.BlockSpec((ts, LANES), lambda i: (i % tab_blocks, 0)),
        ],
        out_specs=[pl.BlockSpec((ts, KV_W), lambda i: (i, 0))] * 2,
        out_shape=[out, out],
        compiler_params=_params("arbitrary"),
        name="shared_kv",
    )(x2d, g, w, kg, bd, cos_t, sin_t)


def _mixer_a(x2d, bsz, g, win, cw, cb, wg, bgx, bga, lru, mqg, bd, mk, mv, wout, rc0, h0):
    rows = x2d.shape[0]
    ts = TS_MIX
    nt = rows // bsz // ts
    consts = (g, win, cw, cb, wg, bgx, bga, lru, mqg, bd)
    per_b3 = lambda b, i: (b, 0, 0)
    return pl.pallas_call(
        _mixer_a_kernel,
        grid=(bsz, nt),
        in_specs=[pl.BlockSpec((ts, D_MODEL), lambda b, i: (b * nt + i, 0))]
        + [_resident_spec(c) for c in consts]
        + [pl.BlockSpec((1, N_MEM, MEM_W), per_b3)] * 2
        + [_resident_spec(wout),
           pl.BlockSpec((1, SUBLANES, D_RNN), per_b3),
           pl.BlockSpec((1, 1, D_RNN), per_b3)],
        out_specs=[
            pl.BlockSpec((ts, D_MODEL), lambda b, i: (b * nt + i, 0)),
            pl.BlockSpec((1, 1, D_RNN), per_b3),
            pl.BlockSpec((1, SUBLANES, D_RNN), per_b3),
        ],
        out_shape=[
            jax.ShapeDtypeStruct((rows, D_MODEL), F32),
            jax.ShapeDtypeStruct((bsz, 1, D_RNN), F32),
            jax.ShapeDtypeStruct((bsz, SUBLANES, D_RNN), F32),
        ],
        scratch_shapes=[
            pltpu.VMEM((D_RNN // LANES, SUBLANES + ts, LANES), F32),
            pltpu.VMEM((D_RNN // LANES, SUBLANES, LANES), F32),
            pltpu.VMEM((D_RNN // LANES, SUBLANES + ts, LANES), F32),
            pltpu.VMEM((D_RNN // LANES, SUBLANES + ts, LANES), F32),
            pltpu.VMEM((SUBLANES, D_RNN), F32),
            pltpu.VMEM((MEM_HEADS * N_MEM, MEM_W), BF16),
            pltpu.VMEM((MEM_HEADS * N_MEM, MEM_W), BF16),
        ],
        compiler_params=_params("arbitrary", "arbitrary"),
        name="mixer_a",
    )(x2d, *[_operand(c) for c in consts], mk, mv, _operand(wout), rc0, h0)


def _mixer_b(x2d, bsz, sink_tab, g, win, qg, cos_t, sin_t, ksh, vsh, mqg, bd, mk, mv, wout):
    rows = x2d.shape[0]
    ts = TS_MIX
    nt = rows // bsz // ts
    wpt = ts // WINDOW
    cur = lambda b, i: (b * nt + i, 0)
    prev = lambda b, i: (jnp.maximum((b * nt + i) * wpt - 1, 0), 0)
    table = lambda b, i: (i, 0)
    per_b3 = lambda b, i: (b, 0, 0)
    return pl.pallas_call(
        _mixer_b_kernel,
        grid=(bsz, nt),
        in_specs=[
            pl.BlockSpec(memory_space=pltpu.SMEM),
            pl.BlockSpec((ts, D_MODEL), cur),
            _const_spec(g.shape), _resident_spec(win), _const_spec(qg.shape),
            pl.BlockSpec((ts, LANES), table),
            pl.BlockSpec((ts, LANES), table),
            pl.BlockSpec((ts, KV_W), cur), pl.BlockSpec((WINDOW, KV_W), prev),
            pl.BlockSpec((ts, KV_W), cur), pl.BlockSpec((WINDOW, KV_W), prev),
            _const_spec(mqg.shape), _const_spec(bd.shape),
            pl.BlockSpec((1, N_MEM, MEM_W), per_b3), pl.BlockSpec((1, N_MEM, MEM_W), per_b3),
            _resident_spec(wout),
        ],
        out_specs=pl.BlockSpec((ts, D_MODEL), cur),
        out_shape=jax.ShapeDtypeStruct((rows, D_MODEL), F32),
        scratch_shapes=[
            pltpu.VMEM((MEM_HEADS * N_MEM, MEM_W), BF16),
            pltpu.VMEM((MEM_HEADS * N_MEM, MEM_W), BF16),
        ],
        compiler_params=_params("arbitrary", "arbitrary"),
        name="mixer_b",
    )(sink_tab, x2d, g, _operand(win), qg, cos_t, sin_t, ksh, ksh, vsh, vsh, mqg, bd, mk, mv,
      _operand(wout))


def _ffn(x2d, bsz, layer, g, wup, cw, cb, wdn):
    rows = x2d.shape[0]
    ts = TS_FFN
    nt = rows // ts
    tiles_per_seq = rows // bsz // ts
    last = nt - 1
    return pl.pallas_call(
        functools.partial(_ffn_kernel, tiles_per_seq=tiles_per_seq),
        grid=(nt + 1,),
        in_specs=[pl.BlockSpec((ts, D_MODEL), lambda i: (jnp.minimum(i, last), 0)),
                  _const_spec(g.shape), _layer_spec(wup, layer), _const_spec(cw.shape),
                  _const_spec(cb.shape), _layer_spec(wdn, layer)],
        out_specs=[
            pl.BlockSpec((ts, D_MODEL), lambda i: (jnp.maximum(i - 1, 0), 0)),
            pl.BlockSpec((1, SUBLANES, 2 * D_FF),
                         lambda i: (jnp.minimum(i, last) // tiles_per_seq, 0, 0)),
        ],
        out_shape=[
            jax.ShapeDtypeStruct((rows, D_MODEL), F32),
            jax.ShapeDtypeStruct((bsz, SUBLANES, 2 * D_FF), F32),
        ],
        scratch_shapes=[pltpu.VMEM((4 * FF_CHUNK // LANES, SUBLANES + ts, LANES), F32),
                        pltpu.VMEM((2 * D_FF // LANES, SUBLANES, LANES), F32),
                        pltpu.VMEM((2, ts, D_FF), BF16),
                        pltpu.VMEM((2, ts, D_MODEL), F32)],
        compiler_params=_params("arbitrary"),
        name="ffn",
    )(x2d, g, wup, cw, cb, wdn)


def _dec_in_a(x, g, win, cw, cb, wg, bgx, bga, lru, mqg, bd, rc, h0):
    n = x.shape[0]
    consts = (x, g, win, cw, cb, wg, bgx, bga, lru, mqg, bd)
    buf = lambda j: pl.BlockSpec((None, n, D_RNN), lambda i: (j, 0, 0))
    return pl.pallas_call(
        _dec_in_a_kernel,
        grid=(1,),
        in_specs=[_resident_spec(c) for c in consts]
        + [buf(0), buf(1), buf(2), _const_spec(h0.shape)],
        out_specs=[_const_spec((n, D_RNN)), _const_spec((n, MEM_W)),
                   _const_spec((n, D_RNN)), _const_spec((n, D_RNN))],
        out_shape=[jax.ShapeDtypeStruct((n, D_RNN), F32), jax.ShapeDtypeStruct((n, MEM_W), F32),
                   jax.ShapeDtypeStruct((n, D_RNN), F32), jax.ShapeDtypeStruct((n, D_RNN), F32)],
        compiler_params=_params("arbitrary"),
        name="dec_in_a",
    )(*[_operand(c) for c in consts], rc, rc, rc, h0)


def _dec_in_b(x, g, win, qg, cos_t, sin_t, mqg, bd):
    n = x.shape[0]
    args = (x, g, win, qg, cos_t, sin_t, mqg, bd)
    return pl.pallas_call(
        _dec_in_b_kernel,
        grid=(1,),
        in_specs=[_resident_spec(a) for a in args],
        out_specs=[_const_spec((n, Q_W)), _const_spec((n, MEM_W))],
        out_shape=[jax.ShapeDtypeStruct((n, Q_W), F32), jax.ShapeDtypeStruct((n, MEM_W), F32)],
        compiler_params=_params("arbitrary"),
        name="dec_in_b",
    )(*[_operand(a) for a in args])


def _dec_mem_attn(qn, ckt, cvt, layer):
    n = qn.shape[0]
    sb = DEC_SEQ_BLOCK
    return pl.pallas_call(
        _dec_mem_attn_kernel,
        grid=(n // sb,),
        in_specs=[pl.BlockSpec((sb, MEM_W), lambda i: (i, 0)),
                  pl.BlockSpec((None, sb, MEM_W, N_MEM), lambda i: (layer, i, 0, 0)),
                  pl.BlockSpec((None, sb, MEM_W, N_MEM), lambda i: (layer, i, 0, 0))],
        out_specs=pl.BlockSpec((sb, MEM_W), lambda i: (i, 0)),
        out_shape=jax.ShapeDtypeStruct((n, MEM_W), F32),
        compiler_params=_params("arbitrary"),
        name="dec_mem_attn",
    )(qn, ckt, cvt)


def _dec_swa(q, kbt, vbt, kn, vn, sink_rows):
    n = q.shape[0]
    wb = kbt.shape[2]
    sb = DEC_SEQ_BLOCK
    assert wb <= WINDOW
    return pl.pallas_call(
        _dec_swa_kernel,
        grid=(n // sb,),
        in_specs=[pl.BlockSpec((sb, Q_W), lambda i: (i, 0)),
                  pl.BlockSpec((sb, KV_W, wb), lambda i: (i, 0, 0)),
                  pl.BlockSpec((sb, KV_W, wb), lambda i: (i, 0, 0)),
                  pl.BlockSpec((sb, KV_W), lambda i: (i, 0)),
                  pl.BlockSpec((sb, KV_W), lambda i: (i, 0)),
                  _const_spec(sink_rows.shape)],
        out_specs=pl.BlockSpec((sb, Q_W), lambda i: (i, 0)),
        out_shape=jax.ShapeDtypeStruct((n, Q_W), F32),
        compiler_params=_params("arbitrary"),
        name="dec_swa",
    )(q, kbt, vbt, kn, vn, sink_rows)


def _dec_out_ffn(x, main, mo, wout, wout_layer, layer, g, wup, cw, cb, wdn, st, st_new):
    n = x.shape[0]
    nch = N_FF_CHUNKS
    blks = FF_CHUNK // MXU_DIM
    lo = lambda j: (0, j)
    hi = lambda j: (0, nch + j)
    state = lambda off: pl.BlockSpec((None, n, CONV_F - 1, FF_CHUNK),
                                     lambda j: (layer, 0, 0, off + j))
    up_blocks = lambda off: pl.BlockSpec((None, blks, D_MODEL, MXU_DIM),
                                         lambda j: (layer, off + j, 0, 0))
    operands = (x, main, mo, wout, g, wup, wup, cw, cw, cb, cb, wdn, st, st, st_new)
    return pl.pallas_call(
        _dec_out_ffn_kernel,
        grid=(nch,),
        in_specs=[_const_spec(x.shape), _const_spec(main.shape), _const_spec(mo.shape),
                  _layer_spec(wout, wout_layer), _const_spec(g.shape),
                  up_blocks(0), up_blocks(nch),
                  pl.BlockSpec((CONV_F, FF_CHUNK), lo), pl.BlockSpec((CONV_F, FF_CHUNK), hi),
                  pl.BlockSpec((1, FF_CHUNK), lo), pl.BlockSpec((1, FF_CHUNK), hi),
                  pl.BlockSpec((None, wdn.shape[1], FF_CHUNK, MXU_DIM), lambda j: (layer, 0, j, 0)),
                  state(0), state(nch),
                  pl.BlockSpec(memory_space=pl.ANY)],
        out_specs=[_const_spec((n, D_MODEL)),
                   pl.BlockSpec((None, n, CONV_F - 1, 2 * D_FF), lambda j: (layer, 0, 0, 0))],
        out_shape=[jax.ShapeDtypeStruct((n, D_MODEL), F32),
                   jax.ShapeDtypeStruct(st_new.shape, F32)],
        input_output_aliases={len(operands) - 1: 1},
        scratch_shapes=[pltpu.VMEM((n, D_MODEL), F32), pltpu.VMEM((n, D_MODEL), BF16),
                        pltpu.VMEM((n, D_MODEL), F32)],
        compiler_params=_params("arbitrary"),
        name="dec_out_ffn",
    )(*operands)


def _rope_tables(pos):
    half = HEAD_DIM // 2
    inv = ROPE_THETA ** (-jnp.arange(half, dtype=F32) / half)
    ang = pos.astype(F32)[:, None] * inv[None, :]
    cos = jnp.cos(ang)
    sin = jnp.sin(ang)
    reps = LANES // HEAD_DIM
    cos_t = jnp.tile(jnp.concatenate([cos, cos], axis=1), (1, reps))
    sin_t = jnp.tile(jnp.concatenate([-sin, sin], axis=1), (1, reps))
    return cos_t, sin_t


def _block_diag_gates(wx, wa):
    per = MXU_DIM // HEAD_DIM
    eye = jnp.eye(per, dtype=F32)

    def bd(w):
        w4 = w.reshape(RNN_BLOCKS // per, per, HEAD_DIM, HEAD_DIM)
        return jnp.einsum('ckij,kK->ckiKj', w4, eye).reshape(RNN_BLOCKS // per, MXU_DIM, MXU_DIM)

    return jnp.concatenate([bd(wx), bd(wa)], axis=2).astype(BF16)


def kernel(x_prompt, x_sample, state_rglru_h, state_rglru_conv, state_ffn_conv, cache_swa_k, cache_swa_v, cache_mem_k, cache_mem_v, mem_prompt, norm_mix_g, norm_ffn_g, w_in_a, rnn_conv_w, rnn_conv_b, w_gate_x, b_gate_x, w_gate_a, b_gate_a, lru_param, w_in_b, q_norm_g, sinks, kv_norm_g, w_kv, k_norm_g, mem_norm_g, w_mem_kv, mem_q_norm_g, mem_k_norm_g, w_out, w_ffn_up, ffn_conv_w, ffn_conv_b, w_ffn_down):
    bsz, seq, _ = x_prompt.shape
    dbsz = x_sample.shape[0]
    depth = norm_mix_g.shape[0]
    n_a = w_in_a.shape[0]
    assert x_sample.shape[1] == 1
    assert seq % TS_MIX == 0 and seq % TS_FFN == 0 and seq % TS_KV == 0 and TS_MIX % WINDOW == 0

    n_b = w_in_b.shape[0]
    wq = w_in_b[:, :, :Q_W].astype(BF16).reshape(n_b, D_MODEL, N_KV, GROUP, HEAD_DIM)
    wq = wq.transpose(0, 1, 3, 2, 4).reshape(n_b, D_MODEL, Q_W)
    wo_main = w_out[n_a:, :Q_W].astype(BF16).reshape(n_b, N_KV, GROUP, HEAD_DIM, D_MODEL)
    wo_main = wo_main.transpose(0, 2, 1, 3, 4).reshape(n_b, Q_W, D_MODEL)
    bd =(jnp.kron(jnp.eye(MXU_DIM // HEAD_DIM, dtype=F32),
                   jnp.ones((HEAD_DIM, HEAD_DIM), F32)) / HEAD_DIM).astype(BF16)

    row = lambda v: v.reshape(1, -1)
    tile_row = lambda v, n: jnp.tile(v, n).reshape(1, -1)
    w_in_a_b = w_in_a.astype(BF16)
    w_in_b_b = jnp.concatenate([wq, w_in_b[:, :, Q_W:].astype(BF16)], axis=2)
    w_out_b = w_out.astype(BF16)
    w_out_perm_b = jnp.concatenate([wo_main, w_out_b[n_a:, Q_W:]], axis=1)
    w_kv_b = w_kv.astype(BF16)
    w_mem_b = w_mem_kv.astype(BF16)
    wup_b = _cast_blocks(w_ffn_up)
    wdn_b = _cast_blocks(w_ffn_down)
    fcw = ffn_conv_w
    fcb = ffn_conv_b.reshape(depth, 1, 2 * D_FF)
    wg_b = jnp.stack([_block_diag_gates(w_gate_x[l], w_gate_a[l]) for l in range(n_a)])
    sink_gk = sinks.reshape(-1, N_KV, GROUP).transpose(0, 2, 1)

    cos_p, sin_p = _rope_tables(jnp.arange(seq, dtype=jnp.int32))
    pos_s = PAST_LEN + jnp.zeros((dbsz,), jnp.int32)
    cos_s, sin_s = _rope_tables(pos_s)

    mem2d = mem_prompt.reshape(bsz * N_MEM, D_MODEL)
    pmk, pmv, pmk_t, pmv_t = _mem_kv(
        mem2d, mem_norm_g.reshape(depth, 1, D_MODEL), w_mem_b,
        jnp.tile(mem_k_norm_g, (1, MEM_HEADS)).reshape(depth, 1, MEM_W), bd)
    pmk4 = pmk.reshape(depth, bsz, N_MEM, MEM_W)
    pmv4 = pmv.reshape(depth, bsz, N_MEM, MEM_W)

    x = x_prompt.reshape(bsz * seq, D_MODEL)
    zeros_rc = jnp.zeros((bsz, SUBLANES, D_RNN), F32)
    zeros_h = jnp.zeros((bsz, 1, D_RNN), F32)
    p_h, p_rc, p_fc = [], [], []
    ksh = vsh = None
    for l in range(depth):
        mqg = tile_row(mem_q_norm_g[l], MEM_HEADS)
        if l < n_a:
            x, hl, rct = _mixer_a(
                x, bsz, row(norm_mix_g[l]), (w_in_a_b, l), rnn_conv_w[l], row(rnn_conv_b[l]),
                (wg_b, l), row(b_gate_x[l]), row(b_gate_a[l]), row(lru_param[l]), mqg, bd,
                pmk4[l], pmv4[l], (w_out_b, l), zeros_rc, zeros_h)
            p_h.append(hl.reshape(bsz, D_RNN))
            p_rc.append(rct[:, SUBLANES - (CONV_A - 1):])
        else:
            j = l - n_a
            x = _mixer_b(
                x, bsz, sink_gk[j].reshape(-1), row(norm_mix_g[l]), (w_in_b_b, j),
                tile_row(q_norm_g[j], N_Q), cos_p, sin_p, ksh, vsh, mqg, bd,
                pmk4[l], pmv4[l], (w_out_perm_b, j))
        x, ut = _ffn(x, bsz, l, row(norm_ffn_g[l]), wup_b, fcw[l], fcb[l], wdn_b)
        p_fc.append(ut[:, SUBLANES - (CONV_F - 1):])
        if l == n_a - 1:
            ksh, vsh = _shared_kv(x, row(kv_norm_g), w_kv_b, tile_row(k_norm_g, N_KV), bd,
                                  cos_p, sin_p, TS_KV)
    y_prompt = x.reshape(bsz, seq, D_MODEL)
    keep = min(WINDOW, seq)
    p_k = ksh.reshape(bsz, seq, KV_W)[:, seq - keep:].reshape(bsz, keep, N_KV, HEAD_DIM)
    p_v = vsh.reshape(bsz, seq, KV_W)[:, seq - keep:].reshape(bsz, keep, N_KV, HEAD_DIM)
    to_cache = lambda t: t.reshape(depth, bsz, MEM_HEADS, HEAD_DIM, N_MEM).transpose(0, 1, 4, 2, 3)
    p_mem_k = to_cache(pmk_t)
    p_mem_v = to_cache(pmv_t)

    xs = x_sample.reshape(dbsz, D_MODEL)
    cmk = cache_mem_k.transpose(0, 1, 3, 4, 2).reshape(depth, dbsz, MEM_W, N_MEM)
    cmv = cache_mem_v.transpose(0, 1, 3, 4, 2).reshape(depth, dbsz, MEM_W, N_MEM)
    wb = cache_swa_k.shape[1]
    ckb = cache_swa_k.transpose(0, 2, 3, 1).reshape(dbsz, KV_W, wb)
    cvb = cache_swa_v.transpose(0, 2, 3, 1).reshape(dbsz, KV_W, wb)
    s_h, s_rc = [], []
    s_fc = jnp.zeros_like(state_ffn_conv)
    kn = vn = None
    for l in range(depth):
        mqg = tile_row(mem_q_norm_g[l], MEM_HEADS)
        if l < n_a:
            main, qn, hnew, xrpre = _dec_in_a(
                xs, row(norm_mix_g[l]), (w_in_a_b, l), rnn_conv_w[l], row(rnn_conv_b[l]),
                (wg_b, l), row(b_gate_x[l]), row(b_gate_a[l]), row(lru_param[l]), mqg, bd,
                state_rglru_conv[l].transpose(1, 0, 2), state_rglru_h[l])
            s_h.append(hnew)
            s_rc.append(jnp.concatenate([state_rglru_conv[l][:, 1:], xrpre[:, None, :]], axis=1))
            wo, wo_layer = w_out_b, l
        else:
            j = l - n_a
            q, qn = _dec_in_b(xs, row(norm_mix_g[l]), (w_in_b_b, j), tile_row(q_norm_g[j], N_Q),
                              cos_s, sin_s, mqg, bd)
            sink_rows = jnp.zeros((DEC_HEAD_ROWS, LANES), F32).at[:N_Q].set(
                jnp.broadcast_to(sink_gk[j].reshape(N_Q, 1), (N_Q, LANES)))
            main = _dec_swa(q, ckb, cvb, kn, vn, sink_rows)
            wo, wo_layer = w_out_perm_b, j
        mo = _dec_mem_attn(qn, cmk, cmv, l)
        xs, s_fc = _dec_out_ffn(xs, main, mo, wo, wo_layer, l, row(norm_ffn_g[l]), wup_b,
                                fcw[l], fcb[l], wdn_b, state_ffn_conv, s_fc)
        if l == n_a - 1:
            kn, vn = _shared_kv(xs, row(kv_norm_g), w_kv_b, tile_row(k_norm_g, N_KV), bd,
                                cos_s, sin_s, dbsz)
    y_sample = xs.reshape(dbsz, 1, D_MODEL)
    s_k = kn.reshape(dbsz, 1, N_KV, HEAD_DIM)
    s_v = vn.reshape(dbsz, 1, N_KV, HEAD_DIM)

    return (y_prompt, y_sample, jnp.stack(p_h), jnp.stack(p_rc), jnp.stack(p_fc), p_k, p_v,
            p_mem_k, p_mem_v, jnp.stack(s_h), jnp.stack(s_rc), s_fc, s_k, s_v)
```

```python
import functools
import math

import jax
import jax.numpy as jnp
from jax import lax
from jax.experimental import pallas as pl
from jax.experimental.pallas import tpu as pltpu

F32 = jnp.float32
BF16 = jnp.bfloat16

D_MODEL = 1024
HEAD_DIM = 64
MEM_HEADS = 4
MEM_W = MEM_HEADS * HEAD_DIM
N_MEM = 256
D_RNN = D_MODEL - MEM_W
RNN_BLOCKS = D_RNN // HEAD_DIM
CONV_A = 4
LRU_C = 8.0
N_Q = D_RNN // HEAD_DIM
N_KV = 4
GROUP = N_Q // N_KV
Q_W = N_Q * HEAD_DIM
KV_W = N_KV * HEAD_DIM
WINDOW = 128
ROPE_THETA = 10000.0
D_FF = 3 * D_MODEL
CONV_F = 3
EPS = 1e-6
NEG = -1e30
ATT_SCALE = HEAD_DIM ** -0.5
PAST_LEN = 8192

SUBLANES = 8
LANES = 128
MXU_DIM = 256
VMEM_LIMIT_BYTES = 56 * 1024 * 1024

TS_MIX_A = 1024
TS_MIX_B = 512
TS_FFN = 512
TS_KV = 1024
FF_CHUNK = 512
N_FF_CHUNKS = D_FF // FF_CHUNK
DEC_SEQ_BLOCK = 32
CAST_TILE = 1024


def _mm(a, b):
    return jnp.dot(a.astype(BF16), b, preferred_element_type=F32)


def _mm_nt(a, b):
    return lax.dot_general(a.astype(BF16), b, (((1,), (1,)), ((), ())),
                           preferred_element_type=F32)


def _mm_split(a, b):
    hi = a.astype(BF16)
    lo = (a - hi.astype(F32)).astype(BF16)
    return (jnp.dot(hi, b, preferred_element_type=F32)
            + jnp.dot(lo, b, preferred_element_type=F32))


def _rmsnorm(x, g):
    ms = jnp.mean(x * x, axis=-1, keepdims=True)
    return x * lax.rsqrt(ms + EPS) * g


def _head_rmsnorm(x, bd, g, f32_stat=True):
    mean_sq = _mm_split if f32_stat else _mm
    parts = []
    for c in range(x.shape[1] // MXU_DIM):
        xc = x[:, c * MXU_DIM:(c + 1) * MXU_DIM]
        ms = mean_sq(xc * xc, bd)
        parts.append(xc * lax.rsqrt(ms + EPS))
    y = parts[0] if len(parts) == 1 else jnp.concatenate(parts, axis=1)
    return y * g


def _tile_lanes(t, width):
    reps = width // t.shape[1]
    return t if reps == 1 else jnp.concatenate([t] * reps, axis=1)


def _rope(x, cos_t, sin_t):
    w = x.shape[1]
    lane = lax.broadcasted_iota(jnp.int32, x.shape, 1)
    first = (lane % HEAD_DIM) < (HEAD_DIM // 2)
    swapped = jnp.where(first, pltpu.roll(x, w - HEAD_DIM // 2, 1),
                        pltpu.roll(x, HEAD_DIM // 2, 1))
    return x * _tile_lanes(cos_t, w) + swapped * _tile_lanes(sin_t, w)


def _gelu(x):
    c = math.sqrt(2.0 / math.pi)
    return x * (0.5 * (1.0 + jnp.tanh(c * (x + 0.044715 * (x * x * x)))))


def _log_sigmoid(x):
    return jnp.minimum(x, 0.0) - jnp.log1p(jnp.exp(-jnp.abs(x)))


def _slab_stage(x, work_ref, work0, tail_ref, tail0):
    ts = x.shape[0]
    for s in range(x.shape[1] // LANES):
        xs = x[:, s * LANES:(s + 1) * LANES]
        buf = work_ref.at[work0 + s]
        buf[0:SUBLANES, :] = tail_ref[tail0 + s]
        buf[SUBLANES:SUBLANES + ts, :] = xs
        tail_ref[tail0 + s] = xs[ts - SUBLANES:]


def _slab_taps(ts, n_slabs, work_ref, work0, w_ref, b_ref, col0):
    k = w_ref.shape[0]
    outs = []
    for s in range(n_slabs):
        lanes = slice(col0 + s * LANES, col0 + (s + 1) * LANES)
        buf = work_ref.at[work0 + s]
        acc = buf[SUBLANES - (k - 1):SUBLANES - (k - 1) + ts, :] * w_ref[0:1, lanes]
        for j in range(1, k):
            off = SUBLANES - (k - 1 - j)
            acc = acc + buf[off:off + ts, :] * w_ref[j:j + 1, lanes]
        outs.append(acc + b_ref[:, lanes])
    return outs


def _sqrt_pos(x):
    return jnp.where(x > 0.0, x * lax.rsqrt(x), 0.0)


def _lru_coeffs(xr, wg_ref, bgx, bga, logsig):
    xb = xr.astype(BF16)
    gxs, gas = [], []
    for c in range(D_RNN // MXU_DIM):
        gg = jnp.dot(xb[:, c * MXU_DIM:(c + 1) * MXU_DIM], wg_ref[c],
                     preferred_element_type=F32)
        gxs.append(gg[:, :MXU_DIM])
        gas.append(gg[:, MXU_DIM:])
    gx = jax.nn.sigmoid(jnp.concatenate(gxs, axis=1) + bgx)
    ga = jax.nn.sigmoid(jnp.concatenate(gas, axis=1) + bga)
    log_a = ga * (LRU_C * logsig)
    a = jnp.exp(log_a)
    mult = _sqrt_pos(-jnp.tanh(log_a) * (a * a + 1.0))
    return a, mult * gx * xr


def _lru_scan(a, b, a_ref, b_ref, hc_ref):
    ts = a.shape[0]
    outs = []
    for s in range(a.shape[1] // LANES):
        lanes = slice(s * LANES, (s + 1) * LANES)
        a_s, b_s = a[:, lanes], b[:, lanes]
        abuf, bbuf = a_ref.at[s], b_ref.at[s]
        d = 1
        while d < SUBLANES:
            abuf[SUBLANES:SUBLANES + ts, :] = a_s
            bbuf[SUBLANES:SUBLANES + ts, :] = b_s
            b_s = a_s * bbuf[SUBLANES - d:SUBLANES - d + ts, :] + b_s
            a_s = a_s * abuf[SUBLANES - d:SUBLANES - d + ts, :]
            d *= 2
        h = hc_ref[:, lanes]
        hs = []
        for q in range(ts // SUBLANES):
            rows = slice(q * SUBLANES, (q + 1) * SUBLANES)
            h = a_s[rows] * h + b_s[rows]
            hs.append(h)
        hc_ref[:, lanes] = jnp.broadcast_to(h[SUBLANES - 1:], (SUBLANES, LANES))
        outs.append(jnp.concatenate(hs, axis=0))
    return jnp.concatenate(outs, axis=1)


def _head_mask(shape, h):
    lane = lax.broadcasted_iota(jnp.int32, shape, 1)
    return (lane >= h * HEAD_DIM) & (lane < (h + 1) * HEAD_DIM)


def _head_masked_stack(blocks, n_heads):
    parts = []
    for h in range(n_heads):
        for blk in blocks:
            parts.append(jnp.where(_head_mask(blk.shape, h), blk, 0.0).astype(BF16))
    return jnp.concatenate(parts, axis=0)


def _mem_attention(qn, kcat, vcat):
    return _mem_softmax_pv(_mm_nt(qn, kcat), vcat)


def _mem_softmax_pv(s, vcat):
    parts = []
    for h in range(MEM_HEADS):
        sh = s[:, h * N_MEM:(h + 1) * N_MEM]
        m = jnp.max(sh, axis=-1, keepdims=True)
        p = jnp.exp(sh - m)
        den = jnp.sum(p, axis=-1, keepdims=True)
        parts.append((p * (1.0 / den)).astype(BF16))
    return jnp.dot(jnp.concatenate(parts, axis=1), vcat, preferred_element_type=F32)


def _cast_blocks_kernel(w_ref, o_ref):
    for b in range(o_ref.shape[0]):
        o_ref[b] = w_ref[:, b * MXU_DIM:(b + 1) * MXU_DIM].astype(BF16)


def _mem_kv_kernel(mem_ref, g_ref, w_ref, kg_ref, bd_ref, k_ref, v_ref, kt_ref, vt_ref):
    h = _mm(_rmsnorm(mem_ref[...], g_ref[0]), w_ref[0])
    k = _head_rmsnorm(h[:, :MEM_W], bd_ref[...], kg_ref[0])
    v = h[:, MEM_W:]
    k_ref[0] = k
    v_ref[0] = v
    for b in range(kt_ref.shape[1]):
        rows = slice(b * N_MEM, (b + 1) * N_MEM)
        kt_ref[0, b] = k[rows].T
        vt_ref[0, b] = v[rows].T


def _shared_kv_kernel(x_ref, g_ref, w_ref, kg_ref, bd_ref, cos_ref, sin_ref,
                      k_ref, v_ref):
    h = _mm(_rmsnorm(x_ref[...], g_ref[...]), w_ref[...])
    k = _head_rmsnorm(h[:, :KV_W], bd_ref[...], kg_ref[...])
    k_ref[...] = _rope(k, cos_ref[...], sin_ref[...])
    v_ref[...] = h[:, KV_W:]


def _mixer_a_kernel(x_ref, g_ref, win_ref, cw_ref, cb_ref, wg_ref, bgx_ref, bga_ref,
                    lru_ref, mqg_ref, bd_ref, mk_ref, mv_ref, wout_ref, rc0_ref, h0_ref,
                    xo_ref, hlast_ref, rctail_ref,
                    conv_s, tail_s, a_s, b_s, hc_s, kcat_s, vcat_s):
    ts = x_ref.shape[0]
    n_slabs = D_RNN // LANES

    @pl.when(pl.program_id(1) == 0)
    def _():
        for s in range(n_slabs):
            tail_s[s] = rc0_ref[0, :, s * LANES:(s + 1) * LANES]
        a_s[:, 0:SUBLANES, :] = jnp.ones((n_slabs, SUBLANES, LANES), F32)
        b_s[:, 0:SUBLANES, :] = jnp.zeros((n_slabs, SUBLANES, LANES), F32)
        hc_s[...] = jnp.broadcast_to(h0_ref[0], hc_s.shape)
        kcat_s[...] = _head_masked_stack([mk_ref[0]], MEM_HEADS)
        vcat_s[...] = _head_masked_stack([mv_ref[0]], MEM_HEADS)

    x = x_ref[...]
    hn = _rmsnorm(x, g_ref[...]).astype(BF16)

    def in_proj(lo, hi):
        return jnp.dot(hn, win_ref[:, lo:hi], preferred_element_type=F32)

    xr_pre = in_proj(D_RNN, 2 * D_RNN)
    _slab_stage(xr_pre, conv_s, 0, tail_s, 0)
    xr = jnp.concatenate(_slab_taps(ts, n_slabs, conv_s, 0, cw_ref, cb_ref, 0), axis=1)
    rctail_ref[0] = xr_pre[ts - SUBLANES:]
    qm = in_proj(2 * D_RNN, 2 * D_RNN + MEM_W)
    a, b = _lru_coeffs(xr, wg_ref, bgx_ref[...], bga_ref[...], _log_sigmoid(lru_ref[...]))
    qn = _head_rmsnorm(qm, bd_ref[...], mqg_ref[...]) * ATT_SCALE
    gate = in_proj(0, D_RNN)
    mo = _mem_attention(qn, kcat_s[...], vcat_s[...])
    h = _lru_scan(a, b, a_s, b_s, hc_s)
    hlast_ref[0] = h[ts - 1:ts]
    main = h * _gelu(gate)

    y = _mm(jnp.concatenate([main, mo], axis=1), wout_ref[...])
    xo_ref[...] = x + y


def _mixer_b_kernel(sink_ref, x_ref, g_ref, win_ref, qg_ref, cos_ref, sin_ref,
                    kcur_ref, kprev_ref, vcur_ref, vprev_ref,
                    mqg_ref, bd_ref, mk_ref, mv_ref, wout_ref,
                    xo_ref, kcat_s, vcat_s):
    ts = x_ref.shape[0]
    i = pl.program_id(1)

    @pl.when(i == 0)
    def _():
        kcat_s[...] = _head_masked_stack([mk_ref[0]], MEM_HEADS)
        vcat_s[...] = _head_masked_stack([mv_ref[0]], MEM_HEADS)

    x = x_ref[...]
    hn = _rmsnorm(x, g_ref[...]).astype(BF16)

    def in_proj(lo, hi):
        return jnp.dot(hn, win_ref[:, lo:hi], preferred_element_type=F32)

    q = _head_rmsnorm(in_proj(0, Q_W), bd_ref[...], qg_ref[...], f32_stat=False)
    q = (_rope(q, cos_ref[...], sin_ref[...]) * ATT_SCALE).astype(BF16)

    row = lax.broadcasted_iota(jnp.int32, (WINDOW, 2 * WINDOW), 0)
    kj = lax.broadcasted_iota(jnp.int32, (WINDOW, 2 * WINDOW), 1) - WINDOW
    mask_inner = (kj <= row) & (kj >= row - WINDOW)
    mask_first = (kj <= row) & (kj >= jnp.maximum(row - WINDOW, jnp.where(i > 0, -WINDOW, 0)))

    kt = kcur_ref[...]
    vt = vcur_ref[...]

    def block_scores(jb):
        lo, hi = jb * WINDOW, (jb + 1) * WINDOW
        if jb == 0:
            kp, vp = kprev_ref[...], vprev_ref[...]
        else:
            kp, vp = kt[lo - WINDOW:lo], vt[lo - WINDOW:lo]
        kcat = _head_masked_stack([kp, kt[lo:hi]], N_KV)
        vcat = _head_masked_stack([vp, vt[lo:hi]], N_KV)
        qs = jnp.concatenate([q[lo:hi, g * KV_W:(g + 1) * KV_W] for g in range(GROUP)], axis=0)
        s = lax.dot_general(qs, kcat, (((1,), (1,)), ((), ())), preferred_element_type=F32)
        return s, vcat

    n_blocks = ts // WINDOW
    mains = []
    nxt = block_scores(0)
    qm = in_proj(Q_W, Q_W + MEM_W)
    qn = _head_rmsnorm(qm, bd_ref[...], mqg_ref[...], f32_stat=False) * ATT_SCALE
    s_mem = _mm_nt(qn, kcat_s[...])
    for jb in range(n_blocks):
        s, vcat = nxt
        if jb + 1 < n_blocks:
            nxt = block_scores(jb + 1)
        mask = mask_first if jb == 0 else mask_inner
        prow = []
        for g in range(GROUP):
            pseg = []
            for kv in range(N_KV):
                seg = s[g * WINDOW:(g + 1) * WINDOW, kv * 2 * WINDOW:(kv + 1) * 2 * WINDOW]
                seg = jnp.where(mask, seg, NEG)
                sink = sink_ref[g * N_KV + kv]
                m = jnp.maximum(jnp.max(seg, axis=-1, keepdims=True), sink)
                p = jnp.exp(seg - m)
                den = jnp.sum(p, axis=-1, keepdims=True) + jnp.exp(sink - m)
                pseg.append((p * (1.0 / den)).astype(BF16))
            prow.append(jnp.concatenate(pseg, axis=1))
        o = jnp.dot(jnp.concatenate(prow, axis=0), vcat, preferred_element_type=F32)
        mains.append(jnp.concatenate([o[g * WINDOW:(g + 1) * WINDOW] for g in range(GROUP)], axis=1))
    main = mains[0] if len(mains) == 1 else jnp.concatenate(mains, axis=0)

    mo = _mem_softmax_pv(s_mem, vcat_s[...])

    y = _mm(jnp.concatenate([main, mo], axis=1), wout_ref[...])
    xo_ref[...] = x + y


def _ffn_kernel(x_ref, g_ref, wup_ref, cw_ref, cb_ref, wdn_ref,
                xo_ref, utail_ref, conv_s, tail_s, act_s, xres_s, *, tiles_per_seq):
    ts = x_ref.shape[0]
    slabs = FF_CHUNK // LANES
    blk_slabs = MXU_DIM // LANES
    blks = FF_CHUNK // MXU_DIM
    i = pl.program_id(0)
    slot = i % 2
    pslot = 1 - slot

    @pl.when(i == 0)
    def _():
        act_s[1] = jnp.zeros(act_s.shape[1:], act_s.dtype)
        xres_s[1] = jnp.zeros(xres_s.shape[1:], xres_s.dtype)

    @pl.when(i % tiles_per_seq == 0)
    def _():
        tail_s[...] = jnp.zeros_like(tail_s)

    x = x_ref[...]
    xres_s[slot] = x
    hn = _rmsnorm(x, g_ref[...]).astype(BF16)

    def chunk_cols(j):
        return [(c, ((j % 2) * 2 + half) * slabs) for half, c in enumerate((j, N_FF_CHUNKS + j))]

    def up_project(j):
        for c, work0 in chunk_cols(j):
            for b in range(blks):
                blk = c * blks + b
                u = jnp.dot(hn, wup_ref[blk], preferred_element_type=F32)
                _slab_stage(u, conv_s, work0 + b * blk_slabs, tail_s, c * slabs + b * blk_slabs)
                utail_ref[0, :, blk * MXU_DIM:(blk + 1) * MXU_DIM] = u[ts - SUBLANES:]

    def activation(j):
        cg, cv = [_slab_taps(ts, slabs, conv_s, work0, cw_ref, cb_ref, c * FF_CHUNK)
                  for c, work0 in chunk_cols(j)]
        return jnp.concatenate([_gelu(a) * b for a, b in zip(cg, cv)], axis=1).astype(BF16)

    def down_prev(n):
        return jnp.dot(act_s[pslot], wdn_ref[n], preferred_element_type=F32)

    n_dn = wdn_ref.shape[0]
    outs = [None] * n_dn
    outs[0] = down_prev(0)
    up_project(0)
    for j in range(N_FF_CHUNKS):
        if j + 1 < N_FF_CHUNKS:
            up_project(j + 1)
        else:
            outs[1] = down_prev(1)
        act_s[slot, :, j * FF_CHUNK:(j + 1) * FF_CHUNK] = activation(j)
    for n in range(2, n_dn):
        outs[n] = down_prev(n)
    xo_ref[...] = xres_s[pslot] + jnp.concatenate(outs, axis=1)


def _dec_in_a_kernel(x_ref, g_ref, win_ref, cw_ref, cb_ref, wg_ref, bgx_ref, bga_ref,
                     lru_ref, mqg_ref, bd_ref, b0_ref, b1_ref, b2_ref, h0_ref,
                     main_ref, qn_ref, hnew_ref, xrpre_ref):
    u = _mm(_rmsnorm(x_ref[...], g_ref[...]), win_ref[...])
    gate = u[:, :D_RNN]
    xr_pre = u[:, D_RNN:2 * D_RNN]
    qm = u[:, 2 * D_RNN:]
    xr = b0_ref[...] * cw_ref[0:1, :]
    xr = xr + b1_ref[...] * cw_ref[1:2, :]
    xr = xr + b2_ref[...] * cw_ref[2:3, :]
    xr = xr + xr_pre * cw_ref[3:4, :]
    xr = xr + cb_ref[...]
    a, b = _lru_coeffs(xr, wg_ref, bgx_ref[...], bga_ref[...], _log_sigmoid(lru_ref[...]))
    h = a * h0_ref[...] + b
    main_ref[...] = h * _gelu(gate)
    qn_ref[...] = _head_rmsnorm(qm, bd_ref[...], mqg_ref[...]) * ATT_SCALE
    hnew_ref[...] = h
    xrpre_ref[...] = xr_pre


def _dec_in_b_kernel(x_ref, g_ref, win_ref, qg_ref, cos_ref, sin_ref, mqg_ref, bd_ref,
                     q_ref, qn_ref):
    u = _mm(_rmsnorm(x_ref[...], g_ref[...]), win_ref[...])
    q = _head_rmsnorm(u[:, :Q_W], bd_ref[...], qg_ref[...])
    q_ref[...] = _rope(q, cos_ref[...], sin_ref[...]) * ATT_SCALE
    qn_ref[...] = _head_rmsnorm(u[:, Q_W:], bd_ref[...], mqg_ref[...]) * ATT_SCALE


DEC_HEAD_ROWS = 16


def _own_head_lanes(n_heads, width):
    row = lax.broadcasted_iota(jnp.int32, (DEC_HEAD_ROWS, width), 0)
    lane = lax.broadcasted_iota(jnp.int32, (DEC_HEAD_ROWS, width), 1)
    start = (row % (width // HEAD_DIM)) * HEAD_DIM
    return (lane >= start) & (lane < start + HEAD_DIM) & (row < n_heads)


def _dec_mem_attn_kernel(q_ref, kt_ref, vt_ref, o_ref):
    own = _own_head_lanes(MEM_HEADS, MEM_W)
    for s in range(q_ref.shape[0]):
        q_rows = jnp.broadcast_to(q_ref[s:s + 1, :], (DEC_HEAD_ROWS, MEM_W))
        qbd = jnp.where(own, q_rows, 0.0).astype(BF16)
        sc = jnp.dot(qbd, kt_ref[s].astype(BF16), preferred_element_type=F32)
        m = jnp.max(sc, axis=-1, keepdims=True)
        p = jnp.exp(sc - m)
        den = jnp.sum(p, axis=-1, keepdims=True)
        pn = (p * (1.0 / den)).astype(BF16)
        o_all = lax.dot_general(pn, vt_ref[s].astype(BF16), (((1,), (1,)), ((), ())),
                                preferred_element_type=F32)
        o_ref[s:s + 1, :] = jnp.sum(jnp.where(own, o_all, 0.0), axis=0, keepdims=True)


def _dec_swa_kernel(q_ref, kt_ref, vt_ref, kn_ref, vn_ref, sink_ref, o_ref):
    n_heads = GROUP * N_KV
    own = _own_head_lanes(n_heads, KV_W)
    grp = lax.broadcasted_iota(jnp.int32, (DEC_HEAD_ROWS, KV_W), 0) // N_KV
    sink = sink_ref[:, 0:1]
    for s in range(q_ref.shape[0]):
        q_rows = jnp.zeros((DEC_HEAD_ROWS, KV_W), F32)
        for g in range(GROUP):
            qg = jnp.broadcast_to(q_ref[s:s + 1, g * KV_W:(g + 1) * KV_W], (DEC_HEAD_ROWS, KV_W))
            q_rows = jnp.where(grp == g, qg, q_rows)
        qbd = jnp.where(own, q_rows, 0.0)
        s_buf = jnp.dot(qbd.astype(BF16), kt_ref[s].astype(BF16), preferred_element_type=F32)
        s_new = jnp.sum(qbd * kn_ref[s:s + 1, :], axis=-1, keepdims=True)
        m = jnp.maximum(jnp.maximum(jnp.max(s_buf, axis=-1, keepdims=True), s_new), sink)
        p_buf = jnp.exp(s_buf - m)
        p_new = jnp.exp(s_new - m)
        den = jnp.sum(p_buf, axis=-1, keepdims=True) + p_new + jnp.exp(sink - m)
        r = 1.0 / den
        o_all = lax.dot_general((p_buf * r).astype(BF16), vt_ref[s].astype(BF16),
                                (((1,), (1,)), ((), ())), preferred_element_type=F32)
        o_all = jnp.where(own, o_all + (p_new * r) * vn_ref[s:s + 1, :], 0.0)
        o_sum = o_all + pltpu.roll(o_all, 1, 0)
        o_sum = o_sum + pltpu.roll(o_sum, 2, 0)
        for g in range(GROUP):
            last = (g + 1) * N_KV - 1
            o_ref[s:s + 1, g * KV_W:(g + 1) * KV_W] = o_sum[last:last + 1, :]


def _dec_out_ffn_kernel(x_ref, main_ref, mo_ref, wout_ref, g_ref,
                        wug_ref, wuv_ref, cwg_ref, cwv_ref, cbg_ref, cbv_ref, wdn_ref,
                        sg_ref, sv_ref, snew_in_ref,
                        xo_ref, snew_ref,
                        xmid_s, hn_s, acc_s):
    del snew_in_ref
    j = pl.program_id(0)

    @pl.when(j == 0)
    def _():
        y = _mm(jnp.concatenate([main_ref[...], mo_ref[...]], axis=1), wout_ref[...])
        xmid = x_ref[...] + y
        xmid_s[...] = xmid
        hn_s[...] = _rmsnorm(xmid, g_ref[...]).astype(BF16)
        acc_s[...] = jnp.zeros_like(acc_s)

    hn = hn_s[...]
    up = lambda w_ref: jnp.concatenate(
        [jnp.dot(hn, w_ref[b], preferred_element_type=F32) for b in range(w_ref.shape[0])], axis=1)
    ug = up(wug_ref)
    uv = up(wuv_ref)
    sg0, sg1 = sg_ref[:, 0, :], sg_ref[:, 1, :]
    sv0, sv1 = sv_ref[:, 0, :], sv_ref[:, 1, :]
    gate_cols = pl.ds(pl.multiple_of(j * FF_CHUNK, FF_CHUNK), FF_CHUNK)
    value_cols = pl.ds(pl.multiple_of(D_FF + j * FF_CHUNK, FF_CHUNK), FF_CHUNK)
    snew_ref[:, 0, gate_cols] = sg1
    snew_ref[:, 1, gate_cols] = ug
    snew_ref[:, 0, value_cols] = sv1
    snew_ref[:, 1, value_cols] = uv
    cg = sg0 * cwg_ref[0:1, :] + sg1 * cwg_ref[1:2, :] + ug * cwg_ref[2:3, :] + cbg_ref[...]
    cv = sv0 * cwv_ref[0:1, :] + sv1 * cwv_ref[1:2, :] + uv * cwv_ref[2:3, :] + cbv_ref[...]
    act = (_gelu(cg) * cv).astype(BF16)
    for n in range(wdn_ref.shape[0]):
        acc_s[:, n * MXU_DIM:(n + 1) * MXU_DIM] += jnp.dot(act, wdn_ref[n],
                                                           preferred_element_type=F32)

    @pl.when(j == pl.num_programs(0) - 1)
    def _():
        xo_ref[...] = xmid_s[...] + acc_s[...]


def _const_spec(shape):
    nd = len(shape)
    return pl.BlockSpec(shape, lambda *_: (0,) * nd)


def _layer_spec(arr, layer):
    nd = arr.ndim - 1
    return pl.BlockSpec((None,) + arr.shape[1:], lambda *_: (layer,) + (0,) * nd)


def _resident_spec(op):
    return _layer_spec(*op) if isinstance(op, tuple) else _const_spec(op.shape)


def _operand(op):
    return op[0] if isinstance(op, tuple) else op


def _params(*sem):
    return pltpu.CompilerParams(dimension_semantics=sem, vmem_limit_bytes=VMEM_LIMIT_BYTES)


def _cast_blocks(w):
    layers, r, c = w.shape
    tile = CAST_TILE
    per = tile // MXU_DIM
    return pl.pallas_call(
        _cast_blocks_kernel,
        grid=(layers, r // tile, c // tile),
        in_specs=[pl.BlockSpec((None, tile, tile), lambda l, i, j: (l, i, j))],
        out_specs=pl.BlockSpec((None, per, tile, MXU_DIM), lambda l, i, j: (l, j, i, 0)),
        out_shape=jax.ShapeDtypeStruct((layers, c // MXU_DIM, r, MXU_DIM), BF16),
        compiler_params=_params("arbitrary", "arbitrary", "arbitrary"),
        name="cast_blocks",
    )(w)


def _mem_kv(mem2d, g, w, kg, bd):
    depth = w.shape[0]
    rows = mem2d.shape[0]
    bsz = rows // N_MEM
    out = jax.ShapeDtypeStruct((depth, rows, MEM_W), F32)
    out_t = jax.ShapeDtypeStruct((depth, bsz, MEM_W, N_MEM), F32)
    return pl.pallas_call(
        _mem_kv_kernel,
        grid=(depth,),
        in_specs=[
            _const_spec(mem2d.shape),
            pl.BlockSpec((1, 1, D_MODEL), lambda l: (l, 0, 0)),
            pl.BlockSpec((1, D_MODEL, 2 * MEM_W), lambda l: (l, 0, 0)),
            pl.BlockSpec((1, 1, MEM_W), lambda l: (l, 0, 0)),
            _const_spec(bd.shape),
        ],
        out_specs=[pl.BlockSpec((1, rows, MEM_W), lambda l: (l, 0, 0))] * 2
        + [pl.BlockSpec((1, bsz, MEM_W, N_MEM), lambda l: (l, 0, 0, 0))] * 2,
        out_shape=[out, out, out_t, out_t],
        compiler_params=_params("arbitrary"),
        name="mem_kv",
    )(mem2d, g, w, kg, bd)


def _shared_kv(x2d, g, w, kg, bd, cos_t, sin_t, ts):
    rows = x2d.shape[0]
    tab_blocks = cos_t.shape[0] // ts
    out = jax.ShapeDtypeStruct((rows, KV_W), F32)
    return pl.pallas_call(
        _shared_kv_kernel,
        grid=(rows // ts,),
        in_specs=[
            pl.BlockSpec((ts, D_MODEL), lambda i: (i, 0)),
            _const_spec(g.shape), _const_spec(w.shape), _const_spec(kg.shape),
            _const_spec(bd.shape),
            pl.BlockSpec((ts, LANES), lambda i: (i % tab_blocks, 0)),
            pl.BlockSpec((ts, LANES), lambda i: (i % tab_blocks, 0)),
        ],
        out_specs=[pl.BlockSpec((ts, KV_W), lambda i: (i, 0))] * 2,
        out_shape=[out, out],
        compiler_params=_params("arbitrary"),
        name="shared_kv",
    )(x2d, g, w, kg, bd, cos_t, sin_t)


def _mixer_a(x2d, bsz, g, win, cw, cb, wg, bgx, bga, lru, mqg, bd, mk, mv, wout, rc0, h0):
    rows = x2d.shape[0]
    ts = TS_MIX_A
    nt = rows // bsz // ts
    consts = (g, win, cw, cb, wg, bgx, bga, lru, mqg, bd)
    per_b3 = lambda b, i: (b, 0, 0)
    return pl.pallas_call(
        _mixer_a_kernel,
        grid=(bsz, nt),
        in_specs=[pl.BlockSpec((ts, D_MODEL), lambda b, i: (b * nt + i, 0))]
        + [_resident_spec(c) for c in consts]
        + [pl.BlockSpec((1, N_MEM, MEM_W), per_b3)] * 2
        + [_resident_spec(wout),
           pl.BlockSpec((1, SUBLANES, D_RNN), per_b3),
           pl.BlockSpec((1, 1, D_RNN), per_b3)],
        out_specs=[
            pl.BlockSpec((ts, D_MODEL), lambda b, i: (b * nt + i, 0)),
            pl.BlockSpec((1, 1, D_RNN), per_b3),
            pl.BlockSpec((1, SUBLANES, D_RNN), per_b3),
        ],
        out_shape=[
            jax.ShapeDtypeStruct((rows, D_MODEL), F32),
            jax.ShapeDtypeStruct((bsz, 1, D_RNN), F32),
            jax.ShapeDtypeStruct((bsz, SUBLANES, D_RNN), F32),
        ],
        scratch_shapes=[
            pltpu.VMEM((D_RNN // LANES, SUBLANES + ts, LANES), F32),
            pltpu.VMEM((D_RNN // LANES, SUBLANES, LANES), F32),
            pltpu.VMEM((D_RNN // LANES, SUBLANES + ts, LANES), F32),
            pltpu.VMEM((D_RNN // LANES, SUBLANES + ts, LANES), F32),
            pltpu.VMEM((SUBLANES, D_RNN), F32),
            pltpu.VMEM((MEM_HEADS * N_MEM, MEM_W), BF16),
            pltpu.VMEM((MEM_HEADS * N_MEM, MEM_W), BF16),
        ],
        compiler_params=_params("arbitrary", "arbitrary"),
        name="mixer_a",
    )(x2d, *[_operand(c) for c in consts], mk, mv, _operand(wout), rc0, h0)


def _mixer_b(x2d, bsz, sink_tab, g, win, qg, cos_t, sin_t, ksh, vsh, mqg, bd, mk, mv, wout):
    rows = x2d.shape[0]
    ts = TS_MIX_B
    nt = rows // bsz // ts
    wpt = ts // WINDOW
    cur = lambda b, i: (b * nt + i, 0)
    prev = lambda b, i: (jnp.maximum((b * nt + i) * wpt - 1, 0), 0)
    table = lambda b, i: (i, 0)
    per_b3 = lambda b, i: (b, 0, 0)
    return pl.pallas_call(
        _mixer_b_kernel,
        grid=(bsz, nt),
        in_specs=[
            pl.BlockSpec(memory_space=pltpu.SMEM),
            pl.BlockSpec((ts, D_MODEL), cur),
            _const_spec(g.shape), _resident_spec(win), _const_spec(qg.shape),
            pl.BlockSpec((ts, LANES), table),
            pl.BlockSpec((ts, LANES), table),
            pl.BlockSpec((ts, KV_W), cur), pl.BlockSpec((WINDOW, KV_W), prev),
            pl.BlockSpec((ts, KV_W), cur), pl.BlockSpec((WINDOW, KV_W), prev),
            _const_spec(mqg.shape), _const_spec(bd.shape),
            pl.BlockSpec((1, N_MEM, MEM_W), per_b3), pl.BlockSpec((1, N_MEM, MEM_W), per_b3),
            _resident_spec(wout),
        ],
        out_specs=pl.BlockSpec((ts, D_MODEL), cur),
        out_shape=jax.ShapeDtypeStruct((rows, D_MODEL), F32),
        scratch_shapes=[
            pltpu.VMEM((MEM_HEADS * N_MEM, MEM_W), BF16),
            pltpu.VMEM((MEM_HEADS * N_MEM, MEM_W), BF16),
        ],
        compiler_params=_params("arbitrary", "arbitrary"),
        name="mixer_b",
    )(sink_tab, x2d, g, _operand(win), qg, cos_t, sin_t, ksh, ksh, vsh, vsh, mqg, bd, mk, mv,
      _operand(wout))


def _ffn(x2d, bsz, layer, g, wup, cw, cb, wdn):
    rows = x2d.shape[0]
    ts = TS_FFN
    nt = rows // ts
    tiles_per_seq = rows // bsz // ts
    last = nt - 1
    return pl.pallas_call(
        functools.partial(_ffn_kernel, tiles_per_seq=tiles_per_seq),
        grid=(nt + 1,),
        in_specs=[pl.BlockSpec((ts, D_MODEL), lambda i: (jnp.minimum(i, last), 0)),
                  _const_spec(g.shape), _layer_spec(wup, layer), _const_spec(cw.shape),
                  _const_spec(cb.shape), _layer_spec(wdn, layer)],
        out_specs=[
            pl.BlockSpec((ts, D_MODEL), lambda i: (jnp.maximum(i - 1, 0), 0)),
            pl.BlockSpec((1, SUBLANES, 2 * D_FF),
                         lambda i: (jnp.minimum(i, last) // tiles_per_seq, 0, 0)),
        ],
        out_shape=[
            jax.ShapeDtypeStruct((rows, D_MODEL), F32),
            jax.ShapeDtypeStruct((bsz, SUBLANES, 2 * D_FF), F32),
        ],
        scratch_shapes=[pltpu.VMEM((4 * FF_CHUNK // LANES, SUBLANES + ts, LANES), F32),
                        pltpu.VMEM((2 * D_FF // LANES, SUBLANES, LANES), F32),
                        pltpu.VMEM((2, ts, D_FF), BF16),
                        pltpu.VMEM((2, ts, D_MODEL), F32)],
        compiler_params=_params("arbitrary"),
        name="ffn",
    )(x2d, g, wup, cw, cb, wdn)


def _dec_in_a(x, g, win, cw, cb, wg, bgx, bga, lru, mqg, bd, rc, h0):
    n = x.shape[0]
    consts = (x, g, win, cw, cb, wg, bgx, bga, lru, mqg, bd)
    buf = lambda j: pl.BlockSpec((None, n, D_RNN), lambda i: (j, 0, 0))
    return pl.pallas_call(
        _dec_in_a_kernel,
        grid=(1,),
        in_specs=[_resident_spec(c) for c in consts]
        + [buf(0), buf(1), buf(2), _const_spec(h0.shape)],
        out_specs=[_const_spec((n, D_RNN)), _const_spec((n, MEM_W)),
                   _const_spec((n, D_RNN)), _const_spec((n, D_RNN))],
        out_shape=[jax.ShapeDtypeStruct((n, D_RNN), F32), jax.ShapeDtypeStruct((n, MEM_W), F32),
                   jax.ShapeDtypeStruct((n, D_RNN), F32), jax.ShapeDtypeStruct((n, D_RNN), F32)],
        compiler_params=_params("arbitrary"),
        name="dec_in_a",
    )(*[_operand(c) for c in consts], rc, rc, rc, h0)


def _dec_in_b(x, g, win, qg, cos_t, sin_t, mqg, bd):
    n = x.shape[0]
    args = (x, g, win, qg, cos_t, sin_t, mqg, bd)
    return pl.pallas_call(
        _dec_in_b_kernel,
        grid=(1,),
        in_specs=[_resident_spec(a) for a in args],
        out_specs=[_const_spec((n, Q_W)), _const_spec((n, MEM_W))],
        out_shape=[jax.ShapeDtypeStruct((n, Q_W), F32), jax.ShapeDtypeStruct((n, MEM_W), F32)],
        compiler_params=_params("arbitrary"),
        name="dec_in_b",
    )(*[_operand(a) for a in args])


def _dec_mem_attn(qn, ckt, cvt, layer):
    n = qn.shape[0]
    sb = DEC_SEQ_BLOCK
    return pl.pallas_call(
        _dec_mem_attn_kernel,
        grid=(n // sb,),
        in_specs=[pl.BlockSpec((sb, MEM_W), lambda i: (i, 0)),
                  pl.BlockSpec((None, sb, MEM_W, N_MEM), lambda i: (layer, i, 0, 0)),
                  pl.BlockSpec((None, sb, MEM_W, N_MEM), lambda i: (layer, i, 0, 0))],
        out_specs=pl.BlockSpec((sb, MEM_W), lambda i: (i, 0)),
        out_shape=jax.ShapeDtypeStruct((n, MEM_W), F32),
        compiler_params=_params("arbitrary"),
        name="dec_mem_attn",
    )(qn, ckt, cvt)


def _dec_swa(q, kbt, vbt, kn, vn, sink_rows):
    n = q.shape[0]
    wb = kbt.shape[2]
    sb = DEC_SEQ_BLOCK
    assert wb <= WINDOW
    return pl.pallas_call(
        _dec_swa_kernel,
        grid=(n // sb,),
        in_specs=[pl.BlockSpec((sb, Q_W), lambda i: (i, 0)),
                  pl.BlockSpec((sb, KV_W, wb), lambda i: (i, 0, 0)),
                  pl.BlockSpec((sb, KV_W, wb), lambda i: (i, 0, 0)),
                  pl.BlockSpec((sb, KV_W), lambda i: (i, 0)),
                  pl.BlockSpec((sb, KV_W), lambda i: (i, 0)),
                  _const_spec(sink_rows.shape)],
        out_specs=pl.BlockSpec((sb, Q_W), lambda i: (i, 0)),
        out_shape=jax.ShapeDtypeStruct((n, Q_W), F32),
        compiler_params=_params("arbitrary"),
        name="dec_swa",
    )(q, kbt, vbt, kn, vn, sink_rows)


def _dec_out_ffn(x, main, mo, wout, wout_layer, layer, g, wup, cw, cb, wdn, st, st_new):
    n = x.shape[0]
    nch = N_FF_CHUNKS
    blks = FF_CHUNK // MXU_DIM
    lo = lambda j: (0, j)
    hi = lambda j: (0, nch + j)
    state = lambda off: pl.BlockSpec((None, n, CONV_F - 1, FF_CHUNK),
                                     lambda j: (layer, 0, 0, off + j))
    up_blocks = lambda off: pl.BlockSpec((None, blks, D_MODEL, MXU_DIM),
                                         lambda j: (layer, off + j, 0, 0))
    operands = (x, main, mo, wout, g, wup, wup, cw, cw, cb, cb, wdn, st, st, st_new)
    return pl.pallas_call(
        _dec_out_ffn_kernel,
        grid=(nch,),
        in_specs=[_const_spec(x.shape), _const_spec(main.shape), _const_spec(mo.shape),
                  _layer_spec(wout, wout_layer), _const_spec(g.shape),
                  up_blocks(0), up_blocks(nch),
                  pl.BlockSpec((CONV_F, FF_CHUNK), lo), pl.BlockSpec((CONV_F, FF_CHUNK), hi),
                  pl.BlockSpec((1, FF_CHUNK), lo), pl.BlockSpec((1, FF_CHUNK), hi),
                  pl.BlockSpec((None, wdn.shape[1], FF_CHUNK, MXU_DIM), lambda j: (layer, 0, j, 0)),
                  state(0), state(nch),
                  pl.BlockSpec(memory_space=pl.ANY)],
        out_specs=[_const_spec((n, D_MODEL)),
                   pl.BlockSpec((None, n, CONV_F - 1, 2 * D_FF), lambda j: (layer, 0, 0, 0))],
        out_shape=[jax.ShapeDtypeStruct((n, D_MODEL), F32),
                   jax.ShapeDtypeStruct(st_new.shape, F32)],
        input_output_aliases={len(operands) - 1: 1},
        scratch_shapes=[pltpu.VMEM((n, D_MODEL), F32), pltpu.VMEM((n, D_MODEL), BF16),
                        pltpu.VMEM((n, D_MODEL), F32)],
        compiler_params=_params("arbitrary"),
        name="dec_out_ffn",
    )(*operands)


def _rope_tables(pos):
    half = HEAD_DIM // 2
    inv = ROPE_THETA ** (-jnp.arange(half, dtype=F32) / half)
    ang = pos.astype(F32)[:, None] * inv[None, :]
    cos = jnp.cos(ang)
    sin = jnp.sin(ang)
    reps = LANES // HEAD_DIM
    cos_t = jnp.tile(jnp.concatenate([cos, cos], axis=1), (1, reps))
    sin_t = jnp.tile(jnp.concatenate([-sin, sin], axis=1), (1, reps))
    return cos_t, sin_t


def _block_diag_gates(wx, wa):
    per = MXU_DIM // HEAD_DIM
    eye = jnp.eye(per, dtype=F32)

    def bd(w):
        w4 = w.reshape(RNN_BLOCKS // per, per, HEAD_DIM, HEAD_DIM)
        return jnp.einsum('ckij,kK->ckiKj', w4, eye).reshape(RNN_BLOCKS // per, MXU_DIM, MXU_DIM)

    return jnp.concatenate([bd(wx), bd(wa)], axis=2).astype(BF16)


def kernel(x_prompt, x_sample, state_rglru_h, state_rglru_conv, state_ffn_conv, cache_swa_k, cache_swa_v, cache_mem_k, cache_mem_v, mem_prompt, norm_mix_g, norm_ffn_g, w_in_a, rnn_conv_w, rnn_conv_b, w_gate_x, b_gate_x, w_gate_a, b_gate_a, lru_param, w_in_b, q_norm_g, sinks, kv_norm_g, w_kv, k_norm_g, mem_norm_g, w_mem_kv, mem_q_norm_g, mem_k_norm_g, w_out, w_ffn_up, ffn_conv_w, ffn_conv_b, w_ffn_down):
    bsz, seq, _ = x_prompt.shape
    dbsz = x_sample.shape[0]
    depth = norm_mix_g.shape[0]
    n_a = w_in_a.shape[0]
    assert x_sample.shape[1] == 1
    assert seq % TS_MIX_A == 0 and seq % TS_MIX_B == 0 and TS_MIX_B % WINDOW == 0
    assert seq % TS_FFN == 0 and seq % TS_KV == 0

    n_b = w_in_b.shape[0]
    wq = w_in_b[:, :, :Q_W].astype(BF16).reshape(n_b, D_MODEL, N_KV, GROUP, HEAD_DIM)
    wq = wq.transpose(0, 1, 3, 2, 4).reshape(n_b, D_MODEL, Q_W)
    wo_main = w_out[n_a:, :Q_W].astype(BF16).reshape(n_b, N_KV, GROUP, HEAD_DIM, D_MODEL)
    wo_main = wo_main.transpose(0, 2, 1, 3, 4).reshape(n_b, Q_W, D_MODEL)
    bd =(jnp.kron(jnp.eye(MXU_DIM // HEAD_DIM, dtype=F32),
                   jnp.ones((HEAD_DIM, HEAD_DIM), F32)) / HEAD_DIM).astype(BF16)

    row = lambda v: v.reshape(1, -1)
    tile_row = lambda v, n: jnp.tile(v, n).reshape(1, -1)
    w_in_a_b = w_in_a.astype(BF16)
    w_in_b_b = jnp.concatenate([wq, w_in_b[:, :, Q_W:].astype(BF16)], axis=2)
    w_out_b = w_out.astype(BF16)
    w_out_perm_b = jnp.concatenate([wo_main, w_out_b[n_a:, Q_W:]], axis=1)
    w_kv_b = w_kv.astype(BF16)
    w_mem_b = w_mem_kv.astype(BF16)
    wup_b = _cast_blocks(w_ffn_up)
    wdn_b = _cast_blocks(w_ffn_down)
    fcw = ffn_conv_w
    fcb = ffn_conv_b.reshape(depth, 1, 2 * D_FF)
    wg_b = jnp.stack([_block_diag_gates(w_gate_x[l], w_gate_a[l]) for l in range(n_a)])
    sink_gk = sinks.reshape(-1, N_KV, GROUP).transpose(0, 2, 1)

    cos_p, sin_p = _rope_tables(jnp.arange(seq, dtype=jnp.int32))
    pos_s = PAST_LEN + jnp.zeros((dbsz,), jnp.int32)
    cos_s, sin_s = _rope_tables(pos_s)

    mem2d = mem_prompt.reshape(bsz * N_MEM, D_MODEL)
    pmk, pmv, pmk_t, pmv_t = _mem_kv(
        mem2d, mem_norm_g.reshape(depth, 1, D_MODEL), w_mem_b,
        jnp.tile(mem_k_norm_g, (1, MEM_HEADS)).reshape(depth, 1, MEM_W), bd)
    pmk4 = pmk.reshape(depth, bsz, N_MEM, MEM_W)
    pmv4 = pmv.reshape(depth, bsz, N_MEM, MEM_W)

    x = x_prompt.reshape(bsz * seq, D_MODEL)
    zeros_rc = jnp.zeros((bsz, SUBLANES, D_RNN), F32)
    zeros_h = jnp.zeros((bsz, 1, D_RNN), F32)
    p_h, p_rc, p_fc = [], [], []
    ksh = vsh = None
    for l in range(depth):
        mqg = tile_row(mem_q_norm_g[l], MEM_HEADS)
        if l < n_a:
            x, hl, rct = _mixer_a(
                x, bsz, row(norm_mix_g[l]), (w_in_a_b, l), rnn_conv_w[l], row(rnn_conv_b[l]),
                (wg_b, l), row(b_gate_x[l]), row(b_gate_a[l]), row(lru_param[l]), mqg, bd,
                pmk4[l], pmv4[l], (w_out_b, l), zeros_rc, zeros_h)
            p_h.append(hl.reshape(bsz, D_RNN))
            p_rc.append(rct[:, SUBLANES - (CONV_A - 1):])
        else:
            j = l - n_a
            x = _mixer_b(
                x, bsz, sink_gk[j].reshape(-1), row(norm_mix_g[l]), (w_in_b_b, j),
                tile_row(q_norm_g[j], N_Q), cos_p, sin_p, ksh, vsh, mqg, bd,
                pmk4[l], pmv4[l], (w_out_perm_b, j))
        x, ut = _ffn(x, bsz, l, row(norm_ffn_g[l]), wup_b, fcw[l], fcb[l], wdn_b)
        p_fc.append(ut[:, SUBLANES - (CONV_F - 1):])
        if l == n_a - 1:
            ksh, vsh = _shared_kv(x, row(kv_norm_g), w_kv_b, tile_row(k_norm_g, N_KV), bd,
                                  cos_p, sin_p, TS_KV)
    y_prompt = x.reshape(bsz, seq, D_MODEL)
    keep = min(WINDOW, seq)
    p_k = ksh.reshape(bsz, seq, KV_W)[:, seq - keep:].reshape(bsz, keep, N_KV, HEAD_DIM)
    p_v = vsh.reshape(bsz, seq, KV_W)[:, seq - keep:].reshape(bsz, keep, N_KV, HEAD_DIM)
    to_cache = lambda t: t.reshape(depth, bsz, MEM_HEADS, HEAD_DIM, N_MEM).transpose(0, 1, 4, 2, 3)
    p_mem_k = to_cache(pmk_t)
    p_mem_v = to_cache(pmv_t)

    xs = x_sample.reshape(dbsz, D_MODEL)
    cmk = cache_mem_k.transpose(0, 1, 3, 4, 2).reshape(depth, dbsz, MEM_W, N_MEM)
    cmv = cache_mem_v.transpose(0, 1, 3, 4, 2).reshape(depth, dbsz, MEM_W, N_MEM)
    wb = cache_swa_k.shape[1]
    ckb = cache_swa_k.transpose(0, 2, 3, 1).reshape(dbsz, KV_W, wb)
    cvb = cache_swa_v.transpose(0, 2, 3, 1).reshape(dbsz, KV_W, wb)
    s_h, s_rc = [], []
    s_fc = jnp.zeros_like(state_ffn_conv)
    kn = vn = None
    for l in range(depth):
        mqg = tile_row(mem_q_norm_g[l], MEM_HEADS)
        if l < n_a:
            main, qn, hnew, xrpre = _dec_in_a(
                xs, row(norm_mix_g[l]), (w_in_a_b, l), rnn_conv_w[l], row(rnn_conv_b[l]),
                (wg_b, l), row(b_gate_x[l]), row(b_gate_a[l]), row(lru_param[l]), mqg, bd,
                state_rglru_conv[l].transpose(1, 0, 2), state_rglru_h[l])
            s_h.append(hnew)
            s_rc.append(jnp.concatenate([state_rglru_conv[l][:, 1:], xrpre[:, None, :]], axis=1))
            wo, wo_layer = w_out_b, l
        else:
            j = l - n_a
            q, qn = _dec_in_b(xs, row(norm_mix_g[l]), (w_in_b_b, j), tile_row(q_norm_g[j], N_Q),
                              cos_s, sin_s, mqg, bd)
            sink_rows = jnp.zeros((DEC_HEAD_ROWS, LANES), F32).at[:N_Q].set(
                jnp.broadcast_to(sink_gk[j].reshape(N_Q, 1), (N_Q, LANES)))
            main = _dec_swa(q, ckb, cvb, kn, vn, sink_rows)
            wo, wo_layer = w_out_perm_b, j
        mo = _dec_mem_attn(qn, cmk, cmv, l)
        xs, s_fc = _dec_out_ffn(xs, main, mo, wo, wo_layer, l, row(norm_ffn_g[l]), wup_b,
                                fcw[l], fcb[l], wdn_b, state_ffn_conv, s_fc)
        if l == n_a - 1:
            kn, vn = _shared_kv(xs, row(kv_norm_g), w_kv_b, tile_row(k_norm_g, N_KV), bd,
                                cos_s, sin_s, dbsz)
    y_sample = xs.reshape(dbsz, 1, D_MODEL)
    s_k = kn.reshape(dbsz, 1, N_KV, HEAD_DIM)
    s_v = vn.reshape(dbsz, 1, N_KV, HEAD_DIM)

    return (y_prompt, y_sample, jnp.stack(p_h), jnp.stack(p_rc), jnp.stack(p_fc), p_k, p_v,
            p_mem_k, p_mem_v, jnp.stack(s_h), jnp.stack(s_rc), s_fc, s_k, s_v)
```

```python
import functools
import math

import jax
import jax.numpy as jnp
from jax import lax
from jax.experimental import pallas as pl
from jax.experimental.pallas import tpu as pltpu

F32 = jnp.float32
BF16 = jnp.bfloat16

D_MODEL = 1024
HEAD_DIM = 64
MEM_HEADS = 4
MEM_W = MEM_HEADS * HEAD_DIM
N_MEM = 256
D_RNN = D_MODEL - MEM_W
RNN_BLOCKS = D_RNN // HEAD_DIM
CONV_A = 4
LRU_C = 8.0
N_Q = D_RNN // HEAD_DIM
N_KV = 4
GROUP = N_Q // N_KV
Q_W = N_Q * HEAD_DIM
KV_W = N_KV * HEAD_DIM
WINDOW = 128
ROPE_THETA = 10000.0
D_FF = 3 * D_MODEL
CONV_F = 3
EPS = 1e-6
NEG = -1e30
ATT_SCALE = HEAD_DIM ** -0.5
PAST_LEN = 8192

SUBLANES = 8
LANES = 128
MXU_DIM = 256
VMEM_LIMIT_BYTES = 56 * 1024 * 1024

TS_MIX_A = 1024
TS_MIX_B = 512
TS_FFN = 512
TS_KV = 1024
FF_CHUNK = 512
N_FF_CHUNKS = D_FF // FF_CHUNK
DEC_SEQ_BLOCK = 32
CAST_TILE = 1024


def _mm(a, b):
    return jnp.dot(a.astype(BF16), b, preferred_element_type=F32)


def _mm_nt(a, b):
    return lax.dot_general(a.astype(BF16), b, (((1,), (1,)), ((), ())),
                           preferred_element_type=F32)


def _mm_split(a, b):
    hi = a.astype(BF16)
    lo = (a - hi.astype(F32)).astype(BF16)
    return (jnp.dot(hi, b, preferred_element_type=F32)
            + jnp.dot(lo, b, preferred_element_type=F32))


def _rmsnorm(x, g):
    ms = jnp.mean(x * x, axis=-1, keepdims=True)
    return x * lax.rsqrt(ms + EPS) * g


def _head_rmsnorm(x, bd, g, f32_stat=True):
    mean_sq = _mm_split if f32_stat else _mm
    parts = []
    for c in range(x.shape[1] // MXU_DIM):
        xc = x[:, c * MXU_DIM:(c + 1) * MXU_DIM]
        ms = mean_sq(xc * xc, bd)
        parts.append(xc * lax.rsqrt(ms + EPS))
    y = parts[0] if len(parts) == 1 else jnp.concatenate(parts, axis=1)
    return y * g


def _tile_lanes(t, width):
    reps = width // t.shape[1]
    return t if reps == 1 else jnp.concatenate([t] * reps, axis=1)


def _rope(x, cos_t, sin_t):
    w = x.shape[1]
    lane = lax.broadcasted_iota(jnp.int32, x.shape, 1)
    first = (lane % HEAD_DIM) < (HEAD_DIM // 2)
    swapped = jnp.where(first, pltpu.roll(x, w - HEAD_DIM // 2, 1),
                        pltpu.roll(x, HEAD_DIM // 2, 1))
    return x * _tile_lanes(cos_t, w) + swapped * _tile_lanes(sin_t, w)


def _gelu(x):
    c = math.sqrt(2.0 / math.pi)
    return x * (0.5 * (1.0 + jnp.tanh(c * (x + 0.044715 * (x * x * x)))))


def _log_sigmoid(x):
    return jnp.minimum(x, 0.0) - jnp.log1p(jnp.exp(-jnp.abs(x)))


def _slab_stage(x, work_ref, work0, tail_ref, tail0):
    ts = x.shape[0]
    for s in range(x.shape[1] // LANES):
        xs = x[:, s * LANES:(s + 1) * LANES]
        buf = work_ref.at[work0 + s]
        buf[0:SUBLANES, :] = tail_ref[tail0 + s]
        buf[SUBLANES:SUBLANES + ts, :] = xs
        tail_ref[tail0 + s] = xs[ts - SUBLANES:]


def _slab_taps(ts, n_slabs, work_ref, work0, w_ref, b_ref, col0):
    k = w_ref.shape[0]
    outs = []
    for s in range(n_slabs):
        lanes = slice(col0 + s * LANES, col0 + (s + 1) * LANES)
        buf = work_ref.at[work0 + s]
        acc = buf[SUBLANES - (k - 1):SUBLANES - (k - 1) + ts, :] * w_ref[0:1, lanes]
        for j in range(1, k):
            off = SUBLANES - (k - 1 - j)
            acc = acc + buf[off:off + ts, :] * w_ref[j:j + 1, lanes]
        outs.append(acc + b_ref[:, lanes])
    return outs


def _sqrt_pos(x):
    return jnp.where(x > 0.0, x * lax.rsqrt(x), 0.0)


def _lru_coeffs(xr, wg_ref, bgx, bga, logsig):
    xb = xr.astype(BF16)
    gxs, gas = [], []
    for c in range(D_RNN // MXU_DIM):
        gg = jnp.dot(xb[:, c * MXU_DIM:(c + 1) * MXU_DIM], wg_ref[c],
                     preferred_element_type=F32)
        gxs.append(gg[:, :MXU_DIM])
        gas.append(gg[:, MXU_DIM:])
    gx = jax.nn.sigmoid(jnp.concatenate(gxs, axis=1) + bgx)
    ga = jax.nn.sigmoid(jnp.concatenate(gas, axis=1) + bga)
    log_a = ga * (LRU_C * logsig)
    a = jnp.exp(log_a)
    mult = _sqrt_pos(-jnp.tanh(log_a) * (a * a + 1.0))
    return a, mult * gx * xr


def _lru_scan(a, b, a_ref, b_ref, hc_ref):
    ts = a.shape[0]
    outs = []
    for s in range(a.shape[1] // LANES):
        lanes = slice(s * LANES, (s + 1) * LANES)
        a_s, b_s = a[:, lanes], b[:, lanes]
        abuf, bbuf = a_ref.at[s], b_ref.at[s]
        d = 1
        while d < SUBLANES:
            abuf[SUBLANES:SUBLANES + ts, :] = a_s
            bbuf[SUBLANES:SUBLANES + ts, :] = b_s
            b_s = a_s * bbuf[SUBLANES - d:SUBLANES - d + ts, :] + b_s
            a_s = a_s * abuf[SUBLANES - d:SUBLANES - d + ts, :]
            d *= 2
        h = hc_ref[:, lanes]
        hs = []
        for q in range(ts // SUBLANES):
            rows = slice(q * SUBLANES, (q + 1) * SUBLANES)
            h = a_s[rows] * h + b_s[rows]
            hs.append(h)
        hc_ref[:, lanes] = jnp.broadcast_to(h[SUBLANES - 1:], (SUBLANES, LANES))
        outs.append(jnp.concatenate(hs, axis=0))
    return jnp.concatenate(outs, axis=1)


def _head_mask(shape, h):
    lane = lax.broadcasted_iota(jnp.int32, shape, 1)
    return (lane >= h * HEAD_DIM) & (lane < (h + 1) * HEAD_DIM)


def _head_masked_stack(blocks, n_heads):
    parts = []
    for h in range(n_heads):
        for blk in blocks:
            parts.append(jnp.where(_head_mask(blk.shape, h), blk, 0.0).astype(BF16))
    return jnp.concatenate(parts, axis=0)


def _mem_attention(qn, kcat, vcat):
    return _mem_softmax_pv(_mm_nt(qn, kcat), vcat)


def _mem_softmax_pv(s, vcat):
    parts = []
    for h in range(MEM_HEADS):
        sh = s[:, h * N_MEM:(h + 1) * N_MEM]
        m = jnp.max(sh, axis=-1, keepdims=True)
        p = jnp.exp(sh - m)
        den = jnp.sum(p, axis=-1, keepdims=True)
        parts.append((p * (1.0 / den)).astype(BF16))
    return jnp.dot(jnp.concatenate(parts, axis=1), vcat, preferred_element_type=F32)


def _cast_blocks_kernel(w_ref, o_ref):
    for b in range(o_ref.shape[0]):
        o_ref[b] = w_ref[:, b * MXU_DIM:(b + 1) * MXU_DIM].astype(BF16)


def _mem_kv_kernel(mem_ref, g_ref, w_ref, kg_ref, bd_ref, k_ref, v_ref, kt_ref, vt_ref):
    h = _mm(_rmsnorm(mem_ref[...], g_ref[0]), w_ref[0])
    k = _head_rmsnorm(h[:, :MEM_W], bd_ref[...], kg_ref[0])
    v = h[:, MEM_W:]
    k_ref[0] = k
    v_ref[0] = v
    for b in range(kt_ref.shape[1]):
        rows = slice(b * N_MEM, (b + 1) * N_MEM)
        kt_ref[0, b] = k[rows].T
        vt_ref[0, b] = v[rows].T


def _shared_kv_kernel(x_ref, g_ref, w_ref, kg_ref, bd_ref, cos_ref, sin_ref,
                      k_ref, v_ref):
    h = _mm(_rmsnorm(x_ref[...], g_ref[...]), w_ref[...])
    k = _head_rmsnorm(h[:, :KV_W], bd_ref[...], kg_ref[...])
    k_ref[...] = _rope(k, cos_ref[...], sin_ref[...])
    v_ref[...] = h[:, KV_W:]


def _mixer_a_kernel(x_ref, g_ref, win_ref, cw_ref, cb_ref, wg_ref, bgx_ref, bga_ref,
                    lru_ref, mqg_ref, bd_ref, mk_ref, mv_ref, wout_ref, rc0_ref, h0_ref,
                    xo_ref, hlast_ref, rctail_ref,
                    conv_s, tail_s, a_s, b_s, hc_s, kcat_s, vcat_s):
    ts = x_ref.shape[0]
    n_slabs = D_RNN // LANES

    @pl.when(pl.program_id(1) == 0)
    def _():
        for s in range(n_slabs):
            tail_s[s] = rc0_ref[0, :, s * LANES:(s + 1) * LANES]
        a_s[:, 0:SUBLANES, :] = jnp.ones((n_slabs, SUBLANES, LANES), F32)
        b_s[:, 0:SUBLANES, :] = jnp.zeros((n_slabs, SUBLANES, LANES), F32)
        hc_s[...] = jnp.broadcast_to(h0_ref[0], hc_s.shape)
        kcat_s[...] = _head_masked_stack([mk_ref[0]], MEM_HEADS)
        vcat_s[...] = _head_masked_stack([mv_ref[0]], MEM_HEADS)

    x = x_ref[...]
    hn = _rmsnorm(x, g_ref[...]).astype(BF16)

    def in_proj(lo, hi):
        return jnp.dot(hn, win_ref[:, lo:hi], preferred_element_type=F32)

    xr_pre = in_proj(D_RNN, 2 * D_RNN)
    _slab_stage(xr_pre, conv_s, 0, tail_s, 0)
    xr = jnp.concatenate(_slab_taps(ts, n_slabs, conv_s, 0, cw_ref, cb_ref, 0), axis=1)
    rctail_ref[0] = xr_pre[ts - SUBLANES:]
    qm = in_proj(2 * D_RNN, 2 * D_RNN + MEM_W)
    a, b = _lru_coeffs(xr, wg_ref, bgx_ref[...], bga_ref[...], _log_sigmoid(lru_ref[...]))
    qn = _head_rmsnorm(qm, bd_ref[...], mqg_ref[...]) * ATT_SCALE
    gate = in_proj(0, D_RNN)
    mo = _mem_attention(qn, kcat_s[...], vcat_s[...])
    h = _lru_scan(a, b, a_s, b_s, hc_s)
    hlast_ref[0] = h[ts - 1:ts]
    main = h * _gelu(gate)

    y = _mm(jnp.concatenate([main, mo], axis=1), wout_ref[...])
    xo_ref[...] = x + y


def _mixer_b_kernel(sink_ref, x_ref, g_ref, win_ref, qg_ref, cos_ref, sin_ref,
                    kcur_ref, kprev_ref, vcur_ref, vprev_ref,
                    mqg_ref, bd_ref, mk_ref, mv_ref, wout_ref,
                    xo_ref, kcat_s, vcat_s):
    ts = x_ref.shape[0]
    i = pl.program_id(1)

    @pl.when(i == 0)
    def _():
        kcat_s[...] = _head_masked_stack([mk_ref[0]], MEM_HEADS)
        vcat_s[...] = _head_masked_stack([mv_ref[0]], MEM_HEADS)

    x = x_ref[...]
    hn = _rmsnorm(x, g_ref[...]).astype(BF16)

    def in_proj(lo, hi):
        return jnp.dot(hn, win_ref[:, lo:hi], preferred_element_type=F32)

    q = _head_rmsnorm(in_proj(0, Q_W), bd_ref[...], qg_ref[...], f32_stat=False)
    q = (_rope(q, cos_ref[...], sin_ref[...]) * ATT_SCALE).astype(BF16)

    row = lax.broadcasted_iota(jnp.int32, (WINDOW, 2 * WINDOW), 0)
    kj = lax.broadcasted_iota(jnp.int32, (WINDOW, 2 * WINDOW), 1) - WINDOW
    mask_inner = (kj <= row) & (kj >= row - WINDOW)
    mask_first = (kj <= row) & (kj >= jnp.maximum(row - WINDOW, jnp.where(i > 0, -WINDOW, 0)))

    kt = kcur_ref[...]
    vt = vcur_ref[...]

    def block_scores(jb):
        lo, hi = jb * WINDOW, (jb + 1) * WINDOW
        if jb == 0:
            kp, vp = kprev_ref[...], vprev_ref[...]
        else:
            kp, vp = kt[lo - WINDOW:lo], vt[lo - WINDOW:lo]
        kcat = _head_masked_stack([kp, kt[lo:hi]], N_KV)
        vcat = _head_masked_stack([vp, vt[lo:hi]], N_KV)
        qs = jnp.concatenate([q[lo:hi, g * KV_W:(g + 1) * KV_W] for g in range(GROUP)], axis=0)
        s = lax.dot_general(qs, kcat, (((1,), (1,)), ((), ())), preferred_element_type=F32)
        return s, vcat

    n_blocks = ts // WINDOW
    mains = []
    nxt = block_scores(0)
    qm = in_proj(Q_W, Q_W + MEM_W)
    qn = _head_rmsnorm(qm, bd_ref[...], mqg_ref[...], f32_stat=False) * ATT_SCALE
    s_mem = _mm_nt(qn, kcat_s[...])
    for jb in range(n_blocks):
        s, vcat = nxt
        if jb + 1 < n_blocks:
            nxt = block_scores(jb + 1)
        mask = mask_first if jb == 0 else mask_inner
        prow = []
        for g in range(GROUP):
            pseg = []
            for kv in range(N_KV):
                seg = s[g * WINDOW:(g + 1) * WINDOW, kv * 2 * WINDOW:(kv + 1) * 2 * WINDOW]
                seg = jnp.where(mask, seg, NEG)
                sink = sink_ref[g * N_KV + kv]
                m = jnp.maximum(jnp.max(seg, axis=-1, keepdims=True), sink)
                p = jnp.exp(seg - m)
                den = jnp.sum(p, axis=-1, keepdims=True) + jnp.exp(sink - m)
                pseg.append((p * (1.0 / den)).astype(BF16))
            prow.append(jnp.concatenate(pseg, axis=1))
        o = jnp.dot(jnp.concatenate(prow, axis=0), vcat, preferred_element_type=F32)
        mains.append(jnp.concatenate([o[g * WINDOW:(g + 1) * WINDOW] for g in range(GROUP)], axis=1))
        if jb == 0:
            mo = _mem_softmax_pv(s_mem, vcat_s[...])
    main = mains[0] if len(mains) == 1 else jnp.concatenate(mains, axis=0)

    y = _mm(jnp.concatenate([main, mo], axis=1), wout_ref[...])
    xo_ref[...] = x + y


def _ffn_kernel(x_ref, g_ref, wup_ref, cw_ref, cb_ref, wdn_ref,
                xo_ref, utail_ref, conv_s, tail_s, act_s, xres_s, *, tiles_per_seq):
    ts = x_ref.shape[0]
    slabs = FF_CHUNK // LANES
    blk_slabs = MXU_DIM // LANES
    blks = FF_CHUNK // MXU_DIM
    i = pl.program_id(0)
    slot = i % 2
    pslot = 1 - slot

    @pl.when(i == 0)
    def _():
        act_s[1] = jnp.zeros(act_s.shape[1:], act_s.dtype)
        xres_s[1] = jnp.zeros(xres_s.shape[1:], xres_s.dtype)

    @pl.when(i % tiles_per_seq == 0)
    def _():
        tail_s[...] = jnp.zeros_like(tail_s)

    x = x_ref[...]
    xres_s[slot] = x
    hn = _rmsnorm(x, g_ref[...]).astype(BF16)

    def chunk_cols(j):
        return [(c, ((j % 2) * 2 + half) * slabs) for half, c in enumerate((j, N_FF_CHUNKS + j))]

    def up_project(j):
        for c, work0 in chunk_cols(j):
            for b in range(blks):
                blk = c * blks + b
                u = jnp.dot(hn, wup_ref[blk], preferred_element_type=F32)
                _slab_stage(u, conv_s, work0 + b * blk_slabs, tail_s, c * slabs + b * blk_slabs)
                utail_ref[0, :, blk * MXU_DIM:(blk + 1) * MXU_DIM] = u[ts - SUBLANES:]

    def activation(j):
        cg, cv = [_slab_taps(ts, slabs, conv_s, work0, cw_ref, cb_ref, c * FF_CHUNK)
                  for c, work0 in chunk_cols(j)]
        return jnp.concatenate([_gelu(a) * b for a, b in zip(cg, cv)], axis=1).astype(BF16)

    def down_prev(n):
        return jnp.dot(act_s[pslot], wdn_ref[n], preferred_element_type=F32)

    n_dn = wdn_ref.shape[0]
    outs = [None] * n_dn
    outs[0] = down_prev(0)
    up_project(0)
    for j in range(N_FF_CHUNKS):
        if j + 1 < N_FF_CHUNKS:
            up_project(j + 1)
        else:
            outs[1] = down_prev(1)
        act_s[slot, :, j * FF_CHUNK:(j + 1) * FF_CHUNK] = activation(j)
    for n in range(2, n_dn):
        outs[n] = down_prev(n)
    xo_ref[...] = xres_s[pslot] + jnp.concatenate(outs, axis=1)


def _dec_in_a_kernel(x_ref, g_ref, win_ref, cw_ref, cb_ref, wg_ref, bgx_ref, bga_ref,
                     lru_ref, mqg_ref, bd_ref, b0_ref, b1_ref, b2_ref, h0_ref,
                     main_ref, qn_ref, hnew_ref, xrpre_ref):
    u = _mm(_rmsnorm(x_ref[...], g_ref[...]), win_ref[...])
    gate = u[:, :D_RNN]
    xr_pre = u[:, D_RNN:2 * D_RNN]
    qm = u[:, 2 * D_RNN:]
    xr = b0_ref[...] * cw_ref[0:1, :]
    xr = xr + b1_ref[...] * cw_ref[1:2, :]
    xr = xr + b2_ref[...] * cw_ref[2:3, :]
    xr = xr + xr_pre * cw_ref[3:4, :]
    xr = xr + cb_ref[...]
    a, b = _lru_coeffs(xr, wg_ref, bgx_ref[...], bga_ref[...], _log_sigmoid(lru_ref[...]))
    h = a * h0_ref[...] + b
    main_ref[...] = h * _gelu(gate)
    qn_ref[...] = _head_rmsnorm(qm, bd_ref[...], mqg_ref[...]) * ATT_SCALE
    hnew_ref[...] = h
    xrpre_ref[...] = xr_pre


def _dec_in_b_kernel(x_ref, g_ref, win_ref, qg_ref, cos_ref, sin_ref, mqg_ref, bd_ref,
                     q_ref, qn_ref):
    u = _mm(_rmsnorm(x_ref[...], g_ref[...]), win_ref[...])
    q = _head_rmsnorm(u[:, :Q_W], bd_ref[...], qg_ref[...])
    q_ref[...] = _rope(q, cos_ref[...], sin_ref[...]) * ATT_SCALE
    qn_ref[...] = _head_rmsnorm(u[:, Q_W:], bd_ref[...], mqg_ref[...]) * ATT_SCALE


DEC_HEAD_ROWS = 16


def _own_head_lanes(n_heads, width):
    row = lax.broadcasted_iota(jnp.int32, (DEC_HEAD_ROWS, width), 0)
    lane = lax.broadcasted_iota(jnp.int32, (DEC_HEAD_ROWS, width), 1)
    start = (row % (width // HEAD_DIM)) * HEAD_DIM
    return (lane >= start) & (lane < start + HEAD_DIM) & (row < n_heads)


def _dec_mem_attn_kernel(q_ref, kt_ref, vt_ref, o_ref):
    own = _own_head_lanes(MEM_HEADS, MEM_W)
    for s in range(q_ref.shape[0]):
        q_rows = jnp.broadcast_to(q_ref[s:s + 1, :], (DEC_HEAD_ROWS, MEM_W))
        qbd = jnp.where(own, q_rows, 0.0).astype(BF16)
        sc = jnp.dot(qbd, kt_ref[s].astype(BF16), preferred_element_type=F32)
        m = jnp.max(sc, axis=-1, keepdims=True)
        p = jnp.exp(sc - m)
        den = jnp.sum(p, axis=-1, keepdims=True)
        pn = (p * (1.0 / den)).astype(BF16)
        o_all = lax.dot_general(pn, vt_ref[s].astype(BF16), (((1,), (1,)), ((), ())),
                                preferred_element_type=F32)
        o_ref[s:s + 1, :] = jnp.sum(jnp.where(own, o_all, 0.0), axis=0, keepdims=True)


def _dec_swa_kernel(q_ref, kt_ref, vt_ref, kn_ref, vn_ref, sink_ref, o_ref):
    n_heads = GROUP * N_KV
    own = _own_head_lanes(n_heads, KV_W)
    grp = lax.broadcasted_iota(jnp.int32, (DEC_HEAD_ROWS, KV_W), 0) // N_KV
    sink = sink_ref[:, 0:1]
    for s in range(q_ref.shape[0]):
        q_rows = jnp.zeros((DEC_HEAD_ROWS, KV_W), F32)
        for g in range(GROUP):
            qg = jnp.broadcast_to(q_ref[s:s + 1, g * KV_W:(g + 1) * KV_W], (DEC_HEAD_ROWS, KV_W))
            q_rows = jnp.where(grp == g, qg, q_rows)
        qbd = jnp.where(own, q_rows, 0.0)
        s_buf = jnp.dot(qbd.astype(BF16), kt_ref[s].astype(BF16), preferred_element_type=F32)
        s_new = jnp.sum(qbd * kn_ref[s:s + 1, :], axis=-1, keepdims=True)
        m = jnp.maximum(jnp.maximum(jnp.max(s_buf, axis=-1, keepdims=True), s_new), sink)
        p_buf = jnp.exp(s_buf - m)
        p_new = jnp.exp(s_new - m)
        den = jnp.sum(p_buf, axis=-1, keepdims=True) + p_new + jnp.exp(sink - m)
        r = 1.0 / den
        o_all = lax.dot_general((p_buf * r).astype(BF16), vt_ref[s].astype(BF16),
                                (((1,), (1,)), ((), ())), preferred_element_type=F32)
        o_all = jnp.where(own, o_all + (p_new * r) * vn_ref[s:s + 1, :], 0.0)
        o_sum = o_all + pltpu.roll(o_all, 1, 0)
        o_sum = o_sum + pltpu.roll(o_sum, 2, 0)
        for g in range(GROUP):
            last = (g + 1) * N_KV - 1
            o_ref[s:s + 1, g * KV_W:(g + 1) * KV_W] = o_sum[last:last + 1, :]


def _dec_out_ffn_kernel(x_ref, main_ref, mo_ref, wout_ref, g_ref,
                        wug_ref, wuv_ref, cwg_ref, cwv_ref, cbg_ref, cbv_ref, wdn_ref,
                        sg_ref, sv_ref, snew_in_ref,
                        xo_ref, snew_ref,
                        xmid_s, hn_s, acc_s):
    del snew_in_ref
    j = pl.program_id(0)

    @pl.when(j == 0)
    def _():
        y = _mm(jnp.concatenate([main_ref[...], mo_ref[...]], axis=1), wout_ref[...])
        xmid = x_ref[...] + y
        xmid_s[...] = xmid
        hn_s[...] = _rmsnorm(xmid, g_ref[...]).astype(BF16)
        acc_s[...] = jnp.zeros_like(acc_s)

    hn = hn_s[...]
    up = lambda w_ref: jnp.concatenate(
        [jnp.dot(hn, w_ref[b], preferred_element_type=F32) for b in range(w_ref.shape[0])], axis=1)
    ug = up(wug_ref)
    uv = up(wuv_ref)
    sg0, sg1 = sg_ref[:, 0, :], sg_ref[:, 1, :]
    sv0, sv1 = sv_ref[:, 0, :], sv_ref[:, 1, :]
    gate_cols = pl.ds(pl.multiple_of(j * FF_CHUNK, FF_CHUNK), FF_CHUNK)
    value_cols = pl.ds(pl.multiple_of(D_FF + j * FF_CHUNK, FF_CHUNK), FF_CHUNK)
    snew_ref[:, 0, gate_cols] = sg1
    snew_ref[:, 1, gate_cols] = ug
    snew_ref[:, 0, value_cols] = sv1
    snew_ref[:, 1, value_cols] = uv
    cg = sg0 * cwg_ref[0:1, :] + sg1 * cwg_ref[1:2, :] + ug * cwg_ref[2:3, :] + cbg_ref[...]
    cv = sv0 * cwv_ref[0:1, :] + sv1 * cwv_ref[1:2, :] + uv * cwv_ref[2:3, :] + cbv_ref[...]
    act = (_gelu(cg) * cv).astype(BF16)
    for n in range(wdn_ref.shape[0]):
        acc_s[:, n * MXU_DIM:(n + 1) * MXU_DIM] += jnp.dot(act, wdn_ref[n],
                                                           preferred_element_type=F32)

    @pl.when(j == pl.num_programs(0) - 1)
    def _():
        xo_ref[...] = xmid_s[...] + acc_s[...]


def _const_spec(shape):
    nd = len(shape)
    return pl.BlockSpec(shape, lambda *_: (0,) * nd)


def _layer_spec(arr, layer):
    nd = arr.ndim - 1
    return pl.BlockSpec((None,) + arr.shape[1:], lambda *_: (layer,) + (0,) * nd)


def _resident_spec(op):
    return _layer_spec(*op) if isinstance(op, tuple) else _const_spec(op.shape)


def _operand(op):
    return op[0] if isinstance(op, tuple) else op


def _params(*sem):
    return pltpu.CompilerParams(dimension_semantics=sem, vmem_limit_bytes=VMEM_LIMIT_BYTES)


def _cast_blocks(w):
    layers, r, c = w.shape
    tile = CAST_TILE
    per = tile // MXU_DIM
    return pl.pallas_call(
        _cast_blocks_kernel,
        grid=(layers, r // tile, c // tile),
        in_specs=[pl.BlockSpec((None, tile, tile), lambda l, i, j: (l, i, j))],
        out_specs=pl.BlockSpec((None, per, tile, MXU_DIM), lambda l, i, j: (l, j, i, 0)),
        out_shape=jax.ShapeDtypeStruct((layers, c // MXU_DIM, r, MXU_DIM), BF16),
        compiler_params=_params("arbitrary", "arbitrary", "arbitrary"),
        name="cast_blocks",
    )(w)


def _mem_kv(mem2d, g, w, kg, bd):
    depth = w.shape[0]
    rows = mem2d.shape[0]
    bsz = rows // N_MEM
    out = jax.ShapeDtypeStruct((depth, rows, MEM_W), F32)
    out_t = jax.ShapeDtypeStruct((depth, bsz, MEM_W, N_MEM), F32)
    return pl.pallas_call(
        _mem_kv_kernel,
        grid=(depth,),
        in_specs=[
            _const_spec(mem2d.shape),
            pl.BlockSpec((1, 1, D_MODEL), lambda l: (l, 0, 0)),
            pl.BlockSpec((1, D_MODEL, 2 * MEM_W), lambda l: (l, 0, 0)),
            pl.BlockSpec((1, 1, MEM_W), lambda l: (l, 0, 0)),
            _const_spec(bd.shape),
        ],
        out_specs=[pl.BlockSpec((1, rows, MEM_W), lambda l: (l, 0, 0))] * 2
        + [pl.BlockSpec((1, bsz, MEM_W, N_MEM), lambda l: (l, 0, 0, 0))] * 2,
        out_shape=[out, out, out_t, out_t],
        compiler_params=_params("arbitrary"),
        name="mem_kv",
    )(mem2d, g, w, kg, bd)


def _shared_kv(x2d, g, w, kg, bd, cos_t, sin_t, ts):
    rows = x2d.shape[0]
    tab_blocks = cos_t.shape[0] // ts
    out = jax.ShapeDtypeStruct((rows, KV_W), F32)
    return pl.pallas_call(
        _shared_kv_kernel,
        grid=(rows // ts,),
        in_specs=[
            pl.BlockSpec((ts, D_MODEL), lambda i: (i, 0)),
            _const_spec(g.shape), _const_spec(w.shape), _const_spec(kg.shape),
            _const_spec(bd.shape),
            pl.BlockSpec((ts, LANES), lambda i: (i % tab_blocks, 0)),
            pl.BlockSpec((ts, LANES), lambda i: (i % tab_blocks, 0)),
        ],
        out_specs=[pl.BlockSpec((ts, KV_W), lambda i: (i, 0))] * 2,
        out_shape=[out, out],
        compiler_params=_params("arbitrary"),
        name="shared_kv",
    )(x2d, g, w, kg, bd, cos_t, sin_t)


def _mixer_a(x2d, bsz, g, win, cw, cb, wg, bgx, bga, lru, mqg, bd, mk, mv, wout, rc0, h0):
    rows = x2d.shape[0]
    ts = TS_MIX_A
    nt = rows // bsz // ts
    consts = (g, win, cw, cb, wg, bgx, bga, lru, mqg, bd)
    per_b3 = lambda b, i: (b, 0, 0)
    return pl.pallas_call(
        _mixer_a_kernel,
        grid=(bsz, nt),
        in_specs=[pl.BlockSpec((ts, D_MODEL), lambda b, i: (b * nt + i, 0))]
        + [_resident_spec(c) for c in consts]
        + [pl.BlockSpec((1, N_MEM, MEM_W), per_b3)] * 2
        + [_resident_spec(wout),
           pl.BlockSpec((1, SUBLANES, D_RNN), per_b3),
           pl.BlockSpec((1, 1, D_RNN), per_b3)],
        out_specs=[
            pl.BlockSpec((ts, D_MODEL), lambda b, i: (b * nt + i, 0)),
            pl.BlockSpec((1, 1, D_RNN), per_b3),
            pl.BlockSpec((1, SUBLANES, D_RNN), per_b3),
        ],
        out_shape=[
            jax.ShapeDtypeStruct((rows, D_MODEL), F32),
            jax.ShapeDtypeStruct((bsz, 1, D_RNN), F32),
            jax.ShapeDtypeStruct((bsz, SUBLANES, D_RNN), F32),
        ],
        scratch_shapes=[
            pltpu.VMEM((D_RNN // LANES, SUBLANES + ts, LANES), F32),
            pltpu.VMEM((D_RNN // LANES, SUBLANES, LANES), F32),
            pltpu.VMEM((D_RNN // LANES, SUBLANES + ts, LANES), F32),
            pltpu.VMEM((D_RNN // LANES, SUBLANES + ts, LANES), F32),
            pltpu.VMEM((SUBLANES, D_RNN), F32),
            pltpu.VMEM((MEM_HEADS * N_MEM, MEM_W), BF16),
            pltpu.VMEM((MEM_HEADS * N_MEM, MEM_W), BF16),
        ],
        compiler_params=_params("arbitrary", "arbitrary"),
        name="mixer_a",
    )(x2d, *[_operand(c) for c in consts], mk, mv, _operand(wout), rc0, h0)


def _mixer_b(x2d, bsz, sink_tab, g, win, qg, cos_t, sin_t, ksh, vsh, mqg, bd, mk, mv, wout):
    rows = x2d.shape[0]
    ts = TS_MIX_B
    nt = rows // bsz // ts
    wpt = ts // WINDOW
    cur = lambda b, i: (b * nt + i, 0)
    prev = lambda b, i: (jnp.maximum((b * nt + i) * wpt - 1, 0), 0)
    table = lambda b, i: (i, 0)
    per_b3 = lambda b, i: (b, 0, 0)
    return pl.pallas_call(
        _mixer_b_kernel,
        grid=(bsz, nt),
        in_specs=[
            pl.BlockSpec(memory_space=pltpu.SMEM),
            pl.BlockSpec((ts, D_MODEL), cur),
            _const_spec(g.shape), _resident_spec(win), _const_spec(qg.shape),
            pl.BlockSpec((ts, LANES), table),
            pl.BlockSpec((ts, LANES), table),
            pl.BlockSpec((ts, KV_W), cur), pl.BlockSpec((WINDOW, KV_W), prev),
            pl.BlockSpec((ts, KV_W), cur), pl.BlockSpec((WINDOW, KV_W), prev),
            _const_spec(mqg.shape), _const_spec(bd.shape),
            pl.BlockSpec((1, N_MEM, MEM_W), per_b3), pl.BlockSpec((1, N_MEM, MEM_W), per_b3),
            _resident_spec(wout),
        ],
        out_specs=pl.BlockSpec((ts, D_MODEL), cur),
        out_shape=jax.ShapeDtypeStruct((rows, D_MODEL), F32),
        scratch_shapes=[
            pltpu.VMEM((MEM_HEADS * N_MEM, MEM_W), BF16),
            pltpu.VMEM((MEM_HEADS * N_MEM, MEM_W), BF16),
        ],
        compiler_params=_params("arbitrary", "arbitrary"),
        name="mixer_b",
    )(sink_tab, x2d, g, _operand(win), qg, cos_t, sin_t, ksh, ksh, vsh, vsh, mqg, bd, mk, mv,
      _operand(wout))


def _ffn(x2d, bsz, layer, g, wup, cw, cb, wdn):
    rows = x2d.shape[0]
    ts = TS_FFN
    nt = rows // ts
    tiles_per_seq = rows // bsz // ts
    last = nt - 1
    return pl.pallas_call(
        functools.partial(_ffn_kernel, tiles_per_seq=tiles_per_seq),
        grid=(nt + 1,),
        in_specs=[pl.BlockSpec((ts, D_MODEL), lambda i: (jnp.minimum(i, last), 0)),
                  _const_spec(g.shape), _layer_spec(wup, layer), _const_spec(cw.shape),
                  _const_spec(cb.shape), _layer_spec(wdn, layer)],
        out_specs=[
            pl.BlockSpec((ts, D_MODEL), lambda i: (jnp.maximum(i - 1, 0), 0)),
            pl.BlockSpec((1, SUBLANES, 2 * D_FF),
                         lambda i: (jnp.minimum(i, last) // tiles_per_seq, 0, 0)),
        ],
        out_shape=[
            jax.ShapeDtypeStruct((rows, D_MODEL), F32),
            jax.ShapeDtypeStruct((bsz, SUBLANES, 2 * D_FF), F32),
        ],
        scratch_shapes=[pltpu.VMEM((4 * FF_CHUNK // LANES, SUBLANES + ts, LANES), F32),
                        pltpu.VMEM((2 * D_FF // LANES, SUBLANES, LANES), F32),
                        pltpu.VMEM((2, ts, D_FF), BF16),
                        pltpu.VMEM((2, ts, D_MODEL), F32)],
        compiler_params=_params("arbitrary"),
        name="ffn",
    )(x2d, g, wup, cw, cb, wdn)


def _dec_in_a(x, g, win, cw, cb, wg, bgx, bga, lru, mqg, bd, rc, h0):
    n = x.shape[0]
    consts = (x, g, win, cw, cb, wg, bgx, bga, lru, mqg, bd)
    buf = lambda j: pl.BlockSpec((None, n, D_RNN), lambda i: (j, 0, 0))
    return pl.pallas_call(
        _dec_in_a_kernel,
        grid=(1,),
        in_specs=[_resident_spec(c) for c in consts]
        + [buf(0), buf(1), buf(2), _const_spec(h0.shape)],
        out_specs=[_const_spec((n, D_RNN)), _const_spec((n, MEM_W)),
                   _const_spec((n, D_RNN)), _const_spec((n, D_RNN))],
        out_shape=[jax.ShapeDtypeStruct((n, D_RNN), F32), jax.ShapeDtypeStruct((n, MEM_W), F32),
                   jax.ShapeDtypeStruct((n, D_RNN), F32), jax.ShapeDtypeStruct((n, D_RNN), F32)],
        compiler_params=_params("arbitrary"),
        name="dec_in_a",
    )(*[_operand(c) for c in consts], rc, rc, rc, h0)


def _dec_in_b(x, g, win, qg, cos_t, sin_t, mqg, bd):
    n = x.shape[0]
    args = (x, g, win, qg, cos_t, sin_t, mqg, bd)
    return pl.pallas_call(
        _dec_in_b_kernel,
        grid=(1,),
        in_specs=[_resident_spec(a) for a in args],
        out_specs=[_const_spec((n, Q_W)), _const_spec((n, MEM_W))],
        out_shape=[jax.ShapeDtypeStruct((n, Q_W), F32), jax.ShapeDtypeStruct((n, MEM_W), F32)],
        compiler_params=_params("arbitrary"),
        name="dec_in_b",
    )(*[_operand(a) for a in args])


def _dec_mem_attn(qn, ckt, cvt, layer):
    n = qn.shape[0]
    sb = DEC_SEQ_BLOCK
    return pl.pallas_call(
        _dec_mem_attn_kernel,
        grid=(n // sb,),
        in_specs=[pl.BlockSpec((sb, MEM_W), lambda i: (i, 0)),
                  pl.BlockSpec((None, sb, MEM_W, N_MEM), lambda i: (layer, i, 0, 0)),
                  pl.BlockSpec((None, sb, MEM_W, N_MEM), lambda i: (layer, i, 0, 0))],
        out_specs=pl.BlockSpec((sb, MEM_W), lambda i: (i, 0)),
        out_shape=jax.ShapeDtypeStruct((n, MEM_W), F32),
        compiler_params=_params("arbitrary"),
        name="dec_mem_attn",
    )(qn, ckt, cvt)


def _dec_swa(q, kbt, vbt, kn, vn, sink_rows):
    n = q.shape[0]
    wb = kbt.shape[2]
    sb = DEC_SEQ_BLOCK
    assert wb <= WINDOW
    return pl.pallas_call(
        _dec_swa_kernel,
        grid=(n // sb,),
        in_specs=[pl.BlockSpec((sb, Q_W), lambda i: (i, 0)),
                  pl.BlockSpec((sb, KV_W, wb), lambda i: (i, 0, 0)),
                  pl.BlockSpec((sb, KV_W, wb), lambda i: (i, 0, 0)),
                  pl.BlockSpec((sb, KV_W), lambda i: (i, 0)),
                  pl.BlockSpec((sb, KV_W), lambda i: (i, 0)),
                  _const_spec(sink_rows.shape)],
        out_specs=pl.BlockSpec((sb, Q_W), lambda i: (i, 0)),
        out_shape=jax.ShapeDtypeStruct((n, Q_W), F32),
        compiler_params=_params("arbitrary"),
        name="dec_swa",
    )(q, kbt, vbt, kn, vn, sink_rows)


def _dec_out_ffn(x, main, mo, wout, wout_layer, layer, g, wup, cw, cb, wdn, st, st_new):
    n = x.shape[0]
    nch = N_FF_CHUNKS
    blks = FF_CHUNK // MXU_DIM
    lo = lambda j: (0, j)
    hi = lambda j: (0, nch + j)
    state = lambda off: pl.BlockSpec((None, n, CONV_F - 1, FF_CHUNK),
                                     lambda j: (layer, 0, 0, off + j))
    up_blocks = lambda off: pl.BlockSpec((None, blks, D_MODEL, MXU_DIM),
                                         lambda j: (layer, off + j, 0, 0))
    operands = (x, main, mo, wout, g, wup, wup, cw, cw, cb, cb, wdn, st, st, st_new)
    return pl.pallas_call(
        _dec_out_ffn_kernel,
        grid=(nch,),
        in_specs=[_const_spec(x.shape), _const_spec(main.shape), _const_spec(mo.shape),
                  _layer_spec(wout, wout_layer), _const_spec(g.shape),
                  up_blocks(0), up_blocks(nch),
                  pl.BlockSpec((CONV_F, FF_CHUNK), lo), pl.BlockSpec((CONV_F, FF_CHUNK), hi),
                  pl.BlockSpec((1, FF_CHUNK), lo), pl.BlockSpec((1, FF_CHUNK), hi),
                  pl.BlockSpec((None, wdn.shape[1], FF_CHUNK, MXU_DIM), lambda j: (layer, 0, j, 0)),
                  state(0), state(nch),
                  pl.BlockSpec(memory_space=pl.ANY)],
        out_specs=[_const_spec((n, D_MODEL)),
                   pl.BlockSpec((None, n, CONV_F - 1, 2 * D_FF), lambda j: (layer, 0, 0, 0))],
        out_shape=[jax.ShapeDtypeStruct((n, D_MODEL), F32),
                   jax.ShapeDtypeStruct(st_new.shape, F32)],
        input_output_aliases={len(operands) - 1: 1},
        scratch_shapes=[pltpu.VMEM((n, D_MODEL), F32), pltpu.VMEM((n, D_MODEL), BF16),
                        pltpu.VMEM((n, D_MODEL), F32)],
        compiler_params=_params("arbitrary"),
        name="dec_out_ffn",
    )(*operands)


def _rope_tables(pos):
    half = HEAD_DIM // 2
    inv = ROPE_THETA ** (-jnp.arange(half, dtype=F32) / half)
    ang = pos.astype(F32)[:, None] * inv[None, :]
    cos = jnp.cos(ang)
    sin = jnp.sin(ang)
    reps = LANES // HEAD_DIM
    cos_t = jnp.tile(jnp.concatenate([cos, cos], axis=1), (1, reps))
    sin_t = jnp.tile(jnp.concatenate([-sin, sin], axis=1), (1, reps))
    return cos_t, sin_t


def _block_diag_gates(wx, wa):
    per = MXU_DIM // HEAD_DIM
    eye = jnp.eye(per, dtype=F32)

    def bd(w):
        w4 = w.reshape(RNN_BLOCKS // per, per, HEAD_DIM, HEAD_DIM)
        return jnp.einsum('ckij,kK->ckiKj', w4, eye).reshape(RNN_BLOCKS // per, MXU_DIM, MXU_DIM)

    return jnp.concatenate([bd(wx), bd(wa)], axis=2).astype(BF16)


def kernel(x_prompt, x_sample, state_rglru_h, state_rglru_conv, state_ffn_conv, cache_swa_k, cache_swa_v, cache_mem_k, cache_mem_v, mem_prompt, norm_mix_g, norm_ffn_g, w_in_a, rnn_conv_w, rnn_conv_b, w_gate_x, b_gate_x, w_gate_a, b_gate_a, lru_param, w_in_b, q_norm_g, sinks, kv_norm_g, w_kv, k_norm_g, mem_norm_g, w_mem_kv, mem_q_norm_g, mem_k_norm_g, w_out, w_ffn_up, ffn_conv_w, ffn_conv_b, w_ffn_down):
    bsz, seq, _ = x_prompt.shape
    dbsz = x_sample.shape[0]
    depth = norm_mix_g.shape[0]
    n_a = w_in_a.shape[0]
    assert x_sample.shape[1] == 1
    assert seq % TS_MIX_A == 0 and seq % TS_MIX_B == 0 and TS_MIX_B % WINDOW == 0
    assert seq % TS_FFN == 0 and seq % TS_KV == 0

    n_b = w_in_b.shape[0]
    wq = w_in_b[:, :, :Q_W].astype(BF16).reshape(n_b, D_MODEL, N_KV, GROUP, HEAD_DIM)
    wq = wq.transpose(0, 1, 3, 2, 4).reshape(n_b, D_MODEL, Q_W)
    wo_main = w_out[n_a:, :Q_W].astype(BF16).reshape(n_b, N_KV, GROUP, HEAD_DIM, D_MODEL)
    wo_main = wo_main.transpose(0, 2, 1, 3, 4).reshape(n_b, Q_W, D_MODEL)
    bd =(jnp.kron(jnp.eye(MXU_DIM // HEAD_DIM, dtype=F32),
                   jnp.ones((HEAD_DIM, HEAD_DIM), F32)) / HEAD_DIM).astype(BF16)

    row = lambda v: v.reshape(1, -1)
    tile_row = lambda v, n: jnp.tile(v, n).reshape(1, -1)
    w_in_a_b = w_in_a.astype(BF16)
    w_in_b_b = jnp.concatenate([wq, w_in_b[:, :, Q_W:].astype(BF16)], axis=2)
    w_out_b = w_out.astype(BF16)
    w_out_perm_b = jnp.concatenate([wo_main, w_out_b[n_a:, Q_W:]], axis=1)
    w_kv_b = w_kv.astype(BF16)
    w_mem_b = w_mem_kv.astype(BF16)
    wup_b = _cast_blocks(w_ffn_up)
    wdn_b = _cast_blocks(w_ffn_down)
    fcw = ffn_conv_w
    fcb = ffn_conv_b.reshape(depth, 1, 2 * D_FF)
    wg_b = jnp.stack([_block_diag_gates(w_gate_x[l], w_gate_a[l]) for l in range(n_a)])
    sink_gk = sinks.reshape(-1, N_KV, GROUP).transpose(0, 2, 1)

    cos_p, sin_p = _rope_tables(jnp.arange(seq, dtype=jnp.int32))
    pos_s = PAST_LEN + jnp.zeros((dbsz,), jnp.int32)
    cos_s, sin_s = _rope_tables(pos_s)

    mem2d = mem_prompt.reshape(bsz * N_MEM, D_MODEL)
    pmk, pmv, pmk_t, pmv_t = _mem_kv(
        mem2d, mem_norm_g.reshape(depth, 1, D_MODEL), w_mem_b,
        jnp.tile(mem_k_norm_g, (1, MEM_HEADS)).reshape(depth, 1, MEM_W), bd)
    pmk4 = pmk.reshape(depth, bsz, N_MEM, MEM_W)
    pmv4 = pmv.reshape(depth, bsz, N_MEM, MEM_W)

    x = x_prompt.reshape(bsz * seq, D_MODEL)
    zeros_rc = jnp.zeros((bsz, SUBLANES, D_RNN), F32)
    zeros_h = jnp.zeros((bsz, 1, D_RNN), F32)
    p_h, p_rc, p_fc = [], [], []
    ksh = vsh = None
    for l in range(depth):
        mqg = tile_row(mem_q_norm_g[l], MEM_HEADS)
        if l < n_a:
            x, hl, rct = _mixer_a(
                x, bsz, row(norm_mix_g[l]), (w_in_a_b, l), rnn_conv_w[l], row(rnn_conv_b[l]),
                (wg_b, l), row(b_gate_x[l]), row(b_gate_a[l]), row(lru_param[l]), mqg, bd,
                pmk4[l], pmv4[l], (w_out_b, l), zeros_rc, zeros_h)
            p_h.append(hl.reshape(bsz, D_RNN))
            p_rc.append(rct[:, SUBLANES - (CONV_A - 1):])
        else:
            j = l - n_a
            x = _mixer_b(
                x, bsz, sink_gk[j].reshape(-1), row(norm_mix_g[l]), (w_in_b_b, j),
                tile_row(q_norm_g[j], N_Q), cos_p, sin_p, ksh, vsh, mqg, bd,
                pmk4[l], pmv4[l], (w_out_perm_b, j))
        x, ut = _ffn(x, bsz, l, row(norm_ffn_g[l]), wup_b, fcw[l], fcb[l], wdn_b)
        p_fc.append(ut[:, SUBLANES - (CONV_F - 1):])
        if l == n_a - 1:
            ksh, vsh = _shared_kv(x, row(kv_norm_g), w_kv_b, tile_row(k_norm_g, N_KV), bd,
                                  cos_p, sin_p, TS_KV)
    y_prompt = x.reshape(bsz, seq, D_MODEL)
    keep = min(WINDOW, seq)
    p_k = ksh.reshape(bsz, seq, KV_W)[:, seq - keep:].reshape(bsz, keep, N_KV, HEAD_DIM)
    p_v = vsh.reshape(bsz, seq, KV_W)[:, seq - keep:].reshape(bsz, keep, N_KV, HEAD_DIM)
    to_cache = lambda t: t.reshape(depth, bsz, MEM_HEADS, HEAD_DIM, N_MEM).transpose(0, 1, 4, 2, 3)
    p_mem_k = to_cache(pmk_t)
    p_mem_v = to_cache(pmv_t)

    xs = x_sample.reshape(dbsz, D_MODEL)
    cmk = cache_mem_k.transpose(0, 1, 3, 4, 2).reshape(depth, dbsz, MEM_W, N_MEM)
    cmv = cache_mem_v.transpose(0, 1, 3, 4, 2).reshape(depth, dbsz, MEM_W, N_MEM)
    wb = cache_swa_k.shape[1]
    ckb = cache_swa_k.transpose(0, 2, 3, 1).reshape(dbsz, KV_W, wb)
    cvb = cache_swa_v.transpose(0, 2, 3, 1).reshape(dbsz, KV_W, wb)
    s_h, s_rc = [], []
    s_fc = jnp.zeros_like(state_ffn_conv)
    kn = vn = None
    for l in range(depth):
        mqg = tile_row(mem_q_norm_g[l], MEM_HEADS)
        if l < n_a:
            main, qn, hnew, xrpre = _dec_in_a(
                xs, row(norm_mix_g[l]), (w_in_a_b, l), rnn_conv_w[l], row(rnn_conv_b[l]),
                (wg_b, l), row(b_gate_x[l]), row(b_gate_a[l]), row(lru_param[l]), mqg, bd,
                state_rglru_conv[l].transpose(1, 0, 2), state_rglru_h[l])
            s_h.append(hnew)
            s_rc.append(jnp.concatenate([state_rglru_conv[l][:, 1:], xrpre[:, None, :]], axis=1))
            wo, wo_layer = w_out_b, l
        else:
            j = l - n_a
            q, qn = _dec_in_b(xs, row(norm_mix_g[l]), (w_in_b_b, j), tile_row(q_norm_g[j], N_Q),
                              cos_s, sin_s, mqg, bd)
            sink_rows = jnp.zeros((DEC_HEAD_ROWS, LANES), F32).at[:N_Q].set(
                jnp.broadcast_to(sink_gk[j].reshape(N_Q, 1), (N_Q, LANES)))
            main = _dec_swa(q, ckb, cvb, kn, vn, sink_rows)
            wo, wo_layer = w_out_perm_b, j
        mo = _dec_mem_attn(qn, cmk, cmv, l)
        xs, s_fc = _dec_out_ffn(xs, main, mo, wo, wo_layer, l, row(norm_ffn_g[l]), wup_b,
                                fcw[l], fcb[l], wdn_b, state_ffn_conv, s_fc)
        if l == n_a - 1:
            kn, vn = _shared_kv(xs, row(kv_norm_g), w_kv_b, tile_row(k_norm_g, N_KV), bd,
                                cos_s, sin_s, dbsz)
    y_sample = xs.reshape(dbsz, 1, D_MODEL)
    s_k = kn.reshape(dbsz, 1, N_KV, HEAD_DIM)
    s_v = vn.reshape(dbsz, 1, N_KV, HEAD_DIM)

    return (y_prompt, y_sample, jnp.stack(p_h), jnp.stack(p_rc), jnp.stack(p_fc), p_k, p_v,
            p_mem_k, p_mem_v, jnp.stack(s_h), jnp.stack(s_rc), s_fc, s_k, s_v)
```
